```python
import jax, jax.numpy as jnp
from jax import lax
import numpy as np


D_MODEL = 2048
BATCH = 4
SEQ = 2048
DEPTH = 1
DEC_BATCH = 128
DEC_SEQ = 1
PAST_LEN = 16384
PAGE_SIZE = 128

D_MIX = D_MODEL
D_CHUNK = D_MIX // 2
CHUNK_HEADS = 8
CHUNK_HEAD_DIM = D_CHUNK // CHUNK_HEADS
CHUNK_LEN = 128
D_SSM = D_MIX - D_CHUNK
SSM_HEAD_DIM = 64
SSM_HEADS = D_SSM // SSM_HEAD_DIM
SSM_GROUPS = 2
SSM_HEADS_PER_GROUP = SSM_HEADS // SSM_GROUPS
D_STATE = 128
CONV_W = 4
CONV_DIM = D_SSM + 2 * SSM_GROUPS * D_STATE
SSD_CHUNK = 128
D_IN_PROJ = 2 * D_CHUNK + D_SSM + CONV_DIM + SSM_HEADS
D_FF = -(-8 * D_MODEL // (3 * 256)) * 256
EPS = 1e-6

kernel_name = 'hybrid_chunkmlp_ssd_decoder_step'


def rmsnorm(x, g):
    xf = x.astype(jnp.float32)
    y = xf * lax.rsqrt(jnp.mean(xf * xf, axis=-1, keepdims=True) + EPS)
    return (y * g.astype(jnp.float32)).astype(x.dtype)


def chunk_spatial_gate(u, v, w_s, b_s):
    bsz, t = v.shape[:2]
    L = min(CHUNK_LEN, t)
    n_c = -(-t // L)
    pad = n_c * L - t
    vc = jnp.pad(v, ((0, 0), (0, pad), (0, 0), (0, 0))).reshape(bsz, n_c, L, CHUNK_HEADS, CHUNK_HEAD_DIM)
    causal = jnp.tril(jnp.ones((L, L), bool))
    w = jnp.where(causal[None], w_s[:, :L, :L], 0.0).astype(vc.dtype)
    mixed = jnp.einsum('hts,bcshd->bcthd', w, vc) + b_s[:, :L].T.astype(vc.dtype)[None, None, :, :, None]
    mixed = mixed.reshape(bsz, n_c * L, CHUNK_HEADS, CHUNK_HEAD_DIM)[:, :t]
    return u * mixed


def ssd_scan(x, dt, a, bm, cm, h0):
    bsz, t = x.shape[:2]
    Q = min(SSD_CHUNK, t)
    n_c = -(-t // Q)
    pad = n_c * Q - t

    def chunks(z):
        z = jnp.pad(z, [(0, 0), (0, pad)] + [(0, 0)] * (z.ndim - 2))
        return z.reshape((bsz, n_c, Q) + z.shape[2:])

    xc, dtc, bc, cc = chunks(x), chunks(dt), chunks(bm), chunks(cm)
    acum = jnp.moveaxis(jnp.cumsum(dtc * a, axis=2), 2, -1)
    diff = acum[..., :, None] - acum[..., None, :]
    causal = jnp.tril(jnp.ones((Q, Q), bool))
    decay = jnp.exp(jnp.where(causal, diff, -jnp.inf))
    xdt = xc * dtc[..., None]
    cb = jnp.einsum('bctgn,bcsgn->bcgts', cc, bc)
    y_diag = jnp.einsum('bcgets,bcsgep->bctgep', cb[:, :, :, None] * decay, xdt)
    a_last = acum[..., -1]
    to_end = jnp.exp(a_last[..., None] - acum)
    states = jnp.einsum('bcsgn,bcges,bcsgep->bcgepn', bc, to_end, xdt)

    def step(h, inp):
        st, al = inp
        return h * jnp.exp(al)[..., None, None] + st, h

    h_final, h_prev = lax.scan(step, h0, (jnp.moveaxis(states, 1, 0), jnp.moveaxis(a_last, 1, 0)))
    h_prev = jnp.moveaxis(h_prev, 0, 1)
    y_off = jnp.einsum('bctgn,bcgepn,bcget->bctgep', cc, h_prev, jnp.exp(acum))
    y = (y_diag + y_off).reshape(bsz, n_c * Q, SSM_GROUPS, SSM_HEADS_PER_GROUP, SSM_HEAD_DIM)[:, :t]
    return y, h_final


def hybrid_layer(x, conv_prefix, h0, norm_mix_g, w_in, chunk_v_norm_g, chunk_w_s, chunk_b_s,
                 chunk_out_norm_g, ssd_conv_w, ssd_conv_b, ssd_dt_bias, ssd_a_log, ssd_d,
                 ssd_norm_g, w_out, norm_ffn_g, w_gate, w_up, w_down):
    bsz, t, _ = x.shape
    n = rmsnorm(x, norm_mix_g)
    proj = n @ w_in
    splits = [D_CHUNK, 2 * D_CHUNK, 2 * D_CHUNK + D_SSM, 2 * D_CHUNK + D_SSM + CONV_DIM]
    u_raw, v_raw, z, xbc, dt_raw = jnp.split(proj, splits, axis=-1)

    u = jax.nn.gelu(u_raw).reshape(bsz, t, CHUNK_HEADS, CHUNK_HEAD_DIM)
    v_rows = rmsnorm(jax.nn.gelu(v_raw), chunk_v_norm_g)
    y_a = chunk_spatial_gate(u, v_rows.reshape(bsz, t, CHUNK_HEADS, CHUNK_HEAD_DIM), chunk_w_s, chunk_b_s)
    y_a = rmsnorm(y_a.reshape(bsz, t, D_CHUNK), chunk_out_norm_g)

    padded = jnp.concatenate([conv_prefix.astype(xbc.dtype), xbc], axis=1)
    conv = ssd_conv_b
    for k in range(CONV_W):
        conv = conv + padded[:, k:k + t] * ssd_conv_w[k]
    xbc_act = jax.nn.silu(conv)
    new_conv = padded[:, -(CONV_W - 1):]
    xs, bm, cm = jnp.split(xbc_act, [D_SSM, D_SSM + SSM_GROUPS * D_STATE], axis=-1)
    xs = xs.astype(jnp.float32).reshape(bsz, t, SSM_GROUPS, SSM_HEADS_PER_GROUP, SSM_HEAD_DIM)
    bm = bm.astype(jnp.float32).reshape(bsz, t, SSM_GROUPS, D_STATE)
    cm = cm.astype(jnp.float32).reshape(bsz, t, SSM_GROUPS, D_STATE)
    dt = jax.nn.softplus(dt_raw.astype(jnp.float32) + ssd_dt_bias.astype(jnp.float32))
    dt = dt.reshape(bsz, t, SSM_GROUPS, SSM_HEADS_PER_GROUP)
    a = -jnp.exp(ssd_a_log.astype(jnp.float32)).reshape(SSM_GROUPS, SSM_HEADS_PER_GROUP)
    h0g = h0.astype(jnp.float32).reshape(bsz, SSM_GROUPS, SSM_HEADS_PER_GROUP, SSM_HEAD_DIM, D_STATE)
    y_b, h_final = ssd_scan(xs, dt, a, bm, cm, h0g)
    y_b = y_b + ssd_d.astype(jnp.float32).reshape(SSM_GROUPS, SSM_HEADS_PER_GROUP)[..., None] * xs
    y_b = y_b.reshape(bsz, t, D_SSM) * jax.nn.silu(z.astype(jnp.float32))
    y_b = y_b.reshape(bsz, t, SSM_GROUPS, D_SSM // SSM_GROUPS)
    y_b = y_b * lax.rsqrt(jnp.mean(y_b * y_b, axis=-1, keepdims=True) + EPS)
    y_b = (y_b.reshape(bsz, t, D_SSM) * ssd_norm_g.astype(jnp.float32)).astype(x.dtype)

    h = x + jnp.concatenate([y_a, y_b], axis=-1) @ w_out
    m = rmsnorm(h, norm_ffn_g)
    h = h + (jax.nn.silu(m @ w_gate) * (m @ w_up)) @ w_down
    new_ssm = h_final.reshape(bsz, SSM_HEADS, SSM_HEAD_DIM, D_STATE)
    return h, new_conv, new_ssm, v_rows


def setup_inputs(seed: int = 0) -> dict:
    key = jax.random.key(seed)
    ks = jax.random.split(key, 24)
    f32 = jnp.float32

    def nrm(k, shape, scale):
        return jax.random.normal(k, shape, f32) * scale

    dt0 = jnp.exp(jax.random.uniform(ks[12], (DEPTH, SSM_HEADS), f32, np.log(1e-3), np.log(1e-1)))
    return {
        'x_prompt': nrm(ks[0], (BATCH, SEQ, D_MODEL), 1.0),
        'x_sample': nrm(ks[1], (DEC_BATCH, DEC_SEQ, D_MODEL), 1.0),
        'state_conv': nrm(ks[2], (DEPTH, DEC_BATCH, CONV_W - 1, CONV_DIM), 1.0),
        'state_ssm': nrm(ks[3], (DEPTH, DEC_BATCH, SSM_HEADS, SSM_HEAD_DIM, D_STATE), 0.3),
        'norm_mix_g': 1.0 + nrm(ks[4], (DEPTH, D_MODEL), 0.05),
        'w_in': nrm(ks[5], (DEPTH, D_MODEL, D_IN_PROJ), D_MODEL ** -0.5),
        'chunk_v_norm_g': 1.0 + nrm(ks[6], (DEPTH, D_CHUNK), 0.05),
        'chunk_w_s': nrm(ks[7], (DEPTH, CHUNK_HEADS, CHUNK_LEN, CHUNK_LEN), CHUNK_LEN ** -0.5),
        'chunk_b_s': 1.0 + nrm(ks[8], (DEPTH, CHUNK_HEADS, CHUNK_LEN), 0.1),
        'chunk_out_norm_g': 1.0 + nrm(ks[9], (DEPTH, D_CHUNK), 0.05),
        'ssd_conv_w': nrm(ks[10], (DEPTH, CONV_W, CONV_DIM), CONV_W ** -0.5),
        'ssd_conv_b': nrm(ks[11], (DEPTH, CONV_DIM), 0.02),
        'ssd_dt_bias': dt0 + jnp.log(-jnp.expm1(-dt0)),
        'ssd_a_log': jnp.log(jax.random.uniform(ks[13], (DEPTH, SSM_HEADS), f32, 1.0, 16.0)),
        'ssd_d': 1.0 + nrm(ks[14], (DEPTH, SSM_HEADS), 0.1),
        'ssd_norm_g': 1.0 + nrm(ks[15], (DEPTH, D_SSM), 0.05),
        'w_out': nrm(ks[16], (DEPTH, D_MIX, D_MODEL), D_MIX ** -0.5),
        'norm_ffn_g': 1.0 + nrm(ks[17], (DEPTH, D_MODEL), 0.05),
        'w_gate': nrm(ks[18], (DEPTH, D_MODEL, D_FF), D_MODEL ** -0.5),
        'w_up': nrm(ks[19], (DEPTH, D_MODEL, D_FF), D_MODEL ** -0.5),
        'w_down': nrm(ks[20], (DEPTH, D_FF, D_MODEL), D_FF ** -0.5),
        'norm_final_g': 1.0 + nrm(ks[21], (D_MODEL,), 0.05),
    }


def reference(x_prompt, x_sample, state_conv, state_ssm, norm_mix_g, w_in, chunk_v_norm_g,
              chunk_w_s, chunk_b_s, chunk_out_norm_g, ssd_conv_w, ssd_conv_b, ssd_dt_bias,
              ssd_a_log, ssd_d, ssd_norm_g, w_out, norm_ffn_g, w_gate, w_up, w_down, norm_final_g):
    hp, hs = x_prompt, x_sample
    conv_p, ssm_p, conv_s, ssm_s, v_s = [], [], [], [], []
    for l in range(DEPTH):
        params = (norm_mix_g[l], w_in[l], chunk_v_norm_g[l], chunk_w_s[l], chunk_b_s[l],
                  chunk_out_norm_g[l], ssd_conv_w[l], ssd_conv_b[l], ssd_dt_bias[l], ssd_a_log[l],
                  ssd_d[l], ssd_norm_g[l], w_out[l], norm_ffn_g[l], w_gate[l], w_up[l], w_down[l])
        zero_conv = jnp.zeros((hp.shape[0], CONV_W - 1, CONV_DIM), hp.dtype)
        zero_ssm = jnp.zeros((hp.shape[0], SSM_HEADS, SSM_HEAD_DIM, D_STATE), jnp.float32)
        hp, cp, sp, _ = hybrid_layer(hp, zero_conv, zero_ssm, *params)
        hs, cs, ss, vs = hybrid_layer(hs, state_conv[l], state_ssm[l], *params)
        conv_p.append(cp)
        ssm_p.append(sp)
        conv_s.append(cs)
        ssm_s.append(ss)
        v_s.append(vs)
    y_prompt = rmsnorm(hp, norm_final_g)
    y_sample = rmsnorm(hs, norm_final_g)
    return (y_prompt, y_sample, jnp.stack(conv_p), jnp.stack(ssm_p), jnp.stack(conv_s), jnp.stack(ssm_s), jnp.stack(v_s))
```

```python
import functools

import jax
import jax.numpy as jnp
from jax import lax
from jax.experimental import pallas as pl
from jax.experimental.pallas import tpu as pltpu

F32 = jnp.float32
BF16 = jnp.bfloat16
HIGHEST = lax.Precision.HIGHEST

D_MODEL = 2048
D_CHUNK = 1024
CHUNK_HEADS = 8
CHUNK_HEAD_DIM = 128
CHUNK_LEN = 128
D_SSM = 1024
SSM_HEAD_DIM = 64
SSM_HEADS = 16
SSM_GROUPS = 2
HEADS_PER_GROUP = SSM_HEADS // SSM_GROUPS
GROUP_WIDTH = D_SSM // SSM_GROUPS
D_STATE = 128
CONV_W = 4
CONV_DIM = D_SSM + 2 * SSM_GROUPS * D_STATE
SSD_CHUNK = 128
D_MAIN = 2 * D_CHUNK + D_SSM + CONV_DIM
D_FF = 5632
EPS = 1e-6

LANES = 128
HEAD_PAD = LANES
VMEM_LIMIT = 56 * 1024 * 1024

U_BLK, V_BLK, Z_BLK = 0, 1, 2
XBC_BLK = 2


def _rms(x, g):
    return x * lax.rsqrt(jnp.mean(x * x, axis=-1, keepdims=True) + EPS) * g


def _silu(x):
    return x * jax.nn.sigmoid(x)


def _softplus(x):
    return jnp.maximum(x, 0.0) + jnp.log1p(jnp.exp(-jnp.abs(x)))


def _dot(a, b):
    return jnp.dot(a, b, preferred_element_type=F32)


def _dot_nt(a, b):
    return lax.dot_general(a, b, (((1,), (1,)), ((), ())), preferred_element_type=F32)


def _dot_exact(a, b):
    return jnp.dot(a, b, precision=HIGHEST, preferred_element_type=F32)


def _params(*semantics):
    return pltpu.CompilerParams(dimension_semantics=semantics, vmem_limit_bytes=VMEM_LIMIT)


IN_TN = 512
N_GELU_BLOCKS = 2 * D_CHUNK // IN_TN
N_SILU_BLOCKS = D_SSM // IN_TN


def _in_proj_kernel(x_ref, g_ref, w_ref, wdt_ref, o_ref, dt_ref, xn_ref):
    j = pl.program_id(1)

    @pl.when(j == 0)
    def _():
        nb = _rms(x_ref[...], g_ref[...]).astype(BF16)
        xn_ref[...] = nb
        dt_ref[...] = _dot(nb, wdt_ref[...])

    acc = _dot(xn_ref[...], w_ref[...])

    @pl.when(j < N_GELU_BLOCKS)
    def _():
        o_ref[...] = jax.nn.gelu(acc)

    @pl.when((j >= N_GELU_BLOCKS) & (j < N_GELU_BLOCKS + N_SILU_BLOCKS))
    def _():
        o_ref[...] = _silu(acc)

    @pl.when(j >= N_GELU_BLOCKS + N_SILU_BLOCKS)
    def _():
        o_ref[...] = acc


def _in_proj(x, g, w_main, w_dt, tm):
    m = x.shape[0]
    return pl.pallas_call(
        _in_proj_kernel,
        grid=(m // tm, D_MAIN // IN_TN),
        in_specs=[
            pl.BlockSpec((tm, D_MODEL), lambda i, j: (i, 0)),
            pl.BlockSpec((1, D_MODEL), lambda i, j: (0, 0)),
            pl.BlockSpec((D_MODEL, IN_TN), lambda i, j: (0, j)),
            pl.BlockSpec((D_MODEL, HEAD_PAD), lambda i, j: (0, 0)),
        ],
        out_specs=[
            pl.BlockSpec((tm, IN_TN), lambda i, j: (i, j)),
            pl.BlockSpec((tm, HEAD_PAD), lambda i, j: (i, 0)),
        ],
        out_shape=[
            jax.ShapeDtypeStruct((m, D_MAIN), F32),
            jax.ShapeDtypeStruct((m, HEAD_PAD), F32),
        ],
        scratch_shapes=[pltpu.VMEM((tm, D_MODEL), BF16)],
        compiler_params=_params("arbitrary", "arbitrary"),
        name="in_proj",
    )(x, g, w_main, w_dt)


def _chunk_gate_kernel(u_ref, v_ref, gv_ref, ws_ref, bst_ref, go_ref, o_ref):
    vb = _rms(v_ref[...], gv_ref[...]).astype(BF16)
    u = u_ref[...]
    row = lax.broadcasted_iota(jnp.int32, (CHUNK_LEN, CHUNK_LEN), 0)
    col = lax.broadcasted_iota(jnp.int32, (CHUNK_LEN, CHUNK_LEN), 1)
    causal = row >= col
    parts = []
    for h in range(CHUNK_HEADS):
        sl = slice(h * CHUNK_HEAD_DIM, (h + 1) * CHUNK_HEAD_DIM)
        w = jnp.where(causal, ws_ref[h], 0.0).astype(BF16)
        mixed = _dot(w, vb[:, sl]) + bst_ref[:, h:h + 1]
        parts.append(u[:, sl] * mixed)
    y = jnp.concatenate(parts, axis=1)
    o_ref[...] = _rms(y, go_ref[...]).astype(BF16)


def _chunk_gate(proj, gv, ws, bst, go):
    m = proj.shape[0]
    return pl.pallas_call(
        _chunk_gate_kernel,
        grid=(m // CHUNK_LEN,),
        in_specs=[
            pl.BlockSpec((CHUNK_LEN, D_CHUNK), lambda c: (c, U_BLK)),
            pl.BlockSpec((CHUNK_LEN, D_CHUNK), lambda c: (c, V_BLK)),
            pl.BlockSpec((1, D_CHUNK), lambda c: (0, 0)),
            pl.BlockSpec((CHUNK_HEADS, CHUNK_LEN, CHUNK_LEN), lambda c: (0, 0, 0)),
            pl.BlockSpec((CHUNK_LEN, CHUNK_HEADS), lambda c: (0, 0)),
            pl.BlockSpec((1, D_CHUNK), lambda c: (0, 0)),
        ],
        out_specs=pl.BlockSpec((CHUNK_LEN, D_CHUNK), lambda c: (c, 0)),
        out_shape=jax.ShapeDtypeStruct((m, D_CHUNK), BF16),
        compiler_params=_params("arbitrary"),
        name="chunk_gate",
    )(proj, proj, gv, ws, bst, go)


CONV_PAD = 8


def _group_norm_out(y, zs, g):
    y = y * zs
    parts = []
    for grp in range(SSM_GROUPS):
        yg = y[:, grp * GROUP_WIDTH:(grp + 1) * GROUP_WIDTH]
        parts.append(yg * lax.rsqrt(jnp.mean(yg * yg, axis=-1, keepdims=True) + EPS))
    return (jnp.concatenate(parts, axis=1) * g).astype(BF16)


def _ssd_kernel(xbc_ref, zs_ref, dt_ref, cw_ref, cb_ref, dtb_ref, alog_ref, drep_ref, ng_ref,
                rep_ref, rept_ref, y_ref, nconv_ref, nssm_ref, ext_ref, h_ref):
    q = SSD_CHUNK
    c = pl.program_id(1)

    @pl.when(c == 0)
    def _():
        ext_ref[0:CONV_PAD, :] = jnp.zeros((CONV_PAD, CONV_DIM), F32)
        h_ref[...] = jnp.zeros_like(h_ref)

    xbc = xbc_ref[...]
    ext_ref[CONV_PAD:CONV_PAD + q, :] = xbc
    conv = cb_ref[...]
    for k in range(CONV_W):
        lo = CONV_PAD - (CONV_W - 1) + k
        conv = conv + ext_ref[lo:lo + q, :] * cw_ref[k:k + 1, :]
    ext_ref[0:CONV_PAD, :] = xbc[q - CONV_PAD:, :]
    nconv_ref[...] = xbc[q - (CONV_W - 1):, :]

    act = _silu(conv)
    xs = act[:, :D_SSM]
    bmat = act[:, D_SSM:D_SSM + SSM_GROUPS * D_STATE]
    cmat = act[:, D_SSM + SSM_GROUPS * D_STATE:]

    dt = _softplus(dt_ref[...] + dtb_ref[...])
    a = -jnp.exp(alog_ref[...])
    row = lax.broadcasted_iota(jnp.int32, (q, q), 0)
    col = lax.broadcasted_iota(jnp.int32, (q, q), 1)
    causal = row >= col
    acum = _dot_exact(causal.astype(F32), dt * a)
    acum_t = acum.T
    rep = rep_ref[...]
    acum_rep = _dot_exact(acum, rep)
    xdt = xs * _dot_exact(dt, rep)
    xw = xdt * jnp.exp(acum_rep[q - 1:q, :] - acum_rep)
    y = drep_ref[...] * xs
    exp_a = jnp.exp(acum_rep)

    ydiag, yoff, states = [], [], []
    for grp in range(SSM_GROUPS):
        bg = bmat[:, grp * D_STATE:(grp + 1) * D_STATE].astype(BF16)
        cg = cmat[:, grp * D_STATE:(grp + 1) * D_STATE].astype(BF16)
        cb = _dot_nt(cg, bg)
        ch = slice(grp * GROUP_WIDTH, (grp + 1) * GROUP_WIDTH)
        yoff.append(_dot_nt(cg, h_ref[ch, :].astype(BF16)))
        for e in range(HEADS_PER_GROUP):
            h = grp * HEADS_PER_GROUP + e
            diff = acum[:, h:h + 1] - acum_t[h:h + 1, :]
            decay = jnp.exp(jnp.where(causal, diff, -jnp.inf))
            xh = xdt[:, h * SSM_HEAD_DIM:(h + 1) * SSM_HEAD_DIM].astype(BF16)
            ydiag.append(_dot((cb * decay).astype(BF16), xh))
        states.append(_dot(xw[:, ch].T.astype(BF16), bg))
    y = y + jnp.concatenate(ydiag, axis=1) + jnp.concatenate(yoff, axis=1) * exp_a

    a_last = jnp.broadcast_to(acum_t[:, q - 1:q], (HEAD_PAD, D_STATE))
    h_new = h_ref[...] * jnp.exp(_dot_exact(rept_ref[...], a_last)) + jnp.concatenate(states, axis=0)
    h_ref[...] = h_new
    nssm_ref[...] = h_new
    y_ref[...] = _group_norm_out(y, zs_ref[...], ng_ref[...])


def _ssd_prompt(proj, dt_raw, cw, cb, dtb, alog, drep, ng, rep, rept, batch, seq):
    n_c = seq // SSD_CHUNK
    m = batch * seq
    const2 = lambda b, c: (0, 0)
    return pl.pallas_call(
        _ssd_kernel,
        grid=(batch, n_c),
        in_specs=[
            pl.BlockSpec((SSD_CHUNK, CONV_DIM), lambda b, c: (b * n_c + c, XBC_BLK)),
            pl.BlockSpec((SSD_CHUNK, D_SSM), lambda b, c: (b * n_c + c, Z_BLK)),
            pl.BlockSpec((SSD_CHUNK, HEAD_PAD), lambda b, c: (b * n_c + c, 0)),
            pl.BlockSpec((CONV_W, CONV_DIM), const2),
            pl.BlockSpec((1, CONV_DIM), const2),
            pl.BlockSpec((1, HEAD_PAD), const2),
            pl.BlockSpec((1, HEAD_PAD), const2),
            pl.BlockSpec((1, D_SSM), const2),
            pl.BlockSpec((1, D_SSM), const2),
            pl.BlockSpec((HEAD_PAD, D_SSM), const2),
            pl.BlockSpec((D_SSM, HEAD_PAD), const2),
        ],
        out_specs=[
            pl.BlockSpec((SSD_CHUNK, D_SSM), lambda b, c: (b * n_c + c, 0)),
            pl.BlockSpec((None, CONV_W - 1, CONV_DIM), lambda b, c: (b, 0, 0)),
            pl.BlockSpec((None, D_SSM, D_STATE), lambda b, c: (b, 0, 0)),
        ],
        out_shape=[
            jax.ShapeDtypeStruct((m, D_SSM), BF16),
            jax.ShapeDtypeStruct((batch, CONV_W - 1, CONV_DIM), F32),
            jax.ShapeDtypeStruct((batch, D_SSM, D_STATE), F32),
        ],
        scratch_shapes=[
            pltpu.VMEM((CONV_PAD + SSD_CHUNK, CONV_DIM), F32),
            pltpu.VMEM((D_SSM, D_STATE), F32),
        ],
        compiler_params=_params("arbitrary", "arbitrary"),
        name="ssd_prompt",
    )(proj, proj, dt_raw, cw, cb, dtb, alog, drep, ng, rep, rept)


N_SPLIT = 3


def _sample_rows_kernel(u_ref, v_ref, xbc_ref, dt_ref, sc_ref, gv_ref, w0_ref, b0_ref, go_ref,
                        cw_ref, cb_ref, dtb_ref, alog_ref, drep_ref, rep_ref,
                        ya_ref, vrows_ref, nconv_ref, split_ref, b_ref, c_ref, dx_ref):
    vr = _rms(v_ref[...], gv_ref[...])
    vrows_ref[...] = vr
    mixed = vr * w0_ref[...] + b0_ref[...]
    ya_ref[...] = _rms(u_ref[...] * mixed, go_ref[...]).astype(BF16)

    xbc = xbc_ref[...]
    conv = cb_ref[...]
    for k in range(CONV_W - 1):
        conv = conv + sc_ref[:, k * CONV_DIM:(k + 1) * CONV_DIM] * cw_ref[k:k + 1, :]
    conv = conv + xbc * cw_ref[CONV_W - 1:CONV_W, :]
    nconv_ref[:, 0:(CONV_W - 2) * CONV_DIM] = sc_ref[:, CONV_DIM:(CONV_W - 1) * CONV_DIM]
    nconv_ref[:, (CONV_W - 2) * CONV_DIM:] = xbc

    act = _silu(conv)
    xs = act[:, :D_SSM]
    b_ref[...] = act[:, D_SSM:D_SSM + SSM_GROUPS * D_STATE]
    c_ref[...] = act[:, D_SSM + SSM_GROUPS * D_STATE:]
    dx_ref[...] = drep_ref[...] * xs

    dt = _softplus(dt_ref[...] + dtb_ref[...])
    decay = jnp.exp(dt * (-jnp.exp(alog_ref[...])))
    rep = rep_ref[...]
    xdt = xs * _dot_exact(dt, rep)
    cols = jnp.concatenate([xdt, _dot_exact(decay, rep)], axis=1).T
    for p in range(N_SPLIT):
        piece = cols.astype(BF16)
        split_ref[:, p * LANES:(p + 1) * LANES] = piece
        cols = cols - piece.astype(F32)


def _sample_rows(proj, dt_raw, sconv, gv, w0, b0, go, cw, cb, dtb, alog, drep, rep):
    n = proj.shape[0]
    full = lambda shape: pl.BlockSpec(shape, lambda i: (0,) * len(shape))
    return pl.pallas_call(
        _sample_rows_kernel,
        grid=(1,),
        in_specs=[
            pl.BlockSpec((n, D_CHUNK), lambda i: (0, U_BLK)),
            pl.BlockSpec((n, D_CHUNK), lambda i: (0, V_BLK)),
            pl.BlockSpec((n, CONV_DIM), lambda i: (0, XBC_BLK)),
            full((n, HEAD_PAD)),
            full((n, (CONV_W - 1) * CONV_DIM)),
            full((1, D_CHUNK)), full((1, D_CHUNK)), full((1, D_CHUNK)), full((1, D_CHUNK)),
            full((CONV_W, CONV_DIM)), full((1, CONV_DIM)),
            full((1, HEAD_PAD)), full((1, HEAD_PAD)), full((1, D_SSM)),
            full((HEAD_PAD, D_SSM)),
        ],
        out_specs=[
            full((n, D_CHUNK)), full((n, D_CHUNK)), full((n, (CONV_W - 1) * CONV_DIM)),
            full((2 * D_SSM, N_SPLIT * LANES)),
            full((n, SSM_GROUPS * D_STATE)), full((n, SSM_GROUPS * D_STATE)), full((n, D_SSM)),
        ],
        out_shape=[
            jax.ShapeDtypeStruct((n, D_CHUNK), BF16),
            jax.ShapeDtypeStruct((n, D_CHUNK), F32),
            jax.ShapeDtypeStruct((n, (CONV_W - 1) * CONV_DIM), F32),
            jax.ShapeDtypeStruct((2 * D_SSM, N_SPLIT * LANES), BF16),
            jax.ShapeDtypeStruct((n, SSM_GROUPS * D_STATE), F32),
            jax.ShapeDtypeStruct((n, SSM_GROUPS * D_STATE), F32),
            jax.ShapeDtypeStruct((n, D_SSM), F32),
        ],
        compiler_params=_params("arbitrary"),
        name="sample_rows",
    )(proj, proj, proj, dt_raw, sconv, gv, w0, b0, go, cw, cb, dtb, alog, drep, rep)


SAMPLE_TB = 8


def _sample_state_kernel(split_ref, b_ref, c_ref, h0_ref, dx_ref, zs_ref, ng_ref,
                         hn_ref, y_ref, yt_ref):
    i = pl.program_id(0)
    n_tok = b_ref.shape[0]

    @pl.when(i == 0)
    def _():
        yt_ref[...] = jnp.zeros_like(yt_ref)

    tok = lax.broadcasted_iota(jnp.int32, (N_SPLIT * LANES, D_STATE), 0) % LANES
    lane = lax.broadcasted_iota(jnp.int32, (D_SSM, n_tok), 1)

    def per_group(row, grp):
        return jnp.broadcast_to(row[:, grp * D_STATE:(grp + 1) * D_STATE], (GROUP_WIDTH, D_STATE))

    def body(k, carry):
        t = i * SAMPLE_TB + k
        onehot = (tok == t).astype(BF16)
        bcast = _dot(split_ref[...], onehot)
        brow = b_ref[pl.ds(t, 1), :]
        crow = c_ref[pl.ds(t, 1), :]
        bfull = jnp.concatenate([per_group(brow, g) for g in range(SSM_GROUPS)], axis=0)
        cfull = jnp.concatenate([per_group(crow, g) for g in range(SSM_GROUPS)], axis=0)
        h_new = h0_ref[k] * bcast[D_SSM:, :] + bcast[:D_SSM, :] * bfull
        hn_ref[k] = h_new
        ysum = jnp.sum(h_new * cfull, axis=-1, keepdims=True)
        yt_ref[...] = jnp.where(lane == t, ysum, yt_ref[...])
        return carry

    lax.fori_loop(0, SAMPLE_TB, body, 0)

    @pl.when(i == pl.num_programs(0) - 1)
    def _():
        y = yt_ref[...].T + dx_ref[...]
        y_ref[...] = _group_norm_out(y, zs_ref[...], ng_ref[...])


def _sample_state(split, bmat, cmat, h0, dx, proj, ng):
    n = h0.shape[0]
    const2 = lambda i: (0, 0)
    return pl.pallas_call(
        _sample_state_kernel,
        grid=(n // SAMPLE_TB,),
        in_specs=[
            pl.BlockSpec((2 * D_SSM, N_SPLIT * LANES), const2),
            pl.BlockSpec((n, SSM_GROUPS * D_STATE), const2),
            pl.BlockSpec((n, SSM_GROUPS * D_STATE), const2),
            pl.BlockSpec((SAMPLE_TB, D_SSM, D_STATE), lambda i: (i, 0, 0)),
            pl.BlockSpec((n, D_SSM), const2),
            pl.BlockSpec((n, D_SSM), lambda i: (0, Z_BLK)),
            pl.BlockSpec((1, D_SSM), const2),
        ],
        out_specs=[
            pl.BlockSpec((SAMPLE_TB, D_SSM, D_STATE), lambda i: (i, 0, 0)),
            pl.BlockSpec((n, D_SSM), const2),
        ],
        out_shape=[
            jax.ShapeDtypeStruct((n, D_SSM, D_STATE), F32),
            jax.ShapeDtypeStruct((n, D_SSM), BF16),
        ],
        scratch_shapes=[pltpu.VMEM((D_SSM, n), F32)],
        compiler_params=_params("arbitrary"),
        name="sample_state",
    )(split, bmat, cmat, h0, dx, proj, ng)


OUT_TN = 512


def _out_proj_kernel(x_ref, ya_ref, yb_ref, wa_ref, wb_ref, o_ref):
    o_ref[...] = x_ref[...] + _dot(ya_ref[...], wa_ref[...]) + _dot(yb_ref[...], wb_ref[...])


def _out_proj(x, ya, yb, w_out, tm):
    m = x.shape[0]
    return pl.pallas_call(
        _out_proj_kernel,
        grid=(m // tm, D_MODEL // OUT_TN),
        in_specs=[
            pl.BlockSpec((tm, OUT_TN), lambda i, j: (i, j)),
            pl.BlockSpec((tm, D_CHUNK), lambda i, j: (i, 0)),
            pl.BlockSpec((tm, D_SSM), lambda i, j: (i, 0)),
            pl.BlockSpec((D_CHUNK, OUT_TN), lambda i, j: (0, j)),
            pl.BlockSpec((D_SSM, OUT_TN), lambda i, j: (D_CHUNK // D_SSM, j)),
        ],
        out_specs=pl.BlockSpec((tm, OUT_TN), lambda i, j: (i, j)),
        out_shape=jax.ShapeDtypeStruct((m, D_MODEL), F32),
        compiler_params=_params("arbitrary", "arbitrary"),
        name="out_proj",
    )(x, ya, yb, w_out, w_out)


FFN_TF = 512


def _ffn_kernel(h_ref, g_ref, wg_ref, wu_ref, wd_ref, gf_ref, o_ref, m_ref):
    f = pl.program_id(1)

    @pl.when(f == 0)
    def _():
        h = h_ref[...]
        m_ref[...] = _rms(h, g_ref[...]).astype(BF16)
        o_ref[...] = h

    m = m_ref[...]
    act = (_silu(_dot(m, wg_ref[...])) * _dot(m, wu_ref[...])).astype(BF16)
    o_ref[...] += _dot(act, wd_ref[...])

    @pl.when(f == pl.num_programs(1) - 1)
    def _():
        o_ref[...] = _rms(o_ref[...], gf_ref[...])


def _ffn(h, g, w_gate, w_up, w_down, g_final, tm):
    m = h.shape[0]
    return pl.pallas_call(
        _ffn_kernel,
        grid=(m // tm, D_FF // FFN_TF),
        in_specs=[
            pl.BlockSpec((tm, D_MODEL), lambda i, f: (i, 0)),
            pl.BlockSpec((1, D_MODEL), lambda i, f: (0, 0)),
            pl.BlockSpec((D_MODEL, FFN_TF), lambda i, f: (0, f)),
            pl.BlockSpec((D_MODEL, FFN_TF), lambda i, f: (0, f)),
            pl.BlockSpec((FFN_TF, D_MODEL), lambda i, f: (f, 0)),
            pl.BlockSpec((1, D_MODEL), lambda i, f: (0, 0)),
        ],
        out_specs=pl.BlockSpec((tm, D_MODEL), lambda i, f: (i, 0)),
        out_shape=jax.ShapeDtypeStruct((m, D_MODEL), F32),
        scratch_shapes=[pltpu.VMEM((tm, D_MODEL), BF16)],
        compiler_params=_params("arbitrary", "arbitrary"),
        name="ffn",
    )(h, g, w_gate, w_up, w_down, g_final)


def _row(v, width=None):
    v = v.astype(F32).reshape(1, -1)
    if width is not None and v.shape[1] < width:
        v = jnp.pad(v, ((0, 0), (0, width - v.shape[1])))
    return v


def kernel(x_prompt, x_sample, state_conv, state_ssm, norm_mix_g, w_in, chunk_v_norm_g, chunk_w_s,
           chunk_b_s, chunk_out_norm_g, ssd_conv_w, ssd_conv_b, ssd_dt_bias, ssd_a_log, ssd_d,
           ssd_norm_g, w_out, norm_ffn_g, w_gate, w_up, w_down, norm_final_g):
    depth = w_in.shape[0]
    batch, seq, _ = x_prompt.shape
    n_s = x_sample.shape[0]
    assert x_sample.shape[1] == 1 and seq % SSD_CHUNK == 0 and depth == 1

    rep = (lax.broadcasted_iota(jnp.int32, (HEAD_PAD, D_SSM), 0)
           == lax.broadcasted_iota(jnp.int32, (HEAD_PAD, D_SSM), 1) // SSM_HEAD_DIM).astype(F32)
    rept = rep.T

    hp = x_prompt.reshape(batch * seq, D_MODEL)
    hs = x_sample.reshape(n_s, D_MODEL)
    l = 0
    w_main = w_in[l, :, :D_MAIN].astype(BF16)
    w_dt = jnp.pad(w_in[l, :, D_MAIN:], ((0, 0), (0, HEAD_PAD - SSM_HEADS))).astype(BF16)
    w_out_b = w_out[l].astype(BF16)
    w_gate_b = w_gate[l].astype(BF16)
    w_up_b = w_up[l].astype(BF16)
    w_down_b = w_down[l].astype(BF16)
    g_mix = _row(norm_mix_g[l])
    gv = _row(chunk_v_norm_g[l])
    go = _row(chunk_out_norm_g[l])
    cw = ssd_conv_w[l].astype(F32)
    cb = _row(ssd_conv_b[l])
    dtb = _row(ssd_dt_bias[l], HEAD_PAD)
    alog = _row(ssd_a_log[l], HEAD_PAD)
    drep = _row(jnp.repeat(ssd_d[l], SSM_HEAD_DIM))
    ng = _row(ssd_norm_g[l])
    g_ffn = _row(norm_ffn_g[l])
    g_final = _row(norm_final_g)

    proj_p, dt_p = _in_proj(hp, g_mix, w_main, w_dt, tm=1024)
    ya_p = _chunk_gate(proj_p, gv, chunk_w_s[l].astype(F32), chunk_b_s[l].astype(F32).T, go)
    yb_p, conv_p, ssm_p = _ssd_prompt(proj_p, dt_p, cw, cb, dtb, alog, drep, ng, rep, rept, batch, seq)
    h_p = _out_proj(hp, ya_p, yb_p, w_out_b, tm=1024)
    y_p = _ffn(h_p, g_ffn, w_gate_b, w_up_b, w_down_b, g_final, tm=512)

    proj_s, dt_s = _in_proj(hs, g_mix, w_main, w_dt, tm=n_s)
    w0 = _row(jnp.repeat(chunk_w_s[l, :, 0, 0], CHUNK_HEAD_DIM))
    b0 = _row(jnp.repeat(chunk_b_s[l, :, 0], CHUNK_HEAD_DIM))
    ya_s, v_s, conv_s, split, b_s, c_s, dx_s = _sample_rows(
        proj_s, dt_s, state_conv[l].reshape(n_s, (CONV_W - 1) * CONV_DIM), gv, w0, b0, go,
        cw, cb, dtb, alog, drep, rep)
    ssm_s, yb_s = _sample_state(split, b_s, c_s, state_ssm[l].reshape(n_s, D_SSM, D_STATE),
                                dx_s, proj_s, ng)
    h_s = _out_proj(hs, ya_s, yb_s, w_out_b, tm=n_s)
    y_s = _ffn(h_s, g_ffn, w_gate_b, w_up_b, w_down_b, g_final, tm=n_s)

    return (
        y_p.reshape(batch, seq, D_MODEL),
        y_s.reshape(n_s, 1, D_MODEL),
        conv_p[None],
        ssm_p.reshape(1, batch, SSM_HEADS, SSM_HEAD_DIM, D_STATE),
        conv_s.reshape(1, n_s, CONV_W - 1, CONV_DIM),
        ssm_s.reshape(1, n_s, SSM_HEADS, SSM_HEAD_DIM, D_STATE),
        v_s.reshape(1, n_s, 1, D_CHUNK),
    )
```

```python
import functools

import jax
import jax.numpy as jnp
from jax import lax
from jax.experimental import pallas as pl
from jax.experimental.pallas import tpu as pltpu

F32 = jnp.float32
BF16 = jnp.bfloat16
HIGHEST = lax.Precision.HIGHEST

D_MODEL = 2048
D_CHUNK = 1024
CHUNK_HEADS = 8
CHUNK_HEAD_DIM = 128
CHUNK_LEN = 128
D_SSM = 1024
SSM_HEAD_DIM = 64
SSM_HEADS = 16
SSM_GROUPS = 2
HEADS_PER_GROUP = SSM_HEADS // SSM_GROUPS
GROUP_WIDTH = D_SSM // SSM_GROUPS
D_STATE = 128
CONV_W = 4
CONV_DIM = D_SSM + 2 * SSM_GROUPS * D_STATE
SSD_CHUNK = 128
D_MAIN = 2 * D_CHUNK + D_SSM + CONV_DIM
D_FF = 5632
EPS = 1e-6

LANES = 128
HEAD_PAD = LANES
VMEM_LIMIT = 56 * 1024 * 1024

U_BLK, V_BLK, Z_BLK = 0, 1, 2
XBC_BLK = 2


def _rms(x, g):
    return x * lax.rsqrt(jnp.mean(x * x, axis=-1, keepdims=True) + EPS) * g


def _silu(x):
    return x * jax.nn.sigmoid(x)


def _softplus(x):
    return jnp.maximum(x, 0.0) + jnp.log1p(jnp.exp(-jnp.abs(x)))


def _dot(a, b):
    return jnp.dot(a, b, preferred_element_type=F32)


def _dot_nt(a, b):
    return lax.dot_general(a, b, (((1,), (1,)), ((), ())), preferred_element_type=F32)


def _dot_exact(a, b):
    return jnp.dot(a, b, precision=HIGHEST, preferred_element_type=F32)


def _params(*semantics):
    return pltpu.CompilerParams(dimension_semantics=semantics, vmem_limit_bytes=VMEM_LIMIT)


IN_TN = 512
N_GELU_BLOCKS = 2 * D_CHUNK // IN_TN
N_SILU_BLOCKS = D_SSM // IN_TN


def _in_proj_kernel(x_ref, g_ref, w_ref, wdt_ref, o_ref, dt_ref, xn_ref):
    j = pl.program_id(1)

    @pl.when(j == 0)
    def _():
        nb = _rms(x_ref[...], g_ref[...]).astype(BF16)
        xn_ref[...] = nb
        dt_ref[...] = _dot(nb, wdt_ref[...])

    acc = _dot(xn_ref[...], w_ref[...].astype(BF16))

    @pl.when(j < N_GELU_BLOCKS)
    def _():
        o_ref[...] = jax.nn.gelu(acc)

    @pl.when((j >= N_GELU_BLOCKS) & (j < N_GELU_BLOCKS + N_SILU_BLOCKS))
    def _():
        o_ref[...] = _silu(acc)

    @pl.when(j >= N_GELU_BLOCKS + N_SILU_BLOCKS)
    def _():
        o_ref[...] = acc


def _in_proj(x, g, w_in, layer, w_dt, tm):
    m = x.shape[0]
    return pl.pallas_call(
        _in_proj_kernel,
        grid=(m // tm, D_MAIN // IN_TN),
        in_specs=[
            pl.BlockSpec((tm, D_MODEL), lambda i, j: (i, 0)),
            pl.BlockSpec((1, D_MODEL), lambda i, j: (0, 0)),
            pl.BlockSpec((None, D_MODEL, IN_TN), lambda i, j: (layer, 0, j)),
            pl.BlockSpec((D_MODEL, HEAD_PAD), lambda i, j: (0, 0)),
        ],
        out_specs=[
            pl.BlockSpec((tm, IN_TN), lambda i, j: (i, j)),
            pl.BlockSpec((tm, HEAD_PAD), lambda i, j: (i, 0)),
        ],
        out_shape=[
            jax.ShapeDtypeStruct((m, D_MAIN), F32),
            jax.ShapeDtypeStruct((m, HEAD_PAD), F32),
        ],
        scratch_shapes=[pltpu.VMEM((tm, D_MODEL), BF16)],
        compiler_params=_params("arbitrary", "arbitrary"),
        name="in_proj",
    )(x, g, w_in, w_dt)


def _chunk_gate_kernel(u_ref, v_ref, gv_ref, ws_ref, bst_ref, go_ref, o_ref):
    vb = _rms(v_ref[...], gv_ref[...]).astype(BF16)
    u = u_ref[...]
    row = lax.broadcasted_iota(jnp.int32, (CHUNK_LEN, CHUNK_LEN), 0)
    col = lax.broadcasted_iota(jnp.int32, (CHUNK_LEN, CHUNK_LEN), 1)
    causal = row >= col
    parts = []
    for h in range(CHUNK_HEADS):
        sl = slice(h * CHUNK_HEAD_DIM, (h + 1) * CHUNK_HEAD_DIM)
        w = jnp.where(causal, ws_ref[h], 0.0).astype(BF16)
        mixed = _dot(w, vb[:, sl]) + bst_ref[:, h:h + 1]
        parts.append(u[:, sl] * mixed)
    y = jnp.concatenate(parts, axis=1)
    o_ref[...] = _rms(y, go_ref[...]).astype(BF16)


def _chunk_gate(proj, gv, ws, bst, go):
    m = proj.shape[0]
    return pl.pallas_call(
        _chunk_gate_kernel,
        grid=(m // CHUNK_LEN,),
        in_specs=[
            pl.BlockSpec((CHUNK_LEN, D_CHUNK), lambda c: (c, U_BLK)),
            pl.BlockSpec((CHUNK_LEN, D_CHUNK), lambda c: (c, V_BLK)),
            pl.BlockSpec((1, D_CHUNK), lambda c: (0, 0)),
            pl.BlockSpec((CHUNK_HEADS, CHUNK_LEN, CHUNK_LEN), lambda c: (0, 0, 0)),
            pl.BlockSpec((CHUNK_LEN, CHUNK_HEADS), lambda c: (0, 0)),
            pl.BlockSpec((1, D_CHUNK), lambda c: (0, 0)),
        ],
        out_specs=pl.BlockSpec((CHUNK_LEN, D_CHUNK), lambda c: (c, 0)),
        out_shape=jax.ShapeDtypeStruct((m, D_CHUNK), BF16),
        compiler_params=_params("arbitrary"),
        name="chunk_gate",
    )(proj, proj, gv, ws, bst, go)


CONV_PAD = 8


def _group_norm_out(y, zs, g):
    y = y * zs
    parts = []
    for grp in range(SSM_GROUPS):
        yg = y[:, grp * GROUP_WIDTH:(grp + 1) * GROUP_WIDTH]
        parts.append(yg * lax.rsqrt(jnp.mean(yg * yg, axis=-1, keepdims=True) + EPS))
    return (jnp.concatenate(parts, axis=1) * g).astype(BF16)


def _ssd_kernel(xbc_ref, zs_ref, dt_ref, cw_ref, cb_ref, dtb_ref, alog_ref, drep_ref, ng_ref,
                rep_ref, rept_ref, wg_ref, wu_ref, wd_ref,
                y_ref, nconv_ref, nssm_ref, wgb_ref, wub_ref, wdb_ref, ext_ref, h_ref):
    q = SSD_CHUNK
    c = pl.program_id(1)

    wgb_ref[...] = wg_ref[...].astype(BF16)
    wub_ref[...] = wu_ref[...].astype(BF16)
    wdb_ref[...] = wd_ref[...].astype(BF16)

    @pl.when(c == 0)
    def _():
        ext_ref[0:CONV_PAD, :] = jnp.zeros((CONV_PAD, CONV_DIM), F32)
        h_ref[...] = jnp.zeros_like(h_ref)

    xbc = xbc_ref[...]
    ext_ref[CONV_PAD:CONV_PAD + q, :] = xbc
    conv = cb_ref[...]
    for k in range(CONV_W):
        lo = CONV_PAD - (CONV_W - 1) + k
        conv = conv + ext_ref[lo:lo + q, :] * cw_ref[k:k + 1, :]
    ext_ref[0:CONV_PAD, :] = xbc[q - CONV_PAD:, :]
    nconv_ref[...] = xbc[q - (CONV_W - 1):, :]

    act = _silu(conv)
    xs = act[:, :D_SSM]
    bmat = act[:, D_SSM:D_SSM + SSM_GROUPS * D_STATE]
    cmat = act[:, D_SSM + SSM_GROUPS * D_STATE:]

    dt = _softplus(dt_ref[...] + dtb_ref[...])
    a = -jnp.exp(alog_ref[...])
    row = lax.broadcasted_iota(jnp.int32, (q, q), 0)
    col = lax.broadcasted_iota(jnp.int32, (q, q), 1)
    causal = row >= col
    acum = _dot_exact(causal.astype(F32), dt * a)
    acum_t = acum.T
    rep = rep_ref[...]
    acum_rep = _dot_exact(acum, rep)
    xdt = xs * _dot_exact(dt, rep)
    xw = xdt * jnp.exp(acum_rep[q - 1:q, :] - acum_rep)
    y = drep_ref[...] * xs
    exp_a = jnp.exp(acum_rep)

    ydiag, yoff, states = [], [], []
    for grp in range(SSM_GROUPS):
        bg = bmat[:, grp * D_STATE:(grp + 1) * D_STATE].astype(BF16)
        cg = cmat[:, grp * D_STATE:(grp + 1) * D_STATE].astype(BF16)
        cb = _dot_nt(cg, bg)
        ch = slice(grp * GROUP_WIDTH, (grp + 1) * GROUP_WIDTH)
        yoff.append(_dot_nt(cg, h_ref[ch, :].astype(BF16)))
        for e in range(HEADS_PER_GROUP):
            h = grp * HEADS_PER_GROUP + e
            diff = acum[:, h:h + 1] - acum_t[h:h + 1, :]
            decay = jnp.exp(jnp.where(causal, diff, -jnp.inf))
            xh = xdt[:, h * SSM_HEAD_DIM:(h + 1) * SSM_HEAD_DIM].astype(BF16)
            ydiag.append(_dot((cb * decay).astype(BF16), xh))
        states.append(_dot(xw[:, ch].T.astype(BF16), bg))
    y = y + jnp.concatenate(ydiag, axis=1) + jnp.concatenate(yoff, axis=1) * exp_a

    a_last = jnp.broadcast_to(acum_t[:, q - 1:q], (HEAD_PAD, D_STATE))
    h_new = h_ref[...] * jnp.exp(_dot_exact(rept_ref[...], a_last)) + jnp.concatenate(states, axis=0)
    h_ref[...] = h_new
    nssm_ref[...] = h_new
    y_ref[...] = _group_norm_out(y, zs_ref[...], ng_ref[...])


BF16_SUBLANES = 16


def _ssd_prompt(proj, dt_raw, cw, cb, dtb, alog, drep, ng, rep, rept, w_gate, w_up, w_down, layer,
                batch, seq):
    n_c = seq // SSD_CHUNK
    m = batch * seq
    n_steps = batch * n_c
    gu_rows = D_MODEL // n_steps
    dn_hold = 2
    dn_rows = D_FF * dn_hold // n_steps
    assert gu_rows * n_steps == D_MODEL and gu_rows % BF16_SUBLANES == 0
    assert dn_rows * n_steps == D_FF * dn_hold and dn_rows % BF16_SUBLANES == 0
    const2 = lambda b, c: (0, 0)
    gu_in = pl.BlockSpec((None, gu_rows, D_FF), lambda b, c: (layer, b * n_c + c, 0))
    gu_out = pl.BlockSpec((gu_rows, D_FF), lambda b, c: (b * n_c + c, 0))
    dn_in = pl.BlockSpec((None, dn_rows, D_MODEL), lambda b, c: (layer, (b * n_c + c) // dn_hold, 0))
    dn_out = pl.BlockSpec((dn_rows, D_MODEL), lambda b, c: ((b * n_c + c) // dn_hold, 0))
    return pl.pallas_call(
        _ssd_kernel,
        grid=(batch, n_c),
        in_specs=[
            pl.BlockSpec((SSD_CHUNK, CONV_DIM), lambda b, c: (b * n_c + c, XBC_BLK)),
            pl.BlockSpec((SSD_CHUNK, D_SSM), lambda b, c: (b * n_c + c, Z_BLK)),
            pl.BlockSpec((SSD_CHUNK, HEAD_PAD), lambda b, c: (b * n_c + c, 0)),
            pl.BlockSpec((CONV_W, CONV_DIM), const2),
            pl.BlockSpec((1, CONV_DIM), const2),
            pl.BlockSpec((1, HEAD_PAD), const2),
            pl.BlockSpec((1, HEAD_PAD), const2),
            pl.BlockSpec((1, D_SSM), const2),
            pl.BlockSpec((1, D_SSM), const2),
            pl.BlockSpec((HEAD_PAD, D_SSM), const2),
            pl.BlockSpec((D_SSM, HEAD_PAD), const2),
            gu_in, gu_in, dn_in,
        ],
        out_specs=[
            pl.BlockSpec((SSD_CHUNK, D_SSM), lambda b, c: (b * n_c + c, 0)),
            pl.BlockSpec((None, CONV_W - 1, CONV_DIM), lambda b, c: (b, 0, 0)),
            pl.BlockSpec((None, D_SSM, D_STATE), lambda b, c: (b, 0, 0)),
            gu_out, gu_out, dn_out,
        ],
        out_shape=[
            jax.ShapeDtypeStruct((m, D_SSM), BF16),
            jax.ShapeDtypeStruct((batch, CONV_W - 1, CONV_DIM), F32),
            jax.ShapeDtypeStruct((batch, D_SSM, D_STATE), F32),
            jax.ShapeDtypeStruct((D_MODEL, D_FF), BF16),
            jax.ShapeDtypeStruct((D_MODEL, D_FF), BF16),
            jax.ShapeDtypeStruct((D_FF, D_MODEL), BF16),
        ],
        scratch_shapes=[
            pltpu.VMEM((CONV_PAD + SSD_CHUNK, CONV_DIM), F32),
            pltpu.VMEM((D_SSM, D_STATE), F32),
        ],
        compiler_params=_params("arbitrary", "arbitrary"),
        name="ssd_prompt",
    )(proj, proj, dt_raw, cw, cb, dtb, alog, drep, ng, rep, rept, w_gate, w_up, w_down)


N_SPLIT = 3


def _sample_rows_kernel(u_ref, v_ref, xbc_ref, dt_ref, sc_ref, gv_ref, w0_ref, b0_ref, go_ref,
                        cw_ref, cb_ref, dtb_ref, alog_ref, drep_ref, rep_ref,
                        ya_ref, vrows_ref, nconv_ref, split_ref, b_ref, c_ref, dx_ref):
    vr = _rms(v_ref[...], gv_ref[...])
    vrows_ref[...] = vr
    mixed = vr * w0_ref[...] + b0_ref[...]
    ya_ref[...] = _rms(u_ref[...] * mixed, go_ref[...]).astype(BF16)

    xbc = xbc_ref[...]
    conv = cb_ref[...]
    for k in range(CONV_W - 1):
        conv = conv + sc_ref[:, k * CONV_DIM:(k + 1) * CONV_DIM] * cw_ref[k:k + 1, :]
    conv = conv + xbc * cw_ref[CONV_W - 1:CONV_W, :]
    nconv_ref[:, 0:(CONV_W - 2) * CONV_DIM] = sc_ref[:, CONV_DIM:(CONV_W - 1) * CONV_DIM]
    nconv_ref[:, (CONV_W - 2) * CONV_DIM:] = xbc

    act = _silu(conv)
    xs = act[:, :D_SSM]
    b_ref[...] = act[:, D_SSM:D_SSM + SSM_GROUPS * D_STATE]
    c_ref[...] = act[:, D_SSM + SSM_GROUPS * D_STATE:]
    dx_ref[...] = drep_ref[...] * xs

    dt = _softplus(dt_ref[...] + dtb_ref[...])
    decay = jnp.exp(dt * (-jnp.exp(alog_ref[...])))
    rep = rep_ref[...]
    xdt = xs * _dot_exact(dt, rep)
    cols = jnp.concatenate([xdt, _dot_exact(decay, rep)], axis=1).T
    for p in range(N_SPLIT):
        piece = cols.astype(BF16)
        split_ref[:, p * LANES:(p + 1) * LANES] = piece
        cols = cols - piece.astype(F32)


def _sample_rows(proj, dt_raw, sconv, gv, w0, b0, go, cw, cb, dtb, alog, drep, rep):
    n = proj.shape[0]
    full = lambda shape: pl.BlockSpec(shape, lambda i: (0,) * len(shape))
    return pl.pallas_call(
        _sample_rows_kernel,
        grid=(1,),
        in_specs=[
            pl.BlockSpec((n, D_CHUNK), lambda i: (0, U_BLK)),
            pl.BlockSpec((n, D_CHUNK), lambda i: (0, V_BLK)),
            pl.BlockSpec((n, CONV_DIM), lambda i: (0, XBC_BLK)),
            full((n, HEAD_PAD)),
            full((n, (CONV_W - 1) * CONV_DIM)),
            full((1, D_CHUNK)), full((1, D_CHUNK)), full((1, D_CHUNK)), full((1, D_CHUNK)),
            full((CONV_W, CONV_DIM)), full((1, CONV_DIM)),
            full((1, HEAD_PAD)), full((1, HEAD_PAD)), full((1, D_SSM)),
            full((HEAD_PAD, D_SSM)),
        ],
        out_specs=[
            full((n, D_CHUNK)), full((n, D_CHUNK)), full((n, (CONV_W - 1) * CONV_DIM)),
            full((2 * D_SSM, N_SPLIT * LANES)),
            full((n, SSM_GROUPS * D_STATE)), full((n, SSM_GROUPS * D_STATE)), full((n, D_SSM)),
        ],
        out_shape=[
            jax.ShapeDtypeStruct((n, D_CHUNK), BF16),
            jax.ShapeDtypeStruct((n, D_CHUNK), F32),
            jax.ShapeDtypeStruct((n, (CONV_W - 1) * CONV_DIM), F32),
            jax.ShapeDtypeStruct((2 * D_SSM, N_SPLIT * LANES), BF16),
            jax.ShapeDtypeStruct((n, SSM_GROUPS * D_STATE), F32),
            jax.ShapeDtypeStruct((n, SSM_GROUPS * D_STATE), F32),
            jax.ShapeDtypeStruct((n, D_SSM), F32),
        ],
        compiler_params=_params("arbitrary"),
        name="sample_rows",
    )(proj, proj, proj, dt_raw, sconv, gv, w0, b0, go, cw, cb, dtb, alog, drep, rep)


SAMPLE_TB = 8


def _sample_state_kernel(split_ref, b_ref, c_ref, h0_ref, dx_ref, zs_ref, ng_ref,
                         hn_ref, y_ref, yt_ref):
    i = pl.program_id(0)
    n_tok = b_ref.shape[0]

    @pl.when(i == 0)
    def _():
        yt_ref[...] = jnp.zeros_like(yt_ref)

    tok = lax.broadcasted_iota(jnp.int32, (N_SPLIT * LANES, D_STATE), 0) % LANES
    lane = lax.broadcasted_iota(jnp.int32, (D_SSM, n_tok), 1)

    def per_group(row, grp):
        return jnp.broadcast_to(row[:, grp * D_STATE:(grp + 1) * D_STATE], (GROUP_WIDTH, D_STATE))

    def body(k, carry):
        t = i * SAMPLE_TB + k
        onehot = (tok == t).astype(BF16)
        bcast = _dot(split_ref[...], onehot)
        brow = b_ref[pl.ds(t, 1), :]
        crow = c_ref[pl.ds(t, 1), :]
        bfull = jnp.concatenate([per_group(brow, g) for g in range(SSM_GROUPS)], axis=0)
        cfull = jnp.concatenate([per_group(crow, g) for g in range(SSM_GROUPS)], axis=0)
        h_new = h0_ref[k] * bcast[D_SSM:, :] + bcast[:D_SSM, :] * bfull
        hn_ref[k] = h_new
        ysum = jnp.sum(h_new * cfull, axis=-1, keepdims=True)
        yt_ref[...] = jnp.where(lane == t, ysum, yt_ref[...])
        return carry

    lax.fori_loop(0, SAMPLE_TB, body, 0)

    @pl.when(i == pl.num_programs(0) - 1)
    def _():
        y = yt_ref[...].T + dx_ref[...]
        y_ref[...] = _group_norm_out(y, zs_ref[...], ng_ref[...])


def _sample_state(split, bmat, cmat, h0, dx, proj, ng):
    n = h0.shape[0]
    const2 = lambda i: (0, 0)
    return pl.pallas_call(
        _sample_state_kernel,
        grid=(n // SAMPLE_TB,),
        in_specs=[
            pl.BlockSpec((2 * D_SSM, N_SPLIT * LANES), const2),
            pl.BlockSpec((n, SSM_GROUPS * D_STATE), const2),
            pl.BlockSpec((n, SSM_GROUPS * D_STATE), const2),
            pl.BlockSpec((SAMPLE_TB, D_SSM, D_STATE), lambda i: (i, 0, 0)),
            pl.BlockSpec((n, D_SSM), const2),
            pl.BlockSpec((n, D_SSM), lambda i: (0, Z_BLK)),
            pl.BlockSpec((1, D_SSM), const2),
        ],
        out_specs=[
            pl.BlockSpec((SAMPLE_TB, D_SSM, D_STATE), lambda i: (i, 0, 0)),
            pl.BlockSpec((n, D_SSM), const2),
        ],
        out_shape=[
            jax.ShapeDtypeStruct((n, D_SSM, D_STATE), F32),
            jax.ShapeDtypeStruct((n, D_SSM), BF16),
        ],
        scratch_shapes=[pltpu.VMEM((D_SSM, n), F32)],
        compiler_params=_params("arbitrary"),
        name="sample_state",
    )(split, bmat, cmat, h0, dx, proj, ng)


OUT_TN = 512


def _out_proj_kernel(x_ref, ya_ref, yb_ref, wa_ref, wb_ref, o_ref):
    o_ref[...] = (x_ref[...] + _dot(ya_ref[...], wa_ref[...].astype(BF16))
                  + _dot(yb_ref[...], wb_ref[...].astype(BF16)))


def _out_proj(x, ya, yb, w_out, layer, tm):
    m = x.shape[0]
    return pl.pallas_call(
        _out_proj_kernel,
        grid=(m // tm, D_MODEL // OUT_TN),
        in_specs=[
            pl.BlockSpec((tm, OUT_TN), lambda i, j: (i, j)),
            pl.BlockSpec((tm, D_CHUNK), lambda i, j: (i, 0)),
            pl.BlockSpec((tm, D_SSM), lambda i, j: (i, 0)),
            pl.BlockSpec((None, D_CHUNK, OUT_TN), lambda i, j: (layer, 0, j)),
            pl.BlockSpec((None, D_SSM, OUT_TN), lambda i, j: (layer, D_CHUNK // D_SSM, j)),
        ],
        out_specs=pl.BlockSpec((tm, OUT_TN), lambda i, j: (i, j)),
        out_shape=jax.ShapeDtypeStruct((m, D_MODEL), F32),
        compiler_params=_params("arbitrary", "arbitrary"),
        name="out_proj",
    )(x, ya, yb, w_out, w_out)


FFN_TF = 512


def _ffn_kernel(h_ref, g_ref, wg_ref, wu_ref, wd_ref, gf_ref, o_ref, m_ref):
    f = pl.program_id(1)

    @pl.when(f == 0)
    def _():
        h = h_ref[...]
        m_ref[...] = _rms(h, g_ref[...]).astype(BF16)
        o_ref[...] = h

    m = m_ref[...]
    act = (_silu(_dot(m, wg_ref[...])) * _dot(m, wu_ref[...])).astype(BF16)
    o_ref[...] += _dot(act, wd_ref[...])

    @pl.when(f == pl.num_programs(1) - 1)
    def _():
        o_ref[...] = _rms(o_ref[...], gf_ref[...])


def _ffn(h, g, w_gate, w_up, w_down, g_final, tm):
    m = h.shape[0]
    return pl.pallas_call(
        _ffn_kernel,
        grid=(m // tm, D_FF // FFN_TF),
        in_specs=[
            pl.BlockSpec((tm, D_MODEL), lambda i, f: (i, 0)),
            pl.BlockSpec((1, D_MODEL), lambda i, f: (0, 0)),
            pl.BlockSpec((D_MODEL, FFN_TF), lambda i, f: (0, f)),
            pl.BlockSpec((D_MODEL, FFN_TF), lambda i, f: (0, f)),
            pl.BlockSpec((FFN_TF, D_MODEL), lambda i, f: (f, 0)),
            pl.BlockSpec((1, D_MODEL), lambda i, f: (0, 0)),
        ],
        out_specs=pl.BlockSpec((tm, D_MODEL), lambda i, f: (i, 0)),
        out_shape=jax.ShapeDtypeStruct((m, D_MODEL), F32),
        scratch_shapes=[pltpu.VMEM((tm, D_MODEL), BF16)],
        compiler_params=_params("arbitrary", "arbitrary"),
        name="ffn",
    )(h, g, w_gate, w_up, w_down, g_final)


def _row(v, width=None):
    v = v.astype(F32).reshape(1, -1)
    if width is not None and v.shape[1] < width:
        v = jnp.pad(v, ((0, 0), (0, width - v.shape[1])))
    return v


def kernel(x_prompt, x_sample, state_conv, state_ssm, norm_mix_g, w_in, chunk_v_norm_g, chunk_w_s,
           chunk_b_s, chunk_out_norm_g, ssd_conv_w, ssd_conv_b, ssd_dt_bias, ssd_a_log, ssd_d,
           ssd_norm_g, w_out, norm_ffn_g, w_gate, w_up, w_down, norm_final_g):
    depth = w_in.shape[0]
    batch, seq, _ = x_prompt.shape
    n_s = x_sample.shape[0]
    assert x_sample.shape[1] == 1 and seq % SSD_CHUNK == 0 and depth == 1

    rep = (lax.broadcasted_iota(jnp.int32, (HEAD_PAD, D_SSM), 0)
           == lax.broadcasted_iota(jnp.int32, (HEAD_PAD, D_SSM), 1) // SSM_HEAD_DIM).astype(F32)
    rept = rep.T

    hp = x_prompt.reshape(batch * seq, D_MODEL)
    hs = x_sample.reshape(n_s, D_MODEL)
    l = 0
    w_dt = jnp.pad(w_in[l, :, D_MAIN:], ((0, 0), (0, HEAD_PAD - SSM_HEADS))).astype(BF16)
    g_mix = _row(norm_mix_g[l])
    gv = _row(chunk_v_norm_g[l])
    go = _row(chunk_out_norm_g[l])
    cw = ssd_conv_w[l].astype(F32)
    cb = _row(ssd_conv_b[l])
    dtb = _row(ssd_dt_bias[l], HEAD_PAD)
    alog = _row(ssd_a_log[l], HEAD_PAD)
    drep = _row(jnp.repeat(ssd_d[l], SSM_HEAD_DIM))
    ng = _row(ssd_norm_g[l])
    g_ffn = _row(norm_ffn_g[l])
    g_final = _row(norm_final_g)

    proj_p, dt_p = _in_proj(hp, g_mix, w_in, l, w_dt, tm=1024)
    ya_p = _chunk_gate(proj_p, gv, chunk_w_s[l].astype(F32), chunk_b_s[l].astype(F32).T, go)
    yb_p, conv_p, ssm_p, w_gate_b, w_up_b, w_down_b = _ssd_prompt(
        proj_p, dt_p, cw, cb, dtb, alog, drep, ng, rep, rept, w_gate, w_up, w_down, l, batch, seq)
    h_p = _out_proj(hp, ya_p, yb_p, w_out, l, tm=1024)
    y_p = _ffn(h_p, g_ffn, w_gate_b, w_up_b, w_down_b, g_final, tm=512)

    proj_s, dt_s = _in_proj(hs, g_mix, w_in, l, w_dt, tm=n_s)
    w0 = _row(jnp.repeat(chunk_w_s[l, :, 0, 0], CHUNK_HEAD_DIM))
    b0 = _row(jnp.repeat(chunk_b_s[l, :, 0], CHUNK_HEAD_DIM))
    ya_s, v_s, conv_s, split, b_s, c_s, dx_s = _sample_rows(
        proj_s, dt_s, state_conv[l].reshape(n_s, (CONV_W - 1) * CONV_DIM), gv, w0, b0, go,
        cw, cb, dtb, alog, drep, rep)
    ssm_s, yb_s = _sample_state(split, b_s, c_s, state_ssm[l].reshape(n_s, D_SSM, D_STATE),
                                dx_s, proj_s, ng)
    h_s = _out_proj(hs, ya_s, yb_s, w_out, l, tm=n_s)
    y_s = _ffn(h_s, g_ffn, w_gate_b, w_up_b, w_down_b, g_final, tm=n_s)

    return (
        y_p.reshape(batch, seq, D_MODEL),
        y_s.reshape(n_s, 1, D_MODEL),
        conv_p[None],
        ssm_p.reshape(1, batch, SSM_HEADS, SSM_HEAD_DIM, D_STATE),
        conv_s.reshape(1, n_s, CONV_W - 1, CONV_DIM),
        ssm_s.reshape(1, n_s, SSM_HEADS, SSM_HEAD_DIM, D_STATE),
        v_s.reshape(1, n_s, 1, D_CHUNK),
    )
```

```python
import functools

import jax
import jax.numpy as jnp
from jax import lax
from jax.experimental import pallas as pl
from jax.experimental.pallas import tpu as pltpu

F32 = jnp.float32
BF16 = jnp.bfloat16
HIGHEST = lax.Precision.HIGHEST

D_MODEL = 2048
D_CHUNK = 1024
CHUNK_HEADS = 8
CHUNK_HEAD_DIM = 128
CHUNK_LEN = 128
D_SSM = 1024
SSM_HEAD_DIM = 64
SSM_HEADS = 16
SSM_GROUPS = 2
HEADS_PER_GROUP = SSM_HEADS // SSM_GROUPS
GROUP_WIDTH = D_SSM // SSM_GROUPS
D_STATE = 128
CONV_W = 4
CONV_DIM = D_SSM + 2 * SSM_GROUPS * D_STATE
SSD_CHUNK = 128
D_MAIN = 2 * D_CHUNK + D_SSM + CONV_DIM
D_FF = 5632
EPS = 1e-6

LANES = 128
HEAD_PAD = LANES
VMEM_LIMIT = 56 * 1024 * 1024

U_BLK, V_BLK, Z_BLK = 0, 1, 2
XBC_BLK = 2


def _rms(x, g):
    return x * lax.rsqrt(jnp.mean(x * x, axis=-1, keepdims=True) + EPS) * g


def _silu(x):
    return x * jax.nn.sigmoid(x)


def _softplus(x):
    return jnp.maximum(x, 0.0) + jnp.log1p(jnp.exp(-jnp.abs(x)))


def _dot(a, b):
    return jnp.dot(a, b, preferred_element_type=F32)


def _dot_nt(a, b):
    return lax.dot_general(a, b, (((1,), (1,)), ((), ())), preferred_element_type=F32)


def _dot_exact(a, b):
    return jnp.dot(a, b, precision=HIGHEST, preferred_element_type=F32)


def _params(*semantics):
    return pltpu.CompilerParams(dimension_semantics=semantics, vmem_limit_bytes=VMEM_LIMIT)


IN_TN = 512


def _in_proj_kernel(x_ref, g_ref, w_ref, wdt_ref, o_ref, dt_ref, xn_ref):
    j = pl.program_id(1)

    @pl.when(j == 0)
    def _():
        nb = _rms(x_ref[...], g_ref[...]).astype(BF16)
        xn_ref[...] = nb
        dt_ref[...] = _dot_nt(nb, wdt_ref[...])

    o_ref[...] = _dot_nt(xn_ref[...], w_ref[...].astype(BF16))


def _in_proj(x, g, w_in_t, layer, w_dt_t, tm):
    m = x.shape[0]
    return pl.pallas_call(
        _in_proj_kernel,
        grid=(m // tm, D_MAIN // IN_TN),
        in_specs=[
            pl.BlockSpec((tm, D_MODEL), lambda i, j: (i, 0)),
            pl.BlockSpec((1, D_MODEL), lambda i, j: (0, 0)),
            pl.BlockSpec((None, IN_TN, D_MODEL), lambda i, j: (layer, j, 0)),
            pl.BlockSpec((HEAD_PAD, D_MODEL), lambda i, j: (0, 0)),
        ],
        out_specs=[
            pl.BlockSpec((tm, IN_TN), lambda i, j: (i, j)),
            pl.BlockSpec((tm, HEAD_PAD), lambda i, j: (i, 0)),
        ],
        out_shape=[
            jax.ShapeDtypeStruct((m, D_MAIN), F32),
            jax.ShapeDtypeStruct((m, HEAD_PAD), F32),
        ],
        scratch_shapes=[pltpu.VMEM((tm, D_MODEL), BF16)],
        compiler_params=_params("arbitrary", "arbitrary"),
        name="in_proj",
    )(x, g, w_in_t, w_dt_t)


def _chunk_gate_kernel(u_ref, v_ref, gv_ref, ws_ref, bst_ref, go_ref, o_ref):
    vb = _rms(jax.nn.gelu(v_ref[...]), gv_ref[...]).astype(BF16)
    u = jax.nn.gelu(u_ref[...])
    row = lax.broadcasted_iota(jnp.int32, (CHUNK_LEN, CHUNK_LEN), 0)
    col = lax.broadcasted_iota(jnp.int32, (CHUNK_LEN, CHUNK_LEN), 1)
    causal = row >= col
    parts = []
    for h in range(CHUNK_HEADS):
        sl = slice(h * CHUNK_HEAD_DIM, (h + 1) * CHUNK_HEAD_DIM)
        w = jnp.where(causal, ws_ref[h], 0.0).astype(BF16)
        mixed = _dot(w, vb[:, sl]) + bst_ref[:, h:h + 1]
        parts.append(u[:, sl] * mixed)
    y = jnp.concatenate(parts, axis=1)
    o_ref[...] = _rms(y, go_ref[...]).astype(BF16)


def _chunk_gate(proj, gv, ws, bst, go):
    m = proj.shape[0]
    return pl.pallas_call(
        _chunk_gate_kernel,
        grid=(m // CHUNK_LEN,),
        in_specs=[
            pl.BlockSpec((CHUNK_LEN, D_CHUNK), lambda c: (c, U_BLK)),
            pl.BlockSpec((CHUNK_LEN, D_CHUNK), lambda c: (c, V_BLK)),
            pl.BlockSpec((1, D_CHUNK), lambda c: (0, 0)),
            pl.BlockSpec((CHUNK_HEADS, CHUNK_LEN, CHUNK_LEN), lambda c: (0, 0, 0)),
            pl.BlockSpec((CHUNK_LEN, CHUNK_HEADS), lambda c: (0, 0)),
            pl.BlockSpec((1, D_CHUNK), lambda c: (0, 0)),
        ],
        out_specs=pl.BlockSpec((CHUNK_LEN, D_CHUNK), lambda c: (c, 0)),
        out_shape=jax.ShapeDtypeStruct((m, D_CHUNK), BF16),
        compiler_params=_params("arbitrary"),
        name="chunk_gate",
    )(proj, proj, gv, ws, bst, go)


CONV_PAD = 8


def _group_norm_out(y, z, g):
    y = y * _silu(z)
    parts = []
    for grp in range(SSM_GROUPS):
        yg = y[:, grp * GROUP_WIDTH:(grp + 1) * GROUP_WIDTH]
        parts.append(yg * lax.rsqrt(jnp.mean(yg * yg, axis=-1, keepdims=True) + EPS))
    return (jnp.concatenate(parts, axis=1) * g).astype(BF16)


def _ssd_kernel(xbc_ref, z_ref, dt_ref, cw_ref, cb_ref, dtb_ref, alog_ref, drep_ref, ng_ref,
                rep_ref, rept_ref, wg_ref, wu_ref, wd_ref,
                y_ref, nconv_ref, nssm_ref, wgb_ref, wub_ref, wdb_ref, ext_ref, h_ref):
    q = SSD_CHUNK
    c = pl.program_id(1)

    wgb_ref[...] = wg_ref[...].astype(BF16)
    wub_ref[...] = wu_ref[...].astype(BF16)
    wdb_ref[...] = wd_ref[...].astype(BF16)

    @pl.when(c == 0)
    def _():
        ext_ref[0:CONV_PAD, :] = jnp.zeros((CONV_PAD, CONV_DIM), F32)
        h_ref[...] = jnp.zeros_like(h_ref)

    xbc = xbc_ref[...]
    ext_ref[CONV_PAD:CONV_PAD + q, :] = xbc
    conv = cb_ref[...]
    for k in range(CONV_W):
        lo = CONV_PAD - (CONV_W - 1) + k
        conv = conv + ext_ref[lo:lo + q, :] * cw_ref[k:k + 1, :]
    ext_ref[0:CONV_PAD, :] = xbc[q - CONV_PAD:, :]
    nconv_ref[...] = xbc[q - (CONV_W - 1):, :]

    act = _silu(conv)
    xs = act[:, :D_SSM]
    bmat = act[:, D_SSM:D_SSM + SSM_GROUPS * D_STATE]
    cmat = act[:, D_SSM + SSM_GROUPS * D_STATE:]

    dt = _softplus(dt_ref[...] + dtb_ref[...])
    a = -jnp.exp(alog_ref[...])
    row = lax.broadcasted_iota(jnp.int32, (q, q), 0)
    col = lax.broadcasted_iota(jnp.int32, (q, q), 1)
    causal = row >= col
    acum = _dot_exact(causal.astype(F32), dt * a)
    acum_t = acum.T
    rep = rep_ref[...]
    acum_rep = _dot_exact(acum, rep)
    xdt = xs * _dot_exact(dt, rep)
    xw = xdt * jnp.exp(acum_rep[q - 1:q, :] - acum_rep)
    y = drep_ref[...] * xs
    exp_a = jnp.exp(acum_rep)

    ydiag, yoff, states = [], [], []
    for grp in range(SSM_GROUPS):
        bg = bmat[:, grp * D_STATE:(grp + 1) * D_STATE].astype(BF16)
        cg = cmat[:, grp * D_STATE:(grp + 1) * D_STATE].astype(BF16)
        cb = _dot_nt(cg, bg)
        ch = slice(grp * GROUP_WIDTH, (grp + 1) * GROUP_WIDTH)
        yoff.append(_dot_nt(cg, h_ref[ch, :].astype(BF16)))
        for e in range(HEADS_PER_GROUP):
            h = grp * HEADS_PER_GROUP + e
            diff = acum[:, h:h + 1] - acum_t[h:h + 1, :]
            decay = jnp.exp(jnp.where(causal, diff, -jnp.inf))
            xh = xdt[:, h * SSM_HEAD_DIM:(h + 1) * SSM_HEAD_DIM].astype(BF16)
            ydiag.append(_dot((cb * decay).astype(BF16), xh))
        states.append(_dot(xw[:, ch].T.astype(BF16), bg))
    y = y + jnp.concatenate(ydiag, axis=1) + jnp.concatenate(yoff, axis=1) * exp_a

    a_last = jnp.broadcast_to(acum_t[:, q - 1:q], (HEAD_PAD, D_STATE))
    h_new = h_ref[...] * jnp.exp(_dot_exact(rept_ref[...], a_last)) + jnp.concatenate(states, axis=0)
    h_ref[...] = h_new
    nssm_ref[...] = h_new
    y_ref[...] = _group_norm_out(y, z_ref[...], ng_ref[...])


BF16_SUBLANES = 16


def _ssd_prompt(proj, dt_raw, cw, cb, dtb, alog, drep, ng, rep, rept, w_gate, w_up, w_down, layer,
                batch, seq):
    n_c = seq // SSD_CHUNK
    m = batch * seq
    n_steps = batch * n_c
    gu_rows = D_MODEL // n_steps
    dn_hold = 2
    dn_rows = D_FF * dn_hold // n_steps
    assert gu_rows * n_steps == D_MODEL and gu_rows % BF16_SUBLANES == 0
    assert dn_rows * n_steps == D_FF * dn_hold and dn_rows % BF16_SUBLANES == 0
    const2 = lambda b, c: (0, 0)
    gu_in = pl.BlockSpec((None, gu_rows, D_FF), lambda b, c: (layer, b * n_c + c, 0))
    gu_out = pl.BlockSpec((gu_rows, D_FF), lambda b, c: (b * n_c + c, 0))
    dn_in = pl.BlockSpec((None, dn_rows, D_MODEL), lambda b, c: (layer, (b * n_c + c) // dn_hold, 0))
    dn_out = pl.BlockSpec((dn_rows, D_MODEL), lambda b, c: ((b * n_c + c) // dn_hold, 0))
    return pl.pallas_call(
        _ssd_kernel,
        grid=(batch, n_c),
        in_specs=[
            pl.BlockSpec((SSD_CHUNK, CONV_DIM), lambda b, c: (b * n_c + c, XBC_BLK)),
            pl.BlockSpec((SSD_CHUNK, D_SSM), lambda b, c: (b * n_c + c, Z_BLK)),
            pl.BlockSpec((SSD_CHUNK, HEAD_PAD), lambda b, c: (b * n_c + c, 0)),
            pl.BlockSpec((CONV_W, CONV_DIM), const2),
            pl.BlockSpec((1, CONV_DIM), const2),
            pl.BlockSpec((1, HEAD_PAD), const2),
            pl.BlockSpec((1, HEAD_PAD), const2),
            pl.BlockSpec((1, D_SSM), const2),
            pl.BlockSpec((1, D_SSM), const2),
            pl.BlockSpec((HEAD_PAD, D_SSM), const2),
            pl.BlockSpec((D_SSM, HEAD_PAD), const2),
            gu_in, gu_in, dn_in,
        ],
        out_specs=[
            pl.BlockSpec((SSD_CHUNK, D_SSM), lambda b, c: (b * n_c + c, 0)),
            pl.BlockSpec((None, CONV_W - 1, CONV_DIM), lambda b, c: (b, 0, 0)),
            pl.BlockSpec((None, D_SSM, D_STATE), lambda b, c: (b, 0, 0)),
            gu_out, gu_out, dn_out,
        ],
        out_shape=[
            jax.ShapeDtypeStruct((m, D_SSM), BF16),
            jax.ShapeDtypeStruct((batch, CONV_W - 1, CONV_DIM), F32),
            jax.ShapeDtypeStruct((batch, D_SSM, D_STATE), F32),
            jax.ShapeDtypeStruct((D_MODEL, D_FF), BF16),
            jax.ShapeDtypeStruct((D_MODEL, D_FF), BF16),
            jax.ShapeDtypeStruct((D_FF, D_MODEL), BF16),
        ],
        scratch_shapes=[
            pltpu.VMEM((CONV_PAD + SSD_CHUNK, CONV_DIM), F32),
            pltpu.VMEM((D_SSM, D_STATE), F32),
        ],
        compiler_params=_params("arbitrary", "arbitrary"),
        name="ssd_prompt",
    )(proj, proj, dt_raw, cw, cb, dtb, alog, drep, ng, rep, rept, w_gate, w_up, w_down)


N_SPLIT = 3


def _sample_rows_kernel(u_ref, v_ref, xbc_ref, dt_ref, sc_ref, gv_ref, w0_ref, b0_ref, go_ref,
                        cw_ref, cb_ref, dtb_ref, alog_ref, drep_ref, rep_ref,
                        ya_ref, vrows_ref, nconv_ref, split_ref, b_ref, c_ref, dx_ref):
    vr = _rms(jax.nn.gelu(v_ref[...]), gv_ref[...])
    vrows_ref[...] = vr
    mixed = vr * w0_ref[...] + b0_ref[...]
    ya_ref[...] = _rms(jax.nn.gelu(u_ref[...]) * mixed, go_ref[...]).astype(BF16)

    xbc = xbc_ref[...]
    conv = cb_ref[...]
    for k in range(CONV_W - 1):
        conv = conv + sc_ref[:, k * CONV_DIM:(k + 1) * CONV_DIM] * cw_ref[k:k + 1, :]
    conv = conv + xbc * cw_ref[CONV_W - 1:CONV_W, :]
    nconv_ref[:, 0:(CONV_W - 2) * CONV_DIM] = sc_ref[:, CONV_DIM:(CONV_W - 1) * CONV_DIM]
    nconv_ref[:, (CONV_W - 2) * CONV_DIM:] = xbc

    act = _silu(conv)
    xs = act[:, :D_SSM]
    b_ref[...] = act[:, D_SSM:D_SSM + SSM_GROUPS * D_STATE]
    c_ref[...] = act[:, D_SSM + SSM_GROUPS * D_STATE:]
    dx_ref[...] = drep_ref[...] * xs

    dt = _softplus(dt_ref[...] + dtb_ref[...])
    decay = jnp.exp(dt * (-jnp.exp(alog_ref[...])))
    rep = rep_ref[...]
    xdt = xs * _dot_exact(dt, rep)
    cols = jnp.concatenate([xdt, _dot_exact(decay, rep)], axis=1).T
    for p in range(N_SPLIT):
        piece = cols.astype(BF16)
        split_ref[:, p * LANES:(p + 1) * LANES] = piece
        cols = cols - piece.astype(F32)


def _sample_rows(proj, dt_raw, sconv, gv, w0, b0, go, cw, cb, dtb, alog, drep, rep):
    n = proj.shape[0]
    full = lambda shape: pl.BlockSpec(shape, lambda i: (0,) * len(shape))
    return pl.pallas_call(
        _sample_rows_kernel,
        grid=(1,),
        in_specs=[
            pl.BlockSpec((n, D_CHUNK), lambda i: (0, U_BLK)),
            pl.BlockSpec((n, D_CHUNK), lambda i: (0, V_BLK)),
            pl.BlockSpec((n, CONV_DIM), lambda i: (0, XBC_BLK)),
            full((n, HEAD_PAD)),
            full((n, (CONV_W - 1) * CONV_DIM)),
            full((1, D_CHUNK)), full((1, D_CHUNK)), full((1, D_CHUNK)), full((1, D_CHUNK)),
            full((CONV_W, CONV_DIM)), full((1, CONV_DIM)),
            full((1, HEAD_PAD)), full((1, HEAD_PAD)), full((1, D_SSM)),
            full((HEAD_PAD, D_SSM)),
        ],
        out_specs=[
            full((n, D_CHUNK)), full((n, D_CHUNK)), full((n, (CONV_W - 1) * CONV_DIM)),
            full((2 * D_SSM, N_SPLIT * LANES)),
            full((n, SSM_GROUPS * D_STATE)), full((n, SSM_GROUPS * D_STATE)), full((n, D_SSM)),
        ],
        out_shape=[
            jax.ShapeDtypeStruct((n, D_CHUNK), BF16),
            jax.ShapeDtypeStruct((n, D_CHUNK), F32),
            jax.ShapeDtypeStruct((n, (CONV_W - 1) * CONV_DIM), F32),
            jax.ShapeDtypeStruct((2 * D_SSM, N_SPLIT * LANES), BF16),
            jax.ShapeDtypeStruct((n, SSM_GROUPS * D_STATE), F32),
            jax.ShapeDtypeStruct((n, SSM_GROUPS * D_STATE), F32),
            jax.ShapeDtypeStruct((n, D_SSM), F32),
        ],
        compiler_params=_params("arbitrary"),
        name="sample_rows",
    )(proj, proj, proj, dt_raw, sconv, gv, w0, b0, go, cw, cb, dtb, alog, drep, rep)


SAMPLE_TB = 8


def _sample_state_kernel(split_ref, b_ref, c_ref, h0_ref, dx_ref, z_ref, ng_ref,
                         hn_ref, y_ref, yt_ref):
    i = pl.program_id(0)
    n_tok = b_ref.shape[0]

    @pl.when(i == 0)
    def _():
        yt_ref[...] = jnp.zeros_like(yt_ref)

    tok = lax.broadcasted_iota(jnp.int32, (N_SPLIT * LANES, D_STATE), 0) % LANES
    lane = lax.broadcasted_iota(jnp.int32, (D_SSM, n_tok), 1)

    def per_group(row, grp):
        return jnp.broadcast_to(row[:, grp * D_STATE:(grp + 1) * D_STATE], (GROUP_WIDTH, D_STATE))

    def body(k, carry):
        t = i * SAMPLE_TB + k
        onehot = (tok == t).astype(BF16)
        bcast = _dot(split_ref[...], onehot)
        brow = b_ref[pl.ds(t, 1), :]
        crow = c_ref[pl.ds(t, 1), :]
        bfull = jnp.concatenate([per_group(brow, g) for g in range(SSM_GROUPS)], axis=0)
        cfull = jnp.concatenate([per_group(crow, g) for g in range(SSM_GROUPS)], axis=0)
        h_new = h0_ref[k] * bcast[D_SSM:, :] + bcast[:D_SSM, :] * bfull
        hn_ref[k] = h_new
        ysum = jnp.sum(h_new * cfull, axis=-1, keepdims=True)
        yt_ref[...] = jnp.where(lane == t, ysum, yt_ref[...])
        return carry

    lax.fori_loop(0, SAMPLE_TB, body, 0)

    @pl.when(i == pl.num_programs(0) - 1)
    def _():
        y = yt_ref[...].T + dx_ref[...]
        y_ref[...] = _group_norm_out(y, z_ref[...], ng_ref[...])


def _sample_state(split, bmat, cmat, h0, dx, proj, ng):
    n = h0.shape[0]
    const2 = lambda i: (0, 0)
    return pl.pallas_call(
        _sample_state_kernel,
        grid=(n // SAMPLE_TB,),
        in_specs=[
            pl.BlockSpec((2 * D_SSM, N_SPLIT * LANES), const2),
            pl.BlockSpec((n, SSM_GROUPS * D_STATE), const2),
            pl.BlockSpec((n, SSM_GROUPS * D_STATE), const2),
            pl.BlockSpec((SAMPLE_TB, D_SSM, D_STATE), lambda i: (i, 0, 0)),
            pl.BlockSpec((n, D_SSM), const2),
            pl.BlockSpec((n, D_SSM), lambda i: (0, Z_BLK)),
            pl.BlockSpec((1, D_SSM), const2),
        ],
        out_specs=[
            pl.BlockSpec((SAMPLE_TB, D_SSM, D_STATE), lambda i: (i, 0, 0)),
            pl.BlockSpec((n, D_SSM), const2),
        ],
        out_shape=[
            jax.ShapeDtypeStruct((n, D_SSM, D_STATE), F32),
            jax.ShapeDtypeStruct((n, D_SSM), BF16),
        ],
        scratch_shapes=[pltpu.VMEM((D_SSM, n), F32)],
        compiler_params=_params("arbitrary"),
        name="sample_state",
    )(split, bmat, cmat, h0, dx, proj, ng)


OUT_TN = 512


def _out_proj_kernel(x_ref, ya_ref, yb_ref, wa_ref, wb_ref, o_ref):
    y = jnp.concatenate([ya_ref[...], yb_ref[...]], axis=1)
    w = jnp.concatenate([wa_ref[...], wb_ref[...]], axis=0).astype(BF16)
    o_ref[...] = x_ref[...] + _dot(y, w)


def _out_proj(x, ya, yb, w_out, layer, tm):
    m = x.shape[0]
    return pl.pallas_call(
        _out_proj_kernel,
        grid=(m // tm, D_MODEL // OUT_TN),
        in_specs=[
            pl.BlockSpec((tm, OUT_TN), lambda i, j: (i, j)),
            pl.BlockSpec((tm, D_CHUNK), lambda i, j: (i, 0)),
            pl.BlockSpec((tm, D_SSM), lambda i, j: (i, 0)),
            pl.BlockSpec((None, D_CHUNK, OUT_TN), lambda i, j: (layer, 0, j)),
            pl.BlockSpec((None, D_SSM, OUT_TN), lambda i, j: (layer, D_CHUNK // D_SSM, j)),
        ],
        out_specs=pl.BlockSpec((tm, OUT_TN), lambda i, j: (i, j)),
        out_shape=jax.ShapeDtypeStruct((m, D_MODEL), F32),
        compiler_params=_params("arbitrary", "arbitrary"),
        name="out_proj",
    )(x, ya, yb, w_out, w_out)


FFN_TF = 512


def _ffn_kernel(h_ref, g_ref, wg_ref, wu_ref, wd_ref, gf_ref, o_ref, m_ref):
    f = pl.program_id(1)

    @pl.when(f == 0)
    def _():
        h = h_ref[...]
        m_ref[...] = _rms(h, g_ref[...]).astype(BF16)
        o_ref[...] = h

    m = m_ref[...]
    act = (_silu(_dot(m, wg_ref[...])) * _dot(m, wu_ref[...])).astype(BF16)
    o_ref[...] += _dot(act, wd_ref[...])

    @pl.when(f == pl.num_programs(1) - 1)
    def _():
        o_ref[...] = _rms(o_ref[...], gf_ref[...])


def _ffn(h, g, w_gate, w_up, w_down, g_final, tm):
    m = h.shape[0]
    return pl.pallas_call(
        _ffn_kernel,
        grid=(m // tm, D_FF // FFN_TF),
        in_specs=[
            pl.BlockSpec((tm, D_MODEL), lambda i, f: (i, 0)),
            pl.BlockSpec((1, D_MODEL), lambda i, f: (0, 0)),
            pl.BlockSpec((D_MODEL, FFN_TF), lambda i, f: (0, f)),
            pl.BlockSpec((D_MODEL, FFN_TF), lambda i, f: (0, f)),
            pl.BlockSpec((FFN_TF, D_MODEL), lambda i, f: (f, 0)),
            pl.BlockSpec((1, D_MODEL), lambda i, f: (0, 0)),
        ],
        out_specs=pl.BlockSpec((tm, D_MODEL), lambda i, f: (i, 0)),
        out_shape=jax.ShapeDtypeStruct((m, D_MODEL), F32),
        scratch_shapes=[pltpu.VMEM((tm, D_MODEL), BF16)],
        compiler_params=_params("arbitrary", "arbitrary"),
        name="ffn",
    )(h, g, w_gate, w_up, w_down, g_final)


def _row(v, width=None):
    v = v.astype(F32).reshape(1, -1)
    if width is not None and v.shape[1] < width:
        v = jnp.pad(v, ((0, 0), (0, width - v.shape[1])))
    return v


def kernel(x_prompt, x_sample, state_conv, state_ssm, norm_mix_g, w_in, chunk_v_norm_g, chunk_w_s,
           chunk_b_s, chunk_out_norm_g, ssd_conv_w, ssd_conv_b, ssd_dt_bias, ssd_a_log, ssd_d,
           ssd_norm_g, w_out, norm_ffn_g, w_gate, w_up, w_down, norm_final_g):
    depth = w_in.shape[0]
    batch, seq, _ = x_prompt.shape
    n_s = x_sample.shape[0]
    assert x_sample.shape[1] == 1 and seq % SSD_CHUNK == 0 and depth == 1

    rep = (lax.broadcasted_iota(jnp.int32, (HEAD_PAD, D_SSM), 0)
           == lax.broadcasted_iota(jnp.int32, (HEAD_PAD, D_SSM), 1) // SSM_HEAD_DIM).astype(F32)
    rept = rep.T

    hp = x_prompt.reshape(batch * seq, D_MODEL)
    hs = x_sample.reshape(n_s, D_MODEL)
    l = 0
    w_in_t = jnp.swapaxes(w_in, 1, 2)
    w_dt_t = jnp.pad(w_in_t[l, D_MAIN:, :], ((0, HEAD_PAD - SSM_HEADS), (0, 0))).astype(BF16)
    g_mix = _row(norm_mix_g[l])
    gv = _row(chunk_v_norm_g[l])
    go = _row(chunk_out_norm_g[l])
    cw = ssd_conv_w[l].astype(F32)
    cb = _row(ssd_conv_b[l])
    dtb = _row(ssd_dt_bias[l], HEAD_PAD)
    alog = _row(ssd_a_log[l], HEAD_PAD)
    drep = _row(jnp.repeat(ssd_d[l], SSM_HEAD_DIM))
    ng = _row(ssd_norm_g[l])
    g_ffn = _row(norm_ffn_g[l])
    g_final = _row(norm_final_g)

    proj_p, dt_p = _in_proj(hp, g_mix, w_in_t, l, w_dt_t, tm=1024)
    ya_p = _chunk_gate(proj_p, gv, chunk_w_s[l].astype(F32), chunk_b_s[l].astype(F32).T, go)
    yb_p, conv_p, ssm_p, w_gate_b, w_up_b, w_down_b = _ssd_prompt(
        proj_p, dt_p, cw, cb, dtb, alog, drep, ng, rep, rept, w_gate, w_up, w_down, l, batch, seq)
    h_p = _out_proj(hp, ya_p, yb_p, w_out, l, tm=1024)
    y_p = _ffn(h_p, g_ffn, w_gate_b, w_up_b, w_down_b, g_final, tm=512)

    proj_s, dt_s = _in_proj(hs, g_mix, w_in_t, l, w_dt_t, tm=n_s)
    w0 = _row(jnp.repeat(chunk_w_s[l, :, 0, 0], CHUNK_HEAD_DIM))
    b0 = _row(jnp.repeat(chunk_b_s[l, :, 0], CHUNK_HEAD_DIM))
    ya_s, v_s, conv_s, split, b_s, c_s, dx_s = _sample_rows(
        proj_s, dt_s, state_conv[l].reshape(n_s, (CONV_W - 1) * CONV_DIM), gv, w0, b0, go,
        cw, cb, dtb, alog, drep, rep)
    ssm_s, yb_s = _sample_state(split, b_s, c_s, state_ssm[l].reshape(n_s, D_SSM, D_STATE),
                                dx_s, proj_s, ng)
    h_s = _out_proj(hs, ya_s, yb_s, w_out, l, tm=n_s)
    y_s = _ffn(h_s, g_ffn, w_gate_b, w_up_b, w_down_b, g_final, tm=n_s)

    return (
        y_p.reshape(batch, seq, D_MODEL),
        y_s.reshape(n_s, 1, D_MODEL),
        conv_p[None],
        ssm_p.reshape(1, batch, SSM_HEADS, SSM_HEAD_DIM, D_STATE),
        conv_s.reshape(1, n_s, CONV_W - 1, CONV_DIM),
        ssm_s.reshape(1, n_s, SSM_HEADS, SSM_HEAD_DIM, D_STATE),
        v_s.reshape(1, n_s, 1, D_CHUNK),
    )
```

```python
import functools

import jax
import jax.numpy as jnp
from jax import lax
from jax.experimental import pallas as pl
from jax.experimental.pallas import tpu as pltpu

F32 = jnp.float32
BF16 = jnp.bfloat16
HIGHEST = lax.Precision.HIGHEST

D_MODEL = 2048
D_CHUNK = 1024
CHUNK_HEADS = 8
CHUNK_HEAD_DIM = 128
CHUNK_LEN = 128
D_SSM = 1024
SSM_HEAD_DIM = 64
SSM_HEADS = 16
SSM_GROUPS = 2
HEADS_PER_GROUP = SSM_HEADS // SSM_GROUPS
GROUP_WIDTH = D_SSM // SSM_GROUPS
D_STATE = 128
CONV_W = 4
CONV_DIM = D_SSM + 2 * SSM_GROUPS * D_STATE
SSD_CHUNK = 128
D_MAIN = 2 * D_CHUNK + D_SSM + CONV_DIM
D_FF = 5632
EPS = 1e-6

LANES = 128
HEAD_PAD = LANES
VMEM_LIMIT = 56 * 1024 * 1024

U_BLK, V_BLK, Z_BLK = 0, 1, 2
XBC_BLK = 2


def _rms(x, g):
    return x * lax.rsqrt(jnp.mean(x * x, axis=-1, keepdims=True) + EPS) * g


def _silu(x):
    return x * jax.nn.sigmoid(x)


def _softplus(x):
    return jnp.maximum(x, 0.0) + jnp.log1p(jnp.exp(-jnp.abs(x)))


def _dot(a, b):
    return jnp.dot(a, b, preferred_element_type=F32)


def _dot_nt(a, b):
    return lax.dot_general(a, b, (((1,), (1,)), ((), ())), preferred_element_type=F32)


def _dot_exact(a, b):
    return jnp.dot(a, b, precision=HIGHEST, preferred_element_type=F32)


def _params(*semantics):
    return pltpu.CompilerParams(dimension_semantics=semantics, vmem_limit_bytes=VMEM_LIMIT)


IN_TN = 512


def _in_proj_kernel(x_ref, g_ref, w_ref, wdt_ref, o_ref, dt_ref, xn_ref, wres_ref):
    i = pl.program_id(0)
    j = pl.program_id(1)
    rows = pl.ds(pl.multiple_of(j * IN_TN, IN_TN), IN_TN)

    @pl.when(i == 0)
    def _():
        wres_ref[rows, :] = w_ref[...].astype(BF16)

    @pl.when(j == 0)
    def _():
        nb = _rms(x_ref[...], g_ref[...]).astype(BF16)
        xn_ref[...] = nb
        dt_ref[...] = _dot_nt(nb, wdt_ref[...])

    o_ref[...] = _dot_nt(xn_ref[...], wres_ref[rows, :])


def _in_proj(x, g, w_in_t, layer, w_dt_t, tm):
    m = x.shape[0]
    n_j = D_MAIN // IN_TN
    return pl.pallas_call(
        _in_proj_kernel,
        grid=(m // tm, n_j),
        in_specs=[
            pl.BlockSpec((tm, D_MODEL), lambda i, j: (i, 0)),
            pl.BlockSpec((1, D_MODEL), lambda i, j: (0, 0)),
            pl.BlockSpec((None, IN_TN, D_MODEL), lambda i, j: (layer, jnp.where(i == 0, j, n_j - 1), 0)),
            pl.BlockSpec((HEAD_PAD, D_MODEL), lambda i, j: (0, 0)),
        ],
        out_specs=[
            pl.BlockSpec((tm, IN_TN), lambda i, j: (i, j)),
            pl.BlockSpec((tm, HEAD_PAD), lambda i, j: (i, 0)),
        ],
        out_shape=[
            jax.ShapeDtypeStruct((m, D_MAIN), F32),
            jax.ShapeDtypeStruct((m, HEAD_PAD), F32),
        ],
        scratch_shapes=[pltpu.VMEM((tm, D_MODEL), BF16), pltpu.VMEM((D_MAIN, D_MODEL), BF16)],
        compiler_params=_params("arbitrary", "arbitrary"),
        name="in_proj",
    )(x, g, w_in_t, w_dt_t)


def _chunk_gate_kernel(u_ref, v_ref, gv_ref, ws_ref, bst_ref, go_ref, o_ref):
    vb = _rms(jax.nn.gelu(v_ref[...]), gv_ref[...]).astype(BF16)
    u = jax.nn.gelu(u_ref[...])
    row = lax.broadcasted_iota(jnp.int32, (CHUNK_LEN, CHUNK_LEN), 0)
    col = lax.broadcasted_iota(jnp.int32, (CHUNK_LEN, CHUNK_LEN), 1)
    causal = row >= col
    parts = []
    for h in range(CHUNK_HEADS):
        sl = slice(h * CHUNK_HEAD_DIM, (h + 1) * CHUNK_HEAD_DIM)
        w = jnp.where(causal, ws_ref[h], 0.0).astype(BF16)
        mixed = _dot(w, vb[:, sl]) + bst_ref[:, h:h + 1]
        parts.append(u[:, sl] * mixed)
    y = jnp.concatenate(parts, axis=1)
    o_ref[...] = _rms(y, go_ref[...]).astype(BF16)


def _chunk_gate(proj, gv, ws, bst, go):
    m = proj.shape[0]
    return pl.pallas_call(
        _chunk_gate_kernel,
        grid=(m // CHUNK_LEN,),
        in_specs=[
            pl.BlockSpec((CHUNK_LEN, D_CHUNK), lambda c: (c, U_BLK)),
            pl.BlockSpec((CHUNK_LEN, D_CHUNK), lambda c: (c, V_BLK)),
            pl.BlockSpec((1, D_CHUNK), lambda c: (0, 0)),
            pl.BlockSpec((CHUNK_HEADS, CHUNK_LEN, CHUNK_LEN), lambda c: (0, 0, 0)),
            pl.BlockSpec((CHUNK_LEN, CHUNK_HEADS), lambda c: (0, 0)),
            pl.BlockSpec((1, D_CHUNK), lambda c: (0, 0)),
        ],
        out_specs=pl.BlockSpec((CHUNK_LEN, D_CHUNK), lambda c: (c, 0)),
        out_shape=jax.ShapeDtypeStruct((m, D_CHUNK), BF16),
        compiler_params=_params("arbitrary"),
        name="chunk_gate",
    )(proj, proj, gv, ws, bst, go)


CONV_PAD = 8


def _group_norm_out(y, z, g):
    y = y * _silu(z)
    parts = []
    for grp in range(SSM_GROUPS):
        yg = y[:, grp * GROUP_WIDTH:(grp + 1) * GROUP_WIDTH]
        parts.append(yg * lax.rsqrt(jnp.mean(yg * yg, axis=-1, keepdims=True) + EPS))
    return (jnp.concatenate(parts, axis=1) * g).astype(BF16)


def _ssd_kernel(xbc_ref, z_ref, dt_ref, cw_ref, cb_ref, dtb_ref, alog_ref, drep_ref, ng_ref,
                rep_ref, rept_ref, wg_ref, wu_ref, wd_ref,
                y_ref, nconv_ref, nssm_ref, wgb_ref, wub_ref, wdb_ref, ext_ref, h_ref):
    q = SSD_CHUNK
    c = pl.program_id(1)

    wgb_ref[...] = wg_ref[...].astype(BF16)
    wub_ref[...] = wu_ref[...].astype(BF16)
    wdb_ref[...] = wd_ref[...].astype(BF16)

    @pl.when(c == 0)
    def _():
        ext_ref[0:CONV_PAD, :] = jnp.zeros((CONV_PAD, CONV_DIM), F32)
        h_ref[...] = jnp.zeros_like(h_ref)

    xbc = xbc_ref[...]
    ext_ref[CONV_PAD:CONV_PAD + q, :] = xbc
    conv = cb_ref[...]
    for k in range(CONV_W):
        lo = CONV_PAD - (CONV_W - 1) + k
        conv = conv + ext_ref[lo:lo + q, :] * cw_ref[k:k + 1, :]
    ext_ref[0:CONV_PAD, :] = xbc[q - CONV_PAD:, :]
    nconv_ref[...] = xbc[q - (CONV_W - 1):, :]

    act = _silu(conv)
    xs = act[:, :D_SSM]
    bmat = act[:, D_SSM:D_SSM + SSM_GROUPS * D_STATE]
    cmat = act[:, D_SSM + SSM_GROUPS * D_STATE:]

    dt = _softplus(dt_ref[...] + dtb_ref[...])
    a = -jnp.exp(alog_ref[...])
    row = lax.broadcasted_iota(jnp.int32, (q, q), 0)
    col = lax.broadcasted_iota(jnp.int32, (q, q), 1)
    causal = row >= col
    acum = _dot_exact(causal.astype(F32), dt * a)
    acum_t = acum.T
    rep = rep_ref[...]
    acum_rep = _dot_exact(acum, rep)
    xdt = xs * _dot_exact(dt, rep)
    xw = xdt * jnp.exp(acum_rep[q - 1:q, :] - acum_rep)
    y = drep_ref[...] * xs
    exp_a = jnp.exp(acum_rep)

    ydiag, yoff, states = [], [], []
    for grp in range(SSM_GROUPS):
        bg = bmat[:, grp * D_STATE:(grp + 1) * D_STATE].astype(BF16)
        cg = cmat[:, grp * D_STATE:(grp + 1) * D_STATE].astype(BF16)
        cb = _dot_nt(cg, bg)
        ch = slice(grp * GROUP_WIDTH, (grp + 1) * GROUP_WIDTH)
        yoff.append(_dot_nt(cg, h_ref[ch, :].astype(BF16)))
        for e in range(HEADS_PER_GROUP):
            h = grp * HEADS_PER_GROUP + e
            diff = acum[:, h:h + 1] - acum_t[h:h + 1, :]
            decay = jnp.exp(jnp.where(causal, diff, -jnp.inf))
            xh = xdt[:, h * SSM_HEAD_DIM:(h + 1) * SSM_HEAD_DIM].astype(BF16)
            ydiag.append(_dot((cb * decay).astype(BF16), xh))
        states.append(_dot(xw[:, ch].T.astype(BF16), bg))
    y = y + jnp.concatenate(ydiag, axis=1) + jnp.concatenate(yoff, axis=1) * exp_a

    a_last = jnp.broadcast_to(acum_t[:, q - 1:q], (HEAD_PAD, D_STATE))
    h_new = h_ref[...] * jnp.exp(_dot_exact(rept_ref[...], a_last)) + jnp.concatenate(states, axis=0)
    h_ref[...] = h_new
    nssm_ref[...] = h_new
    y_ref[...] = _group_norm_out(y, z_ref[...], ng_ref[...])


BF16_SUBLANES = 16


def _ssd_prompt(proj, dt_raw, cw, cb, dtb, alog, drep, ng, rep, rept, w_gate, w_up, w_down, layer,
                batch, seq):
    n_c = seq // SSD_CHUNK
    m = batch * seq
    n_steps = batch * n_c
    gu_rows = D_MODEL // n_steps
    dn_hold = 2
    dn_rows = D_FF * dn_hold // n_steps
    assert gu_rows * n_steps == D_MODEL and gu_rows % BF16_SUBLANES == 0
    assert dn_rows * n_steps == D_FF * dn_hold and dn_rows % BF16_SUBLANES == 0
    const2 = lambda b, c: (0, 0)
    gu_in = pl.BlockSpec((None, gu_rows, D_FF), lambda b, c: (layer, b * n_c + c, 0))
    gu_out = pl.BlockSpec((gu_rows, D_FF), lambda b, c: (b * n_c + c, 0))
    dn_in = pl.BlockSpec((None, dn_rows, D_MODEL), lambda b, c: (layer, (b * n_c + c) // dn_hold, 0))
    dn_out = pl.BlockSpec((dn_rows, D_MODEL), lambda b, c: ((b * n_c + c) // dn_hold, 0))
    return pl.pallas_call(
        _ssd_kernel,
        grid=(batch, n_c),
        in_specs=[
            pl.BlockSpec((SSD_CHUNK, CONV_DIM), lambda b, c: (b * n_c + c, XBC_BLK)),
            pl.BlockSpec((SSD_CHUNK, D_SSM), lambda b, c: (b * n_c + c, Z_BLK)),
            pl.BlockSpec((SSD_CHUNK, HEAD_PAD), lambda b, c: (b * n_c + c, 0)),
            pl.BlockSpec((CONV_W, CONV_DIM), const2),
            pl.BlockSpec((1, CONV_DIM), const2),
            pl.BlockSpec((1, HEAD_PAD), const2),
            pl.BlockSpec((1, HEAD_PAD), const2),
            pl.BlockSpec((1, D_SSM), const2),
            pl.BlockSpec((1, D_SSM), const2),
            pl.BlockSpec((HEAD_PAD, D_SSM), const2),
            pl.BlockSpec((D_SSM, HEAD_PAD), const2),
            gu_in, gu_in, dn_in,
        ],
        out_specs=[
            pl.BlockSpec((SSD_CHUNK, D_SSM), lambda b, c: (b * n_c + c, 0)),
            pl.BlockSpec((None, CONV_W - 1, CONV_DIM), lambda b, c: (b, 0, 0)),
            pl.BlockSpec((None, D_SSM, D_STATE), lambda b, c: (b, 0, 0)),
            gu_out, gu_out, dn_out,
        ],
        out_shape=[
            jax.ShapeDtypeStruct((m, D_SSM), BF16),
            jax.ShapeDtypeStruct((batch, CONV_W - 1, CONV_DIM), F32),
            jax.ShapeDtypeStruct((batch, D_SSM, D_STATE), F32),
            jax.ShapeDtypeStruct((D_MODEL, D_FF), BF16),
            jax.ShapeDtypeStruct((D_MODEL, D_FF), BF16),
            jax.ShapeDtypeStruct((D_FF, D_MODEL), BF16),
        ],
        scratch_shapes=[
            pltpu.VMEM((CONV_PAD + SSD_CHUNK, CONV_DIM), F32),
            pltpu.VMEM((D_SSM, D_STATE), F32),
        ],
        compiler_params=_params("arbitrary", "arbitrary"),
        name="ssd_prompt",
    )(proj, proj, dt_raw, cw, cb, dtb, alog, drep, ng, rep, rept, w_gate, w_up, w_down)


N_SPLIT = 3


def _sample_rows_kernel(u_ref, v_ref, xbc_ref, dt_ref, sc_ref, gv_ref, w0_ref, b0_ref, go_ref,
                        cw_ref, cb_ref, dtb_ref, alog_ref, drep_ref, rep_ref,
                        ya_ref, vrows_ref, nconv_ref, split_ref, b_ref, c_ref, dx_ref):
    vr = _rms(jax.nn.gelu(v_ref[...]), gv_ref[...])
    vrows_ref[...] = vr
    mixed = vr * w0_ref[...] + b0_ref[...]
    ya_ref[...] = _rms(jax.nn.gelu(u_ref[...]) * mixed, go_ref[...]).astype(BF16)

    xbc = xbc_ref[...]
    conv = cb_ref[...]
    for k in range(CONV_W - 1):
        conv = conv + sc_ref[:, k * CONV_DIM:(k + 1) * CONV_DIM] * cw_ref[k:k + 1, :]
    conv = conv + xbc * cw_ref[CONV_W - 1:CONV_W, :]
    nconv_ref[:, 0:(CONV_W - 2) * CONV_DIM] = sc_ref[:, CONV_DIM:(CONV_W - 1) * CONV_DIM]
    nconv_ref[:, (CONV_W - 2) * CONV_DIM:] = xbc

    act = _silu(conv)
    xs = act[:, :D_SSM]
    b_ref[...] = act[:, D_SSM:D_SSM + SSM_GROUPS * D_STATE]
    c_ref[...] = act[:, D_SSM + SSM_GROUPS * D_STATE:]
    dx_ref[...] = drep_ref[...] * xs

    dt = _softplus(dt_ref[...] + dtb_ref[...])
    decay = jnp.exp(dt * (-jnp.exp(alog_ref[...])))
    rep = rep_ref[...]
    xdt = xs * _dot_exact(dt, rep)
    cols = jnp.concatenate([xdt, _dot_exact(decay, rep)], axis=1).T
    for p in range(N_SPLIT):
        piece = cols.astype(BF16)
        split_ref[:, p * LANES:(p + 1) * LANES] = piece
        cols = cols - piece.astype(F32)


def _sample_rows(proj, dt_raw, sconv, gv, w0, b0, go, cw, cb, dtb, alog, drep, rep):
    n = proj.shape[0]
    full = lambda shape: pl.BlockSpec(shape, lambda i: (0,) * len(shape))
    return pl.pallas_call(
        _sample_rows_kernel,
        grid=(1,),
        in_specs=[
            pl.BlockSpec((n, D_CHUNK), lambda i: (0, U_BLK)),
            pl.BlockSpec((n, D_CHUNK), lambda i: (0, V_BLK)),
            pl.BlockSpec((n, CONV_DIM), lambda i: (0, XBC_BLK)),
            full((n, HEAD_PAD)),
            full((n, (CONV_W - 1) * CONV_DIM)),
            full((1, D_CHUNK)), full((1, D_CHUNK)), full((1, D_CHUNK)), full((1, D_CHUNK)),
            full((CONV_W, CONV_DIM)), full((1, CONV_DIM)),
            full((1, HEAD_PAD)), full((1, HEAD_PAD)), full((1, D_SSM)),
            full((HEAD_PAD, D_SSM)),
        ],
        out_specs=[
            full((n, D_CHUNK)), full((n, D_CHUNK)), full((n, (CONV_W - 1) * CONV_DIM)),
            full((2 * D_SSM, N_SPLIT * LANES)),
            full((n, SSM_GROUPS * D_STATE)), full((n, SSM_GROUPS * D_STATE)), full((n, D_SSM)),
        ],
        out_shape=[
            jax.ShapeDtypeStruct((n, D_CHUNK), BF16),
            jax.ShapeDtypeStruct((n, D_CHUNK), F32),
            jax.ShapeDtypeStruct((n, (CONV_W - 1) * CONV_DIM), F32),
            jax.ShapeDtypeStruct((2 * D_SSM, N_SPLIT * LANES), BF16),
            jax.ShapeDtypeStruct((n, SSM_GROUPS * D_STATE), F32),
            jax.ShapeDtypeStruct((n, SSM_GROUPS * D_STATE), F32),
            jax.ShapeDtypeStruct((n, D_SSM), F32),
        ],
        compiler_params=_params("arbitrary"),
        name="sample_rows",
    )(proj, proj, proj, dt_raw, sconv, gv, w0, b0, go, cw, cb, dtb, alog, drep, rep)


SAMPLE_TB = 8


def _sample_state_kernel(split_ref, b_ref, c_ref, h0_ref, dx_ref, z_ref, ng_ref,
                         hn_ref, y_ref, yt_ref):
    i = pl.program_id(0)
    n_tok = b_ref.shape[0]

    @pl.when(i == 0)
    def _():
        yt_ref[...] = jnp.zeros_like(yt_ref)

    tok = lax.broadcasted_iota(jnp.int32, (N_SPLIT * LANES, D_STATE), 0) % LANES
    lane = lax.broadcasted_iota(jnp.int32, (D_SSM, n_tok), 1)

    def per_group(row, grp):
        return jnp.broadcast_to(row[:, grp * D_STATE:(grp + 1) * D_STATE], (GROUP_WIDTH, D_STATE))

    def body(k, carry):
        t = i * SAMPLE_TB + k
        onehot = (tok == t).astype(BF16)
        bcast = _dot(split_ref[...], onehot)
        brow = b_ref[pl.ds(t, 1), :]
        crow = c_ref[pl.ds(t, 1), :]
        bfull = jnp.concatenate([per_group(brow, g) for g in range(SSM_GROUPS)], axis=0)
        cfull = jnp.concatenate([per_group(crow, g) for g in range(SSM_GROUPS)], axis=0)
        h_new = h0_ref[k] * bcast[D_SSM:, :] + bcast[:D_SSM, :] * bfull
        hn_ref[k] = h_new
        ysum = jnp.sum(h_new * cfull, axis=-1, keepdims=True)
        yt_ref[...] = jnp.where(lane == t, ysum, yt_ref[...])
        return carry

    lax.fori_loop(0, SAMPLE_TB, body, 0)

    @pl.when(i == pl.num_programs(0) - 1)
    def _():
        y = yt_ref[...].T + dx_ref[...]
        y_ref[...] = _group_norm_out(y, z_ref[...], ng_ref[...])


def _sample_state(split, bmat, cmat, h0, dx, proj, ng):
    n = h0.shape[0]
    const2 = lambda i: (0, 0)
    return pl.pallas_call(
        _sample_state_kernel,
        grid=(n // SAMPLE_TB,),
        in_specs=[
            pl.BlockSpec((2 * D_SSM, N_SPLIT * LANES), const2),
            pl.BlockSpec((n, SSM_GROUPS * D_STATE), const2),
            pl.BlockSpec((n, SSM_GROUPS * D_STATE), const2),
            pl.BlockSpec((SAMPLE_TB, D_SSM, D_STATE), lambda i: (i, 0, 0)),
            pl.BlockSpec((n, D_SSM), const2),
            pl.BlockSpec((n, D_SSM), lambda i: (0, Z_BLK)),
            pl.BlockSpec((1, D_SSM), const2),
        ],
        out_specs=[
            pl.BlockSpec((SAMPLE_TB, D_SSM, D_STATE), lambda i: (i, 0, 0)),
            pl.BlockSpec((n, D_SSM), const2),
        ],
        out_shape=[
            jax.ShapeDtypeStruct((n, D_SSM, D_STATE), F32),
            jax.ShapeDtypeStruct((n, D_SSM), BF16),
        ],
        scratch_shapes=[pltpu.VMEM((D_SSM, n), F32)],
        compiler_params=_params("arbitrary"),
        name="sample_state",
    )(split, bmat, cmat, h0, dx, proj, ng)


OUT_TN = 1024


def _out_proj_kernel(x_ref, ya_ref, yb_ref, w_ref, o_ref, wres_ref):
    i = pl.program_id(0)
    j = pl.program_id(1)

    @pl.when(i == 0)
    def _():
        wres_ref[j] = w_ref[...].astype(BF16)

    y = jnp.concatenate([ya_ref[...], yb_ref[...]], axis=1)
    o_ref[...] = x_ref[...] + _dot(y, wres_ref[j])


def _out_proj(x, ya, yb, w_out, layer, tm):
    m = x.shape[0]
    n_j = D_MODEL // OUT_TN
    return pl.pallas_call(
        _out_proj_kernel,
        grid=(m // tm, n_j),
        in_specs=[
            pl.BlockSpec((tm, OUT_TN), lambda i, j: (i, j)),
            pl.BlockSpec((tm, D_CHUNK), lambda i, j: (i, 0)),
            pl.BlockSpec((tm, D_SSM), lambda i, j: (i, 0)),
            pl.BlockSpec((None, D_MODEL, OUT_TN), lambda i, j: (layer, 0, jnp.where(i == 0, j, n_j - 1))),
        ],
        out_specs=pl.BlockSpec((tm, OUT_TN), lambda i, j: (i, j)),
        out_shape=jax.ShapeDtypeStruct((m, D_MODEL), F32),
        scratch_shapes=[pltpu.VMEM((n_j, D_MODEL, OUT_TN), BF16)],
        compiler_params=_params("arbitrary", "arbitrary"),
        name="out_proj",
    )(x, ya, yb, w_out)


FFN_TF = 512


def _ffn_kernel(h_ref, g_ref, wg_ref, wu_ref, wd_ref, gf_ref, o_ref, m_ref):
    f = pl.program_id(1)

    @pl.when(f == 0)
    def _():
        h = h_ref[...]
        m_ref[...] = _rms(h, g_ref[...]).astype(BF16)
        o_ref[...] = h

    m = m_ref[...]
    act = (_silu(_dot(m, wg_ref[...])) * _dot(m, wu_ref[...])).astype(BF16)
    o_ref[...] += _dot(act, wd_ref[...])

    @pl.when(f == pl.num_programs(1) - 1)
    def _():
        o_ref[...] = _rms(o_ref[...], gf_ref[...])


def _ffn(h, g, w_gate, w_up, w_down, g_final, tm):
    m = h.shape[0]
    return pl.pallas_call(
        _ffn_kernel,
        grid=(m // tm, D_FF // FFN_TF),
        in_specs=[
            pl.BlockSpec((tm, D_MODEL), lambda i, f: (i, 0)),
            pl.BlockSpec((1, D_MODEL), lambda i, f: (0, 0)),
            pl.BlockSpec((D_MODEL, FFN_TF), lambda i, f: (0, f)),
            pl.BlockSpec((D_MODEL, FFN_TF), lambda i, f: (0, f)),
            pl.BlockSpec((FFN_TF, D_MODEL), lambda i, f: (f, 0)),
            pl.BlockSpec((1, D_MODEL), lambda i, f: (0, 0)),
        ],
        out_specs=pl.BlockSpec((tm, D_MODEL), lambda i, f: (i, 0)),
        out_shape=jax.ShapeDtypeStruct((m, D_MODEL), F32),
        scratch_shapes=[pltpu.VMEM((tm, D_MODEL), BF16)],
        compiler_params=_params("arbitrary", "arbitrary"),
        name="ffn",
    )(h, g, w_gate, w_up, w_down, g_final)


def _row(v, width=None):
    v = v.astype(F32).reshape(1, -1)
    if width is not None and v.shape[1] < width:
        v = jnp.pad(v, ((0, 0), (0, width - v.shape[1])))
    return v


def kernel(x_prompt, x_sample, state_conv, state_ssm, norm_mix_g, w_in, chunk_v_norm_g, chunk_w_s,
           chunk_b_s, chunk_out_norm_g, ssd_conv_w, ssd_conv_b, ssd_dt_bias, ssd_a_log, ssd_d,
           ssd_norm_g, w_out, norm_ffn_g, w_gate, w_up, w_down, norm_final_g):
    depth = w_in.shape[0]
    batch, seq, _ = x_prompt.shape
    n_s = x_sample.shape[0]
    assert x_sample.shape[1] == 1 and seq % SSD_CHUNK == 0 and depth == 1

    rep = (lax.broadcasted_iota(jnp.int32, (HEAD_PAD, D_SSM), 0)
           == lax.broadcasted_iota(jnp.int32, (HEAD_PAD, D_SSM), 1) // SSM_HEAD_DIM).astype(F32)
    rept = rep.T

    hp = x_prompt.reshape(batch * seq, D_MODEL)
    hs = x_sample.reshape(n_s, D_MODEL)
    l = 0
    w_in_t = jnp.swapaxes(w_in, 1, 2)
    w_dt_t = jnp.pad(w_in_t[l, D_MAIN:, :], ((0, HEAD_PAD - SSM_HEADS), (0, 0))).astype(BF16)
    g_mix = _row(norm_mix_g[l])
    gv = _row(chunk_v_norm_g[l])
    go = _row(chunk_out_norm_g[l])
    cw = ssd_conv_w[l].astype(F32)
    cb = _row(ssd_conv_b[l])
    dtb = _row(ssd_dt_bias[l], HEAD_PAD)
    alog = _row(ssd_a_log[l], HEAD_PAD)
    drep = _row(jnp.repeat(ssd_d[l], SSM_HEAD_DIM))
    ng = _row(ssd_norm_g[l])
    g_ffn = _row(norm_ffn_g[l])
    g_final = _row(norm_final_g)

    proj_p, dt_p = _in_proj(hp, g_mix, w_in_t, l, w_dt_t, tm=1024)
    ya_p = _chunk_gate(proj_p, gv, chunk_w_s[l].astype(F32), chunk_b_s[l].astype(F32).T, go)
    yb_p, conv_p, ssm_p, w_gate_b, w_up_b, w_down_b = _ssd_prompt(
        proj_p, dt_p, cw, cb, dtb, alog, drep, ng, rep, rept, w_gate, w_up, w_down, l, batch, seq)
    h_p = _out_proj(hp, ya_p, yb_p, w_out, l, tm=1024)
    y_p = _ffn(h_p, g_ffn, w_gate_b, w_up_b, w_down_b, g_final, tm=512)

    proj_s, dt_s = _in_proj(hs, g_mix, w_in_t, l, w_dt_t, tm=n_s)
    w0 = _row(jnp.repeat(chunk_w_s[l, :, 0, 0], CHUNK_HEAD_DIM))
    b0 = _row(jnp.repeat(chunk_b_s[l, :, 0], CHUNK_HEAD_DIM))
    ya_s, v_s, conv_s, split, b_s, c_s, dx_s = _sample_rows(
        proj_s, dt_s, state_conv[l].reshape(n_s, (CONV_W - 1) * CONV_DIM), gv, w0, b0, go,
        cw, cb, dtb, alog, drep, rep)
    ssm_s, yb_s = _sample_state(split, b_s, c_s, state_ssm[l].reshape(n_s, D_SSM, D_STATE),
                                dx_s, proj_s, ng)
    h_s = _out_proj(hs, ya_s, yb_s, w_out, l, tm=n_s)
    y_s = _ffn(h_s, g_ffn, w_gate_b, w_up_b, w_down_b, g_final, tm=n_s)

    return (
        y_p.reshape(batch, seq, D_MODEL),
        y_s.reshape(n_s, 1, D_MODEL),
        conv_p[None],
        ssm_p.reshape(1, batch, SSM_HEADS, SSM_HEAD_DIM, D_STATE),
        conv_s.reshape(1, n_s, CONV_W - 1, CONV_DIM),
        ssm_s.reshape(1, n_s, SSM_HEADS, SSM_HEAD_DIM, D_STATE),
        v_s.reshape(1, n_s, 1, D_CHUNK),
    )
```

```python
import functools

import jax
import jax.numpy as jnp
from jax import lax
from jax.experimental import pallas as pl
from jax.experimental.pallas import tpu as pltpu

F32 = jnp.float32
BF16 = jnp.bfloat16
HIGHEST = lax.Precision.HIGHEST

D_MODEL = 2048
D_CHUNK = 1024
CHUNK_HEADS = 8
CHUNK_HEAD_DIM = 128
CHUNK_LEN = 128
D_SSM = 1024
SSM_HEAD_DIM = 64
SSM_HEADS = 16
SSM_GROUPS = 2
HEADS_PER_GROUP = SSM_HEADS // SSM_GROUPS
GROUP_WIDTH = D_SSM // SSM_GROUPS
D_STATE = 128
CONV_W = 4
CONV_DIM = D_SSM + 2 * SSM_GROUPS * D_STATE
SSD_CHUNK = 128
D_MAIN = 2 * D_CHUNK + D_SSM + CONV_DIM
D_FF = 5632
EPS = 1e-6

LANES = 128
HEAD_PAD = LANES
N_SPLIT = 3
VMEM_LIMIT = 56 * 1024 * 1024

U_BLK, V_BLK, Z_BLK = 0, 1, 2
XBC_BLK = 2


def _rms(x, g):
    return x * lax.rsqrt(jnp.mean(x * x, axis=-1, keepdims=True) + EPS) * g


def _silu(x):
    return x * jax.nn.sigmoid(x)


def _softplus(x):
    return jnp.maximum(x, 0.0) + jnp.log1p(jnp.exp(-jnp.abs(x)))


def _dot(a, b):
    return jnp.dot(a, b, preferred_element_type=F32)


def _dot_nt(a, b):
    return lax.dot_general(a, b, (((1,), (1,)), ((), ())), preferred_element_type=F32)


def _dot_exact(a, b):
    return jnp.dot(a, b, precision=HIGHEST, preferred_element_type=F32)


def _params(*semantics):
    return pltpu.CompilerParams(dimension_semantics=semantics, vmem_limit_bytes=VMEM_LIMIT)


IN_TN = 512


def _in_proj_kernel(x_ref, g_ref, w_ref, wdt_ref, o_ref, dt_ref, xn_ref, wres_ref):
    i = pl.program_id(0)
    j = pl.program_id(1)
    rows = pl.ds(pl.multiple_of(j * IN_TN, IN_TN), IN_TN)

    @pl.when(i == 0)
    def _():
        wres_ref[rows, :] = w_ref[...].astype(BF16)

    @pl.when(j == 0)
    def _():
        nb = _rms(x_ref[...], g_ref[...]).astype(BF16)
        xn_ref[...] = nb
        dt_ref[...] = _dot_nt(nb, wdt_ref[...])

    o_ref[...] = _dot_nt(xn_ref[...], wres_ref[rows, :])


def _in_proj(x, g, w_in_t, layer, w_dt_t, tm):
    m = x.shape[0]
    n_j = D_MAIN // IN_TN
    return pl.pallas_call(
        _in_proj_kernel,
        grid=(m // tm, n_j),
        in_specs=[
            pl.BlockSpec((tm, D_MODEL), lambda i, j: (i, 0)),
            pl.BlockSpec((1, D_MODEL), lambda i, j: (0, 0)),
            pl.BlockSpec((None, IN_TN, D_MODEL), lambda i, j: (layer, jnp.where(i == 0, j, n_j - 1), 0)),
            pl.BlockSpec((HEAD_PAD, D_MODEL), lambda i, j: (0, 0)),
        ],
        out_specs=[
            pl.BlockSpec((tm, IN_TN), lambda i, j: (i, j)),
            pl.BlockSpec((tm, HEAD_PAD), lambda i, j: (i, 0)),
        ],
        out_shape=[
            jax.ShapeDtypeStruct((m, D_MAIN), F32),
            jax.ShapeDtypeStruct((m, HEAD_PAD), F32),
        ],
        scratch_shapes=[pltpu.VMEM((tm, D_MODEL), BF16), pltpu.VMEM((D_MAIN, D_MODEL), BF16)],
        compiler_params=_params("arbitrary", "arbitrary"),
        name="in_proj",
    )(x, g, w_in_t, w_dt_t)


def _chunk_gate_kernel(u_ref, v_ref, gv_ref, ws_ref, bst_ref, go_ref, o_ref):
    vb = _rms(jax.nn.gelu(v_ref[...]), gv_ref[...]).astype(BF16)
    u = jax.nn.gelu(u_ref[...])
    row = lax.broadcasted_iota(jnp.int32, (CHUNK_LEN, CHUNK_LEN), 0)
    col = lax.broadcasted_iota(jnp.int32, (CHUNK_LEN, CHUNK_LEN), 1)
    causal = row >= col
    parts = []
    for h in range(CHUNK_HEADS):
        sl = slice(h * CHUNK_HEAD_DIM, (h + 1) * CHUNK_HEAD_DIM)
        w = jnp.where(causal, ws_ref[h], 0.0).astype(BF16)
        mixed = _dot(w, vb[:, sl]) + bst_ref[:, h:h + 1]
        parts.append(u[:, sl] * mixed)
    y = jnp.concatenate(parts, axis=1)
    o_ref[...] = _rms(y, go_ref[...]).astype(BF16)


def _chunk_gate(proj, gv, ws, bst, go):
    m = proj.shape[0]
    return pl.pallas_call(
        _chunk_gate_kernel,
        grid=(m // CHUNK_LEN,),
        in_specs=[
            pl.BlockSpec((CHUNK_LEN, D_CHUNK), lambda c: (c, U_BLK)),
            pl.BlockSpec((CHUNK_LEN, D_CHUNK), lambda c: (c, V_BLK)),
            pl.BlockSpec((1, D_CHUNK), lambda c: (0, 0)),
            pl.BlockSpec((CHUNK_HEADS, CHUNK_LEN, CHUNK_LEN), lambda c: (0, 0, 0)),
            pl.BlockSpec((CHUNK_LEN, CHUNK_HEADS), lambda c: (0, 0)),
            pl.BlockSpec((1, D_CHUNK), lambda c: (0, 0)),
        ],
        out_specs=pl.BlockSpec((CHUNK_LEN, D_CHUNK), lambda c: (c, 0)),
        out_shape=jax.ShapeDtypeStruct((m, D_CHUNK), BF16),
        compiler_params=_params("arbitrary"),
        name="chunk_gate",
    )(proj, proj, gv, ws, bst, go)


CONV_PAD = 8


def _group_norm_out(y, z, g):
    y = y * _silu(z)
    parts = []
    for grp in range(SSM_GROUPS):
        yg = y[:, grp * GROUP_WIDTH:(grp + 1) * GROUP_WIDTH]
        parts.append(yg * lax.rsqrt(jnp.mean(yg * yg, axis=-1, keepdims=True) + EPS))
    return (jnp.concatenate(parts, axis=1) * g).astype(BF16)


def _split3(x):
    p1 = x.astype(BF16)
    r1 = x - p1.astype(F32)
    p2 = r1.astype(BF16)
    p3 = (r1 - p2.astype(F32)).astype(BF16)
    return p1, p2, p3


def _select_rows(sel3, x):
    return _dot(sel3, jnp.concatenate(_split3(x), axis=0))


def _select_cols(x, sel3):
    return _dot(jnp.concatenate(_split3(x), axis=1), sel3)


def _ssd_kernel(xbc_ref, z_ref, dt_ref, cw_ref, cb_ref, dtb_ref, alog_ref, drep_ref, ng_ref,
                tril3_ref, rep3_ref, wg_ref, wu_ref, wd_ref,
                y_ref, nconv_ref, nssm_ref, wgb_ref, wub_ref, wdb_ref, tail_ref, ht_ref):
    q = SSD_CHUNK
    c = pl.program_id(1)

    wgb_ref[...] = wg_ref[...].astype(BF16)
    wub_ref[...] = wu_ref[...].astype(BF16)
    wdb_ref[...] = wd_ref[...].astype(BF16)

    @pl.when(c == 0)
    def _():
        tail_ref[...] = jnp.zeros_like(tail_ref)
        ht_ref[...] = jnp.zeros_like(ht_ref)

    xbc = xbc_ref[...]
    tail = tail_ref[...]
    sub = lax.broadcasted_iota(jnp.int32, (CONV_PAD, CONV_DIM), 0)
    conv = cb_ref[...] + xbc * cw_ref[CONV_W - 1:CONV_W, :]
    for k in range(1, CONV_W):
        shifted = pltpu.roll(xbc, k, 0)
        head = jnp.where(sub < k, pltpu.roll(tail, k, 0), shifted[0:CONV_PAD, :])
        shifted = jnp.concatenate([head, shifted[CONV_PAD:, :]], axis=0)
        conv = conv + shifted * cw_ref[CONV_W - 1 - k:CONV_W - k, :]
    tail_ref[...] = xbc[q - CONV_PAD:, :]
    nconv_ref[...] = xbc[q - (CONV_W - 1):, :]

    act = _silu(conv)
    xs = act[:, :D_SSM]
    bmat = act[:, D_SSM:D_SSM + SSM_GROUPS * D_STATE]
    cmat = act[:, D_SSM + SSM_GROUPS * D_STATE:]

    dt = _softplus(dt_ref[...] + dtb_ref[...])
    a = -jnp.exp(alog_ref[...])
    acum = _select_rows(tril3_ref[...], dt * a)
    acum_t = acum.T
    heads = jnp.concatenate([dt, jnp.exp(acum), jnp.exp(acum[q - 1:q, :] - acum)], axis=0)
    expanded = _select_cols(heads, rep3_ref[...])
    xdt = xs * expanded[0:q, :]
    exp_a = expanded[q:2 * q, :]
    xw = xdt * expanded[2 * q:3 * q, :]

    row = lax.broadcasted_iota(jnp.int32, (q, q), 0)
    col = lax.broadcasted_iota(jnp.int32, (q, q), 1)
    causal = row >= col
    low_half = col < SSM_HEAD_DIM
    ydiag, yoff, states = [], [], []
    for grp in range(SSM_GROUPS):
        bg_f = bmat[:, grp * D_STATE:(grp + 1) * D_STATE]
        bg = bg_f.astype(BF16)
        cg = cmat[:, grp * D_STATE:(grp + 1) * D_STATE].astype(BF16)
        cb = _dot_nt(cg, bg)
        ch = slice(grp * GROUP_WIDTH, (grp + 1) * GROUP_WIDTH)
        yoff.append(_dot(cg, ht_ref[:, ch].astype(BF16)))
        for pair in range(HEADS_PER_GROUP // 2):
            h0 = grp * HEADS_PER_GROUP + 2 * pair
            masks = []
            for h in (h0, h0 + 1):
                diff = acum[:, h:h + 1] - acum_t[h:h + 1, :]
                masks.append((cb * jnp.exp(jnp.where(causal, diff, -jnp.inf))).astype(BF16))
            xp = xdt[:, h0 * SSM_HEAD_DIM:(h0 + 2) * SSM_HEAD_DIM]
            rhs = jnp.concatenate([jnp.where(low_half, xp, 0.0), jnp.where(low_half, 0.0, xp)], axis=0)
            ydiag.append(_dot(jnp.concatenate(masks, axis=1), rhs.astype(BF16)))
        states.append(_dot(bg_f.T.astype(BF16), xw[:, ch].astype(BF16)))
    y = drep_ref[...] * xs + jnp.concatenate(ydiag, axis=1) + jnp.concatenate(yoff, axis=1) * exp_a

    ht_new = ht_ref[...] * exp_a[q - 1:q, :] + jnp.concatenate(states, axis=1)
    ht_ref[...] = ht_new

    @pl.when(c == pl.num_programs(1) - 1)
    def _():
        nssm_ref[...] = ht_new.T

    y_ref[...] = _group_norm_out(y, z_ref[...], ng_ref[...])


BF16_SUBLANES = 16


def _ssd_prompt(proj, dt_raw, cw, cb, dtb, alog, drep, ng, tril3, rep3, w_gate, w_up, w_down, layer,
                batch, seq):
    n_c = seq // SSD_CHUNK
    m = batch * seq
    n_steps = batch * n_c
    gu_rows = D_MODEL // n_steps
    dn_hold = 2
    dn_rows = D_FF * dn_hold // n_steps
    assert gu_rows * n_steps == D_MODEL and gu_rows % BF16_SUBLANES == 0
    assert dn_rows * n_steps == D_FF * dn_hold and dn_rows % BF16_SUBLANES == 0
    const2 = lambda b, c: (0, 0)
    gu_in = pl.BlockSpec((None, gu_rows, D_FF), lambda b, c: (layer, b * n_c + c, 0))
    gu_out = pl.BlockSpec((gu_rows, D_FF), lambda b, c: (b * n_c + c, 0))
    dn_in = pl.BlockSpec((None, dn_rows, D_MODEL), lambda b, c: (layer, (b * n_c + c) // dn_hold, 0))
    dn_out = pl.BlockSpec((dn_rows, D_MODEL), lambda b, c: ((b * n_c + c) // dn_hold, 0))
    return pl.pallas_call(
        _ssd_kernel,
        grid=(batch, n_c),
        in_specs=[
            pl.BlockSpec((SSD_CHUNK, CONV_DIM), lambda b, c: (b * n_c + c, XBC_BLK)),
            pl.BlockSpec((SSD_CHUNK, D_SSM), lambda b, c: (b * n_c + c, Z_BLK)),
            pl.BlockSpec((SSD_CHUNK, HEAD_PAD), lambda b, c: (b * n_c + c, 0)),
            pl.BlockSpec((CONV_W, CONV_DIM), const2),
            pl.BlockSpec((1, CONV_DIM), const2),
            pl.BlockSpec((1, HEAD_PAD), const2),
            pl.BlockSpec((1, HEAD_PAD), const2),
            pl.BlockSpec((1, D_SSM), const2),
            pl.BlockSpec((1, D_SSM), const2),
            pl.BlockSpec((SSD_CHUNK, N_SPLIT * SSD_CHUNK), const2),
            pl.BlockSpec((N_SPLIT * HEAD_PAD, D_SSM), const2),
            gu_in, gu_in, dn_in,
        ],
        out_specs=[
            pl.BlockSpec((SSD_CHUNK, D_SSM), lambda b, c: (b * n_c + c, 0)),
            pl.BlockSpec((None, CONV_W - 1, CONV_DIM), lambda b, c: (b, 0, 0)),
            pl.BlockSpec((None, D_SSM, D_STATE), lambda b, c: (b, 0, 0)),
            gu_out, gu_out, dn_out,
        ],
        out_shape=[
            jax.ShapeDtypeStruct((m, D_SSM), BF16),
            jax.ShapeDtypeStruct((batch, CONV_W - 1, CONV_DIM), F32),
            jax.ShapeDtypeStruct((batch, D_SSM, D_STATE), F32),
            jax.ShapeDtypeStruct((D_MODEL, D_FF), BF16),
            jax.ShapeDtypeStruct((D_MODEL, D_FF), BF16),
            jax.ShapeDtypeStruct((D_FF, D_MODEL), BF16),
        ],
        scratch_shapes=[
            pltpu.VMEM((CONV_PAD, CONV_DIM), F32),
            pltpu.VMEM((D_STATE, D_SSM), F32),
        ],
        compiler_params=_params("arbitrary", "arbitrary"),
        name="ssd_prompt",
    )(proj, proj, dt_raw, cw, cb, dtb, alog, drep, ng, tril3, rep3, w_gate, w_up, w_down)


def _sample_rows_kernel(u_ref, v_ref, xbc_ref, dt_ref, sc_ref, gv_ref, w0_ref, b0_ref, go_ref,
                        cw_ref, cb_ref, dtb_ref, alog_ref, drep_ref, rep_ref,
                        ya_ref, vrows_ref, nconv_ref, split_ref, b_ref, c_ref, dx_ref):
    vr = _rms(jax.nn.gelu(v_ref[...]), gv_ref[...])
    vrows_ref[...] = vr
    mixed = vr * w0_ref[...] + b0_ref[...]
    ya_ref[...] = _rms(jax.nn.gelu(u_ref[...]) * mixed, go_ref[...]).astype(BF16)

    xbc = xbc_ref[...]
    conv = cb_ref[...]
    for k in range(CONV_W - 1):
        conv = conv + sc_ref[:, k * CONV_DIM:(k + 1) * CONV_DIM] * cw_ref[k:k + 1, :]
    conv = conv + xbc * cw_ref[CONV_W - 1:CONV_W, :]
    nconv_ref[:, 0:(CONV_W - 2) * CONV_DIM] = sc_ref[:, CONV_DIM:(CONV_W - 1) * CONV_DIM]
    nconv_ref[:, (CONV_W - 2) * CONV_DIM:] = xbc

    act = _silu(conv)
    xs = act[:, :D_SSM]
    b_ref[...] = act[:, D_SSM:D_SSM + SSM_GROUPS * D_STATE]
    c_ref[...] = act[:, D_SSM + SSM_GROUPS * D_STATE:]
    dx_ref[...] = drep_ref[...] * xs

    dt = _softplus(dt_ref[...] + dtb_ref[...])
    decay = jnp.exp(dt * (-jnp.exp(alog_ref[...])))
    rep = rep_ref[...]
    xdt = xs * _dot_exact(dt, rep)
    cols = jnp.concatenate([xdt, _dot_exact(decay, rep)], axis=1).T
    for p in range(N_SPLIT):
        piece = cols.astype(BF16)
        split_ref[:, p * LANES:(p + 1) * LANES] = piece
        cols = cols - piece.astype(F32)


def _sample_rows(proj, dt_raw, sconv, gv, w0, b0, go, cw, cb, dtb, alog, drep, rep):
    n = proj.shape[0]
    full = lambda shape: pl.BlockSpec(shape, lambda i: (0,) * len(shape))
    return pl.pallas_call(
        _sample_rows_kernel,
        grid=(1,),
        in_specs=[
            pl.BlockSpec((n, D_CHUNK), lambda i: (0, U_BLK)),
            pl.BlockSpec((n, D_CHUNK), lambda i: (0, V_BLK)),
            pl.BlockSpec((n, CONV_DIM), lambda i: (0, XBC_BLK)),
            full((n, HEAD_PAD)),
            full((n, (CONV_W - 1) * CONV_DIM)),
            full((1, D_CHUNK)), full((1, D_CHUNK)), full((1, D_CHUNK)), full((1, D_CHUNK)),
            full((CONV_W, CONV_DIM)), full((1, CONV_DIM)),
            full((1, HEAD_PAD)), full((1, HEAD_PAD)), full((1, D_SSM)),
            full((HEAD_PAD, D_SSM)),
        ],
        out_specs=[
            full((n, D_CHUNK)), full((n, D_CHUNK)), full((n, (CONV_W - 1) * CONV_DIM)),
            full((2 * D_SSM, N_SPLIT * LANES)),
            full((n, SSM_GROUPS * D_STATE)), full((n, SSM_GROUPS * D_STATE)), full((n, D_SSM)),
        ],
        out_shape=[
            jax.ShapeDtypeStruct((n, D_CHUNK), BF16),
            jax.ShapeDtypeStruct((n, D_CHUNK), F32),
            jax.ShapeDtypeStruct((n, (CONV_W - 1) * CONV_DIM), F32),
            jax.ShapeDtypeStruct((2 * D_SSM, N_SPLIT * LANES), BF16),
            jax.ShapeDtypeStruct((n, SSM_GROUPS * D_STATE), F32),
            jax.ShapeDtypeStruct((n, SSM_GROUPS * D_STATE), F32),
            jax.ShapeDtypeStruct((n, D_SSM), F32),
        ],
        compiler_params=_params("arbitrary"),
        name="sample_rows",
    )(proj, proj, proj, dt_raw, sconv, gv, w0, b0, go, cw, cb, dtb, alog, drep, rep)


SAMPLE_TB = 8


def _sample_state_kernel(split_ref, b_ref, c_ref, h0_ref, dx_ref, z_ref, ng_ref,
                         hn_ref, y_ref, yt_ref):
    i = pl.program_id(0)
    n_tok = b_ref.shape[0]

    @pl.when(i == 0)
    def _():
        yt_ref[...] = jnp.zeros_like(yt_ref)

    tok = lax.broadcasted_iota(jnp.int32, (N_SPLIT * LANES, D_STATE), 0) % LANES
    lane = lax.broadcasted_iota(jnp.int32, (D_SSM, n_tok), 1)

    def per_group(row, grp):
        return jnp.broadcast_to(row[:, grp * D_STATE:(grp + 1) * D_STATE], (GROUP_WIDTH, D_STATE))

    def body(k, carry):
        t = i * SAMPLE_TB + k
        onehot = (tok == t).astype(BF16)
        bcast = _dot(split_ref[...], onehot)
        brow = b_ref[pl.ds(t, 1), :]
        crow = c_ref[pl.ds(t, 1), :]
        bfull = jnp.concatenate([per_group(brow, g) for g in range(SSM_GROUPS)], axis=0)
        cfull = jnp.concatenate([per_group(crow, g) for g in range(SSM_GROUPS)], axis=0)
        h_new = h0_ref[k] * bcast[D_SSM:, :] + bcast[:D_SSM, :] * bfull
        hn_ref[k] = h_new
        ysum = jnp.sum(h_new * cfull, axis=-1, keepdims=True)
        yt_ref[...] = jnp.where(lane == t, ysum, yt_ref[...])
        return carry

    lax.fori_loop(0, SAMPLE_TB, body, 0)

    @pl.when(i == pl.num_programs(0) - 1)
    def _():
        y = yt_ref[...].T + dx_ref[...]
        y_ref[...] = _group_norm_out(y, z_ref[...], ng_ref[...])


def _sample_state(split, bmat, cmat, h0, dx, proj, ng):
    n = h0.shape[0]
    const2 = lambda i: (0, 0)
    return pl.pallas_call(
        _sample_state_kernel,
        grid=(n // SAMPLE_TB,),
        in_specs=[
            pl.BlockSpec((2 * D_SSM, N_SPLIT * LANES), const2),
            pl.BlockSpec((n, SSM_GROUPS * D_STATE), const2),
            pl.BlockSpec((n, SSM_GROUPS * D_STATE), const2),
            pl.BlockSpec((SAMPLE_TB, D_SSM, D_STATE), lambda i: (i, 0, 0)),
            pl.BlockSpec((n, D_SSM), const2),
            pl.BlockSpec((n, D_SSM), lambda i: (0, Z_BLK)),
            pl.BlockSpec((1, D_SSM), const2),
        ],
        out_specs=[
            pl.BlockSpec((SAMPLE_TB, D_SSM, D_STATE), lambda i: (i, 0, 0)),
            pl.BlockSpec((n, D_SSM), const2),
        ],
        out_shape=[
            jax.ShapeDtypeStruct((n, D_SSM, D_STATE), F32),
            jax.ShapeDtypeStruct((n, D_SSM), BF16),
        ],
        scratch_shapes=[pltpu.VMEM((D_SSM, n), F32)],
        compiler_params=_params("arbitrary"),
        name="sample_state",
    )(split, bmat, cmat, h0, dx, proj, ng)


OUT_TN = 1024


def _out_proj_kernel(x_ref, ya_ref, yb_ref, w_ref, o_ref, wres_ref):
    i = pl.program_id(0)
    j = pl.program_id(1)

    @pl.when(i == 0)
    def _():
        wres_ref[j] = w_ref[...].astype(BF16)

    y = jnp.concatenate([ya_ref[...], yb_ref[...]], axis=1)
    o_ref[...] = x_ref[...] + _dot(y, wres_ref[j])


def _out_proj(x, ya, yb, w_out, layer, tm):
    m = x.shape[0]
    n_j = D_MODEL // OUT_TN
    return pl.pallas_call(
        _out_proj_kernel,
        grid=(m // tm, n_j),
        in_specs=[
            pl.BlockSpec((tm, OUT_TN), lambda i, j: (i, j)),
            pl.BlockSpec((tm, D_CHUNK), lambda i, j: (i, 0)),
            pl.BlockSpec((tm, D_SSM), lambda i, j: (i, 0)),
            pl.BlockSpec((None, D_MODEL, OUT_TN), lambda i, j: (layer, 0, jnp.where(i == 0, j, n_j - 1))),
        ],
        out_specs=pl.BlockSpec((tm, OUT_TN), lambda i, j: (i, j)),
        out_shape=jax.ShapeDtypeStruct((m, D_MODEL), F32),
        scratch_shapes=[pltpu.VMEM((n_j, D_MODEL, OUT_TN), BF16)],
        compiler_params=_params("arbitrary", "arbitrary"),
        name="out_proj",
    )(x, ya, yb, w_out)


FFN_TF = 512


def _ffn_kernel(h_ref, g_ref, wg_ref, wu_ref, wd_ref, gf_ref, o_ref, m_ref):
    f = pl.program_id(1)

    @pl.when(f == 0)
    def _():
        h = h_ref[...]
        m_ref[...] = _rms(h, g_ref[...]).astype(BF16)
        o_ref[...] = h

    m = m_ref[...]
    act = (_silu(_dot(m, wg_ref[...])) * _dot(m, wu_ref[...])).astype(BF16)
    o_ref[...] += _dot(act, wd_ref[...])

    @pl.when(f == pl.num_programs(1) - 1)
    def _():
        o_ref[...] = _rms(o_ref[...], gf_ref[...])


def _ffn(h, g, w_gate, w_up, w_down, g_final, tm):
    m = h.shape[0]
    return pl.pallas_call(
        _ffn_kernel,
        grid=(m // tm, D_FF // FFN_TF),
        in_specs=[
            pl.BlockSpec((tm, D_MODEL), lambda i, f: (i, 0)),
            pl.BlockSpec((1, D_MODEL), lambda i, f: (0, 0)),
            pl.BlockSpec((D_MODEL, FFN_TF), lambda i, f: (0, f)),
            pl.BlockSpec((D_MODEL, FFN_TF), lambda i, f: (0, f)),
            pl.BlockSpec((FFN_TF, D_MODEL), lambda i, f: (f, 0)),
            pl.BlockSpec((1, D_MODEL), lambda i, f: (0, 0)),
        ],
        out_specs=pl.BlockSpec((tm, D_MODEL), lambda i, f: (i, 0)),
        out_shape=jax.ShapeDtypeStruct((m, D_MODEL), F32),
        scratch_shapes=[pltpu.VMEM((tm, D_MODEL), BF16)],
        compiler_params=_params("arbitrary", "arbitrary"),
        name="ffn",
    )(h, g, w_gate, w_up, w_down, g_final)


def _row(v, width=None):
    v = v.astype(F32).reshape(1, -1)
    if width is not None and v.shape[1] < width:
        v = jnp.pad(v, ((0, 0), (0, width - v.shape[1])))
    return v


def kernel(x_prompt, x_sample, state_conv, state_ssm, norm_mix_g, w_in, chunk_v_norm_g, chunk_w_s,
           chunk_b_s, chunk_out_norm_g, ssd_conv_w, ssd_conv_b, ssd_dt_bias, ssd_a_log, ssd_d,
           ssd_norm_g, w_out, norm_ffn_g, w_gate, w_up, w_down, norm_final_g):
    depth = w_in.shape[0]
    batch, seq, _ = x_prompt.shape
    n_s = x_sample.shape[0]
    assert x_sample.shape[1] == 1 and seq % SSD_CHUNK == 0 and depth == 1

    rep = (lax.broadcasted_iota(jnp.int32, (HEAD_PAD, D_SSM), 0)
           == lax.broadcasted_iota(jnp.int32, (HEAD_PAD, D_SSM), 1) // SSM_HEAD_DIM).astype(F32)
    rep3 = jnp.tile(rep, (N_SPLIT, 1)).astype(BF16)
    tril3 = jnp.tile(jnp.tril(jnp.ones((SSD_CHUNK, SSD_CHUNK), F32)), (1, N_SPLIT)).astype(BF16)

    hp = x_prompt.reshape(batch * seq, D_MODEL)
    hs = x_sample.reshape(n_s, D_MODEL)
    l = 0
    w_in_t = jnp.swapaxes(w_in, 1, 2)
    w_dt_t = jnp.pad(w_in_t[l, D_MAIN:, :], ((0, HEAD_PAD - SSM_HEADS), (0, 0))).astype(BF16)
    g_mix = _row(norm_mix_g[l])
    gv = _row(chunk_v_norm_g[l])
    go = _row(chunk_out_norm_g[l])
    cw = ssd_conv_w[l].astype(F32)
    cb = _row(ssd_conv_b[l])
    dtb = _row(ssd_dt_bias[l], HEAD_PAD)
    alog = _row(ssd_a_log[l], HEAD_PAD)
    drep = _row(jnp.repeat(ssd_d[l], SSM_HEAD_DIM))
    ng = _row(ssd_norm_g[l])
    g_ffn = _row(norm_ffn_g[l])
    g_final = _row(norm_final_g)

    proj_p, dt_p = _in_proj(hp, g_mix, w_in_t, l, w_dt_t, tm=1024)
    ya_p = _chunk_gate(proj_p, gv, chunk_w_s[l].astype(F32), chunk_b_s[l].astype(F32).T, go)
    yb_p, conv_p, ssm_p, w_gate_b, w_up_b, w_down_b = _ssd_prompt(
        proj_p, dt_p, cw, cb, dtb, alog, drep, ng, tril3, rep3, w_gate, w_up, w_down, l, batch, seq)
    h_p = _out_proj(hp, ya_p, yb_p, w_out, l, tm=1024)
    y_p = _ffn(h_p, g_ffn, w_gate_b, w_up_b, w_down_b, g_final, tm=512)

    proj_s, dt_s = _in_proj(hs, g_mix, w_in_t, l, w_dt_t, tm=n_s)
    w0 = _row(jnp.repeat(chunk_w_s[l, :, 0, 0], CHUNK_HEAD_DIM))
    b0 = _row(jnp.repeat(chunk_b_s[l, :, 0], CHUNK_HEAD_DIM))
    ya_s, v_s, conv_s, split, b_s, c_s, dx_s = _sample_rows(
        proj_s, dt_s, state_conv[l].reshape(n_s, (CONV_W - 1) * CONV_DIM), gv, w0, b0, go,
        cw, cb, dtb, alog, drep, rep)
    ssm_s, yb_s = _sample_state(split, b_s, c_s, state_ssm[l].reshape(n_s, D_SSM, D_STATE),
                                dx_s, proj_s, ng)
    h_s = _out_proj(hs, ya_s, yb_s, w_out, l, tm=n_s)
    y_s = _ffn(h_s, g_ffn, w_gate_b, w_up_b, w_down_b, g_final, tm=n_s)

    return (
        y_p.reshape(batch, seq, D_MODEL),
        y_s.reshape(n_s, 1, D_MODEL),
        conv_p[None],
        ssm_p.reshape(1, batch, SSM_HEADS, SSM_HEAD_DIM, D_STATE),
        conv_s.reshape(1, n_s, CONV_W - 1, CONV_DIM),
        ssm_s.reshape(1, n_s, SSM_HEADS, SSM_HEAD_DIM, D_STATE),
        v_s.reshape(1, n_s, 1, D_CHUNK),
    )
```

```python
import functools

import jax
import jax.numpy as jnp
from jax import lax
from jax.experimental import pallas as pl
from jax.experimental.pallas import tpu as pltpu

F32 = jnp.float32
BF16 = jnp.bfloat16
HIGHEST = lax.Precision.HIGHEST

D_MODEL = 2048
D_CHUNK = 1024
CHUNK_HEADS = 8
CHUNK_HEAD_DIM = 128
CHUNK_LEN = 128
D_SSM = 1024
SSM_HEAD_DIM = 64
SSM_HEADS = 16
SSM_GROUPS = 2
HEADS_PER_GROUP = SSM_HEADS // SSM_GROUPS
GROUP_WIDTH = D_SSM // SSM_GROUPS
D_STATE = 128
CONV_W = 4
CONV_DIM = D_SSM + 2 * SSM_GROUPS * D_STATE
SSD_CHUNK = 128
D_MAIN = 2 * D_CHUNK + D_SSM + CONV_DIM
D_FF = 5632
EPS = 1e-6

LANES = 128
SUBLANES = 8
HEAD_PAD = LANES
N_SPLIT = 3
VMEM_LIMIT = 56 * 1024 * 1024

U_BLK, V_BLK, Z_BLK = 0, 1, 2
XBC_BLK = 2


def _rms(x, g):
    return x * lax.rsqrt(jnp.mean(x * x, axis=-1, keepdims=True) + EPS) * g


def _silu(x):
    return x * jax.nn.sigmoid(x)


def _softplus(x):
    return jnp.maximum(x, 0.0) + jnp.log1p(jnp.exp(-jnp.abs(x)))


def _dot(a, b):
    return jnp.dot(a, b, preferred_element_type=F32)


def _dot_nt(a, b):
    return lax.dot_general(a, b, (((1,), (1,)), ((), ())), preferred_element_type=F32)


def _dot_exact(a, b):
    return jnp.dot(a, b, precision=HIGHEST, preferred_element_type=F32)


def _params(*semantics):
    return pltpu.CompilerParams(dimension_semantics=semantics, vmem_limit_bytes=VMEM_LIMIT)


IN_TN = 512


def _in_proj_kernel(x_ref, g_ref, w_ref, wdt_ref, o_ref, dt_ref, xn_ref, wres_ref):
    i = pl.program_id(0)
    j = pl.program_id(1)
    rows = pl.ds(pl.multiple_of(j * IN_TN, IN_TN), IN_TN)

    @pl.when(i == 0)
    def _():
        wres_ref[rows, :] = w_ref[...].astype(BF16)

    @pl.when(j == 0)
    def _():
        nb = _rms(x_ref[...], g_ref[...]).astype(BF16)
        xn_ref[...] = nb
        dt_ref[...] = _dot_nt(nb, wdt_ref[...])

    o_ref[...] = _dot_nt(xn_ref[...], wres_ref[rows, :])


def _in_proj(x, g, w_in_t, layer, w_dt_t, tm):
    m = x.shape[0]
    n_j = D_MAIN // IN_TN
    return pl.pallas_call(
        _in_proj_kernel,
        grid=(m // tm, n_j),
        in_specs=[
            pl.BlockSpec((tm, D_MODEL), lambda i, j: (i, 0)),
            pl.BlockSpec((1, D_MODEL), lambda i, j: (0, 0)),
            pl.BlockSpec((None, IN_TN, D_MODEL), lambda i, j: (layer, jnp.where(i == 0, j, n_j - 1), 0)),
            pl.BlockSpec((HEAD_PAD, D_MODEL), lambda i, j: (0, 0)),
        ],
        out_specs=[
            pl.BlockSpec((tm, IN_TN), lambda i, j: (i, j)),
            pl.BlockSpec((tm, HEAD_PAD), lambda i, j: (i, 0)),
        ],
        out_shape=[
            jax.ShapeDtypeStruct((m, D_MAIN), F32),
            jax.ShapeDtypeStruct((m, HEAD_PAD), F32),
        ],
        scratch_shapes=[pltpu.VMEM((tm, D_MODEL), BF16), pltpu.VMEM((D_MAIN, D_MODEL), BF16)],
        compiler_params=_params("arbitrary", "arbitrary"),
        name="in_proj",
    )(x, g, w_in_t, w_dt_t)


def _chunk_gate_kernel(u_ref, v_ref, gv_ref, ws_ref, bst_ref, go_ref, o_ref):
    vb = _rms(jax.nn.gelu(v_ref[...]), gv_ref[...]).astype(BF16)
    u = jax.nn.gelu(u_ref[...])
    row = lax.broadcasted_iota(jnp.int32, (CHUNK_LEN, CHUNK_LEN), 0)
    col = lax.broadcasted_iota(jnp.int32, (CHUNK_LEN, CHUNK_LEN), 1)
    causal = row >= col
    parts = []
    for h in range(CHUNK_HEADS):
        sl = slice(h * CHUNK_HEAD_DIM, (h + 1) * CHUNK_HEAD_DIM)
        w = jnp.where(causal, ws_ref[h], 0.0).astype(BF16)
        mixed = _dot(w, vb[:, sl]) + bst_ref[:, h:h + 1]
        parts.append(u[:, sl] * mixed)
    y = jnp.concatenate(parts, axis=1)
    o_ref[...] = _rms(y, go_ref[...]).astype(BF16)


def _chunk_gate(proj, gv, ws, bst, go):
    m = proj.shape[0]
    return pl.pallas_call(
        _chunk_gate_kernel,
        grid=(m // CHUNK_LEN,),
        in_specs=[
            pl.BlockSpec((CHUNK_LEN, D_CHUNK), lambda c: (c, U_BLK)),
            pl.BlockSpec((CHUNK_LEN, D_CHUNK), lambda c: (c, V_BLK)),
            pl.BlockSpec((1, D_CHUNK), lambda c: (0, 0)),
            pl.BlockSpec((CHUNK_HEADS, CHUNK_LEN, CHUNK_LEN), lambda c: (0, 0, 0)),
            pl.BlockSpec((CHUNK_LEN, CHUNK_HEADS), lambda c: (0, 0)),
            pl.BlockSpec((1, D_CHUNK), lambda c: (0, 0)),
        ],
        out_specs=pl.BlockSpec((CHUNK_LEN, D_CHUNK), lambda c: (c, 0)),
        out_shape=jax.ShapeDtypeStruct((m, D_CHUNK), BF16),
        compiler_params=_params("arbitrary"),
        name="chunk_gate",
    )(proj, proj, gv, ws, bst, go)


CONV_PAD = 8


def _group_norm_out(y, z, g):
    y = y * _silu(z)
    parts = []
    for grp in range(SSM_GROUPS):
        yg = y[:, grp * GROUP_WIDTH:(grp + 1) * GROUP_WIDTH]
        parts.append(yg * lax.rsqrt(jnp.mean(yg * yg, axis=-1, keepdims=True) + EPS))
    return (jnp.concatenate(parts, axis=1) * g).astype(BF16)


def _split3(x):
    p1 = x.astype(BF16)
    r1 = x - p1.astype(F32)
    p2 = r1.astype(BF16)
    p3 = (r1 - p2.astype(F32)).astype(BF16)
    return p1, p2, p3


def _select_rows(sel3, x):
    return _dot(sel3, jnp.concatenate(_split3(x), axis=0))


def _select_cols(x, sel3):
    return _dot(jnp.concatenate(_split3(x), axis=1), sel3)


def _ssd_kernel(xbc_ref, z_ref, dt_ref, cw_ref, cb_ref, dtb_ref, alog_ref, drep_ref, ng_ref,
                tril3_ref, rep3_ref, wg_ref, wu_ref, wd_ref,
                y_ref, nconv_ref, nssm_ref, wgb_ref, wub_ref, wdb_ref, tail_ref, ht_ref):
    q = SSD_CHUNK
    c = pl.program_id(1)

    wgb_ref[...] = wg_ref[...].astype(BF16)
    wub_ref[...] = wu_ref[...].astype(BF16)
    wdb_ref[...] = wd_ref[...].astype(BF16)

    @pl.when(c == 0)
    def _():
        tail_ref[...] = jnp.zeros_like(tail_ref)
        ht_ref[...] = jnp.zeros_like(ht_ref)

    xbc = xbc_ref[...]
    tail = tail_ref[...]
    sub = lax.broadcasted_iota(jnp.int32, (CONV_PAD, CONV_DIM), 0)
    conv = cb_ref[...] + xbc * cw_ref[CONV_W - 1:CONV_W, :]
    for k in range(1, CONV_W):
        shifted = pltpu.roll(xbc, k, 0)
        head = jnp.where(sub < k, pltpu.roll(tail, k, 0), shifted[0:CONV_PAD, :])
        shifted = jnp.concatenate([head, shifted[CONV_PAD:, :]], axis=0)
        conv = conv + shifted * cw_ref[CONV_W - 1 - k:CONV_W - k, :]
    tail_ref[...] = xbc[q - CONV_PAD:, :]
    nconv_ref[...] = xbc[q - (CONV_W - 1):, :]

    act = _silu(conv)
    xs = act[:, :D_SSM]
    bmat = act[:, D_SSM:D_SSM + SSM_GROUPS * D_STATE]
    cmat = act[:, D_SSM + SSM_GROUPS * D_STATE:]

    dt = _softplus(dt_ref[...] + dtb_ref[...])
    a = -jnp.exp(alog_ref[...])
    acum = _select_rows(tril3_ref[...], dt * a)
    acum_t = acum.T
    heads = jnp.concatenate([dt, jnp.exp(acum), jnp.exp(acum[q - 1:q, :] - acum)], axis=0)
    expanded = _select_cols(heads, rep3_ref[...])
    xdt = xs * expanded[0:q, :]
    exp_a = expanded[q:2 * q, :]
    xw = xdt * expanded[2 * q:3 * q, :]

    row = lax.broadcasted_iota(jnp.int32, (q, q), 0)
    col = lax.broadcasted_iota(jnp.int32, (q, q), 1)
    causal = row >= col
    low_half = col < SSM_HEAD_DIM
    ydiag, yoff, states = [], [], []
    for grp in range(SSM_GROUPS):
        bg_f = bmat[:, grp * D_STATE:(grp + 1) * D_STATE]
        bg = bg_f.astype(BF16)
        cg = cmat[:, grp * D_STATE:(grp + 1) * D_STATE].astype(BF16)
        cb = _dot_nt(cg, bg)
        ch = slice(grp * GROUP_WIDTH, (grp + 1) * GROUP_WIDTH)
        yoff.append(_dot(cg, ht_ref[:, ch].astype(BF16)))
        for pair in range(HEADS_PER_GROUP // 2):
            h0 = grp * HEADS_PER_GROUP + 2 * pair
            masks = []
            for h in (h0, h0 + 1):
                diff = acum[:, h:h + 1] - acum_t[h:h + 1, :]
                masks.append((cb * jnp.exp(jnp.where(causal, diff, -jnp.inf))).astype(BF16))
            xp = xdt[:, h0 * SSM_HEAD_DIM:(h0 + 2) * SSM_HEAD_DIM]
            rhs = jnp.concatenate([jnp.where(low_half, xp, 0.0), jnp.where(low_half, 0.0, xp)], axis=0)
            ydiag.append(_dot(jnp.concatenate(masks, axis=1), rhs.astype(BF16)))
        states.append(_dot(bg_f.T.astype(BF16), xw[:, ch].astype(BF16)))
    y = drep_ref[...] * xs + jnp.concatenate(ydiag, axis=1) + jnp.concatenate(yoff, axis=1) * exp_a

    ht_new = ht_ref[...] * exp_a[q - 1:q, :] + jnp.concatenate(states, axis=1)
    ht_ref[...] = ht_new

    @pl.when(c == pl.num_programs(1) - 1)
    def _():
        nssm_ref[...] = ht_new.T

    y_ref[...] = _group_norm_out(y, z_ref[...], ng_ref[...])


BF16_SUBLANES = 16


def _ssd_prompt(proj, dt_raw, cw, cb, dtb, alog, drep, ng, tril3, rep3, w_gate, w_up, w_down, layer,
                batch, seq):
    n_c = seq // SSD_CHUNK
    m = batch * seq
    n_steps = batch * n_c
    gu_rows = D_MODEL // n_steps
    dn_hold = 2
    dn_rows = D_FF * dn_hold // n_steps
    assert gu_rows * n_steps == D_MODEL and gu_rows % BF16_SUBLANES == 0
    assert dn_rows * n_steps == D_FF * dn_hold and dn_rows % BF16_SUBLANES == 0
    const2 = lambda b, c: (0, 0)
    gu_in = pl.BlockSpec((None, gu_rows, D_FF), lambda b, c: (layer, b * n_c + c, 0))
    gu_out = pl.BlockSpec((gu_rows, D_FF), lambda b, c: (b * n_c + c, 0))
    dn_in = pl.BlockSpec((None, dn_rows, D_MODEL), lambda b, c: (layer, (b * n_c + c) // dn_hold, 0))
    dn_out = pl.BlockSpec((dn_rows, D_MODEL), lambda b, c: ((b * n_c + c) // dn_hold, 0))
    return pl.pallas_call(
        _ssd_kernel,
        grid=(batch, n_c),
        in_specs=[
            pl.BlockSpec((SSD_CHUNK, CONV_DIM), lambda b, c: (b * n_c + c, XBC_BLK)),
            pl.BlockSpec((SSD_CHUNK, D_SSM), lambda b, c: (b * n_c + c, Z_BLK)),
            pl.BlockSpec((SSD_CHUNK, HEAD_PAD), lambda b, c: (b * n_c + c, 0)),
            pl.BlockSpec((CONV_W, CONV_DIM), const2),
            pl.BlockSpec((1, CONV_DIM), const2),
            pl.BlockSpec((1, HEAD_PAD), const2),
            pl.BlockSpec((1, HEAD_PAD), const2),
            pl.BlockSpec((1, D_SSM), const2),
            pl.BlockSpec((1, D_SSM), const2),
            pl.BlockSpec((SSD_CHUNK, N_SPLIT * SSD_CHUNK), const2),
            pl.BlockSpec((N_SPLIT * HEAD_PAD, D_SSM), const2),
            gu_in, gu_in, dn_in,
        ],
        out_specs=[
            pl.BlockSpec((SSD_CHUNK, D_SSM), lambda b, c: (b * n_c + c, 0)),
            pl.BlockSpec((None, CONV_W - 1, CONV_DIM), lambda b, c: (b, 0, 0)),
            pl.BlockSpec((None, D_SSM, D_STATE), lambda b, c: (b, 0, 0)),
            gu_out, gu_out, dn_out,
        ],
        out_shape=[
            jax.ShapeDtypeStruct((m, D_SSM), BF16),
            jax.ShapeDtypeStruct((batch, CONV_W - 1, CONV_DIM), F32),
            jax.ShapeDtypeStruct((batch, D_SSM, D_STATE), F32),
            jax.ShapeDtypeStruct((D_MODEL, D_FF), BF16),
            jax.ShapeDtypeStruct((D_MODEL, D_FF), BF16),
            jax.ShapeDtypeStruct((D_FF, D_MODEL), BF16),
        ],
        scratch_shapes=[
            pltpu.VMEM((CONV_PAD, CONV_DIM), F32),
            pltpu.VMEM((D_STATE, D_SSM), F32),
        ],
        compiler_params=_params("arbitrary", "arbitrary"),
        name="ssd_prompt",
    )(proj, proj, dt_raw, cw, cb, dtb, alog, drep, ng, tril3, rep3, w_gate, w_up, w_down)


def _sample_rows_kernel(u_ref, v_ref, xbc_ref, dt_ref, sc_ref, gv_ref, w0_ref, b0_ref, go_ref,
                        cw_ref, cb_ref, dtb_ref, alog_ref, drep_ref, rep_ref, rep8_ref,
                        ya_ref, vrows_ref, nconv_ref, split_ref, b_ref, c_ref, dx_ref):
    vr = _rms(jax.nn.gelu(v_ref[...]), gv_ref[...])
    vrows_ref[...] = vr
    mixed = vr * w0_ref[...] + b0_ref[...]
    ya_ref[...] = _rms(jax.nn.gelu(u_ref[...]) * mixed, go_ref[...]).astype(BF16)

    xbc = xbc_ref[...]
    conv = cb_ref[...]
    for k in range(CONV_W - 1):
        conv = conv + sc_ref[:, k * CONV_DIM:(k + 1) * CONV_DIM] * cw_ref[k:k + 1, :]
    conv = conv + xbc * cw_ref[CONV_W - 1:CONV_W, :]
    nconv_ref[:, 0:(CONV_W - 2) * CONV_DIM] = sc_ref[:, CONV_DIM:(CONV_W - 1) * CONV_DIM]
    nconv_ref[:, (CONV_W - 2) * CONV_DIM:] = xbc

    act = _silu(conv)
    xs = act[:, :D_SSM]
    b_ref[...] = act[:, D_SSM:D_SSM + SSM_GROUPS * D_STATE]
    c_ref[...] = act[:, D_SSM + SSM_GROUPS * D_STATE:]
    dx_ref[...] = drep_ref[...] * xs

    dt = _softplus(dt_ref[...] + dtb_ref[...])
    decay = jnp.exp(dt * (-jnp.exp(alog_ref[...])))
    xdt = xs * _dot_exact(dt, rep_ref[...])
    cols = jnp.concatenate([xdt, _dot_exact(decay, rep8_ref[...])], axis=1).T
    for p in range(N_SPLIT):
        piece = cols.astype(BF16)
        split_ref[:, p * LANES:(p + 1) * LANES] = piece
        cols = cols - piece.astype(F32)


SPLIT_ROWS = D_SSM + HEAD_PAD


def _sample_rows(proj, dt_raw, sconv, gv, w0, b0, go, cw, cb, dtb, alog, drep, rep, rep8):
    n = proj.shape[0]
    full = lambda shape: pl.BlockSpec(shape, lambda i: (0,) * len(shape))
    return pl.pallas_call(
        _sample_rows_kernel,
        grid=(1,),
        in_specs=[
            pl.BlockSpec((n, D_CHUNK), lambda i: (0, U_BLK)),
            pl.BlockSpec((n, D_CHUNK), lambda i: (0, V_BLK)),
            pl.BlockSpec((n, CONV_DIM), lambda i: (0, XBC_BLK)),
            full((n, HEAD_PAD)),
            full((n, (CONV_W - 1) * CONV_DIM)),
            full((1, D_CHUNK)), full((1, D_CHUNK)), full((1, D_CHUNK)), full((1, D_CHUNK)),
            full((CONV_W, CONV_DIM)), full((1, CONV_DIM)),
            full((1, HEAD_PAD)), full((1, HEAD_PAD)), full((1, D_SSM)),
            full((HEAD_PAD, D_SSM)), full((HEAD_PAD, HEAD_PAD)),
        ],
        out_specs=[
            full((n, D_CHUNK)), full((n, D_CHUNK)), full((n, (CONV_W - 1) * CONV_DIM)),
            full((SPLIT_ROWS, N_SPLIT * LANES)),
            full((n, SSM_GROUPS * D_STATE)), full((n, SSM_GROUPS * D_STATE)), full((n, D_SSM)),
        ],
        out_shape=[
            jax.ShapeDtypeStruct((n, D_CHUNK), BF16),
            jax.ShapeDtypeStruct((n, D_CHUNK), F32),
            jax.ShapeDtypeStruct((n, (CONV_W - 1) * CONV_DIM), F32),
            jax.ShapeDtypeStruct((SPLIT_ROWS, N_SPLIT * LANES), BF16),
            jax.ShapeDtypeStruct((n, SSM_GROUPS * D_STATE), F32),
            jax.ShapeDtypeStruct((n, SSM_GROUPS * D_STATE), F32),
            jax.ShapeDtypeStruct((n, D_SSM), F32),
        ],
        compiler_params=_params("arbitrary"),
        name="sample_rows",
    )(proj, proj, proj, dt_raw, sconv, gv, w0, b0, go, cw, cb, dtb, alog, drep, rep, rep8)


SAMPLE_TB = 8


def _sample_state_kernel(split_ref, b_ref, c_ref, h0_ref, dx_ref, z_ref, ng_ref,
                         hn_ref, y_ref, yt_ref):
    i = pl.program_id(0)
    n_tok = b_ref.shape[0]

    @pl.when(i == 0)
    def _():
        yt_ref[...] = jnp.zeros_like(yt_ref)

    tok = lax.broadcasted_iota(jnp.int32, (N_SPLIT * LANES, D_STATE), 0) % LANES
    lane = lax.broadcasted_iota(jnp.int32, (D_SSM, n_tok), 1)

    def per_group(row, grp):
        return jnp.broadcast_to(row[:, grp * D_STATE:(grp + 1) * D_STATE], (GROUP_WIDTH, D_STATE))

    for k in range(SAMPLE_TB):
        t = i * SAMPLE_TB + k
        onehot = (tok == t).astype(BF16)
        bcast = _dot(split_ref[...], onehot)
        decay = jnp.broadcast_to(
            bcast[D_SSM:, :].reshape(SSM_HEADS, 1, SUBLANES, D_STATE),
            (SSM_HEADS, SSM_HEAD_DIM // SUBLANES, SUBLANES, D_STATE)).reshape(D_SSM, D_STATE)
        brow = b_ref[pl.ds(t, 1), :]
        crow = c_ref[pl.ds(t, 1), :]
        bfull = jnp.concatenate([per_group(brow, g) for g in range(SSM_GROUPS)], axis=0)
        cfull = jnp.concatenate([per_group(crow, g) for g in range(SSM_GROUPS)], axis=0)
        h_new = h0_ref[k] * decay + bcast[:D_SSM, :] * bfull
        hn_ref[k] = h_new
        ysum = jnp.sum(h_new * cfull, axis=-1, keepdims=True)
        yt_ref[...] = jnp.where(lane == t, ysum, yt_ref[...])

    @pl.when(i == pl.num_programs(0) - 1)
    def _():
        y = yt_ref[...].T + dx_ref[...]
        y_ref[...] = _group_norm_out(y, z_ref[...], ng_ref[...])


def _sample_state(split, bmat, cmat, h0, dx, proj, ng):
    n = h0.shape[0]
    const2 = lambda i: (0, 0)
    return pl.pallas_call(
        _sample_state_kernel,
        grid=(n // SAMPLE_TB,),
        in_specs=[
            pl.BlockSpec((SPLIT_ROWS, N_SPLIT * LANES), const2),
            pl.BlockSpec((n, SSM_GROUPS * D_STATE), const2),
            pl.BlockSpec((n, SSM_GROUPS * D_STATE), const2),
            pl.BlockSpec((SAMPLE_TB, D_SSM, D_STATE), lambda i: (i, 0, 0)),
            pl.BlockSpec((n, D_SSM), const2),
            pl.BlockSpec((n, D_SSM), lambda i: (0, Z_BLK)),
            pl.BlockSpec((1, D_SSM), const2),
        ],
        out_specs=[
            pl.BlockSpec((SAMPLE_TB, D_SSM, D_STATE), lambda i: (i, 0, 0)),
            pl.BlockSpec((n, D_SSM), const2),
        ],
        out_shape=[
            jax.ShapeDtypeStruct((n, D_SSM, D_STATE), F32),
            jax.ShapeDtypeStruct((n, D_SSM), BF16),
        ],
        scratch_shapes=[pltpu.VMEM((D_SSM, n), F32)],
        compiler_params=_params("arbitrary"),
        name="sample_state",
    )(split, bmat, cmat, h0, dx, proj, ng)


OUT_TN = 1024


def _out_proj_kernel(x_ref, ya_ref, yb_ref, w_ref, o_ref, wres_ref):
    i = pl.program_id(0)
    j = pl.program_id(1)

    @pl.when(i == 0)
    def _():
        wres_ref[j] = w_ref[...].astype(BF16)

    y = jnp.concatenate([ya_ref[...], yb_ref[...]], axis=1)
    o_ref[...] = x_ref[...] + _dot(y, wres_ref[j])


def _out_proj(x, ya, yb, w_out, layer, tm):
    m = x.shape[0]
    n_j = D_MODEL // OUT_TN
    return pl.pallas_call(
        _out_proj_kernel,
        grid=(m // tm, n_j),
        in_specs=[
            pl.BlockSpec((tm, OUT_TN), lambda i, j: (i, j)),
            pl.BlockSpec((tm, D_CHUNK), lambda i, j: (i, 0)),
            pl.BlockSpec((tm, D_SSM), lambda i, j: (i, 0)),
            pl.BlockSpec((None, D_MODEL, OUT_TN), lambda i, j: (layer, 0, jnp.where(i == 0, j, n_j - 1))),
        ],
        out_specs=pl.BlockSpec((tm, OUT_TN), lambda i, j: (i, j)),
        out_shape=jax.ShapeDtypeStruct((m, D_MODEL), F32),
        scratch_shapes=[pltpu.VMEM((n_j, D_MODEL, OUT_TN), BF16)],
        compiler_params=_params("arbitrary", "arbitrary"),
        name="out_proj",
    )(x, ya, yb, w_out)


FFN_TF = 512


def _ffn_kernel(h_ref, g_ref, wg_ref, wu_ref, wd_ref, gf_ref, o_ref, m_ref):
    f = pl.program_id(1)

    @pl.when(f == 0)
    def _():
        h = h_ref[...]
        m_ref[...] = _rms(h, g_ref[...]).astype(BF16)
        o_ref[...] = h

    m = m_ref[...]
    act = (_silu(_dot(m, wg_ref[...])) * _dot(m, wu_ref[...])).astype(BF16)
    o_ref[...] += _dot(act, wd_ref[...])

    @pl.when(f == pl.num_programs(1) - 1)
    def _():
        o_ref[...] = _rms(o_ref[...], gf_ref[...])


def _ffn(h, g, w_gate, w_up, w_down, g_final, tm):
    m = h.shape[0]
    return pl.pallas_call(
        _ffn_kernel,
        grid=(m // tm, D_FF // FFN_TF),
        in_specs=[
            pl.BlockSpec((tm, D_MODEL), lambda i, f: (i, 0)),
            pl.BlockSpec((1, D_MODEL), lambda i, f: (0, 0)),
            pl.BlockSpec((D_MODEL, FFN_TF), lambda i, f: (0, f)),
            pl.BlockSpec((D_MODEL, FFN_TF), lambda i, f: (0, f)),
            pl.BlockSpec((FFN_TF, D_MODEL), lambda i, f: (f, 0)),
            pl.BlockSpec((1, D_MODEL), lambda i, f: (0, 0)),
        ],
        out_specs=pl.BlockSpec((tm, D_MODEL), lambda i, f: (i, 0)),
        out_shape=jax.ShapeDtypeStruct((m, D_MODEL), F32),
        scratch_shapes=[pltpu.VMEM((tm, D_MODEL), BF16)],
        compiler_params=_params("arbitrary", "arbitrary"),
        name="ffn",
    )(h, g, w_gate, w_up, w_down, g_final)


def _row(v, width=None):
    v = v.astype(F32).reshape(1, -1)
    if width is not None and v.shape[1] < width:
        v = jnp.pad(v, ((0, 0), (0, width - v.shape[1])))
    return v


def kernel(x_prompt, x_sample, state_conv, state_ssm, norm_mix_g, w_in, chunk_v_norm_g, chunk_w_s,
           chunk_b_s, chunk_out_norm_g, ssd_conv_w, ssd_conv_b, ssd_dt_bias, ssd_a_log, ssd_d,
           ssd_norm_g, w_out, norm_ffn_g, w_gate, w_up, w_down, norm_final_g):
    depth = w_in.shape[0]
    batch, seq, _ = x_prompt.shape
    n_s = x_sample.shape[0]
    assert x_sample.shape[1] == 1 and seq % SSD_CHUNK == 0 and depth == 1

    rep = (lax.broadcasted_iota(jnp.int32, (HEAD_PAD, D_SSM), 0)
           == lax.broadcasted_iota(jnp.int32, (HEAD_PAD, D_SSM), 1) // SSM_HEAD_DIM).astype(F32)
    rep8 = (lax.broadcasted_iota(jnp.int32, (HEAD_PAD, HEAD_PAD), 0)
            == lax.broadcasted_iota(jnp.int32, (HEAD_PAD, HEAD_PAD), 1) // SUBLANES).astype(F32)
    rep3 = jnp.tile(rep, (N_SPLIT, 1)).astype(BF16)
    tril3 = jnp.tile(jnp.tril(jnp.ones((SSD_CHUNK, SSD_CHUNK), F32)), (1, N_SPLIT)).astype(BF16)

    hp = x_prompt.reshape(batch * seq, D_MODEL)
    hs = x_sample.reshape(n_s, D_MODEL)
    l = 0
    w_in_t = jnp.swapaxes(w_in, 1, 2)
    w_dt_t = jnp.pad(w_in_t[l, D_MAIN:, :], ((0, HEAD_PAD - SSM_HEADS), (0, 0))).astype(BF16)
    g_mix = _row(norm_mix_g[l])
    gv = _row(chunk_v_norm_g[l])
    go = _row(chunk_out_norm_g[l])
    cw = ssd_conv_w[l].astype(F32)
    cb = _row(ssd_conv_b[l])
    dtb = _row(ssd_dt_bias[l], HEAD_PAD)
    alog = _row(ssd_a_log[l], HEAD_PAD)
    drep = _row(jnp.repeat(ssd_d[l], SSM_HEAD_DIM))
    ng = _row(ssd_norm_g[l])
    g_ffn = _row(norm_ffn_g[l])
    g_final = _row(norm_final_g)

    proj_p, dt_p = _in_proj(hp, g_mix, w_in_t, l, w_dt_t, tm=1024)
    ya_p = _chunk_gate(proj_p, gv, chunk_w_s[l].astype(F32), chunk_b_s[l].astype(F32).T, go)
    yb_p, conv_p, ssm_p, w_gate_b, w_up_b, w_down_b = _ssd_prompt(
        proj_p, dt_p, cw, cb, dtb, alog, drep, ng, tril3, rep3, w_gate, w_up, w_down, l, batch, seq)
    h_p = _out_proj(hp, ya_p, yb_p, w_out, l, tm=1024)
    y_p = _ffn(h_p, g_ffn, w_gate_b, w_up_b, w_down_b, g_final, tm=512)

    proj_s, dt_s = _in_proj(hs, g_mix, w_in_t, l, w_dt_t, tm=n_s)
    w0 = _row(jnp.repeat(chunk_w_s[l, :, 0, 0], CHUNK_HEAD_DIM))
    b0 = _row(jnp.repeat(chunk_b_s[l, :, 0], CHUNK_HEAD_DIM))
    ya_s, v_s, conv_s, split, b_s, c_s, dx_s = _sample_rows(
        proj_s, dt_s, state_conv[l].reshape(n_s, (CONV_W - 1) * CONV_DIM), gv, w0, b0, go,
        cw, cb, dtb, alog, drep, rep, rep8)
    ssm_s, yb_s = _sample_state(split, b_s, c_s, state_ssm[l].reshape(n_s, D_SSM, D_STATE),
                                dx_s, proj_s, ng)
    h_s = _out_proj(hs, ya_s, yb_s, w_out, l, tm=n_s)
    y_s = _ffn(h_s, g_ffn, w_gate_b, w_up_b, w_down_b, g_final, tm=n_s)

    return (
        y_p.reshape(batch, seq, D_MODEL),
        y_s.reshape(n_s, 1, D_MODEL),
        conv_p[None],
        ssm_p.reshape(1, batch, SSM_HEADS, SSM_HEAD_DIM, D_STATE),
        conv_s.reshape(1, n_s, CONV_W - 1, CONV_DIM),
        ssm_s.reshape(1, n_s, SSM_HEADS, SSM_HEAD_DIM, D_STATE),
        v_s.reshape(1, n_s, 1, D_CHUNK),
    )
```

```python
import functools

import jax
import jax.numpy as jnp
from jax import lax
from jax.experimental import pallas as pl
from jax.experimental.pallas import tpu as pltpu

F32 = jnp.float32
BF16 = jnp.bfloat16
HIGHEST = lax.Precision.HIGHEST

D_MODEL = 2048
D_CHUNK = 1024
CHUNK_HEADS = 8
CHUNK_HEAD_DIM = 128
CHUNK_LEN = 128
D_SSM = 1024
SSM_HEAD_DIM = 64
SSM_HEADS = 16
SSM_GROUPS = 2
HEADS_PER_GROUP = SSM_HEADS // SSM_GROUPS
GROUP_WIDTH = D_SSM // SSM_GROUPS
D_STATE = 128
CONV_W = 4
CONV_DIM = D_SSM + 2 * SSM_GROUPS * D_STATE
SSD_CHUNK = 128
D_MAIN = 2 * D_CHUNK + D_SSM + CONV_DIM
D_FF = 5632
EPS = 1e-6

LANES = 128
SUBLANES = 8
HEAD_PAD = LANES
N_SPLIT = 3
VMEM_LIMIT = 56 * 1024 * 1024

U_BLK, V_BLK, Z_BLK = 0, 1, 2
XBC_BLK = 2


def _rms(x, g):
    return x * lax.rsqrt(jnp.mean(x * x, axis=-1, keepdims=True) + EPS) * g


def _silu(x):
    return x * jax.nn.sigmoid(x)


def _softplus(x):
    return jnp.maximum(x, 0.0) + jnp.log1p(jnp.exp(-jnp.abs(x)))


def _dot(a, b):
    return jnp.dot(a, b, preferred_element_type=F32)


def _dot_nt(a, b):
    return lax.dot_general(a, b, (((1,), (1,)), ((), ())), preferred_element_type=F32)


def _dot_exact(a, b):
    return jnp.dot(a, b, precision=HIGHEST, preferred_element_type=F32)


def _params(*semantics):
    return pltpu.CompilerParams(dimension_semantics=semantics, vmem_limit_bytes=VMEM_LIMIT)


IN_TN = 512


def _in_proj_kernel(x_ref, g_ref, w_ref, wdt_ref, o_ref, dt_ref, xn_ref, wres_ref):
    i = pl.program_id(0)
    j = pl.program_id(1)
    rows = pl.ds(pl.multiple_of(j * IN_TN, IN_TN), IN_TN)

    @pl.when(i == 0)
    def _():
        wres_ref[rows, :] = w_ref[...].astype(BF16)

    @pl.when(j == 0)
    def _():
        nb = _rms(x_ref[...], g_ref[...]).astype(BF16)
        xn_ref[...] = nb
        dt_ref[...] = _dot_nt(nb, wdt_ref[...])

    o_ref[...] = _dot_nt(xn_ref[...], wres_ref[rows, :])


def _in_proj(x, g, w_in_t, layer, w_dt_t, tm):
    m = x.shape[0]
    n_j = D_MAIN // IN_TN
    return pl.pallas_call(
        _in_proj_kernel,
        grid=(m // tm, n_j),
        in_specs=[
            pl.BlockSpec((tm, D_MODEL), lambda i, j: (i, 0)),
            pl.BlockSpec((1, D_MODEL), lambda i, j: (0, 0)),
            pl.BlockSpec((None, IN_TN, D_MODEL), lambda i, j: (layer, jnp.where(i == 0, j, n_j - 1), 0)),
            pl.BlockSpec((HEAD_PAD, D_MODEL), lambda i, j: (0, 0)),
        ],
        out_specs=[
            pl.BlockSpec((tm, IN_TN), lambda i, j: (i, j)),
            pl.BlockSpec((tm, HEAD_PAD), lambda i, j: (i, 0)),
        ],
        out_shape=[
            jax.ShapeDtypeStruct((m, D_MAIN), F32),
            jax.ShapeDtypeStruct((m, HEAD_PAD), F32),
        ],
        scratch_shapes=[pltpu.VMEM((tm, D_MODEL), BF16), pltpu.VMEM((D_MAIN, D_MODEL), BF16)],
        compiler_params=_params("arbitrary", "arbitrary"),
        name="in_proj",
    )(x, g, w_in_t, w_dt_t)


BF16_SUBLANES = 16


def _chunk_gate_kernel(u_ref, v_ref, gv_ref, ws_ref, bst_ref, go_ref, wg_ref, wu_ref, wd_ref, wo_ref,
                       o_ref, wgb_ref, wub_ref, wdb_ref, wob_ref):
    wgb_ref[...] = wg_ref[...].astype(BF16)
    wub_ref[...] = wu_ref[...].astype(BF16)
    wdb_ref[...] = wd_ref[...].astype(BF16)
    wob_ref[...] = wo_ref[...].astype(BF16)

    vb = _rms(jax.nn.gelu(v_ref[...]), gv_ref[...]).astype(BF16)
    u = jax.nn.gelu(u_ref[...])
    row = lax.broadcasted_iota(jnp.int32, (CHUNK_LEN, CHUNK_LEN), 0)
    col = lax.broadcasted_iota(jnp.int32, (CHUNK_LEN, CHUNK_LEN), 1)
    causal = row >= col
    parts = []
    for h in range(CHUNK_HEADS):
        sl = slice(h * CHUNK_HEAD_DIM, (h + 1) * CHUNK_HEAD_DIM)
        w = jnp.where(causal, ws_ref[h], 0.0).astype(BF16)
        mixed = _dot(w, vb[:, sl]) + bst_ref[:, h:h + 1]
        parts.append(u[:, sl] * mixed)
    y = jnp.concatenate(parts, axis=1)
    o_ref[...] = _rms(y, go_ref[...]).astype(BF16)


def _chunk_gate(proj, gv, ws, bst, go, w_gate, w_up, w_down, w_out, layer):
    m = proj.shape[0]
    n_steps = m // CHUNK_LEN
    rows = D_MODEL // n_steps
    dn_hold = 2
    dn_rows = D_FF * dn_hold // n_steps
    assert rows * n_steps == D_MODEL and rows % BF16_SUBLANES == 0
    assert dn_rows * n_steps == D_FF * dn_hold and dn_rows % BF16_SUBLANES == 0
    gu_in = pl.BlockSpec((None, rows, D_FF), lambda c: (layer, c, 0))
    gu_out = pl.BlockSpec((rows, D_FF), lambda c: (c, 0))
    dn_in = pl.BlockSpec((None, dn_rows, D_MODEL), lambda c: (layer, c // dn_hold, 0))
    dn_out = pl.BlockSpec((dn_rows, D_MODEL), lambda c: (c // dn_hold, 0))
    wo_in = pl.BlockSpec((None, rows, D_MODEL), lambda c: (layer, c, 0))
    wo_out = pl.BlockSpec((rows, D_MODEL), lambda c: (c, 0))
    return pl.pallas_call(
        _chunk_gate_kernel,
        grid=(n_steps,),
        in_specs=[
            pl.BlockSpec((CHUNK_LEN, D_CHUNK), lambda c: (c, U_BLK)),
            pl.BlockSpec((CHUNK_LEN, D_CHUNK), lambda c: (c, V_BLK)),
            pl.BlockSpec((1, D_CHUNK), lambda c: (0, 0)),
            pl.BlockSpec((CHUNK_HEADS, CHUNK_LEN, CHUNK_LEN), lambda c: (0, 0, 0)),
            pl.BlockSpec((CHUNK_LEN, CHUNK_HEADS), lambda c: (0, 0)),
            pl.BlockSpec((1, D_CHUNK), lambda c: (0, 0)),
            gu_in, gu_in, dn_in, wo_in,
        ],
        out_specs=[
            pl.BlockSpec((CHUNK_LEN, D_CHUNK), lambda c: (c, 0)),
            gu_out, gu_out, dn_out, wo_out,
        ],
        out_shape=[
            jax.ShapeDtypeStruct((m, D_CHUNK), BF16),
            jax.ShapeDtypeStruct((D_MODEL, D_FF), BF16),
            jax.ShapeDtypeStruct((D_MODEL, D_FF), BF16),
            jax.ShapeDtypeStruct((D_FF, D_MODEL), BF16),
            jax.ShapeDtypeStruct((D_MODEL, D_MODEL), BF16),
        ],
        compiler_params=_params("arbitrary"),
        name="chunk_gate",
    )(proj, proj, gv, ws, bst, go, w_gate, w_up, w_down, w_out)


CONV_PAD = 8


def _group_norm_out(y, z, g):
    y = y * _silu(z)
    parts = []
    for grp in range(SSM_GROUPS):
        yg = y[:, grp * GROUP_WIDTH:(grp + 1) * GROUP_WIDTH]
        parts.append(yg * lax.rsqrt(jnp.mean(yg * yg, axis=-1, keepdims=True) + EPS))
    return (jnp.concatenate(parts, axis=1) * g).astype(BF16)


def _split3(x):
    p1 = x.astype(BF16)
    r1 = x - p1.astype(F32)
    p2 = r1.astype(BF16)
    p3 = (r1 - p2.astype(F32)).astype(BF16)
    return p1, p2, p3


def _select_rows(sel3, x):
    return _dot(sel3, jnp.concatenate(_split3(x), axis=0))


def _select_cols(x, sel3):
    return _dot(jnp.concatenate(_split3(x), axis=1), sel3)


def _ssd_chunk(first, xbc_ref, z_ref, dt_ref, cw_ref, cb_ref, dtb_ref, alog_ref, drep_ref, ng_ref,
               tril3_ref, rep3_ref, nconv_ref, nssm_ref, tail_ref, ht_ref, store_y):
    q = SSD_CHUNK

    xbc = xbc_ref[...]
    tail = jnp.where(first, 0.0, tail_ref[...])
    ht_prev = jnp.where(first, 0.0, ht_ref[...])
    sub = lax.broadcasted_iota(jnp.int32, (CONV_PAD, CONV_DIM), 0)
    conv = cb_ref[...] + xbc * cw_ref[CONV_W - 1:CONV_W, :]
    for k in range(1, CONV_W):
        shifted = pltpu.roll(xbc, k, 0)
        head = jnp.where(sub < k, pltpu.roll(tail, k, 0), shifted[0:CONV_PAD, :])
        shifted = jnp.concatenate([head, shifted[CONV_PAD:, :]], axis=0)
        conv = conv + shifted * cw_ref[CONV_W - 1 - k:CONV_W - k, :]
    tail_ref[...] = xbc[q - CONV_PAD:, :]
    nconv_ref[...] = xbc[q - (CONV_W - 1):, :]
    yield

    act = _silu(conv)
    xs = act[:, :D_SSM]
    bmat = act[:, D_SSM:D_SSM + SSM_GROUPS * D_STATE]
    cmat = act[:, D_SSM + SSM_GROUPS * D_STATE:]
    yield

    dt = _softplus(dt_ref[...] + dtb_ref[...])
    a = -jnp.exp(alog_ref[...])
    acum = _select_rows(tril3_ref[...], dt * a)
    acum_t = acum.T
    heads = jnp.concatenate([dt, jnp.exp(acum), jnp.exp(acum[q - 1:q, :] - acum)], axis=0)
    expanded = _select_cols(heads, rep3_ref[...])
    xdt = xs * expanded[0:q, :]
    exp_a = expanded[q:2 * q, :]
    xw = xdt * expanded[2 * q:3 * q, :]
    yield

    row = lax.broadcasted_iota(jnp.int32, (q, q), 0)
    col = lax.broadcasted_iota(jnp.int32, (q, q), 1)
    causal = row >= col
    low_half = col < SSM_HEAD_DIM
    ydiag, yoff, states = [], [], []
    for grp in range(SSM_GROUPS):
        bg_f = bmat[:, grp * D_STATE:(grp + 1) * D_STATE]
        bg = bg_f.astype(BF16)
        cg = cmat[:, grp * D_STATE:(grp + 1) * D_STATE].astype(BF16)
        cb = _dot_nt(cg, bg)
        ch = slice(grp * GROUP_WIDTH, (grp + 1) * GROUP_WIDTH)
        yoff.append(_dot(cg, ht_prev[:, ch].astype(BF16)))
        yield
        for pair in range(HEADS_PER_GROUP // 2):
            h0 = grp * HEADS_PER_GROUP + 2 * pair
            masks = []
            for h in (h0, h0 + 1):
                diff = acum[:, h:h + 1] - acum_t[h:h + 1, :]
                masks.append((cb * jnp.exp(jnp.where(causal, diff, -jnp.inf))).astype(BF16))
            xp = xdt[:, h0 * SSM_HEAD_DIM:(h0 + 2) * SSM_HEAD_DIM]
            rhs = jnp.concatenate([jnp.where(low_half, xp, 0.0), jnp.where(low_half, 0.0, xp)], axis=0)
            ydiag.append(_dot(jnp.concatenate(masks, axis=1), rhs.astype(BF16)))
            yield
        states.append(_dot(bg_f.T.astype(BF16), xw[:, ch].astype(BF16)))
        yield
    y = drep_ref[...] * xs + jnp.concatenate(ydiag, axis=1) + jnp.concatenate(yoff, axis=1) * exp_a

    ht_new = ht_prev * exp_a[q - 1:q, :] + jnp.concatenate(states, axis=1)
    ht_ref[...] = ht_new
    nssm_ref[...] = ht_new.T
    yield
    store_y(_group_norm_out(y, z_ref[...], ng_ref[...]))


def _sample_rows_kernel(u_ref, v_ref, xbc_ref, dt_ref, sc_ref, gv_ref, w0_ref, b0_ref, go_ref,
                        cw_ref, cb_ref, dtb_ref, alog_ref, drep_ref, rep_ref, rep8_ref,
                        ya_ref, vrows_ref, nconv_ref, split_ref, b_ref, c_ref, dx_ref):
    vr = _rms(jax.nn.gelu(v_ref[...]), gv_ref[...])
    vrows_ref[...] = vr
    mixed = vr * w0_ref[...] + b0_ref[...]
    ya_ref[...] = _rms(jax.nn.gelu(u_ref[...]) * mixed, go_ref[...]).astype(BF16)

    xbc = xbc_ref[...]
    conv = cb_ref[...]
    for k in range(CONV_W - 1):
        conv = conv + sc_ref[:, k * CONV_DIM:(k + 1) * CONV_DIM] * cw_ref[k:k + 1, :]
    conv = conv + xbc * cw_ref[CONV_W - 1:CONV_W, :]
    nconv_ref[:, 0:(CONV_W - 2) * CONV_DIM] = sc_ref[:, CONV_DIM:(CONV_W - 1) * CONV_DIM]
    nconv_ref[:, (CONV_W - 2) * CONV_DIM:] = xbc

    act = _silu(conv)
    xs = act[:, :D_SSM]
    b_ref[...] = act[:, D_SSM:D_SSM + SSM_GROUPS * D_STATE]
    c_ref[...] = act[:, D_SSM + SSM_GROUPS * D_STATE:]
    dx_ref[...] = drep_ref[...] * xs

    dt = _softplus(dt_ref[...] + dtb_ref[...])
    decay = jnp.exp(dt * (-jnp.exp(alog_ref[...])))
    xdt = xs * _dot_exact(dt, rep_ref[...])
    cols = jnp.concatenate([xdt, _dot_exact(decay, rep8_ref[...])], axis=1).T
    for p in range(N_SPLIT):
        piece = cols.astype(BF16)
        split_ref[:, p * LANES:(p + 1) * LANES] = piece
        cols = cols - piece.astype(F32)


SPLIT_ROWS = D_SSM + HEAD_PAD


def _sample_rows(proj, dt_raw, sconv, gv, w0, b0, go, cw, cb, dtb, alog, drep, rep, rep8):
    n = proj.shape[0]
    full = lambda shape: pl.BlockSpec(shape, lambda i: (0,) * len(shape))
    return pl.pallas_call(
        _sample_rows_kernel,
        grid=(1,),
        in_specs=[
            pl.BlockSpec((n, D_CHUNK), lambda i: (0, U_BLK)),
            pl.BlockSpec((n, D_CHUNK), lambda i: (0, V_BLK)),
            pl.BlockSpec((n, CONV_DIM), lambda i: (0, XBC_BLK)),
            full((n, HEAD_PAD)),
            full((n, (CONV_W - 1) * CONV_DIM)),
            full((1, D_CHUNK)), full((1, D_CHUNK)), full((1, D_CHUNK)), full((1, D_CHUNK)),
            full((CONV_W, CONV_DIM)), full((1, CONV_DIM)),
            full((1, HEAD_PAD)), full((1, HEAD_PAD)), full((1, D_SSM)),
            full((HEAD_PAD, D_SSM)), full((HEAD_PAD, HEAD_PAD)),
        ],
        out_specs=[
            full((n, D_CHUNK)), full((n, D_CHUNK)), full((n, (CONV_W - 1) * CONV_DIM)),
            full((SPLIT_ROWS, N_SPLIT * LANES)),
            full((n, SSM_GROUPS * D_STATE)), full((n, SSM_GROUPS * D_STATE)), full((n, D_SSM)),
        ],
        out_shape=[
            jax.ShapeDtypeStruct((n, D_CHUNK), BF16),
            jax.ShapeDtypeStruct((n, D_CHUNK), F32),
            jax.ShapeDtypeStruct((n, (CONV_W - 1) * CONV_DIM), F32),
            jax.ShapeDtypeStruct((SPLIT_ROWS, N_SPLIT * LANES), BF16),
            jax.ShapeDtypeStruct((n, SSM_GROUPS * D_STATE), F32),
            jax.ShapeDtypeStruct((n, SSM_GROUPS * D_STATE), F32),
            jax.ShapeDtypeStruct((n, D_SSM), F32),
        ],
        compiler_params=_params("arbitrary"),
        name="sample_rows",
    )(proj, proj, proj, dt_raw, sconv, gv, w0, b0, go, cw, cb, dtb, alog, drep, rep, rep8)


SAMPLE_TB = 8


def _sample_state_kernel(split_ref, b_ref, c_ref, h0_ref, dx_ref, z_ref, ng_ref,
                         hn_ref, y_ref, yt_ref):
    i = pl.program_id(0)
    n_tok = b_ref.shape[0]

    @pl.when(i == 0)
    def _():
        yt_ref[...] = jnp.zeros_like(yt_ref)

    tok = lax.broadcasted_iota(jnp.int32, (N_SPLIT * LANES, D_STATE), 0) % LANES
    lane = lax.broadcasted_iota(jnp.int32, (D_SSM, n_tok), 1)

    def per_group(row, grp):
        return jnp.broadcast_to(row[:, grp * D_STATE:(grp + 1) * D_STATE], (GROUP_WIDTH, D_STATE))

    for k in range(SAMPLE_TB):
        t = i * SAMPLE_TB + k
        onehot = (tok == t).astype(BF16)
        bcast = _dot(split_ref[...], onehot)
        decay = jnp.broadcast_to(
            bcast[D_SSM:, :].reshape(SSM_HEADS, 1, SUBLANES, D_STATE),
            (SSM_HEADS, SSM_HEAD_DIM // SUBLANES, SUBLANES, D_STATE)).reshape(D_SSM, D_STATE)
        brow = b_ref[pl.ds(t, 1), :]
        crow = c_ref[pl.ds(t, 1), :]
        bfull = jnp.concatenate([per_group(brow, g) for g in range(SSM_GROUPS)], axis=0)
        cfull = jnp.concatenate([per_group(crow, g) for g in range(SSM_GROUPS)], axis=0)
        h_new = h0_ref[k] * decay + bcast[:D_SSM, :] * bfull
        hn_ref[k] = h_new
        ysum = jnp.sum(h_new * cfull, axis=-1, keepdims=True)
        yt_ref[...] = jnp.where(lane == t, ysum, yt_ref[...])

    @pl.when(i == pl.num_programs(0) - 1)
    def _():
        y = yt_ref[...].T + dx_ref[...]
        y_ref[...] = _group_norm_out(y, z_ref[...], ng_ref[...])


def _sample_state(split, bmat, cmat, h0, dx, proj, ng):
    n = h0.shape[0]
    const2 = lambda i: (0, 0)
    return pl.pallas_call(
        _sample_state_kernel,
        grid=(n // SAMPLE_TB,),
        in_specs=[
            pl.BlockSpec((SPLIT_ROWS, N_SPLIT * LANES), const2),
            pl.BlockSpec((n, SSM_GROUPS * D_STATE), const2),
            pl.BlockSpec((n, SSM_GROUPS * D_STATE), const2),
            pl.BlockSpec((SAMPLE_TB, D_SSM, D_STATE), lambda i: (i, 0, 0)),
            pl.BlockSpec((n, D_SSM), const2),
            pl.BlockSpec((n, D_SSM), lambda i: (0, Z_BLK)),
            pl.BlockSpec((1, D_SSM), const2),
        ],
        out_specs=[
            pl.BlockSpec((SAMPLE_TB, D_SSM, D_STATE), lambda i: (i, 0, 0)),
            pl.BlockSpec((n, D_SSM), const2),
        ],
        out_shape=[
            jax.ShapeDtypeStruct((n, D_SSM, D_STATE), F32),
            jax.ShapeDtypeStruct((n, D_SSM), BF16),
        ],
        scratch_shapes=[pltpu.VMEM((D_SSM, n), F32)],
        compiler_params=_params("arbitrary"),
        name="sample_state",
    )(split, bmat, cmat, h0, dx, proj, ng)


OUT_TN = 1024


def _out_proj_kernel(x_ref, ya_ref, yb_ref, w_ref, o_ref, wres_ref):
    i = pl.program_id(0)
    j = pl.program_id(1)

    @pl.when(i == 0)
    def _():
        wres_ref[j] = w_ref[...].astype(BF16)

    y = jnp.concatenate([ya_ref[...], yb_ref[...]], axis=1)
    o_ref[...] = x_ref[...] + _dot(y, wres_ref[j])


def _out_proj(x, ya, yb, w_out, layer, tm):
    m = x.shape[0]
    n_j = D_MODEL // OUT_TN
    return pl.pallas_call(
        _out_proj_kernel,
        grid=(m // tm, n_j),
        in_specs=[
            pl.BlockSpec((tm, OUT_TN), lambda i, j: (i, j)),
            pl.BlockSpec((tm, D_CHUNK), lambda i, j: (i, 0)),
            pl.BlockSpec((tm, D_SSM), lambda i, j: (i, 0)),
            pl.BlockSpec((None, D_MODEL, OUT_TN), lambda i, j: (layer, 0, jnp.where(i == 0, j, n_j - 1))),
        ],
        out_specs=pl.BlockSpec((tm, OUT_TN), lambda i, j: (i, j)),
        out_shape=jax.ShapeDtypeStruct((m, D_MODEL), F32),
        scratch_shapes=[pltpu.VMEM((n_j, D_MODEL, OUT_TN), BF16)],
        compiler_params=_params("arbitrary", "arbitrary"),
        name="out_proj",
    )(x, ya, yb, w_out)


FFN_TF = 512


def _ffn_kernel(h_ref, g_ref, wg_ref, wu_ref, wd_ref, gf_ref, o_ref, m_ref):
    f = pl.program_id(1)

    @pl.when(f == 0)
    def _():
        h = h_ref[...]
        m_ref[...] = _rms(h, g_ref[...]).astype(BF16)
        o_ref[...] = h

    m = m_ref[...]
    act = (_silu(_dot(m, wg_ref[...])) * _dot(m, wu_ref[...])).astype(BF16)
    o_ref[...] += _dot(act, wd_ref[...])

    @pl.when(f == pl.num_programs(1) - 1)
    def _():
        o_ref[...] = _rms(o_ref[...], gf_ref[...])


def _ffn(h, g, w_gate, w_up, w_down, g_final, tm):
    m = h.shape[0]
    return pl.pallas_call(
        _ffn_kernel,
        grid=(m // tm, D_FF // FFN_TF),
        in_specs=[
            pl.BlockSpec((tm, D_MODEL), lambda i, f: (i, 0)),
            pl.BlockSpec((1, D_MODEL), lambda i, f: (0, 0)),
            pl.BlockSpec((D_MODEL, FFN_TF), lambda i, f: (0, f)),
            pl.BlockSpec((D_MODEL, FFN_TF), lambda i, f: (0, f)),
            pl.BlockSpec((FFN_TF, D_MODEL), lambda i, f: (f, 0)),
            pl.BlockSpec((1, D_MODEL), lambda i, f: (0, 0)),
        ],
        out_specs=pl.BlockSpec((tm, D_MODEL), lambda i, f: (i, 0)),
        out_shape=jax.ShapeDtypeStruct((m, D_MODEL), F32),
        scratch_shapes=[pltpu.VMEM((tm, D_MODEL), BF16)],
        compiler_params=_params("arbitrary", "arbitrary"),
        name="ffn",
    )(h, g, w_gate, w_up, w_down, g_final)


MIX_TM = 512
CHUNKS_PER_TILE = MIX_TM // SSD_CHUNK
FFN_UP_PIECE = 256
FFN_UP_K = 2048
FFN_DOWN_PIECE = 512
MIX_PHASES_PER_FFN_PIECE = 2


def _interleave(main, side, side_per_main):
    for _ in main:
        for _ in range(side_per_main):
            next(side, None)
    for _ in side:
        pass


def _mix_ffn_kernel(x_ref, ya_ref, wo_ref, g_ref, wg_ref, wu_ref, wd_ref, gf_ref,
                    xbc_ref, z_ref, dt_ref, cw_ref, cb_ref, dtb_ref, alog_ref, drep_ref, ng_ref,
                    tril3_ref, rep3_ref,
                    o_ref, nconv_ref, nssm_ref,
                    m_ref, yb_ref, tail_ref, ht_ref, *, chunks_per_seq):
    s = pl.program_id(0)
    f = pl.program_id(1)
    has_ffn = s >= 1
    has_mix = (s < pl.num_programs(0) - 1) & (f < CHUNKS_PER_TILE)

    @pl.when(has_ffn & (f == 0))
    def _():
        y = jnp.concatenate([ya_ref[...], yb_ref[...]], axis=1)
        h = x_ref[...] + _dot(y, wo_ref[...])
        m_ref[...] = _rms(h, g_ref[...]).astype(BF16)
        o_ref[...] = h

    def ffn_step():
        acts = []
        for c in range(FFN_TF // FFN_UP_PIECE):
            cols = slice(c * FFN_UP_PIECE, (c + 1) * FFN_UP_PIECE)
            halves = []
            for w_ref in (wg_ref, wu_ref):
                acc = None
                for k in range(D_MODEL // FFN_UP_K):
                    rows = slice(k * FFN_UP_K, (k + 1) * FFN_UP_K)
                    part = _dot(m_ref[:, rows], w_ref[rows, cols])
                    acc = part if acc is None else acc + part
                    yield
                halves.append(acc)
            acts.append((_silu(halves[0]) * halves[1]).astype(BF16))
        act = jnp.concatenate(acts, axis=1)
        for c in range(D_MODEL // FFN_DOWN_PIECE):
            cols = slice(c * FFN_DOWN_PIECE, (c + 1) * FFN_DOWN_PIECE)
            o_ref[:, cols] += _dot(act, wd_ref[:, cols])
            yield

    def store_y(yb):
        yb_ref[pl.ds(pl.multiple_of(f * SSD_CHUNK, SSD_CHUNK), SSD_CHUNK), :] = yb

    def mix_step():
        first = (s * CHUNKS_PER_TILE + f) % chunks_per_seq == 0
        return _ssd_chunk(first, xbc_ref, z_ref, dt_ref, cw_ref, cb_ref, dtb_ref, alog_ref, drep_ref,
                          ng_ref, tril3_ref, rep3_ref, nconv_ref, nssm_ref, tail_ref, ht_ref, store_y)

    @pl.when(has_ffn & has_mix)
    def _():
        _interleave(ffn_step(), mix_step(), MIX_PHASES_PER_FFN_PIECE)

    @pl.when(has_ffn & jnp.logical_not(has_mix))
    def _():
        _interleave(ffn_step(), iter(()), 0)

    @pl.when(jnp.logical_not(has_ffn) & has_mix)
    def _():
        _interleave(iter(()), mix_step(), 0)

    @pl.when(has_ffn & (f == pl.num_programs(1) - 1))
    def _():
        o_ref[...] = _rms(o_ref[...], gf_ref[...])


def _mix_ffn(x, ya, proj, dt_raw, w_out_b, g, w_gate, w_up, w_down, g_final,
             cw, cb, dtb, alog, drep, ng, tril3, rep3, batch, seq):
    m = x.shape[0]
    n_tiles = m // MIX_TM
    n_chunks = m // SSD_CHUNK
    chunks_per_seq = seq // SSD_CHUNK
    assert MIX_TM % SSD_CHUNK == 0 and seq % MIX_TM == 0 and CHUNKS_PER_TILE <= D_FF // FFN_TF

    def tile(s, f):
        return jnp.maximum(s - 1, 0)

    def chunk(s, f):
        return jnp.where(s >= n_tiles, n_chunks - 1,
                         s * CHUNKS_PER_TILE + jnp.minimum(f, CHUNKS_PER_TILE - 1))

    def seq_of(s, f):
        return chunk(s, f) // chunks_per_seq

    const2 = lambda s, f: (0, 0)
    return pl.pallas_call(
        functools.partial(_mix_ffn_kernel, chunks_per_seq=chunks_per_seq),
        grid=(n_tiles + 1, D_FF // FFN_TF),
        in_specs=[
            pl.BlockSpec((MIX_TM, D_MODEL), lambda s, f: (tile(s, f), 0)),
            pl.BlockSpec((MIX_TM, D_CHUNK), lambda s, f: (tile(s, f), 0)),
            pl.BlockSpec((D_MODEL, D_MODEL), const2, pipeline_mode=pl.Buffered(1)),
            pl.BlockSpec((1, D_MODEL), const2),
            pl.BlockSpec((D_MODEL, FFN_TF), lambda s, f: (0, f)),
            pl.BlockSpec((D_MODEL, FFN_TF), lambda s, f: (0, f)),
            pl.BlockSpec((FFN_TF, D_MODEL), lambda s, f: (f, 0)),
            pl.BlockSpec((1, D_MODEL), const2),
            pl.BlockSpec((SSD_CHUNK, CONV_DIM), lambda s, f: (chunk(s, f), XBC_BLK)),
            pl.BlockSpec((SSD_CHUNK, D_SSM), lambda s, f: (chunk(s, f), Z_BLK)),
            pl.BlockSpec((SSD_CHUNK, HEAD_PAD), lambda s, f: (chunk(s, f), 0)),
            pl.BlockSpec((CONV_W, CONV_DIM), const2),
            pl.BlockSpec((1, CONV_DIM), const2),
            pl.BlockSpec((1, HEAD_PAD), const2),
            pl.BlockSpec((1, HEAD_PAD), const2),
            pl.BlockSpec((1, D_SSM), const2),
            pl.BlockSpec((1, D_SSM), const2),
            pl.BlockSpec((SSD_CHUNK, N_SPLIT * SSD_CHUNK), const2),
            pl.BlockSpec((N_SPLIT * HEAD_PAD, D_SSM), const2),
        ],
        out_specs=[
            pl.BlockSpec((MIX_TM, D_MODEL), lambda s, f: (tile(s, f), 0)),
            pl.BlockSpec((None, CONV_W - 1, CONV_DIM), lambda s, f: (seq_of(s, f), 0, 0)),
            pl.BlockSpec((None, D_SSM, D_STATE), lambda s, f: (seq_of(s, f), 0, 0)),
        ],
        out_shape=[
            jax.ShapeDtypeStruct((m, D_MODEL), F32),
            jax.ShapeDtypeStruct((batch, CONV_W - 1, CONV_DIM), F32),
            jax.ShapeDtypeStruct((batch, D_SSM, D_STATE), F32),
        ],
        scratch_shapes=[
            pltpu.VMEM((MIX_TM, D_MODEL), BF16),
            pltpu.VMEM((MIX_TM, D_SSM), BF16),
            pltpu.VMEM((CONV_PAD, CONV_DIM), F32),
            pltpu.VMEM((D_STATE, D_SSM), F32),
        ],
        compiler_params=_params("arbitrary", "arbitrary"),
        name="mix_ffn",
    )(x, ya, w_out_b, g, w_gate, w_up, w_down, g_final, proj, proj, dt_raw,
      cw, cb, dtb, alog, drep, ng, tril3, rep3)


def _row(v, width=None):
    v = v.astype(F32).reshape(1, -1)
    if width is not None and v.shape[1] < width:
        v = jnp.pad(v, ((0, 0), (0, width - v.shape[1])))
    return v


def kernel(x_prompt, x_sample, state_conv, state_ssm, norm_mix_g, w_in, chunk_v_norm_g, chunk_w_s,
           chunk_b_s, chunk_out_norm_g, ssd_conv_w, ssd_conv_b, ssd_dt_bias, ssd_a_log, ssd_d,
           ssd_norm_g, w_out, norm_ffn_g, w_gate, w_up, w_down, norm_final_g):
    depth = w_in.shape[0]
    batch, seq, _ = x_prompt.shape
    n_s = x_sample.shape[0]
    assert x_sample.shape[1] == 1 and seq % SSD_CHUNK == 0 and depth == 1

    rep = (lax.broadcasted_iota(jnp.int32, (HEAD_PAD, D_SSM), 0)
           == lax.broadcasted_iota(jnp.int32, (HEAD_PAD, D_SSM), 1) // SSM_HEAD_DIM).astype(F32)
    rep8 = (lax.broadcasted_iota(jnp.int32, (HEAD_PAD, HEAD_PAD), 0)
            == lax.broadcasted_iota(jnp.int32, (HEAD_PAD, HEAD_PAD), 1) // SUBLANES).astype(F32)
    rep3 = jnp.tile(rep, (N_SPLIT, 1)).astype(BF16)
    tril3 = jnp.tile(jnp.tril(jnp.ones((SSD_CHUNK, SSD_CHUNK), F32)), (1, N_SPLIT)).astype(BF16)

    hp = x_prompt.reshape(batch * seq, D_MODEL)
    hs = x_sample.reshape(n_s, D_MODEL)
    l = 0
    w_in_t = jnp.swapaxes(w_in, 1, 2)
    w_dt_t = jnp.pad(w_in_t[l, D_MAIN:, :], ((0, HEAD_PAD - SSM_HEADS), (0, 0))).astype(BF16)
    g_mix = _row(norm_mix_g[l])
    gv = _row(chunk_v_norm_g[l])
    go = _row(chunk_out_norm_g[l])
    cw = ssd_conv_w[l].astype(F32)
    cb = _row(ssd_conv_b[l])
    dtb = _row(ssd_dt_bias[l], HEAD_PAD)
    alog = _row(ssd_a_log[l], HEAD_PAD)
    drep = _row(jnp.repeat(ssd_d[l], SSM_HEAD_DIM))
    ng = _row(ssd_norm_g[l])
    g_ffn = _row(norm_ffn_g[l])
    g_final = _row(norm_final_g)

    proj_p, dt_p = _in_proj(hp, g_mix, w_in_t, l, w_dt_t, tm=1024)
    ya_p, w_gate_b, w_up_b, w_down_b, w_out_b = _chunk_gate(
        proj_p, gv, chunk_w_s[l].astype(F32), chunk_b_s[l].astype(F32).T, go, w_gate, w_up, w_down, w_out, l)
    y_p, conv_p, ssm_p = _mix_ffn(hp, ya_p, proj_p, dt_p, w_out_b, g_ffn, w_gate_b, w_up_b, w_down_b,
                                  g_final, cw, cb, dtb, alog, drep, ng, tril3, rep3, batch, seq)

    proj_s, dt_s = _in_proj(hs, g_mix, w_in_t, l, w_dt_t, tm=n_s)
    w0 = _row(jnp.repeat(chunk_w_s[l, :, 0, 0], CHUNK_HEAD_DIM))
    b0 = _row(jnp.repeat(chunk_b_s[l, :, 0], CHUNK_HEAD_DIM))
    ya_s, v_s, conv_s, split, b_s, c_s, dx_s = _sample_rows(
        proj_s, dt_s, state_conv[l].reshape(n_s, (CONV_W - 1) * CONV_DIM), gv, w0, b0, go,
        cw, cb, dtb, alog, drep, rep, rep8)
    ssm_s, yb_s = _sample_state(split, b_s, c_s, state_ssm[l].reshape(n_s, D_SSM, D_STATE),
                                dx_s, proj_s, ng)
    h_s = _out_proj(hs, ya_s, yb_s, w_out, l, tm=n_s)
    y_s = _ffn(h_s, g_ffn, w_gate_b, w_up_b, w_down_b, g_final, tm=n_s)

    return (
        y_p.reshape(batch, seq, D_MODEL),
        y_s.reshape(n_s, 1, D_MODEL),
        conv_p[None],
        ssm_p.reshape(1, batch, SSM_HEADS, SSM_HEAD_DIM, D_STATE),
        conv_s.reshape(1, n_s, CONV_W - 1, CONV_DIM),
        ssm_s.reshape(1, n_s, SSM_HEADS, SSM_HEAD_DIM, D_STATE),
        v_s.reshape(1, n_s, 1, D_CHUNK),
    )
```

```python
import functools

import jax
import jax.numpy as jnp
from jax import lax
from jax.experimental import pallas as pl
from jax.experimental.pallas import tpu as pltpu

F32 = jnp.float32
BF16 = jnp.bfloat16
HIGHEST = lax.Precision.HIGHEST

D_MODEL = 2048
D_CHUNK = 1024
CHUNK_HEADS = 8
CHUNK_HEAD_DIM = 128
CHUNK_LEN = 128
D_SSM = 1024
SSM_HEAD_DIM = 64
SSM_HEADS = 16
SSM_GROUPS = 2
HEADS_PER_GROUP = SSM_HEADS // SSM_GROUPS
GROUP_WIDTH = D_SSM // SSM_GROUPS
D_STATE = 128
CONV_W = 4
CONV_DIM = D_SSM + 2 * SSM_GROUPS * D_STATE
SSD_CHUNK = 128
D_MAIN = 2 * D_CHUNK + D_SSM + CONV_DIM
D_FF = 5632
EPS = 1e-6

LANES = 128
SUBLANES = 8
BF16_SUBLANES = 16
SLAB_ROWS = 32
HEAD_PAD = LANES
N_SPLIT = 3
VMEM_LIMIT = 56 * 1024 * 1024

U_BLK, V_BLK, Z_BLK = 0, 1, 2
XBC_BLK = 2


def _rms(x, g):
    return x * lax.rsqrt(jnp.mean(x * x, axis=-1, keepdims=True) + EPS) * g


def _silu(x):
    return x * jax.nn.sigmoid(x)


def _softplus(x):
    return jnp.maximum(x, 0.0) + jnp.log1p(jnp.exp(-jnp.abs(x)))


def _dot(a, b):
    return jnp.dot(a, b, preferred_element_type=F32)


def _dot_nt(a, b):
    return lax.dot_general(a, b, (((1,), (1,)), ((), ())), preferred_element_type=F32)


def _dot_exact(a, b):
    return jnp.dot(a, b, precision=HIGHEST, preferred_element_type=F32)


def _params(*semantics):
    return pltpu.CompilerParams(dimension_semantics=semantics, vmem_limit_bytes=VMEM_LIMIT)


IN_TN = 512


def _in_proj_kernel(x_hbm, g_ref, w_ref, wdt_ref, *refs, tm, n_cast):
    cast_in = refs[:n_cast]
    o_ref, dt_ref = refs[n_cast:n_cast + 2]
    cast_out = refs[n_cast + 2:2 * n_cast + 2]
    xn_ref, wres_ref, xbuf_ref, sem = refs[2 * n_cast + 2:]
    i = pl.program_id(0)
    j = pl.program_id(1)
    rows = pl.ds(pl.multiple_of(j * IN_TN, IN_TN), IN_TN)

    for src, dst in zip(cast_in, cast_out, strict=True):
        dst[...] = src[...].astype(BF16)

    def x_copy(tile):
        return pltpu.make_async_copy(x_hbm.at[pl.ds(tile * tm, tm), :], xbuf_ref, sem)

    @pl.when((i == 0) & (j == 0))
    def _():
        x_copy(0).start()

    @pl.when(i == 0)
    def _():
        wres_ref[rows, :] = w_ref[...].astype(BF16)

    @pl.when(j == 0)
    def _():
        x_copy(i).wait()
        nb = _rms(xbuf_ref[...], g_ref[...]).astype(BF16)
        xn_ref[...] = nb
        dt_ref[...] = _dot_nt(nb, wdt_ref[...])

    @pl.when((j == 1) & (i + 1 < pl.num_programs(0)))
    def _():
        x_copy(i + 1).start()

    o_ref[...] = _dot_nt(xn_ref[...], wres_ref[rows, :])


def _in_proj(x, g, w_in_t, layer, w_dt_t, tm, cast=()):
    m = x.shape[0]
    n_i = m // tm
    n_j = D_MAIN // IN_TN
    assert n_j >= 2
    cast_in, cast_out, cast_shapes = [], [], []
    for w in cast:
        _, w_rows, w_cols = w.shape
        n_slabs = w_rows // SLAB_ROWS
        assert n_slabs * SLAB_ROWS == w_rows and n_slabs <= n_i * n_j and SLAB_ROWS % BF16_SUBLANES == 0
        slab = lambda i, j, n_slabs=n_slabs: jnp.minimum(i * n_j + j, n_slabs - 1)
        cast_in.append(pl.BlockSpec((None, SLAB_ROWS, w_cols), lambda i, j, slab=slab: (layer, slab(i, j), 0)))
        cast_out.append(pl.BlockSpec((SLAB_ROWS, w_cols), lambda i, j, slab=slab: (slab(i, j), 0)))
        cast_shapes.append(jax.ShapeDtypeStruct((w_rows, w_cols), BF16))
    return pl.pallas_call(
        functools.partial(_in_proj_kernel, tm=tm, n_cast=len(cast)),
        grid=(n_i, n_j),
        in_specs=[
            pl.BlockSpec(memory_space=pl.ANY),
            pl.BlockSpec((1, D_MODEL), lambda i, j: (0, 0)),
            pl.BlockSpec((None, IN_TN, D_MODEL), lambda i, j: (layer, jnp.where(i == 0, j, n_j - 1), 0)),
            pl.BlockSpec((HEAD_PAD, D_MODEL), lambda i, j: (0, 0)),
            *cast_in,
        ],
        out_specs=[
            pl.BlockSpec((tm, IN_TN), lambda i, j: (i, j)),
            pl.BlockSpec((tm, HEAD_PAD), lambda i, j: (i, 0)),
            *cast_out,
        ],
        out_shape=[
            jax.ShapeDtypeStruct((m, D_MAIN), F32),
            jax.ShapeDtypeStruct((m, HEAD_PAD), F32),
            *cast_shapes,
        ],
        scratch_shapes=[
            pltpu.VMEM((tm, D_MODEL), BF16),
            pltpu.VMEM((D_MAIN, D_MODEL), BF16),
            pltpu.VMEM((tm, D_MODEL), F32),
            pltpu.SemaphoreType.DMA(()),
        ],
        compiler_params=_params("arbitrary", "arbitrary"),
        name="in_proj",
    )(x, g, w_in_t, w_dt_t, *cast)


def _chunk_gate_kernel(u_ref, v_ref, gv_ref, ws_ref, bst_ref, go_ref, wd_ref, wo_ref,
                       o_ref, wdb_ref, wob_ref):
    wdb_ref[...] = wd_ref[...].astype(BF16)
    wob_ref[...] = wo_ref[...].astype(BF16)

    vb = _rms(jax.nn.gelu(v_ref[...]), gv_ref[...]).astype(BF16)
    u = jax.nn.gelu(u_ref[...])
    row = lax.broadcasted_iota(jnp.int32, (CHUNK_LEN, CHUNK_LEN), 0)
    col = lax.broadcasted_iota(jnp.int32, (CHUNK_LEN, CHUNK_LEN), 1)
    causal = row >= col
    parts = []
    for h in range(CHUNK_HEADS):
        sl = slice(h * CHUNK_HEAD_DIM, (h + 1) * CHUNK_HEAD_DIM)
        w = jnp.where(causal, ws_ref[h], 0.0).astype(BF16)
        mixed = _dot(w, vb[:, sl]) + bst_ref[:, h:h + 1]
        parts.append(u[:, sl] * mixed)
    y = jnp.concatenate(parts, axis=1)
    o_ref[...] = _rms(y, go_ref[...]).astype(BF16)


def _chunk_gate(proj, gv, ws, bst, go, w_down, w_out, layer):
    m = proj.shape[0]
    n_steps = m // CHUNK_LEN
    rows = D_MODEL // n_steps
    dn_hold = 2
    dn_rows = D_FF * dn_hold // n_steps
    assert rows * n_steps == D_MODEL and rows % BF16_SUBLANES == 0
    assert dn_rows * n_steps == D_FF * dn_hold and dn_rows % BF16_SUBLANES == 0
    dn_in = pl.BlockSpec((None, dn_rows, D_MODEL), lambda c: (layer, c // dn_hold, 0))
    dn_out = pl.BlockSpec((dn_rows, D_MODEL), lambda c: (c // dn_hold, 0))
    wo_in = pl.BlockSpec((None, rows, D_MODEL), lambda c: (layer, c, 0))
    wo_out = pl.BlockSpec((rows, D_MODEL), lambda c: (c, 0))
    return pl.pallas_call(
        _chunk_gate_kernel,
        grid=(n_steps,),
        in_specs=[
            pl.BlockSpec((CHUNK_LEN, D_CHUNK), lambda c: (c, U_BLK)),
            pl.BlockSpec((CHUNK_LEN, D_CHUNK), lambda c: (c, V_BLK)),
            pl.BlockSpec((1, D_CHUNK), lambda c: (0, 0)),
            pl.BlockSpec((CHUNK_HEADS, CHUNK_LEN, CHUNK_LEN), lambda c: (0, 0, 0)),
            pl.BlockSpec((CHUNK_LEN, CHUNK_HEADS), lambda c: (0, 0)),
            pl.BlockSpec((1, D_CHUNK), lambda c: (0, 0)),
            dn_in, wo_in,
        ],
        out_specs=[
            pl.BlockSpec((CHUNK_LEN, D_CHUNK), lambda c: (c, 0)),
            dn_out, wo_out,
        ],
        out_shape=[
            jax.ShapeDtypeStruct((m, D_CHUNK), BF16),
            jax.ShapeDtypeStruct((D_FF, D_MODEL), BF16),
            jax.ShapeDtypeStruct((D_MODEL, D_MODEL), BF16),
        ],
        compiler_params=_params("arbitrary"),
        name="chunk_gate",
    )(proj, proj, gv, ws, bst, go, w_down, w_out)


CONV_PAD = 8


def _group_norm_out(y, z, g):
    y = y * _silu(z)
    parts = []
    for grp in range(SSM_GROUPS):
        yg = y[:, grp * GROUP_WIDTH:(grp + 1) * GROUP_WIDTH]
        parts.append(yg * lax.rsqrt(jnp.mean(yg * yg, axis=-1, keepdims=True) + EPS))
    return (jnp.concatenate(parts, axis=1) * g).astype(BF16)


def _split3(x):
    p1 = x.astype(BF16)
    r1 = x - p1.astype(F32)
    p2 = r1.astype(BF16)
    p3 = (r1 - p2.astype(F32)).astype(BF16)
    return p1, p2, p3


def _select_rows(sel3, x):
    return _dot(sel3, jnp.concatenate(_split3(x), axis=0))


def _select_cols(x, sel3):
    return _dot(jnp.concatenate(_split3(x), axis=1), sel3)


def _ssd_chunk(first, xbc_ref, z_ref, dt_ref, cw_ref, cb_ref, dtb_ref, alog_ref, drep_ref, ng_ref,
               tril3_ref, rep3_ref, nconv_ref, nssm_ref, tail_ref, ht_ref, store_y):
    q = SSD_CHUNK

    xbc = xbc_ref[...]
    tail = jnp.where(first, 0.0, tail_ref[...])
    ht_prev = jnp.where(first, 0.0, ht_ref[...])
    sub = lax.broadcasted_iota(jnp.int32, (CONV_PAD, CONV_DIM), 0)
    conv = cb_ref[...] + xbc * cw_ref[CONV_W - 1:CONV_W, :]
    for k in range(1, CONV_W):
        shifted = pltpu.roll(xbc, k, 0)
        head = jnp.where(sub < k, pltpu.roll(tail, k, 0), shifted[0:CONV_PAD, :])
        shifted = jnp.concatenate([head, shifted[CONV_PAD:, :]], axis=0)
        conv = conv + shifted * cw_ref[CONV_W - 1 - k:CONV_W - k, :]
    tail_ref[...] = xbc[q - CONV_PAD:, :]
    nconv_ref[...] = xbc[q - (CONV_W - 1):, :]
    yield

    act = _silu(conv)
    xs = act[:, :D_SSM]
    bmat = act[:, D_SSM:D_SSM + SSM_GROUPS * D_STATE]
    cmat = act[:, D_SSM + SSM_GROUPS * D_STATE:]
    yield

    dt = _softplus(dt_ref[...] + dtb_ref[...])
    a = -jnp.exp(alog_ref[...])
    acum = _select_rows(tril3_ref[...], dt * a)
    acum_t = acum.T
    heads = jnp.concatenate([dt, jnp.exp(acum), jnp.exp(acum[q - 1:q, :] - acum)], axis=0)
    expanded = _select_cols(heads, rep3_ref[...])
    xdt = xs * expanded[0:q, :]
    exp_a = expanded[q:2 * q, :]
    xw = xdt * expanded[2 * q:3 * q, :]
    yield

    row = lax.broadcasted_iota(jnp.int32, (q, q), 0)
    col = lax.broadcasted_iota(jnp.int32, (q, q), 1)
    causal = row >= col
    low_half = col < SSM_HEAD_DIM
    ydiag, yoff, states = [], [], []
    for grp in range(SSM_GROUPS):
        bg_f = bmat[:, grp * D_STATE:(grp + 1) * D_STATE]
        bg = bg_f.astype(BF16)
        cg = cmat[:, grp * D_STATE:(grp + 1) * D_STATE].astype(BF16)
        cb = _dot_nt(cg, bg)
        ch = slice(grp * GROUP_WIDTH, (grp + 1) * GROUP_WIDTH)
        yoff.append(_dot(cg, ht_prev[:, ch].astype(BF16)))
        yield
        for pair in range(HEADS_PER_GROUP // 2):
            h0 = grp * HEADS_PER_GROUP + 2 * pair
            masks = []
            for h in (h0, h0 + 1):
                diff = acum[:, h:h + 1] - acum_t[h:h + 1, :]
                masks.append((cb * jnp.exp(jnp.where(causal, diff, -jnp.inf))).astype(BF16))
            xp = xdt[:, h0 * SSM_HEAD_DIM:(h0 + 2) * SSM_HEAD_DIM]
            rhs = jnp.concatenate([jnp.where(low_half, xp, 0.0), jnp.where(low_half, 0.0, xp)], axis=0)
            ydiag.append(_dot(jnp.concatenate(masks, axis=1), rhs.astype(BF16)))
            yield
        states.append(_dot(bg_f.T.astype(BF16), xw[:, ch].astype(BF16)))
        yield
    y = drep_ref[...] * xs + jnp.concatenate(ydiag, axis=1) + jnp.concatenate(yoff, axis=1) * exp_a

    ht_new = ht_prev * exp_a[q - 1:q, :] + jnp.concatenate(states, axis=1)
    ht_ref[...] = ht_new
    nssm_ref[...] = ht_new.T
    yield
    store_y(_group_norm_out(y, z_ref[...], ng_ref[...]))


def _sample_rows_kernel(u_ref, v_ref, xbc_ref, dt_ref, sc_ref, gv_ref, w0_ref, b0_ref, go_ref,
                        cw_ref, cb_ref, dtb_ref, alog_ref, drep_ref, rep_ref, rep8_ref,
                        ya_ref, vrows_ref, nconv_ref, split_ref, b_ref, c_ref, dx_ref):
    vr = _rms(jax.nn.gelu(v_ref[...]), gv_ref[...])
    vrows_ref[...] = vr
    mixed = vr * w0_ref[...] + b0_ref[...]
    ya_ref[...] = _rms(jax.nn.gelu(u_ref[...]) * mixed, go_ref[...]).astype(BF16)

    xbc = xbc_ref[...]
    conv = cb_ref[...]
    for k in range(CONV_W - 1):
        conv = conv + sc_ref[:, k * CONV_DIM:(k + 1) * CONV_DIM] * cw_ref[k:k + 1, :]
    conv = conv + xbc * cw_ref[CONV_W - 1:CONV_W, :]
    nconv_ref[:, 0:(CONV_W - 2) * CONV_DIM] = sc_ref[:, CONV_DIM:(CONV_W - 1) * CONV_DIM]
    nconv_ref[:, (CONV_W - 2) * CONV_DIM:] = xbc

    act = _silu(conv)
    xs = act[:, :D_SSM]
    b_ref[...] = act[:, D_SSM:D_SSM + SSM_GROUPS * D_STATE]
    c_ref[...] = act[:, D_SSM + SSM_GROUPS * D_STATE:]
    dx_ref[...] = drep_ref[...] * xs

    dt = _softplus(dt_ref[...] + dtb_ref[...])
    decay = jnp.exp(dt * (-jnp.exp(alog_ref[...])))
    xdt = xs * _dot_exact(dt, rep_ref[...])
    cols = jnp.concatenate([xdt, _dot_exact(decay, rep8_ref[...])], axis=1).T
    for p in range(N_SPLIT):
        piece = cols.astype(BF16)
        split_ref[:, p * LANES:(p + 1) * LANES] = piece
        cols = cols - piece.astype(F32)


SPLIT_ROWS = D_SSM + HEAD_PAD


def _sample_rows(proj, dt_raw, sconv, gv, w0, b0, go, cw, cb, dtb, alog, drep, rep, rep8):
    n = proj.shape[0]
    full = lambda shape: pl.BlockSpec(shape, lambda i: (0,) * len(shape))
    return pl.pallas_call(
        _sample_rows_kernel,
        grid=(1,),
        in_specs=[
            pl.BlockSpec((n, D_CHUNK), lambda i: (0, U_BLK)),
            pl.BlockSpec((n, D_CHUNK), lambda i: (0, V_BLK)),
            pl.BlockSpec((n, CONV_DIM), lambda i: (0, XBC_BLK)),
            full((n, HEAD_PAD)),
            full((n, (CONV_W - 1) * CONV_DIM)),
            full((1, D_CHUNK)), full((1, D_CHUNK)), full((1, D_CHUNK)), full((1, D_CHUNK)),
            full((CONV_W, CONV_DIM)), full((1, CONV_DIM)),
            full((1, HEAD_PAD)), full((1, HEAD_PAD)), full((1, D_SSM)),
            full((HEAD_PAD, D_SSM)), full((HEAD_PAD, HEAD_PAD)),
        ],
        out_specs=[
            full((n, D_CHUNK)), full((n, D_CHUNK)), full((n, (CONV_W - 1) * CONV_DIM)),
            full((SPLIT_ROWS, N_SPLIT * LANES)),
            full((n, SSM_GROUPS * D_STATE)), full((n, SSM_GROUPS * D_STATE)), full((n, D_SSM)),
        ],
        out_shape=[
            jax.ShapeDtypeStruct((n, D_CHUNK), BF16),
            jax.ShapeDtypeStruct((n, D_CHUNK), F32),
            jax.ShapeDtypeStruct((n, (CONV_W - 1) * CONV_DIM), F32),
            jax.ShapeDtypeStruct((SPLIT_ROWS, N_SPLIT * LANES), BF16),
            jax.ShapeDtypeStruct((n, SSM_GROUPS * D_STATE), F32),
            jax.ShapeDtypeStruct((n, SSM_GROUPS * D_STATE), F32),
            jax.ShapeDtypeStruct((n, D_SSM), F32),
        ],
        compiler_params=_params("arbitrary"),
        name="sample_rows",
    )(proj, proj, proj, dt_raw, sconv, gv, w0, b0, go, cw, cb, dtb, alog, drep, rep, rep8)


SAMPLE_TB = 8


def _sample_state_kernel(split_ref, b_ref, c_ref, h0_ref, dx_ref, z_ref, ng_ref,
                         hn_ref, y_ref, yt_ref):
    i = pl.program_id(0)
    n_tok = b_ref.shape[0]

    @pl.when(i == 0)
    def _():
        yt_ref[...] = jnp.zeros_like(yt_ref)

    tok = lax.broadcasted_iota(jnp.int32, (N_SPLIT * LANES, D_STATE), 0) % LANES
    lane = lax.broadcasted_iota(jnp.int32, (D_SSM, n_tok), 1)

    def per_group(row, grp):
        return jnp.broadcast_to(row[:, grp * D_STATE:(grp + 1) * D_STATE], (GROUP_WIDTH, D_STATE))

    for k in range(SAMPLE_TB):
        t = i * SAMPLE_TB + k
        onehot = (tok == t).astype(BF16)
        bcast = _dot(split_ref[...], onehot)
        decay = jnp.broadcast_to(
            bcast[D_SSM:, :].reshape(SSM_HEADS, 1, SUBLANES, D_STATE),
            (SSM_HEADS, SSM_HEAD_DIM // SUBLANES, SUBLANES, D_STATE)).reshape(D_SSM, D_STATE)
        brow = b_ref[pl.ds(t, 1), :]
        crow = c_ref[pl.ds(t, 1), :]
        bfull = jnp.concatenate([per_group(brow, g) for g in range(SSM_GROUPS)], axis=0)
        cfull = jnp.concatenate([per_group(crow, g) for g in range(SSM_GROUPS)], axis=0)
        h_new = h0_ref[k] * decay + bcast[:D_SSM, :] * bfull
        hn_ref[k] = h_new
        ysum = jnp.sum(h_new * cfull, axis=-1, keepdims=True)
        yt_ref[...] = jnp.where(lane == t, ysum, yt_ref[...])

    @pl.when(i == pl.num_programs(0) - 1)
    def _():
        y = yt_ref[...].T + dx_ref[...]
        y_ref[...] = _group_norm_out(y, z_ref[...], ng_ref[...])


def _sample_state(split, bmat, cmat, h0, dx, proj, ng):
    n = h0.shape[0]
    const2 = lambda i: (0, 0)
    return pl.pallas_call(
        _sample_state_kernel,
        grid=(n // SAMPLE_TB,),
        in_specs=[
            pl.BlockSpec((SPLIT_ROWS, N_SPLIT * LANES), const2),
            pl.BlockSpec((n, SSM_GROUPS * D_STATE), const2),
            pl.BlockSpec((n, SSM_GROUPS * D_STATE), const2),
            pl.BlockSpec((SAMPLE_TB, D_SSM, D_STATE), lambda i: (i, 0, 0)),
            pl.BlockSpec((n, D_SSM), const2),
            pl.BlockSpec((n, D_SSM), lambda i: (0, Z_BLK)),
            pl.BlockSpec((1, D_SSM), const2),
        ],
        out_specs=[
            pl.BlockSpec((SAMPLE_TB, D_SSM, D_STATE), lambda i: (i, 0, 0)),
            pl.BlockSpec((n, D_SSM), const2),
        ],
        out_shape=[
            jax.ShapeDtypeStruct((n, D_SSM, D_STATE), F32),
            jax.ShapeDtypeStruct((n, D_SSM), BF16),
        ],
        scratch_shapes=[pltpu.VMEM((D_SSM, n), F32)],
        compiler_params=_params("arbitrary"),
        name="sample_state",
    )(split, bmat, cmat, h0, dx, proj, ng)


OUT_TN = 1024


def _out_proj_kernel(x_ref, ya_ref, yb_ref, w_ref, o_ref, wres_ref):
    i = pl.program_id(0)
    j = pl.program_id(1)

    @pl.when(i == 0)
    def _():
        wres_ref[j] = w_ref[...].astype(BF16)

    y = jnp.concatenate([ya_ref[...], yb_ref[...]], axis=1)
    o_ref[...] = x_ref[...] + _dot(y, wres_ref[j])


def _out_proj(x, ya, yb, w_out, layer, tm):
    m = x.shape[0]
    n_j = D_MODEL // OUT_TN
    return pl.pallas_call(
        _out_proj_kernel,
        grid=(m // tm, n_j),
        in_specs=[
            pl.BlockSpec((tm, OUT_TN), lambda i, j: (i, j)),
            pl.BlockSpec((tm, D_CHUNK), lambda i, j: (i, 0)),
            pl.BlockSpec((tm, D_SSM), lambda i, j: (i, 0)),
            pl.BlockSpec((None, D_MODEL, OUT_TN), lambda i, j: (layer, 0, jnp.where(i == 0, j, n_j - 1))),
        ],
        out_specs=pl.BlockSpec((tm, OUT_TN), lambda i, j: (i, j)),
        out_shape=jax.ShapeDtypeStruct((m, D_MODEL), F32),
        scratch_shapes=[pltpu.VMEM((n_j, D_MODEL, OUT_TN), BF16)],
        compiler_params=_params("arbitrary", "arbitrary"),
        name="out_proj",
    )(x, ya, yb, w_out)


FFN_TF = 512


def _ffn_kernel(h_ref, g_ref, wg_ref, wu_ref, wd_ref, gf_ref, o_ref, m_ref):
    f = pl.program_id(1)

    @pl.when(f == 0)
    def _():
        h = h_ref[...]
        m_ref[...] = _rms(h, g_ref[...]).astype(BF16)
        o_ref[...] = h

    m = m_ref[...]
    act = (_silu(_dot(m, wg_ref[...])) * _dot(m, wu_ref[...])).astype(BF16)
    o_ref[...] += _dot(act, wd_ref[...])

    @pl.when(f == pl.num_programs(1) - 1)
    def _():
        o_ref[...] = _rms(o_ref[...], gf_ref[...])


def _ffn(h, g, w_gate, w_up, w_down, g_final, tm):
    m = h.shape[0]
    return pl.pallas_call(
        _ffn_kernel,
        grid=(m // tm, D_FF // FFN_TF),
        in_specs=[
            pl.BlockSpec((tm, D_MODEL), lambda i, f: (i, 0)),
            pl.BlockSpec((1, D_MODEL), lambda i, f: (0, 0)),
            pl.BlockSpec((D_MODEL, FFN_TF), lambda i, f: (0, f)),
            pl.BlockSpec((D_MODEL, FFN_TF), lambda i, f: (0, f)),
            pl.BlockSpec((FFN_TF, D_MODEL), lambda i, f: (f, 0)),
            pl.BlockSpec((1, D_MODEL), lambda i, f: (0, 0)),
        ],
        out_specs=pl.BlockSpec((tm, D_MODEL), lambda i, f: (i, 0)),
        out_shape=jax.ShapeDtypeStruct((m, D_MODEL), F32),
        scratch_shapes=[pltpu.VMEM((tm, D_MODEL), BF16)],
        compiler_params=_params("arbitrary", "arbitrary"),
        name="ffn",
    )(h, g, w_gate, w_up, w_down, g_final)


MIX_TM = 512
CHUNKS_PER_TILE = MIX_TM // SSD_CHUNK
FFN_UP_PIECE = 256
FFN_UP_K = 2048
FFN_DOWN_PIECE = 512
MIX_PHASES_PER_FFN_PIECE = 2


def _interleave(main, side, side_per_main):
    for _ in main:
        for _ in range(side_per_main):
            next(side, None)
    for _ in side:
        pass


def _mix_ffn_kernel(x_ref, ya_ref, wo_ref, g_ref, wg_ref, wu_ref, wd_ref, gf_ref,
                    xbc_ref, z_ref, dt_ref, cw_ref, cb_ref, dtb_ref, alog_ref, drep_ref, ng_ref,
                    tril3_ref, rep3_ref,
                    o_ref, nconv_ref, nssm_ref,
                    m_ref, yb_ref, tail_ref, ht_ref, *, chunks_per_seq):
    s = pl.program_id(0)
    f = pl.program_id(1)
    has_ffn = s >= 1
    has_mix = (s < pl.num_programs(0) - 1) & (f < CHUNKS_PER_TILE)

    @pl.when(has_ffn & (f == 0))
    def _():
        y = jnp.concatenate([ya_ref[...], yb_ref[...]], axis=1)
        h = x_ref[...] + _dot(y, wo_ref[...])
        m_ref[...] = _rms(h, g_ref[...]).astype(BF16)
        o_ref[...] = h

    def ffn_step():
        acts = []
        for c in range(FFN_TF // FFN_UP_PIECE):
            cols = slice(c * FFN_UP_PIECE, (c + 1) * FFN_UP_PIECE)
            halves = []
            for w_ref in (wg_ref, wu_ref):
                acc = None
                for k in range(D_MODEL // FFN_UP_K):
                    rows = slice(k * FFN_UP_K, (k + 1) * FFN_UP_K)
                    part = _dot(m_ref[:, rows], w_ref[rows, cols])
                    acc = part if acc is None else acc + part
                    yield
                halves.append(acc)
            acts.append((_silu(halves[0]) * halves[1]).astype(BF16))
        act = jnp.concatenate(acts, axis=1)
        for c in range(D_MODEL // FFN_DOWN_PIECE):
            cols = slice(c * FFN_DOWN_PIECE, (c + 1) * FFN_DOWN_PIECE)
            o_ref[:, cols] += _dot(act, wd_ref[:, cols])
            yield

    def store_y(yb):
        yb_ref[pl.ds(pl.multiple_of(f * SSD_CHUNK, SSD_CHUNK), SSD_CHUNK), :] = yb

    def mix_step():
        first = (s * CHUNKS_PER_TILE + f) % chunks_per_seq == 0
        return _ssd_chunk(first, xbc_ref, z_ref, dt_ref, cw_ref, cb_ref, dtb_ref, alog_ref, drep_ref,
                          ng_ref, tril3_ref, rep3_ref, nconv_ref, nssm_ref, tail_ref, ht_ref, store_y)

    @pl.when(has_ffn & has_mix)
    def _():
        _interleave(ffn_step(), mix_step(), MIX_PHASES_PER_FFN_PIECE)

    @pl.when(has_ffn & jnp.logical_not(has_mix))
    def _():
        _interleave(ffn_step(), iter(()), 0)

    @pl.when(jnp.logical_not(has_ffn) & has_mix)
    def _():
        _interleave(iter(()), mix_step(), 0)

    @pl.when(has_ffn & (f == pl.num_programs(1) - 1))
    def _():
        o_ref[...] = _rms(o_ref[...], gf_ref[...])


def _mix_ffn(x, ya, proj, dt_raw, w_out_b, g, w_gate, w_up, w_down, g_final,
             cw, cb, dtb, alog, drep, ng, tril3, rep3, batch, seq):
    m = x.shape[0]
    n_tiles = m // MIX_TM
    n_chunks = m // SSD_CHUNK
    chunks_per_seq = seq // SSD_CHUNK
    assert MIX_TM % SSD_CHUNK == 0 and seq % MIX_TM == 0 and CHUNKS_PER_TILE <= D_FF // FFN_TF

    def tile(s, f):
        return jnp.maximum(s - 1, 0)

    def chunk(s, f):
        return jnp.where(s >= n_tiles, n_chunks - 1,
                         s * CHUNKS_PER_TILE + jnp.minimum(f, CHUNKS_PER_TILE - 1))

    def seq_of(s, f):
        return chunk(s, f) // chunks_per_seq

    const2 = lambda s, f: (0, 0)
    return pl.pallas_call(
        functools.partial(_mix_ffn_kernel, chunks_per_seq=chunks_per_seq),
        grid=(n_tiles + 1, D_FF // FFN_TF),
        in_specs=[
            pl.BlockSpec((MIX_TM, D_MODEL), lambda s, f: (tile(s, f), 0)),
            pl.BlockSpec((MIX_TM, D_CHUNK), lambda s, f: (tile(s, f), 0)),
            pl.BlockSpec((D_MODEL, D_MODEL), const2, pipeline_mode=pl.Buffered(1)),
            pl.BlockSpec((1, D_MODEL), const2),
            pl.BlockSpec((D_MODEL, FFN_TF), lambda s, f: (0, f)),
            pl.BlockSpec((D_MODEL, FFN_TF), lambda s, f: (0, f)),
            pl.BlockSpec((FFN_TF, D_MODEL), lambda s, f: (f, 0)),
            pl.BlockSpec((1, D_MODEL), const2),
            pl.BlockSpec((SSD_CHUNK, CONV_DIM), lambda s, f: (chunk(s, f), XBC_BLK)),
            pl.BlockSpec((SSD_CHUNK, D_SSM), lambda s, f: (chunk(s, f), Z_BLK)),
            pl.BlockSpec((SSD_CHUNK, HEAD_PAD), lambda s, f: (chunk(s, f), 0)),
            pl.BlockSpec((CONV_W, CONV_DIM), const2),
            pl.BlockSpec((1, CONV_DIM), const2),
            pl.BlockSpec((1, HEAD_PAD), const2),
            pl.BlockSpec((1, HEAD_PAD), const2),
            pl.BlockSpec((1, D_SSM), const2),
            pl.BlockSpec((1, D_SSM), const2),
            pl.BlockSpec((SSD_CHUNK, N_SPLIT * SSD_CHUNK), const2),
            pl.BlockSpec((N_SPLIT * HEAD_PAD, D_SSM), const2),
        ],
        out_specs=[
            pl.BlockSpec((MIX_TM, D_MODEL), lambda s, f: (tile(s, f), 0)),
            pl.BlockSpec((None, CONV_W - 1, CONV_DIM), lambda s, f: (seq_of(s, f), 0, 0)),
            pl.BlockSpec((None, D_SSM, D_STATE), lambda s, f: (seq_of(s, f), 0, 0)),
        ],
        out_shape=[
            jax.ShapeDtypeStruct((m, D_MODEL), F32),
            jax.ShapeDtypeStruct((batch, CONV_W - 1, CONV_DIM), F32),
            jax.ShapeDtypeStruct((batch, D_SSM, D_STATE), F32),
        ],
        scratch_shapes=[
            pltpu.VMEM((MIX_TM, D_MODEL), BF16),
            pltpu.VMEM((MIX_TM, D_SSM), BF16),
            pltpu.VMEM((CONV_PAD, CONV_DIM), F32),
            pltpu.VMEM((D_STATE, D_SSM), F32),
        ],
        compiler_params=_params("arbitrary", "arbitrary"),
        name="mix_ffn",
    )(x, ya, w_out_b, g, w_gate, w_up, w_down, g_final, proj, proj, dt_raw,
      cw, cb, dtb, alog, drep, ng, tril3, rep3)


def _row(v, width=None):
    v = v.astype(F32).reshape(1, -1)
    if width is not None and v.shape[1] < width:
        v = jnp.pad(v, ((0, 0), (0, width - v.shape[1])))
    return v


def kernel(x_prompt, x_sample, state_conv, state_ssm, norm_mix_g, w_in, chunk_v_norm_g, chunk_w_s,
           chunk_b_s, chunk_out_norm_g, ssd_conv_w, ssd_conv_b, ssd_dt_bias, ssd_a_log, ssd_d,
           ssd_norm_g, w_out, norm_ffn_g, w_gate, w_up, w_down, norm_final_g):
    depth = w_in.shape[0]
    batch, seq, _ = x_prompt.shape
    n_s = x_sample.shape[0]
    assert x_sample.shape[1] == 1 and seq % SSD_CHUNK == 0 and depth == 1

    rep = (lax.broadcasted_iota(jnp.int32, (HEAD_PAD, D_SSM), 0)
           == lax.broadcasted_iota(jnp.int32, (HEAD_PAD, D_SSM), 1) // SSM_HEAD_DIM).astype(F32)
    rep8 = (lax.broadcasted_iota(jnp.int32, (HEAD_PAD, HEAD_PAD), 0)
            == lax.broadcasted_iota(jnp.int32, (HEAD_PAD, HEAD_PAD), 1) // SUBLANES).astype(F32)
    rep3 = jnp.tile(rep, (N_SPLIT, 1)).astype(BF16)
    tril3 = jnp.tile(jnp.tril(jnp.ones((SSD_CHUNK, SSD_CHUNK), F32)), (1, N_SPLIT)).astype(BF16)

    hp = x_prompt.reshape(batch * seq, D_MODEL)
    hs = x_sample.reshape(n_s, D_MODEL)
    l = 0
    w_in_t = jnp.swapaxes(w_in, 1, 2)
    w_dt_t = jnp.pad(w_in_t[l, D_MAIN:, :], ((0, HEAD_PAD - SSM_HEADS), (0, 0))).astype(BF16)
    g_mix = _row(norm_mix_g[l])
    gv = _row(chunk_v_norm_g[l])
    go = _row(chunk_out_norm_g[l])
    cw = ssd_conv_w[l].astype(F32)
    cb = _row(ssd_conv_b[l])
    dtb = _row(ssd_dt_bias[l], HEAD_PAD)
    alog = _row(ssd_a_log[l], HEAD_PAD)
    drep = _row(jnp.repeat(ssd_d[l], SSM_HEAD_DIM))
    ng = _row(ssd_norm_g[l])
    g_ffn = _row(norm_ffn_g[l])
    g_final = _row(norm_final_g)

    proj_p, dt_p, w_gate_b, w_up_b = _in_proj(hp, g_mix, w_in_t, l, w_dt_t, tm=1024, cast=(w_gate, w_up))
    ya_p, w_down_b, w_out_b = _chunk_gate(
        proj_p, gv, chunk_w_s[l].astype(F32), chunk_b_s[l].astype(F32).T, go, w_down, w_out, l)
    y_p, conv_p, ssm_p = _mix_ffn(hp, ya_p, proj_p, dt_p, w_out_b, g_ffn, w_gate_b, w_up_b, w_down_b,
                                  g_final, cw, cb, dtb, alog, drep, ng, tril3, rep3, batch, seq)

    proj_s, dt_s = _in_proj(hs, g_mix, w_in_t, l, w_dt_t, tm=n_s)
    w0 = _row(jnp.repeat(chunk_w_s[l, :, 0, 0], CHUNK_HEAD_DIM))
    b0 = _row(jnp.repeat(chunk_b_s[l, :, 0], CHUNK_HEAD_DIM))
    ya_s, v_s, conv_s, split, b_s, c_s, dx_s = _sample_rows(
        proj_s, dt_s, state_conv[l].reshape(n_s, (CONV_W - 1) * CONV_DIM), gv, w0, b0, go,
        cw, cb, dtb, alog, drep, rep, rep8)
    ssm_s, yb_s = _sample_state(split, b_s, c_s, state_ssm[l].reshape(n_s, D_SSM, D_STATE),
                                dx_s, proj_s, ng)
    h_s = _out_proj(hs, ya_s, yb_s, w_out, l, tm=n_s)
    y_s = _ffn(h_s, g_ffn, w_gate_b, w_up_b, w_down_b, g_final, tm=n_s)

    return (
        y_p.reshape(batch, seq, D_MODEL),
        y_s.reshape(n_s, 1, D_MODEL),
        conv_p[None],
        ssm_p.reshape(1, batch, SSM_HEADS, SSM_HEAD_DIM, D_STATE),
        conv_s.reshape(1, n_s, CONV_W - 1, CONV_DIM),
        ssm_s.reshape(1, n_s, SSM_HEADS, SSM_HEAD_DIM, D_STATE),
        v_s.reshape(1, n_s, 1, D_CHUNK),
    )
```

```python
import functools

import jax
import jax.numpy as jnp
from jax import lax
from jax.experimental import pallas as pl
from jax.experimental.pallas import tpu as pltpu

F32 = jnp.float32
BF16 = jnp.bfloat16
HIGHEST = lax.Precision.HIGHEST

D_MODEL = 2048
D_CHUNK = 1024
CHUNK_HEADS = 8
CHUNK_HEAD_DIM = 128
CHUNK_LEN = 128
D_SSM = 1024
SSM_HEAD_DIM = 64
SSM_HEADS = 16
SSM_GROUPS = 2
HEADS_PER_GROUP = SSM_HEADS // SSM_GROUPS
GROUP_WIDTH = D_SSM // SSM_GROUPS
D_STATE = 128
CONV_W = 4
CONV_DIM = D_SSM + 2 * SSM_GROUPS * D_STATE
SSD_CHUNK = 128
D_MAIN = 2 * D_CHUNK + D_SSM + CONV_DIM
D_FF = 5632
EPS = 1e-6

LANES = 128
SUBLANES = 8
BF16_SUBLANES = 16
HEAD_PAD = LANES
N_SPLIT = 3
VMEM_LIMIT = 56 * 1024 * 1024

U_BLK, V_BLK, Z_BLK = 0, 1, 2
XBC_BLK = 2


def _rms(x, g):
    return x * lax.rsqrt(jnp.mean(x * x, axis=-1, keepdims=True) + EPS) * g


def _silu(x):
    return x * jax.nn.sigmoid(x)


def _softplus(x):
    return jnp.maximum(x, 0.0) + jnp.log1p(jnp.exp(-jnp.abs(x)))


def _dot(a, b):
    return jnp.dot(a, b, preferred_element_type=F32)


def _dot_nt(a, b):
    return lax.dot_general(a, b, (((1,), (1,)), ((), ())), preferred_element_type=F32)


def _dot_exact(a, b):
    return jnp.dot(a, b, precision=HIGHEST, preferred_element_type=F32)


def _params(*semantics):
    return pltpu.CompilerParams(dimension_semantics=semantics, vmem_limit_bytes=VMEM_LIMIT)


IN_TN = 512


def _in_proj_kernel(x_hbm, g_ref, w_ref, wdt_ref, *refs, tm, n_cast):
    cast_in = refs[:n_cast]
    o_ref, dt_ref = refs[n_cast:n_cast + 2]
    cast_out = refs[n_cast + 2:2 * n_cast + 2]
    xn_ref, wres_ref, xbuf_ref, sem = refs[2 * n_cast + 2:]
    i = pl.program_id(0)
    j = pl.program_id(1)
    rows = pl.ds(pl.multiple_of(j * IN_TN, IN_TN), IN_TN)

    for src, dst in zip(cast_in, cast_out, strict=True):
        dst[...] = src[...].astype(BF16)

    def x_copy(tile):
        return pltpu.make_async_copy(x_hbm.at[pl.ds(tile * tm, tm), :], xbuf_ref, sem)

    @pl.when((i == 0) & (j == 0))
    def _():
        x_copy(0).start()

    @pl.when(i == 0)
    def _():
        wres_ref[rows, :] = w_ref[...].astype(BF16)

    @pl.when(j == 0)
    def _():
        x_copy(i).wait()
        nb = _rms(xbuf_ref[...], g_ref[...]).astype(BF16)
        xn_ref[...] = nb
        dt_ref[...] = _dot_nt(nb, wdt_ref[...])

    @pl.when((j == 1) & (i + 1 < pl.num_programs(0)))
    def _():
        x_copy(i + 1).start()

    o_ref[...] = _dot_nt(xn_ref[...], wres_ref[rows, :])


def _in_proj(x, g, w_in_t, layer, w_dt_t, tm, cast=()):
    m = x.shape[0]
    n_i = m // tm
    n_j = D_MAIN // IN_TN
    assert n_j >= 2
    cast_in, cast_out, cast_shapes = [], [], []
    n_steps = n_i * n_j
    for w in cast:
        _, w_rows, w_cols = w.shape
        slab_rows = next(r for r in range(BF16_SUBLANES, w_rows + 1, BF16_SUBLANES)
                         if w_rows % r == 0 and w_rows // r <= n_steps)
        n_slabs = w_rows // slab_rows
        hold = n_steps // n_slabs
        slab = lambda i, j, n_slabs=n_slabs, hold=hold: jnp.minimum((i * n_j + j) // hold, n_slabs - 1)
        cast_in.append(pl.BlockSpec((None, slab_rows, w_cols), lambda i, j, slab=slab: (layer, slab(i, j), 0)))
        cast_out.append(pl.BlockSpec((slab_rows, w_cols), lambda i, j, slab=slab: (slab(i, j), 0)))
        cast_shapes.append(jax.ShapeDtypeStruct((w_rows, w_cols), BF16))
    return pl.pallas_call(
        functools.partial(_in_proj_kernel, tm=tm, n_cast=len(cast)),
        grid=(n_i, n_j),
        in_specs=[
            pl.BlockSpec(memory_space=pl.ANY),
            pl.BlockSpec((1, D_MODEL), lambda i, j: (0, 0)),
            pl.BlockSpec((None, IN_TN, D_MODEL), lambda i, j: (layer, jnp.where(i == 0, j, n_j - 1), 0)),
            pl.BlockSpec((HEAD_PAD, D_MODEL), lambda i, j: (0, 0)),
            *cast_in,
        ],
        out_specs=[
            pl.BlockSpec((tm, IN_TN), lambda i, j: (i, j)),
            pl.BlockSpec((tm, HEAD_PAD), lambda i, j: (i, 0)),
            *cast_out,
        ],
        out_shape=[
            jax.ShapeDtypeStruct((m, D_MAIN), F32),
            jax.ShapeDtypeStruct((m, HEAD_PAD), F32),
            *cast_shapes,
        ],
        scratch_shapes=[
            pltpu.VMEM((tm, D_MODEL), BF16),
            pltpu.VMEM((D_MAIN, D_MODEL), BF16),
            pltpu.VMEM((tm, D_MODEL), F32),
            pltpu.SemaphoreType.DMA(()),
        ],
        compiler_params=_params("arbitrary", "arbitrary"),
        name="in_proj",
    )(x, g, w_in_t, w_dt_t, *cast)


def _chunk_gate_kernel(u_ref, v_ref, gv_ref, ws_ref, bst_ref, go_ref, o_ref):
    vb =_rms(jax.nn.gelu(v_ref[...]), gv_ref[...]).astype(BF16)
    u = jax.nn.gelu(u_ref[...])
    row = lax.broadcasted_iota(jnp.int32, (CHUNK_LEN, CHUNK_LEN), 0)
    col = lax.broadcasted_iota(jnp.int32, (CHUNK_LEN, CHUNK_LEN), 1)
    causal = row >= col
    parts = []
    for h in range(CHUNK_HEADS):
        sl = slice(h * CHUNK_HEAD_DIM, (h + 1) * CHUNK_HEAD_DIM)
        w = jnp.where(causal, ws_ref[h], 0.0).astype(BF16)
        mixed = _dot(w, vb[:, sl]) + bst_ref[:, h:h + 1]
        parts.append(u[:, sl] * mixed)
    y = jnp.concatenate(parts, axis=1)
    o_ref[...] = _rms(y, go_ref[...]).astype(BF16)


def _chunk_gate(proj, gv, ws, bst, go):
    m = proj.shape[0]
    return pl.pallas_call(
        _chunk_gate_kernel,
        grid=(m // CHUNK_LEN,),
        in_specs=[
            pl.BlockSpec((CHUNK_LEN, D_CHUNK), lambda c: (c, U_BLK)),
            pl.BlockSpec((CHUNK_LEN, D_CHUNK), lambda c: (c, V_BLK)),
            pl.BlockSpec((1, D_CHUNK), lambda c: (0, 0)),
            pl.BlockSpec((CHUNK_HEADS, CHUNK_LEN, CHUNK_LEN), lambda c: (0, 0, 0)),
            pl.BlockSpec((CHUNK_LEN, CHUNK_HEADS), lambda c: (0, 0)),
            pl.BlockSpec((1, D_CHUNK), lambda c: (0, 0)),
        ],
        out_specs=pl.BlockSpec((CHUNK_LEN, D_CHUNK), lambda c: (c, 0)),
        out_shape=jax.ShapeDtypeStruct((m, D_CHUNK), BF16),
        compiler_params=_params("arbitrary"),
        name="chunk_gate",
    )(proj, proj, gv, ws, bst, go)


CONV_PAD = 8


def _group_norm_out(y, z, g):
    y = y * _silu(z)
    parts = []
    for grp in range(SSM_GROUPS):
        yg = y[:, grp * GROUP_WIDTH:(grp + 1) * GROUP_WIDTH]
        parts.append(yg * lax.rsqrt(jnp.mean(yg * yg, axis=-1, keepdims=True) + EPS))
    return (jnp.concatenate(parts, axis=1) * g).astype(BF16)


def _split3(x):
    p1 = x.astype(BF16)
    r1 = x - p1.astype(F32)
    p2 = r1.astype(BF16)
    p3 = (r1 - p2.astype(F32)).astype(BF16)
    return p1, p2, p3


def _select_rows(sel3, x):
    return _dot(sel3, jnp.concatenate(_split3(x), axis=0))


def _select_cols(x, sel3):
    return _dot(jnp.concatenate(_split3(x), axis=1), sel3)


def _ssd_chunk(first, xbc_ref, z_ref, dt_ref, cw_ref, cb_ref, dtb_ref, alog_ref, drep_ref, ng_ref,
               tril3_ref, rep3_ref, nconv_ref, nssm_ref, tail_ref, ht_ref, store_y):
    q = SSD_CHUNK

    xbc = xbc_ref[...]
    tail = jnp.where(first, 0.0, tail_ref[...])
    ht_prev = jnp.where(first, 0.0, ht_ref[...])
    sub = lax.broadcasted_iota(jnp.int32, (CONV_PAD, CONV_DIM), 0)
    conv = cb_ref[...] + xbc * cw_ref[CONV_W - 1:CONV_W, :]
    for k in range(1, CONV_W):
        shifted = pltpu.roll(xbc, k, 0)
        head = jnp.where(sub < k, pltpu.roll(tail, k, 0), shifted[0:CONV_PAD, :])
        shifted = jnp.concatenate([head, shifted[CONV_PAD:, :]], axis=0)
        conv = conv + shifted * cw_ref[CONV_W - 1 - k:CONV_W - k, :]
    tail_ref[...] = xbc[q - CONV_PAD:, :]
    nconv_ref[...] = xbc[q - (CONV_W - 1):, :]
    yield

    act = _silu(conv)
    xs = act[:, :D_SSM]
    bmat = act[:, D_SSM:D_SSM + SSM_GROUPS * D_STATE]
    cmat = act[:, D_SSM + SSM_GROUPS * D_STATE:]
    yield

    dt = _softplus(dt_ref[...] + dtb_ref[...])
    a = -jnp.exp(alog_ref[...])
    acum = _select_rows(tril3_ref[...], dt * a)
    acum_t = acum.T
    heads = jnp.concatenate([dt, jnp.exp(acum), jnp.exp(acum[q - 1:q, :] - acum)], axis=0)
    expanded = _select_cols(heads, rep3_ref[...])
    xdt = xs * expanded[0:q, :]
    exp_a = expanded[q:2 * q, :]
    xw = xdt * expanded[2 * q:3 * q, :]
    yield

    row = lax.broadcasted_iota(jnp.int32, (q, q), 0)
    col = lax.broadcasted_iota(jnp.int32, (q, q), 1)
    causal = row >= col
    low_half = col < SSM_HEAD_DIM
    ydiag, yoff, states = [], [], []
    for grp in range(SSM_GROUPS):
        bg_f = bmat[:, grp * D_STATE:(grp + 1) * D_STATE]
        bg = bg_f.astype(BF16)
        cg = cmat[:, grp * D_STATE:(grp + 1) * D_STATE].astype(BF16)
        cb = _dot_nt(cg, bg)
        ch = slice(grp * GROUP_WIDTH, (grp + 1) * GROUP_WIDTH)
        yoff.append(_dot(cg, ht_prev[:, ch].astype(BF16)))
        yield
        for pair in range(HEADS_PER_GROUP // 2):
            h0 = grp * HEADS_PER_GROUP + 2 * pair
            masks = []
            for h in (h0, h0 + 1):
                diff = acum[:, h:h + 1] - acum_t[h:h + 1, :]
                masks.append((cb * jnp.exp(jnp.where(causal, diff, -jnp.inf))).astype(BF16))
            xp = xdt[:, h0 * SSM_HEAD_DIM:(h0 + 2) * SSM_HEAD_DIM]
            rhs = jnp.concatenate([jnp.where(low_half, xp, 0.0), jnp.where(low_half, 0.0, xp)], axis=0)
            ydiag.append(_dot(jnp.concatenate(masks, axis=1), rhs.astype(BF16)))
            yield
        states.append(_dot(bg_f.T.astype(BF16), xw[:, ch].astype(BF16)))
        yield
    y = drep_ref[...] * xs + jnp.concatenate(ydiag, axis=1) + jnp.concatenate(yoff, axis=1) * exp_a

    ht_new = ht_prev * exp_a[q - 1:q, :] + jnp.concatenate(states, axis=1)
    ht_ref[...] = ht_new
    nssm_ref[...] = ht_new.T
    yield
    store_y(_group_norm_out(y, z_ref[...], ng_ref[...]))


def _sample_rows_kernel(u_ref, v_ref, xbc_ref, dt_ref, sc_ref, gv_ref, w0_ref, b0_ref, go_ref,
                        cw_ref, cb_ref, dtb_ref, alog_ref, drep_ref, rep_ref, rep8_ref,
                        ya_ref, vrows_ref, nconv_ref, split_ref, b_ref, c_ref, dx_ref):
    vr = _rms(jax.nn.gelu(v_ref[...]), gv_ref[...])
    vrows_ref[...] = vr
    mixed = vr * w0_ref[...] + b0_ref[...]
    ya_ref[...] = _rms(jax.nn.gelu(u_ref[...]) * mixed, go_ref[...]).astype(BF16)

    xbc = xbc_ref[...]
    conv = cb_ref[...]
    for k in range(CONV_W - 1):
        conv = conv + sc_ref[:, k * CONV_DIM:(k + 1) * CONV_DIM] * cw_ref[k:k + 1, :]
    conv = conv + xbc * cw_ref[CONV_W - 1:CONV_W, :]
    nconv_ref[:, 0:(CONV_W - 2) * CONV_DIM] = sc_ref[:, CONV_DIM:(CONV_W - 1) * CONV_DIM]
    nconv_ref[:, (CONV_W - 2) * CONV_DIM:] = xbc

    act = _silu(conv)
    xs = act[:, :D_SSM]
    b_ref[...] = act[:, D_SSM:D_SSM + SSM_GROUPS * D_STATE]
    c_ref[...] = act[:, D_SSM + SSM_GROUPS * D_STATE:]
    dx_ref[...] = drep_ref[...] * xs

    dt = _softplus(dt_ref[...] + dtb_ref[...])
    decay = jnp.exp(dt * (-jnp.exp(alog_ref[...])))
    xdt = xs * _dot_exact(dt, rep_ref[...])
    cols = jnp.concatenate([xdt, _dot_exact(decay, rep8_ref[...])], axis=1).T
    for p in range(N_SPLIT):
        piece = cols.astype(BF16)
        split_ref[:, p * LANES:(p + 1) * LANES] = piece
        cols = cols - piece.astype(F32)


SPLIT_ROWS = D_SSM + HEAD_PAD


def _sample_rows(proj, dt_raw, sconv, gv, w0, b0, go, cw, cb, dtb, alog, drep, rep, rep8):
    n = proj.shape[0]
    full = lambda shape: pl.BlockSpec(shape, lambda i: (0,) * len(shape))
    return pl.pallas_call(
        _sample_rows_kernel,
        grid=(1,),
        in_specs=[
            pl.BlockSpec((n, D_CHUNK), lambda i: (0, U_BLK)),
            pl.BlockSpec((n, D_CHUNK), lambda i: (0, V_BLK)),
            pl.BlockSpec((n, CONV_DIM), lambda i: (0, XBC_BLK)),
            full((n, HEAD_PAD)),
            full((n, (CONV_W - 1) * CONV_DIM)),
            full((1, D_CHUNK)), full((1, D_CHUNK)), full((1, D_CHUNK)), full((1, D_CHUNK)),
            full((CONV_W, CONV_DIM)), full((1, CONV_DIM)),
            full((1, HEAD_PAD)), full((1, HEAD_PAD)), full((1, D_SSM)),
            full((HEAD_PAD, D_SSM)), full((HEAD_PAD, HEAD_PAD)),
        ],
        out_specs=[
            full((n, D_CHUNK)), full((n, D_CHUNK)), full((n, (CONV_W - 1) * CONV_DIM)),
            full((SPLIT_ROWS, N_SPLIT * LANES)),
            full((n, SSM_GROUPS * D_STATE)), full((n, SSM_GROUPS * D_STATE)), full((n, D_SSM)),
        ],
        out_shape=[
            jax.ShapeDtypeStruct((n, D_CHUNK), BF16),
            jax.ShapeDtypeStruct((n, D_CHUNK), F32),
            jax.ShapeDtypeStruct((n, (CONV_W - 1) * CONV_DIM), F32),
            jax.ShapeDtypeStruct((SPLIT_ROWS, N_SPLIT * LANES), BF16),
            jax.ShapeDtypeStruct((n, SSM_GROUPS * D_STATE), F32),
            jax.ShapeDtypeStruct((n, SSM_GROUPS * D_STATE), F32),
            jax.ShapeDtypeStruct((n, D_SSM), F32),
        ],
        compiler_params=_params("arbitrary"),
        name="sample_rows",
    )(proj, proj, proj, dt_raw, sconv, gv, w0, b0, go, cw, cb, dtb, alog, drep, rep, rep8)


SAMPLE_TB = 8


def _sample_state_kernel(split_ref, b_ref, c_ref, h0_ref, dx_ref, z_ref, ng_ref,
                         hn_ref, y_ref, yt_ref):
    i = pl.program_id(0)
    n_tok = b_ref.shape[0]

    @pl.when(i == 0)
    def _():
        yt_ref[...] = jnp.zeros_like(yt_ref)

    tok = lax.broadcasted_iota(jnp.int32, (N_SPLIT * LANES, D_STATE), 0) % LANES
    lane = lax.broadcasted_iota(jnp.int32, (D_SSM, n_tok), 1)

    def per_group(row, grp):
        return jnp.broadcast_to(row[:, grp * D_STATE:(grp + 1) * D_STATE], (GROUP_WIDTH, D_STATE))

    for k in range(SAMPLE_TB):
        t = i * SAMPLE_TB + k
        onehot = (tok == t).astype(BF16)
        bcast = _dot(split_ref[...], onehot)
        decay = jnp.broadcast_to(
            bcast[D_SSM:, :].reshape(SSM_HEADS, 1, SUBLANES, D_STATE),
            (SSM_HEADS, SSM_HEAD_DIM // SUBLANES, SUBLANES, D_STATE)).reshape(D_SSM, D_STATE)
        brow = b_ref[pl.ds(t, 1), :]
        crow = c_ref[pl.ds(t, 1), :]
        bfull = jnp.concatenate([per_group(brow, g) for g in range(SSM_GROUPS)], axis=0)
        cfull = jnp.concatenate([per_group(crow, g) for g in range(SSM_GROUPS)], axis=0)
        h_new = h0_ref[k] * decay + bcast[:D_SSM, :] * bfull
        hn_ref[k] = h_new
        ysum = jnp.sum(h_new * cfull, axis=-1, keepdims=True)
        yt_ref[...] = jnp.where(lane == t, ysum, yt_ref[...])

    @pl.when(i == pl.num_programs(0) - 1)
    def _():
        y = yt_ref[...].T + dx_ref[...]
        y_ref[...] = _group_norm_out(y, z_ref[...], ng_ref[...])


def _sample_state(split, bmat, cmat, h0, dx, proj, ng):
    n = h0.shape[0]
    const2 = lambda i: (0, 0)
    return pl.pallas_call(
        _sample_state_kernel,
        grid=(n // SAMPLE_TB,),
        in_specs=[
            pl.BlockSpec((SPLIT_ROWS, N_SPLIT * LANES), const2),
            pl.BlockSpec((n, SSM_GROUPS * D_STATE), const2),
            pl.BlockSpec((n, SSM_GROUPS * D_STATE), const2),
            pl.BlockSpec((SAMPLE_TB, D_SSM, D_STATE), lambda i: (i, 0, 0)),
            pl.BlockSpec((n, D_SSM), const2),
            pl.BlockSpec((n, D_SSM), lambda i: (0, Z_BLK)),
            pl.BlockSpec((1, D_SSM), const2),
        ],
        out_specs=[
            pl.BlockSpec((SAMPLE_TB, D_SSM, D_STATE), lambda i: (i, 0, 0)),
            pl.BlockSpec((n, D_SSM), const2),
        ],
        out_shape=[
            jax.ShapeDtypeStruct((n, D_SSM, D_STATE), F32),
            jax.ShapeDtypeStruct((n, D_SSM), BF16),
        ],
        scratch_shapes=[pltpu.VMEM((D_SSM, n), F32)],
        compiler_params=_params("arbitrary"),
        name="sample_state",
    )(split, bmat, cmat, h0, dx, proj, ng)


OUT_TN = 1024


def _out_proj_kernel(x_ref, ya_ref, yb_ref, w_ref, o_ref, wres_ref):
    i = pl.program_id(0)
    j = pl.program_id(1)

    @pl.when(i == 0)
    def _():
        wres_ref[j] = w_ref[...].astype(BF16)

    y = jnp.concatenate([ya_ref[...], yb_ref[...]], axis=1)
    o_ref[...] = x_ref[...] + _dot(y, wres_ref[j])


def _out_proj(x, ya, yb, w_out, layer, tm):
    m = x.shape[0]
    n_j = D_MODEL // OUT_TN
    return pl.pallas_call(
        _out_proj_kernel,
        grid=(m // tm, n_j),
        in_specs=[
            pl.BlockSpec((tm, OUT_TN), lambda i, j: (i, j)),
            pl.BlockSpec((tm, D_CHUNK), lambda i, j: (i, 0)),
            pl.BlockSpec((tm, D_SSM), lambda i, j: (i, 0)),
            pl.BlockSpec((None, D_MODEL, OUT_TN), lambda i, j: (layer, 0, jnp.where(i == 0, j, n_j - 1))),
        ],
        out_specs=pl.BlockSpec((tm, OUT_TN), lambda i, j: (i, j)),
        out_shape=jax.ShapeDtypeStruct((m, D_MODEL), F32),
        scratch_shapes=[pltpu.VMEM((n_j, D_MODEL, OUT_TN), BF16)],
        compiler_params=_params("arbitrary", "arbitrary"),
        name="out_proj",
    )(x, ya, yb, w_out)


FFN_TF = 512


def _ffn_kernel(h_ref, g_ref, wg_ref, wu_ref, wd_ref, gf_ref, o_ref, m_ref):
    f = pl.program_id(1)

    @pl.when(f == 0)
    def _():
        h = h_ref[...]
        m_ref[...] = _rms(h, g_ref[...]).astype(BF16)
        o_ref[...] = h

    m = m_ref[...]
    act = (_silu(_dot(m, wg_ref[...])) * _dot(m, wu_ref[...])).astype(BF16)
    o_ref[...] += _dot(act, wd_ref[...])

    @pl.when(f == pl.num_programs(1) - 1)
    def _():
        o_ref[...] = _rms(o_ref[...], gf_ref[...])


def _ffn(h, g, w_gate, w_up, w_down, g_final, tm):
    m = h.shape[0]
    return pl.pallas_call(
        _ffn_kernel,
        grid=(m // tm, D_FF // FFN_TF),
        in_specs=[
            pl.BlockSpec((tm, D_MODEL), lambda i, f: (i, 0)),
            pl.BlockSpec((1, D_MODEL), lambda i, f: (0, 0)),
            pl.BlockSpec((D_MODEL, FFN_TF), lambda i, f: (0, f)),
            pl.BlockSpec((D_MODEL, FFN_TF), lambda i, f: (0, f)),
            pl.BlockSpec((FFN_TF, D_MODEL), lambda i, f: (f, 0)),
            pl.BlockSpec((1, D_MODEL), lambda i, f: (0, 0)),
        ],
        out_specs=pl.BlockSpec((tm, D_MODEL), lambda i, f: (i, 0)),
        out_shape=jax.ShapeDtypeStruct((m, D_MODEL), F32),
        scratch_shapes=[pltpu.VMEM((tm, D_MODEL), BF16)],
        compiler_params=_params("arbitrary", "arbitrary"),
        name="ffn",
    )(h, g, w_gate, w_up, w_down, g_final)


MIX_TM = 512
CHUNKS_PER_TILE = MIX_TM // SSD_CHUNK
FFN_UP_PIECE = 256
FFN_UP_K = 2048
FFN_DOWN_PIECE = 512
MIX_PHASES_PER_FFN_PIECE = 2


def _interleave(main, side, side_per_main):
    for _ in main:
        for _ in range(side_per_main):
            next(side, None)
    for _ in side:
        pass


def _mix_ffn_kernel(x_ref, ya_ref, wo_ref, g_ref, wg_ref, wu_ref, wd_ref, gf_ref,
                    xbc_ref, z_ref, dt_ref, cw_ref, cb_ref, dtb_ref, alog_ref, drep_ref, ng_ref,
                    tril3_ref, rep3_ref,
                    o_ref, nconv_ref, nssm_ref,
                    m_ref, yb_ref, tail_ref, ht_ref, *, chunks_per_seq):
    s = pl.program_id(0)
    f = pl.program_id(1)
    has_ffn = s >= 1
    has_mix = (s < pl.num_programs(0) - 1) & (f < CHUNKS_PER_TILE)

    @pl.when(has_ffn & (f == 0))
    def _():
        y = jnp.concatenate([ya_ref[...], yb_ref[...]], axis=1)
        h = x_ref[...] + _dot(y, wo_ref[...])
        m_ref[...] = _rms(h, g_ref[...]).astype(BF16)
        o_ref[...] = h

    def ffn_step():
        acts = []
        for c in range(FFN_TF // FFN_UP_PIECE):
            cols = slice(c * FFN_UP_PIECE, (c + 1) * FFN_UP_PIECE)
            halves = []
            for w_ref in (wg_ref, wu_ref):
                acc = None
                for k in range(D_MODEL // FFN_UP_K):
                    rows = slice(k * FFN_UP_K, (k + 1) * FFN_UP_K)
                    part = _dot(m_ref[:, rows], w_ref[rows, cols])
                    acc = part if acc is None else acc + part
                    yield
                halves.append(acc)
            acts.append((_silu(halves[0]) * halves[1]).astype(BF16))
        act = jnp.concatenate(acts, axis=1)
        for c in range(D_MODEL // FFN_DOWN_PIECE):
            cols = slice(c * FFN_DOWN_PIECE, (c + 1) * FFN_DOWN_PIECE)
            o_ref[:, cols] += _dot(act, wd_ref[:, cols])
            yield

    def store_y(yb):
        yb_ref[pl.ds(pl.multiple_of(f * SSD_CHUNK, SSD_CHUNK), SSD_CHUNK), :] = yb

    def mix_step():
        first = (s * CHUNKS_PER_TILE + f) % chunks_per_seq == 0
        return _ssd_chunk(first, xbc_ref, z_ref, dt_ref, cw_ref, cb_ref, dtb_ref, alog_ref, drep_ref,
                          ng_ref, tril3_ref, rep3_ref, nconv_ref, nssm_ref, tail_ref, ht_ref, store_y)

    @pl.when(has_ffn & has_mix)
    def _():
        _interleave(ffn_step(), mix_step(), MIX_PHASES_PER_FFN_PIECE)

    @pl.when(has_ffn & jnp.logical_not(has_mix))
    def _():
        _interleave(ffn_step(), iter(()), 0)

    @pl.when(jnp.logical_not(has_ffn) & has_mix)
    def _():
        _interleave(iter(()), mix_step(), 0)

    @pl.when(has_ffn & (f == pl.num_programs(1) - 1))
    def _():
        o_ref[...] = _rms(o_ref[...], gf_ref[...])


def _mix_ffn(x, ya, proj, dt_raw, w_out_b, g, w_gate, w_up, w_down, g_final,
             cw, cb, dtb, alog, drep, ng, tril3, rep3, batch, seq):
    m = x.shape[0]
    n_tiles = m // MIX_TM
    n_chunks = m // SSD_CHUNK
    chunks_per_seq = seq // SSD_CHUNK
    assert MIX_TM % SSD_CHUNK == 0 and seq % MIX_TM == 0 and CHUNKS_PER_TILE <= D_FF // FFN_TF

    def tile(s, f):
        return jnp.maximum(s - 1, 0)

    def chunk(s, f):
        return jnp.where(s >= n_tiles, n_chunks - 1,
                         s * CHUNKS_PER_TILE + jnp.minimum(f, CHUNKS_PER_TILE - 1))

    def seq_of(s, f):
        return chunk(s, f) // chunks_per_seq

    const2 = lambda s, f: (0, 0)
    return pl.pallas_call(
        functools.partial(_mix_ffn_kernel, chunks_per_seq=chunks_per_seq),
        grid=(n_tiles + 1, D_FF // FFN_TF),
        in_specs=[
            pl.BlockSpec((MIX_TM, D_MODEL), lambda s, f: (tile(s, f), 0)),
            pl.BlockSpec((MIX_TM, D_CHUNK), lambda s, f: (tile(s, f), 0)),
            pl.BlockSpec((D_MODEL, D_MODEL), const2, pipeline_mode=pl.Buffered(1)),
            pl.BlockSpec((1, D_MODEL), const2),
            pl.BlockSpec((D_MODEL, FFN_TF), lambda s, f: (0, f)),
            pl.BlockSpec((D_MODEL, FFN_TF), lambda s, f: (0, f)),
            pl.BlockSpec((FFN_TF, D_MODEL), lambda s, f: (f, 0)),
            pl.BlockSpec((1, D_MODEL), const2),
            pl.BlockSpec((SSD_CHUNK, CONV_DIM), lambda s, f: (chunk(s, f), XBC_BLK)),
            pl.BlockSpec((SSD_CHUNK, D_SSM), lambda s, f: (chunk(s, f), Z_BLK)),
            pl.BlockSpec((SSD_CHUNK, HEAD_PAD), lambda s, f: (chunk(s, f), 0)),
            pl.BlockSpec((CONV_W, CONV_DIM), const2),
            pl.BlockSpec((1, CONV_DIM), const2),
            pl.BlockSpec((1, HEAD_PAD), const2),
            pl.BlockSpec((1, HEAD_PAD), const2),
            pl.BlockSpec((1, D_SSM), const2),
            pl.BlockSpec((1, D_SSM), const2),
            pl.BlockSpec((SSD_CHUNK, N_SPLIT * SSD_CHUNK), const2),
            pl.BlockSpec((N_SPLIT * HEAD_PAD, D_SSM), const2),
        ],
        out_specs=[
            pl.BlockSpec((MIX_TM, D_MODEL), lambda s, f: (tile(s, f), 0)),
            pl.BlockSpec((None, CONV_W - 1, CONV_DIM), lambda s, f: (seq_of(s, f), 0, 0)),
            pl.BlockSpec((None, D_SSM, D_STATE), lambda s, f: (seq_of(s, f), 0, 0)),
        ],
        out_shape=[
            jax.ShapeDtypeStruct((m, D_MODEL), F32),
            jax.ShapeDtypeStruct((batch, CONV_W - 1, CONV_DIM), F32),
            jax.ShapeDtypeStruct((batch, D_SSM, D_STATE), F32),
        ],
        scratch_shapes=[
            pltpu.VMEM((MIX_TM, D_MODEL), BF16),
            pltpu.VMEM((MIX_TM, D_SSM), BF16),
            pltpu.VMEM((CONV_PAD, CONV_DIM), F32),
            pltpu.VMEM((D_STATE, D_SSM), F32),
        ],
        compiler_params=_params("arbitrary", "arbitrary"),
        name="mix_ffn",
    )(x, ya, w_out_b, g, w_gate, w_up, w_down, g_final, proj, proj, dt_raw,
      cw, cb, dtb, alog, drep, ng, tril3, rep3)


def _row(v, width=None):
    v = v.astype(F32).reshape(1, -1)
    if width is not None and v.shape[1] < width:
        v = jnp.pad(v, ((0, 0), (0, width - v.shape[1])))
    return v


def kernel(x_prompt, x_sample, state_conv, state_ssm, norm_mix_g, w_in, chunk_v_norm_g, chunk_w_s,
           chunk_b_s, chunk_out_norm_g, ssd_conv_w, ssd_conv_b, ssd_dt_bias, ssd_a_log, ssd_d,
           ssd_norm_g, w_out, norm_ffn_g, w_gate, w_up, w_down, norm_final_g):
    depth = w_in.shape[0]
    batch, seq, _ = x_prompt.shape
    n_s = x_sample.shape[0]
    assert x_sample.shape[1] == 1 and seq % SSD_CHUNK == 0 and depth == 1

    rep = (lax.broadcasted_iota(jnp.int32, (HEAD_PAD, D_SSM), 0)
           == lax.broadcasted_iota(jnp.int32, (HEAD_PAD, D_SSM), 1) // SSM_HEAD_DIM).astype(F32)
    rep8 = (lax.broadcasted_iota(jnp.int32, (HEAD_PAD, HEAD_PAD), 0)
            == lax.broadcasted_iota(jnp.int32, (HEAD_PAD, HEAD_PAD), 1) // SUBLANES).astype(F32)
    rep3 = jnp.tile(rep, (N_SPLIT, 1)).astype(BF16)
    tril3 = jnp.tile(jnp.tril(jnp.ones((SSD_CHUNK, SSD_CHUNK), F32)), (1, N_SPLIT)).astype(BF16)

    hp = x_prompt.reshape(batch * seq, D_MODEL)
    hs = x_sample.reshape(n_s, D_MODEL)
    l = 0
    w_in_t = jnp.swapaxes(w_in, 1, 2)
    w_dt_t = jnp.pad(w_in_t[l, D_MAIN:, :], ((0, HEAD_PAD - SSM_HEADS), (0, 0))).astype(BF16)
    g_mix = _row(norm_mix_g[l])
    gv = _row(chunk_v_norm_g[l])
    go = _row(chunk_out_norm_g[l])
    cw = ssd_conv_w[l].astype(F32)
    cb = _row(ssd_conv_b[l])
    dtb = _row(ssd_dt_bias[l], HEAD_PAD)
    alog = _row(ssd_a_log[l], HEAD_PAD)
    drep = _row(jnp.repeat(ssd_d[l], SSM_HEAD_DIM))
    ng = _row(ssd_norm_g[l])
    g_ffn = _row(norm_ffn_g[l])
    g_final = _row(norm_final_g)

    proj_p, dt_p, w_gate_b, w_up_b, w_down_b, w_out_b = _in_proj(
        hp, g_mix, w_in_t, l, w_dt_t, tm=1024, cast=(w_gate, w_up, w_down, w_out))
    ya_p = _chunk_gate(proj_p, gv, chunk_w_s[l].astype(F32), chunk_b_s[l].astype(F32).T, go)
    y_p, conv_p, ssm_p = _mix_ffn(hp, ya_p, proj_p, dt_p, w_out_b, g_ffn, w_gate_b, w_up_b, w_down_b,
                                  g_final, cw, cb, dtb, alog, drep, ng, tril3, rep3, batch, seq)

    proj_s, dt_s = _in_proj(hs, g_mix, w_in_t, l, w_dt_t, tm=n_s)
    w0 = _row(jnp.repeat(chunk_w_s[l, :, 0, 0], CHUNK_HEAD_DIM))
    b0 = _row(jnp.repeat(chunk_b_s[l, :, 0], CHUNK_HEAD_DIM))
    ya_s, v_s, conv_s, split, b_s, c_s, dx_s = _sample_rows(
        proj_s, dt_s, state_conv[l].reshape(n_s, (CONV_W - 1) * CONV_DIM), gv, w0, b0, go,
        cw, cb, dtb, alog, drep, rep, rep8)
    ssm_s, yb_s = _sample_state(split, b_s, c_s, state_ssm[l].reshape(n_s, D_SSM, D_STATE),
                                dx_s, proj_s, ng)
    h_s = _out_proj(hs, ya_s, yb_s, w_out, l, tm=n_s)
    y_s = _ffn(h_s, g_ffn, w_gate_b, w_up_b, w_down_b, g_final, tm=n_s)

    return (
        y_p.reshape(batch, seq, D_MODEL),
        y_s.reshape(n_s, 1, D_MODEL),
        conv_p[None],
        ssm_p.reshape(1, batch, SSM_HEADS, SSM_HEAD_DIM, D_STATE),
        conv_s.reshape(1, n_s, CONV_W - 1, CONV_DIM),
        ssm_s.reshape(1, n_s, SSM_HEADS, SSM_HEAD_DIM, D_STATE),
        v_s.reshape(1, n_s, 1, D_CHUNK),
    )
```

```python
import functools

import jax
import jax.numpy as jnp
from jax import lax
from jax.experimental import pallas as pl
from jax.experimental.pallas import tpu as pltpu

F32 = jnp.float32
BF16 = jnp.bfloat16
HIGHEST = lax.Precision.HIGHEST

D_MODEL = 2048
D_CHUNK = 1024
CHUNK_HEADS = 8
CHUNK_HEAD_DIM = 128
CHUNK_LEN = 128
D_SSM = 1024
SSM_HEAD_DIM = 64
SSM_HEADS = 16
SSM_GROUPS = 2
HEADS_PER_GROUP = SSM_HEADS // SSM_GROUPS
GROUP_WIDTH = D_SSM // SSM_GROUPS
D_STATE = 128
CONV_W = 4
CONV_DIM = D_SSM + 2 * SSM_GROUPS * D_STATE
SSD_CHUNK = 128
D_MAIN = 2 * D_CHUNK + D_SSM + CONV_DIM
D_FF = 5632
EPS = 1e-6

LANES = 128
SUBLANES = 8
BF16_SUBLANES = 16
HEAD_PAD = LANES
N_SPLIT = 3
VMEM_LIMIT = 56 * 1024 * 1024

U_BLK, V_BLK, Z_BLK = 0, 1, 2
XBC_BLK = 2


def _rms(x, g):
    return x * lax.rsqrt(jnp.mean(x * x, axis=-1, keepdims=True) + EPS) * g


def _silu(x):
    return x * jax.nn.sigmoid(x)


def _softplus(x):
    return jnp.maximum(x, 0.0) + jnp.log1p(jnp.exp(-jnp.abs(x)))


def _dot(a, b):
    return jnp.dot(a, b, preferred_element_type=F32)


def _dot_nt(a, b):
    return lax.dot_general(a, b, (((1,), (1,)), ((), ())), preferred_element_type=F32)


def _dot_exact(a, b):
    return jnp.dot(a, b, precision=HIGHEST, preferred_element_type=F32)


def _params(*semantics):
    return pltpu.CompilerParams(dimension_semantics=semantics, vmem_limit_bytes=VMEM_LIMIT)


IN_TN = 512


def _in_proj_kernel(x_hbm, g_ref, w_ref, wdt_ref, *refs, tm, n_cast):
    cast_in = refs[:n_cast]
    o_ref, dt_ref = refs[n_cast:n_cast + 2]
    cast_out = refs[n_cast + 2:2 * n_cast + 2]
    xn_ref, wres_ref, xbuf_ref, sem = refs[2 * n_cast + 2:]
    i = pl.program_id(0)
    j = pl.program_id(1)
    rows = pl.ds(pl.multiple_of(j * IN_TN, IN_TN), IN_TN)

    for src, dst in zip(cast_in, cast_out, strict=True):
        dst[...] = src[...].astype(BF16)

    def x_copy(tile):
        return pltpu.make_async_copy(x_hbm.at[pl.ds(tile * tm, tm), :], xbuf_ref, sem)

    @pl.when((i == 0) & (j == 0))
    def _():
        x_copy(0).start()

    @pl.when(i == 0)
    def _():
        wres_ref[rows, :] = w_ref[...].astype(BF16)

    @pl.when(j == 0)
    def _():
        x_copy(i).wait()
        nb = _rms(xbuf_ref[...], g_ref[...]).astype(BF16)
        xn_ref[...] = nb
        dt_ref[...] = _dot_nt(nb, wdt_ref[...])

    @pl.when((j == 1) & (i + 1 < pl.num_programs(0)))
    def _():
        x_copy(i + 1).start()

    o_ref[...] = _dot_nt(xn_ref[...], wres_ref[rows, :])


def _in_proj(x, g, w_in_t, layer, w_dt_t, tm, cast=()):
    m = x.shape[0]
    n_i = m // tm
    n_j = D_MAIN // IN_TN
    assert n_j >= 2
    cast_in, cast_out, cast_shapes = [], [], []
    n_steps = n_i * n_j
    for w in cast:
        _, w_rows, w_cols = w.shape
        slab_rows = next(r for r in range(BF16_SUBLANES, w_rows + 1, BF16_SUBLANES)
                         if w_rows % r == 0 and w_rows // r <= n_steps)
        n_slabs = w_rows // slab_rows
        hold = n_steps // n_slabs
        slab = lambda i, j, n_slabs=n_slabs, hold=hold: jnp.minimum((i * n_j + j) // hold, n_slabs - 1)
        cast_in.append(pl.BlockSpec((None, slab_rows, w_cols), lambda i, j, slab=slab: (layer, slab(i, j), 0)))
        cast_out.append(pl.BlockSpec((slab_rows, w_cols), lambda i, j, slab=slab: (slab(i, j), 0)))
        cast_shapes.append(jax.ShapeDtypeStruct((w_rows, w_cols), BF16))
    return pl.pallas_call(
        functools.partial(_in_proj_kernel, tm=tm, n_cast=len(cast)),
        grid=(n_i, n_j),
        in_specs=[
            pl.BlockSpec(memory_space=pl.ANY),
            pl.BlockSpec((1, D_MODEL), lambda i, j: (0, 0)),
            pl.BlockSpec((None, IN_TN, D_MODEL), lambda i, j: (layer, jnp.where(i == 0, j, n_j - 1), 0)),
            pl.BlockSpec((HEAD_PAD, D_MODEL), lambda i, j: (0, 0)),
            *cast_in,
        ],
        out_specs=[
            pl.BlockSpec((tm, IN_TN), lambda i, j: (i, j)),
            pl.BlockSpec((tm, HEAD_PAD), lambda i, j: (i, 0)),
            *cast_out,
        ],
        out_shape=[
            jax.ShapeDtypeStruct((m, D_MAIN), F32),
            jax.ShapeDtypeStruct((m, HEAD_PAD), F32),
            *cast_shapes,
        ],
        scratch_shapes=[
            pltpu.VMEM((tm, D_MODEL), BF16),
            pltpu.VMEM((D_MAIN, D_MODEL), BF16),
            pltpu.VMEM((tm, D_MODEL), F32),
            pltpu.SemaphoreType.DMA(()),
        ],
        compiler_params=_params("arbitrary", "arbitrary"),
        name="in_proj",
    )(x, g, w_in_t, w_dt_t, *cast)


def _chunk_gate(u_ref, v_ref, gv_ref, ws_ref, bst_ref, go_ref, store_y):
    vb = _rms(jax.nn.gelu(v_ref[...]), gv_ref[...]).astype(BF16)
    yield
    u = jax.nn.gelu(u_ref[...])
    yield
    row = lax.broadcasted_iota(jnp.int32, (CHUNK_LEN, CHUNK_LEN), 0)
    col = lax.broadcasted_iota(jnp.int32, (CHUNK_LEN, CHUNK_LEN), 1)
    causal = row >= col
    zeros = jnp.zeros((CHUNK_LEN, CHUNK_HEAD_DIM), BF16)
    parts = []
    for h0 in range(0, CHUNK_HEADS, 2):
        w = jnp.concatenate([jnp.where(causal, ws_ref[h], 0.0).astype(BF16) for h in (h0, h0 + 1)], axis=1)
        va = vb[:, h0 * CHUNK_HEAD_DIM:(h0 + 1) * CHUNK_HEAD_DIM]
        vb2 = vb[:, (h0 + 1) * CHUNK_HEAD_DIM:(h0 + 2) * CHUNK_HEAD_DIM]
        rhs = jnp.concatenate([jnp.concatenate([va, zeros], axis=1),
                               jnp.concatenate([zeros, vb2], axis=1)], axis=0)
        mixed = _dot(w, rhs)
        for k, h in enumerate((h0, h0 + 1)):
            sl = slice(h * CHUNK_HEAD_DIM, (h + 1) * CHUNK_HEAD_DIM)
            mh = mixed[:, k * CHUNK_HEAD_DIM:(k + 1) * CHUNK_HEAD_DIM] + bst_ref[:, h:h + 1]
            parts.append(u[:, sl] * mh)
        yield
    y = jnp.concatenate(parts, axis=1)
    store_y(_rms(y, go_ref[...]).astype(BF16))


CONV_PAD = 8


def _group_norm_out(y, z, g):
    y = y * _silu(z)
    parts = []
    for grp in range(SSM_GROUPS):
        yg = y[:, grp * GROUP_WIDTH:(grp + 1) * GROUP_WIDTH]
        parts.append(yg * lax.rsqrt(jnp.mean(yg * yg, axis=-1, keepdims=True) + EPS))
    return (jnp.concatenate(parts, axis=1) * g).astype(BF16)


def _split3(x):
    p1 = x.astype(BF16)
    r1 = x - p1.astype(F32)
    p2 = r1.astype(BF16)
    p3 = (r1 - p2.astype(F32)).astype(BF16)
    return p1, p2, p3


def _select_rows(sel3, x):
    return _dot(sel3, jnp.concatenate(_split3(x), axis=0))


def _select_cols(x, sel3):
    return _dot(jnp.concatenate(_split3(x), axis=1), sel3)


def _ssd_chunk(first, xbc_ref, z_ref, dt_ref, cw_ref, cb_ref, dtb_ref, alog_ref, drep_ref, ng_ref,
               tril3_ref, rep3_ref, nconv_ref, nssm_ref, tail_ref, ht_ref, store_y):
    q = SSD_CHUNK

    xbc = xbc_ref[...]
    tail = jnp.where(first, 0.0, tail_ref[...])
    ht_prev = jnp.where(first, 0.0, ht_ref[...])
    sub = lax.broadcasted_iota(jnp.int32, (CONV_PAD, CONV_DIM), 0)
    conv = cb_ref[...] + xbc * cw_ref[CONV_W - 1:CONV_W, :]
    for k in range(1, CONV_W):
        shifted = pltpu.roll(xbc, k, 0)
        head = jnp.where(sub < k, pltpu.roll(tail, k, 0), shifted[0:CONV_PAD, :])
        shifted = jnp.concatenate([head, shifted[CONV_PAD:, :]], axis=0)
        conv = conv + shifted * cw_ref[CONV_W - 1 - k:CONV_W - k, :]
    tail_ref[...] = xbc[q - CONV_PAD:, :]
    nconv_ref[...] = xbc[q - (CONV_W - 1):, :]
    yield

    act = _silu(conv)
    xs = act[:, :D_SSM]
    bmat = act[:, D_SSM:D_SSM + SSM_GROUPS * D_STATE]
    cmat = act[:, D_SSM + SSM_GROUPS * D_STATE:]
    yield

    dt = _softplus(dt_ref[...] + dtb_ref[...])
    a = -jnp.exp(alog_ref[...])
    acum = _select_rows(tril3_ref[...], dt * a)
    acum_t = acum.T
    expanded = _select_cols(jnp.concatenate([dt, acum], axis=0), rep3_ref[...])
    yield
    xdt = xs * expanded[0:q, :]
    acum_rep = expanded[q:2 * q, :]
    exp_a = jnp.exp(acum_rep)
    xw = xdt * jnp.exp(acum_rep[q - 1:q, :] - acum_rep)
    yield

    row = lax.broadcasted_iota(jnp.int32, (q, q), 0)
    col = lax.broadcasted_iota(jnp.int32, (q, q), 1)
    causal = row >= col
    low_half = col < SSM_HEAD_DIM
    ydiag, yoff, states = [], [], []
    for grp in range(SSM_GROUPS):
        bg_f = bmat[:, grp * D_STATE:(grp + 1) * D_STATE]
        bg = bg_f.astype(BF16)
        cg = cmat[:, grp * D_STATE:(grp + 1) * D_STATE].astype(BF16)
        cb = _dot_nt(cg, bg)
        ch = slice(grp * GROUP_WIDTH, (grp + 1) * GROUP_WIDTH)
        yoff.append(_dot(cg, ht_prev[:, ch].astype(BF16)))
        yield
        for quad in range(HEADS_PER_GROUP // 4):
            h0 = grp * HEADS_PER_GROUP + 4 * quad
            masks, blocks = [], []
            for k, h in enumerate(range(h0, h0 + 4)):
                diff = acum[:, h:h + 1] - acum_t[h:h + 1, :]
                masks.append((cb * jnp.exp(jnp.where(causal, diff, -jnp.inf))).astype(BF16))
                xp = xdt[:, (h - k % 2) * SSM_HEAD_DIM:(h - k % 2 + 2) * SSM_HEAD_DIM]
                own = jnp.where(low_half == (k % 2 == 0), xp, 0.0).astype(BF16)
                zero = jnp.zeros_like(own)
                blocks.append(jnp.concatenate([own, zero] if k < 2 else [zero, own], axis=1))
                if k % 2 == 1:
                    yield
            ydiag.append(_dot(jnp.concatenate(masks, axis=1), jnp.concatenate(blocks, axis=0)))
            yield
        states.append(_dot(bg_f.T.astype(BF16), xw[:, ch].astype(BF16)))
        yield
    y = drep_ref[...] * xs + jnp.concatenate(ydiag, axis=1) + jnp.concatenate(yoff, axis=1) * exp_a

    ht_new = ht_prev * exp_a[q - 1:q, :] + jnp.concatenate(states, axis=1)
    ht_ref[...] = ht_new
    nssm_ref[...] = ht_new.T
    yield
    store_y(_group_norm_out(y, z_ref[...], ng_ref[...]))


def _sample_rows_kernel(u_ref, v_ref, xbc_ref, dt_ref, sc_ref, gv_ref, w0_ref, b0_ref, go_ref,
                        cw_ref, cb_ref, dtb_ref, alog_ref, drep_ref, rep_ref, rep8_ref,
                        ya_ref, vrows_ref, nconv_ref, split_ref, b_ref, c_ref, dx_ref):
    vr = _rms(jax.nn.gelu(v_ref[...]), gv_ref[...])
    vrows_ref[...] = vr
    mixed = vr * w0_ref[...] + b0_ref[...]
    ya_ref[...] = _rms(jax.nn.gelu(u_ref[...]) * mixed, go_ref[...]).astype(BF16)

    xbc = xbc_ref[...]
    conv = cb_ref[...]
    for k in range(CONV_W - 1):
        conv = conv + sc_ref[:, k * CONV_DIM:(k + 1) * CONV_DIM] * cw_ref[k:k + 1, :]
    conv = conv + xbc * cw_ref[CONV_W - 1:CONV_W, :]
    nconv_ref[:, 0:(CONV_W - 2) * CONV_DIM] = sc_ref[:, CONV_DIM:(CONV_W - 1) * CONV_DIM]
    nconv_ref[:, (CONV_W - 2) * CONV_DIM:] = xbc

    act = _silu(conv)
    xs = act[:, :D_SSM]
    b_ref[...] = act[:, D_SSM:D_SSM + SSM_GROUPS * D_STATE]
    c_ref[...] = act[:, D_SSM + SSM_GROUPS * D_STATE:]
    dx_ref[...] = drep_ref[...] * xs

    dt = _softplus(dt_ref[...] + dtb_ref[...])
    decay = jnp.exp(dt * (-jnp.exp(alog_ref[...])))
    xdt = xs * _dot_exact(dt, rep_ref[...])
    cols = jnp.concatenate([xdt, _dot_exact(decay, rep8_ref[...])], axis=1).T
    for p in range(N_SPLIT):
        piece = cols.astype(BF16)
        split_ref[:, p * LANES:(p + 1) * LANES] = piece
        cols = cols - piece.astype(F32)


SPLIT_ROWS = D_SSM + HEAD_PAD


def _sample_rows(proj, dt_raw, sconv, gv, w0, b0, go, cw, cb, dtb, alog, drep, rep, rep8):
    n = proj.shape[0]
    full = lambda shape: pl.BlockSpec(shape, lambda i: (0,) * len(shape))
    return pl.pallas_call(
        _sample_rows_kernel,
        grid=(1,),
        in_specs=[
            pl.BlockSpec((n, D_CHUNK), lambda i: (0, U_BLK)),
            pl.BlockSpec((n, D_CHUNK), lambda i: (0, V_BLK)),
            pl.BlockSpec((n, CONV_DIM), lambda i: (0, XBC_BLK)),
            full((n, HEAD_PAD)),
            full((n, (CONV_W - 1) * CONV_DIM)),
            full((1, D_CHUNK)), full((1, D_CHUNK)), full((1, D_CHUNK)), full((1, D_CHUNK)),
            full((CONV_W, CONV_DIM)), full((1, CONV_DIM)),
            full((1, HEAD_PAD)), full((1, HEAD_PAD)), full((1, D_SSM)),
            full((HEAD_PAD, D_SSM)), full((HEAD_PAD, HEAD_PAD)),
        ],
        out_specs=[
            full((n, D_CHUNK)), full((n, D_CHUNK)), full((n, (CONV_W - 1) * CONV_DIM)),
            full((SPLIT_ROWS, N_SPLIT * LANES)),
            full((n, SSM_GROUPS * D_STATE)), full((n, SSM_GROUPS * D_STATE)), full((n, D_SSM)),
        ],
        out_shape=[
            jax.ShapeDtypeStruct((n, D_CHUNK), BF16),
            jax.ShapeDtypeStruct((n, D_CHUNK), F32),
            jax.ShapeDtypeStruct((n, (CONV_W - 1) * CONV_DIM), F32),
            jax.ShapeDtypeStruct((SPLIT_ROWS, N_SPLIT * LANES), BF16),
            jax.ShapeDtypeStruct((n, SSM_GROUPS * D_STATE), F32),
            jax.ShapeDtypeStruct((n, SSM_GROUPS * D_STATE), F32),
            jax.ShapeDtypeStruct((n, D_SSM), F32),
        ],
        compiler_params=_params("arbitrary"),
        name="sample_rows",
    )(proj, proj, proj, dt_raw, sconv, gv, w0, b0, go, cw, cb, dtb, alog, drep, rep, rep8)


SAMPLE_TB = 8


def _sample_state_kernel(split_ref, b_ref, c_ref, h0_ref, dx_ref, z_ref, ng_ref,
                         hn_ref, y_ref, yt_ref):
    i = pl.program_id(0)
    n_tok = b_ref.shape[0]

    @pl.when(i == 0)
    def _():
        yt_ref[...] = jnp.zeros_like(yt_ref)

    tok = lax.broadcasted_iota(jnp.int32, (N_SPLIT * LANES, D_STATE), 0) % LANES
    lane = lax.broadcasted_iota(jnp.int32, (D_SSM, n_tok), 1)

    def per_group(row, grp):
        return jnp.broadcast_to(row[:, grp * D_STATE:(grp + 1) * D_STATE], (GROUP_WIDTH, D_STATE))

    for k in range(SAMPLE_TB):
        t = i * SAMPLE_TB + k
        onehot = (tok == t).astype(BF16)
        bcast = _dot(split_ref[...], onehot)
        decay = jnp.broadcast_to(
            bcast[D_SSM:, :].reshape(SSM_HEADS, 1, SUBLANES, D_STATE),
            (SSM_HEADS, SSM_HEAD_DIM // SUBLANES, SUBLANES, D_STATE)).reshape(D_SSM, D_STATE)
        brow = b_ref[pl.ds(t, 1), :]
        crow = c_ref[pl.ds(t, 1), :]
        bfull = jnp.concatenate([per_group(brow, g) for g in range(SSM_GROUPS)], axis=0)
        cfull = jnp.concatenate([per_group(crow, g) for g in range(SSM_GROUPS)], axis=0)
        h_new = h0_ref[k] * decay + bcast[:D_SSM, :] * bfull
        hn_ref[k] = h_new
        ysum = jnp.sum(h_new * cfull, axis=-1, keepdims=True)
        yt_ref[...] = jnp.where(lane == t, ysum, yt_ref[...])

    @pl.when(i == pl.num_programs(0) - 1)
    def _():
        y = yt_ref[...].T + dx_ref[...]
        y_ref[...] = _group_norm_out(y, z_ref[...], ng_ref[...])


def _sample_state(split, bmat, cmat, h0, dx, proj, ng):
    n = h0.shape[0]
    const2 = lambda i: (0, 0)
    return pl.pallas_call(
        _sample_state_kernel,
        grid=(n // SAMPLE_TB,),
        in_specs=[
            pl.BlockSpec((SPLIT_ROWS, N_SPLIT * LANES), const2),
            pl.BlockSpec((n, SSM_GROUPS * D_STATE), const2),
            pl.BlockSpec((n, SSM_GROUPS * D_STATE), const2),
            pl.BlockSpec((SAMPLE_TB, D_SSM, D_STATE), lambda i: (i, 0, 0)),
            pl.BlockSpec((n, D_SSM), const2),
            pl.BlockSpec((n, D_SSM), lambda i: (0, Z_BLK)),
            pl.BlockSpec((1, D_SSM), const2),
        ],
        out_specs=[
            pl.BlockSpec((SAMPLE_TB, D_SSM, D_STATE), lambda i: (i, 0, 0)),
            pl.BlockSpec((n, D_SSM), const2),
        ],
        out_shape=[
            jax.ShapeDtypeStruct((n, D_SSM, D_STATE), F32),
            jax.ShapeDtypeStruct((n, D_SSM), BF16),
        ],
        scratch_shapes=[pltpu.VMEM((D_SSM, n), F32)],
        compiler_params=_params("arbitrary"),
        name="sample_state",
    )(split, bmat, cmat, h0, dx, proj, ng)


OUT_TN = 1024


def _out_proj_kernel(x_ref, ya_ref, yb_ref, w_ref, o_ref, wres_ref):
    i = pl.program_id(0)
    j = pl.program_id(1)

    @pl.when(i == 0)
    def _():
        wres_ref[j] = w_ref[...].astype(BF16)

    y = jnp.concatenate([ya_ref[...], yb_ref[...]], axis=1)
    o_ref[...] = x_ref[...] + _dot(y, wres_ref[j])


def _out_proj(x, ya, yb, w_out, layer, tm):
    m = x.shape[0]
    n_j = D_MODEL // OUT_TN
    return pl.pallas_call(
        _out_proj_kernel,
        grid=(m // tm, n_j),
        in_specs=[
            pl.BlockSpec((tm, OUT_TN), lambda i, j: (i, j)),
            pl.BlockSpec((tm, D_CHUNK), lambda i, j: (i, 0)),
            pl.BlockSpec((tm, D_SSM), lambda i, j: (i, 0)),
            pl.BlockSpec((None, D_MODEL, OUT_TN), lambda i, j: (layer, 0, jnp.where(i == 0, j, n_j - 1))),
        ],
        out_specs=pl.BlockSpec((tm, OUT_TN), lambda i, j: (i, j)),
        out_shape=jax.ShapeDtypeStruct((m, D_MODEL), F32),
        scratch_shapes=[pltpu.VMEM((n_j, D_MODEL, OUT_TN), BF16)],
        compiler_params=_params("arbitrary", "arbitrary"),
        name="out_proj",
    )(x, ya, yb, w_out)


FFN_TF = 512


def _ffn_kernel(h_ref, g_ref, wg_ref, wu_ref, wd_ref, gf_ref, o_ref, m_ref):
    f = pl.program_id(1)

    @pl.when(f == 0)
    def _():
        h = h_ref[...]
        m_ref[...] = _rms(h, g_ref[...]).astype(BF16)
        o_ref[...] = h

    m = m_ref[...]
    act = (_silu(_dot(m, wg_ref[...])) * _dot(m, wu_ref[...])).astype(BF16)
    o_ref[...] += _dot(act, wd_ref[...])

    @pl.when(f == pl.num_programs(1) - 1)
    def _():
        o_ref[...] = _rms(o_ref[...], gf_ref[...])


def _ffn(h, g, w_gate, w_up, w_down, g_final, tm):
    m = h.shape[0]
    return pl.pallas_call(
        _ffn_kernel,
        grid=(m // tm, D_FF // FFN_TF),
        in_specs=[
            pl.BlockSpec((tm, D_MODEL), lambda i, f: (i, 0)),
            pl.BlockSpec((1, D_MODEL), lambda i, f: (0, 0)),
            pl.BlockSpec((D_MODEL, FFN_TF), lambda i, f: (0, f)),
            pl.BlockSpec((D_MODEL, FFN_TF), lambda i, f: (0, f)),
            pl.BlockSpec((FFN_TF, D_MODEL), lambda i, f: (f, 0)),
            pl.BlockSpec((1, D_MODEL), lambda i, f: (0, 0)),
        ],
        out_specs=pl.BlockSpec((tm, D_MODEL), lambda i, f: (i, 0)),
        out_shape=jax.ShapeDtypeStruct((m, D_MODEL), F32),
        scratch_shapes=[pltpu.VMEM((tm, D_MODEL), BF16)],
        compiler_params=_params("arbitrary", "arbitrary"),
        name="ffn",
    )(h, g, w_gate, w_up, w_down, g_final)


MIX_TM = 512
CHUNKS_PER_TILE = MIX_TM // SSD_CHUNK
FFN_UP_PIECE = 256
FFN_UP_K = 2048
FFN_DOWN_PIECE = 512
MIX_PHASES_PER_FFN_PIECE = 6


def _interleave(main, side, side_per_main):
    for _ in main:
        for _ in range(side_per_main):
            next(side, None)
    for _ in side:
        pass


def _mix_ffn_kernel(x_ref, wo_ref, g_ref, wg_ref, wu_ref, wd_ref, gf_ref,
                    u_ref, v_ref, gv_ref, ws_ref, bst_ref, go_ref,
                    xbc_ref, z_ref, dt_ref, cw_ref, cb_ref, dtb_ref, alog_ref, drep_ref, ng_ref,
                    tril3_ref, rep3_ref,
                    o_ref, nconv_ref, nssm_ref,
                    m_ref, ya_ref, yb_ref, tail_ref, ht_ref, *, chunks_per_seq):
    s = pl.program_id(0)
    f = pl.program_id(1)
    has_ffn = s >= 1
    has_mix = (s < pl.num_programs(0) - 1) & (f < CHUNKS_PER_TILE)

    @pl.when(has_ffn & (f == 0))
    def _():
        y = jnp.concatenate([ya_ref[...], yb_ref[...]], axis=1)
        h = x_ref[...] + _dot(y, wo_ref[...])
        m_ref[...] = _rms(h, g_ref[...]).astype(BF16)
        o_ref[...] = h

    def ffn_step():
        acts = []
        for c in range(FFN_TF // FFN_UP_PIECE):
            cols = slice(c * FFN_UP_PIECE, (c + 1) * FFN_UP_PIECE)
            halves = []
            for w_ref in (wg_ref, wu_ref):
                acc = None
                for k in range(D_MODEL // FFN_UP_K):
                    rows = slice(k * FFN_UP_K, (k + 1) * FFN_UP_K)
                    part = _dot(m_ref[:, rows], w_ref[rows, cols])
                    acc = part if acc is None else acc + part
                    yield
                halves.append(acc)
            acts.append((_silu(halves[0]) * halves[1]).astype(BF16))
        act = jnp.concatenate(acts, axis=1)
        for c in range(D_MODEL // FFN_DOWN_PIECE):
            cols = slice(c * FFN_DOWN_PIECE, (c + 1) * FFN_DOWN_PIECE)
            o_ref[:, cols] += _dot(act, wd_ref[:, cols])
            yield

    chunk_rows = pl.ds(pl.multiple_of(f * SSD_CHUNK, SSD_CHUNK), SSD_CHUNK)

    def store_ya(ya):
        ya_ref[chunk_rows, :] = ya

    def store_yb(yb):
        yb_ref[chunk_rows, :] = yb

    def mix_step():
        first = (s * CHUNKS_PER_TILE + f) % chunks_per_seq == 0
        yield from _chunk_gate(u_ref, v_ref, gv_ref, ws_ref, bst_ref, go_ref, store_ya)
        yield
        yield from _ssd_chunk(first, xbc_ref, z_ref, dt_ref, cw_ref, cb_ref, dtb_ref, alog_ref, drep_ref,
                              ng_ref, tril3_ref, rep3_ref, nconv_ref, nssm_ref, tail_ref, ht_ref, store_yb)

    @pl.when(has_ffn & has_mix)
    def _():
        _interleave(ffn_step(), mix_step(), MIX_PHASES_PER_FFN_PIECE)

    @pl.when(has_ffn & jnp.logical_not(has_mix))
    def _():
        _interleave(ffn_step(), iter(()), 0)

    @pl.when(jnp.logical_not(has_ffn) & has_mix)
    def _():
        _interleave(iter(()), mix_step(), 0)

    @pl.when(has_ffn & (f == pl.num_programs(1) - 1))
    def _():
        o_ref[...] = _rms(o_ref[...], gf_ref[...])


def _mix_ffn(x, proj, dt_raw, w_out_b, g, w_gate, w_up, w_down, g_final, gv, ws, bst, go,
             cw, cb, dtb, alog, drep, ng, tril3, rep3, batch, seq):
    m = x.shape[0]
    n_tiles = m // MIX_TM
    n_chunks = m // SSD_CHUNK
    chunks_per_seq = seq // SSD_CHUNK
    assert MIX_TM % SSD_CHUNK == 0 and seq % MIX_TM == 0 and CHUNKS_PER_TILE <= D_FF // FFN_TF

    def tile(s, f):
        return jnp.maximum(s - 1, 0)

    def chunk(s, f):
        return jnp.where(s >= n_tiles, n_chunks - 1,
                         s * CHUNKS_PER_TILE + jnp.minimum(f, CHUNKS_PER_TILE - 1))

    def seq_of(s, f):
        return chunk(s, f) // chunks_per_seq

    const2 = lambda s, f: (0, 0)
    return pl.pallas_call(
        functools.partial(_mix_ffn_kernel, chunks_per_seq=chunks_per_seq),
        grid=(n_tiles + 1, D_FF // FFN_TF),
        in_specs=[
            pl.BlockSpec((MIX_TM, D_MODEL), lambda s, f: (tile(s, f), 0)),
            pl.BlockSpec((D_MODEL, D_MODEL), const2, pipeline_mode=pl.Buffered(1)),
            pl.BlockSpec((1, D_MODEL), const2),
            pl.BlockSpec((D_MODEL, FFN_TF), lambda s, f: (0, f)),
            pl.BlockSpec((D_MODEL, FFN_TF), lambda s, f: (0, f)),
            pl.BlockSpec((FFN_TF, D_MODEL), lambda s, f: (f, 0)),
            pl.BlockSpec((1, D_MODEL), const2),
            pl.BlockSpec((CHUNK_LEN, D_CHUNK), lambda s, f: (chunk(s, f), U_BLK)),
            pl.BlockSpec((CHUNK_LEN, D_CHUNK), lambda s, f: (chunk(s, f), V_BLK)),
            pl.BlockSpec((1, D_CHUNK), const2),
            pl.BlockSpec((CHUNK_HEADS, CHUNK_LEN, CHUNK_LEN), lambda s, f: (0, 0, 0)),
            pl.BlockSpec((CHUNK_LEN, CHUNK_HEADS), const2),
            pl.BlockSpec((1, D_CHUNK), const2),
            pl.BlockSpec((SSD_CHUNK, CONV_DIM), lambda s, f: (chunk(s, f), XBC_BLK)),
            pl.BlockSpec((SSD_CHUNK, D_SSM), lambda s, f: (chunk(s, f), Z_BLK)),
            pl.BlockSpec((SSD_CHUNK, HEAD_PAD), lambda s, f: (chunk(s, f), 0)),
            pl.BlockSpec((CONV_W, CONV_DIM), const2),
            pl.BlockSpec((1, CONV_DIM), const2),
            pl.BlockSpec((1, HEAD_PAD), const2),
            pl.BlockSpec((1, HEAD_PAD), const2),
            pl.BlockSpec((1, D_SSM), const2),
            pl.BlockSpec((1, D_SSM), const2),
            pl.BlockSpec((SSD_CHUNK, N_SPLIT * SSD_CHUNK), const2),
            pl.BlockSpec((N_SPLIT * HEAD_PAD, D_SSM), const2),
        ],
        out_specs=[
            pl.BlockSpec((MIX_TM, D_MODEL), lambda s, f: (tile(s, f), 0)),
            pl.BlockSpec((None, CONV_W - 1, CONV_DIM), lambda s, f: (seq_of(s, f), 0, 0)),
            pl.BlockSpec((None, D_SSM, D_STATE), lambda s, f: (seq_of(s, f), 0, 0)),
        ],
        out_shape=[
            jax.ShapeDtypeStruct((m, D_MODEL), F32),
            jax.ShapeDtypeStruct((batch, CONV_W - 1, CONV_DIM), F32),
            jax.ShapeDtypeStruct((batch, D_SSM, D_STATE), F32),
        ],
        scratch_shapes=[
            pltpu.VMEM((MIX_TM, D_MODEL), BF16),
            pltpu.VMEM((MIX_TM, D_CHUNK), BF16),
            pltpu.VMEM((MIX_TM, D_SSM), BF16),
            pltpu.VMEM((CONV_PAD, CONV_DIM), F32),
            pltpu.VMEM((D_STATE, D_SSM), F32),
        ],
        compiler_params=_params("arbitrary", "arbitrary"),
        name="mix_ffn",
    )(x, w_out_b, g, w_gate, w_up, w_down, g_final, proj, proj, gv, ws, bst, go, proj, proj, dt_raw,
      cw, cb, dtb, alog, drep, ng, tril3, rep3)


def _row(v, width=None):
    v = v.astype(F32).reshape(1, -1)
    if width is not None and v.shape[1] < width:
        v = jnp.pad(v, ((0, 0), (0, width - v.shape[1])))
    return v


def kernel(x_prompt, x_sample, state_conv, state_ssm, norm_mix_g, w_in, chunk_v_norm_g, chunk_w_s,
           chunk_b_s, chunk_out_norm_g, ssd_conv_w, ssd_conv_b, ssd_dt_bias, ssd_a_log, ssd_d,
           ssd_norm_g, w_out, norm_ffn_g, w_gate, w_up, w_down, norm_final_g):
    depth = w_in.shape[0]
    batch, seq, _ = x_prompt.shape
    n_s = x_sample.shape[0]
    assert x_sample.shape[1] == 1 and seq % SSD_CHUNK == 0 and depth == 1

    rep = (lax.broadcasted_iota(jnp.int32, (HEAD_PAD, D_SSM), 0)
           == lax.broadcasted_iota(jnp.int32, (HEAD_PAD, D_SSM), 1) // SSM_HEAD_DIM).astype(F32)
    rep8 = (lax.broadcasted_iota(jnp.int32, (HEAD_PAD, HEAD_PAD), 0)
            == lax.broadcasted_iota(jnp.int32, (HEAD_PAD, HEAD_PAD), 1) // SUBLANES).astype(F32)
    rep3 = jnp.tile(rep, (N_SPLIT, 1)).astype(BF16)
    tril3 = jnp.tile(jnp.tril(jnp.ones((SSD_CHUNK, SSD_CHUNK), F32)), (1, N_SPLIT)).astype(BF16)

    hp = x_prompt.reshape(batch * seq, D_MODEL)
    hs = x_sample.reshape(n_s, D_MODEL)
    l = 0
    w_in_t = jnp.swapaxes(w_in, 1, 2)
    w_dt_t = jnp.pad(w_in_t[l, D_MAIN:, :], ((0, HEAD_PAD - SSM_HEADS), (0, 0))).astype(BF16)
    g_mix = _row(norm_mix_g[l])
    gv = _row(chunk_v_norm_g[l])
    go = _row(chunk_out_norm_g[l])
    cw = ssd_conv_w[l].astype(F32)
    cb = _row(ssd_conv_b[l])
    dtb = _row(ssd_dt_bias[l], HEAD_PAD)
    alog = _row(ssd_a_log[l], HEAD_PAD)
    drep = _row(jnp.repeat(ssd_d[l], SSM_HEAD_DIM))
    ng = _row(ssd_norm_g[l])
    g_ffn = _row(norm_ffn_g[l])
    g_final = _row(norm_final_g)

    proj_p, dt_p, w_gate_b, w_up_b, w_down_b, w_out_b = _in_proj(
        hp, g_mix, w_in_t, l, w_dt_t, tm=1024, cast=(w_gate, w_up, w_down, w_out))
    y_p, conv_p, ssm_p = _mix_ffn(hp, proj_p, dt_p, w_out_b, g_ffn, w_gate_b, w_up_b, w_down_b, g_final,
                                  gv, chunk_w_s[l].astype(F32), chunk_b_s[l].astype(F32).T, go,
                                  cw, cb, dtb, alog, drep, ng, tril3, rep3, batch, seq)

    proj_s, dt_s = _in_proj(hs, g_mix, w_in_t, l, w_dt_t, tm=n_s)
    w0 = _row(jnp.repeat(chunk_w_s[l, :, 0, 0], CHUNK_HEAD_DIM))
    b0 = _row(jnp.repeat(chunk_b_s[l, :, 0], CHUNK_HEAD_DIM))
    ya_s, v_s, conv_s, split, b_s, c_s, dx_s = _sample_rows(
        proj_s, dt_s, state_conv[l].reshape(n_s, (CONV_W - 1) * CONV_DIM), gv, w0, b0, go,
        cw, cb, dtb, alog, drep, rep, rep8)
    ssm_s, yb_s = _sample_state(split, b_s, c_s, state_ssm[l].reshape(n_s, D_SSM, D_STATE),
                                dx_s, proj_s, ng)
    h_s = _out_proj(hs, ya_s, yb_s, w_out, l, tm=n_s)
    y_s = _ffn(h_s, g_ffn, w_gate_b, w_up_b, w_down_b, g_final, tm=n_s)

    return (
        y_p.reshape(batch, seq, D_MODEL),
        y_s.reshape(n_s, 1, D_MODEL),
        conv_p[None],
        ssm_p.reshape(1, batch, SSM_HEADS, SSM_HEAD_DIM, D_STATE),
        conv_s.reshape(1, n_s, CONV_W - 1, CONV_DIM),
        ssm_s.reshape(1, n_s, SSM_HEADS, SSM_HEAD_DIM, D_STATE),
        v_s.reshape(1, n_s, 1, D_CHUNK),
    )
```

```python
import functools

import jax
import jax.numpy as jnp
from jax import lax
from jax.experimental import pallas as pl
from jax.experimental.pallas import tpu as pltpu

F32 = jnp.float32
BF16 = jnp.bfloat16
HIGHEST = lax.Precision.HIGHEST

D_MODEL = 2048
D_CHUNK = 1024
CHUNK_HEADS = 8
CHUNK_HEAD_DIM = 128
CHUNK_LEN = 128
D_SSM = 1024
SSM_HEAD_DIM = 64
SSM_HEADS = 16
SSM_GROUPS = 2
HEADS_PER_GROUP = SSM_HEADS // SSM_GROUPS
GROUP_WIDTH = D_SSM // SSM_GROUPS
D_STATE = 128
CONV_W = 4
CONV_DIM = D_SSM + 2 * SSM_GROUPS * D_STATE
SSD_CHUNK = 128
D_MAIN = 2 * D_CHUNK + D_SSM + CONV_DIM
D_FF = 5632
EPS = 1e-6

LANES = 128
SUBLANES = 8
BF16_SUBLANES = 16
HEAD_PAD = LANES
N_SPLIT = 3
VMEM_LIMIT = 56 * 1024 * 1024

U_BLK, V_BLK, Z_BLK = 0, 1, 2
XBC_BLK = 2


def _rms(x, g):
    return x * lax.rsqrt(jnp.mean(x * x, axis=-1, keepdims=True) + EPS) * g


def _silu(x):
    return x * jax.nn.sigmoid(x)


def _softplus(x):
    return jnp.maximum(x, 0.0) + jnp.log1p(jnp.exp(-jnp.abs(x)))


def _dot(a, b):
    return jnp.dot(a, b, preferred_element_type=F32)


def _dot_nt(a, b):
    return lax.dot_general(a, b, (((1,), (1,)), ((), ())), preferred_element_type=F32)


def _dot_exact(a, b):
    return jnp.dot(a, b, precision=HIGHEST, preferred_element_type=F32)


def _params(*semantics):
    return pltpu.CompilerParams(dimension_semantics=semantics, vmem_limit_bytes=VMEM_LIMIT)


IN_TN = 512


def _in_proj_kernel(x_hbm, g_ref, w_ref, wdt_ref, *refs, tm, n_cast):
    cast_in = refs[:n_cast]
    o_ref, dt_ref = refs[n_cast:n_cast + 2]
    cast_out = refs[n_cast + 2:2 * n_cast + 2]
    xn_ref, wres_ref, xbuf_ref, sem = refs[2 * n_cast + 2:]
    i = pl.program_id(0)
    j = pl.program_id(1)
    rows = pl.ds(pl.multiple_of(j * IN_TN, IN_TN), IN_TN)

    for src, dst in zip(cast_in, cast_out, strict=True):
        dst[...] = src[...].astype(BF16)

    def x_copy(tile):
        return pltpu.make_async_copy(x_hbm.at[pl.ds(tile * tm, tm), :], xbuf_ref, sem)

    @pl.when((i == 0) & (j == 0))
    def _():
        x_copy(0).start()

    @pl.when(i == 0)
    def _():
        wres_ref[rows, :] = w_ref[...].astype(BF16)

    @pl.when(j == 0)
    def _():
        x_copy(i).wait()
        nb = _rms(xbuf_ref[...], g_ref[...]).astype(BF16)
        xn_ref[...] = nb
        dt_ref[...] = _dot_nt(nb, wdt_ref[...])

    @pl.when((j == 1) & (i + 1 < pl.num_programs(0)))
    def _():
        x_copy(i + 1).start()

    o_ref[...] = _dot_nt(xn_ref[...], wres_ref[rows, :])


def _in_proj(x, g, w_in_t, layer, w_dt_t, tm, cast=()):
    m = x.shape[0]
    n_i = m // tm
    n_j = D_MAIN // IN_TN
    assert n_j >= 2
    cast_in, cast_out, cast_shapes = [], [], []
    n_steps = n_i * n_j
    for w in cast:
        _, w_rows, w_cols = w.shape
        slab_rows = next(r for r in range(BF16_SUBLANES, w_rows + 1, BF16_SUBLANES)
                         if w_rows % r == 0 and w_rows // r <= n_steps)
        n_slabs = w_rows // slab_rows
        hold = n_steps // n_slabs
        slab = lambda i, j, n_slabs=n_slabs, hold=hold: jnp.minimum((i * n_j + j) // hold, n_slabs - 1)
        cast_in.append(pl.BlockSpec((None, slab_rows, w_cols), lambda i, j, slab=slab: (layer, slab(i, j), 0)))
        cast_out.append(pl.BlockSpec((slab_rows, w_cols), lambda i, j, slab=slab: (slab(i, j), 0)))
        cast_shapes.append(jax.ShapeDtypeStruct((w_rows, w_cols), BF16))
    return pl.pallas_call(
        functools.partial(_in_proj_kernel, tm=tm, n_cast=len(cast)),
        grid=(n_i, n_j),
        in_specs=[
            pl.BlockSpec(memory_space=pl.ANY),
            pl.BlockSpec((1, D_MODEL), lambda i, j: (0, 0)),
            pl.BlockSpec((None, IN_TN, D_MODEL), lambda i, j: (layer, jnp.where(i == 0, j, n_j - 1), 0)),
            pl.BlockSpec((HEAD_PAD, D_MODEL), lambda i, j: (0, 0)),
            *cast_in,
        ],
        out_specs=[
            pl.BlockSpec((tm, IN_TN), lambda i, j: (i, j)),
            pl.BlockSpec((tm, HEAD_PAD), lambda i, j: (i, 0)),
            *cast_out,
        ],
        out_shape=[
            jax.ShapeDtypeStruct((m, D_MAIN), F32),
            jax.ShapeDtypeStruct((m, HEAD_PAD), F32),
            *cast_shapes,
        ],
        scratch_shapes=[
            pltpu.VMEM((tm, D_MODEL), BF16),
            pltpu.VMEM((D_MAIN, D_MODEL), BF16),
            pltpu.VMEM((tm, D_MODEL), F32),
            pltpu.SemaphoreType.DMA(()),
        ],
        compiler_params=_params("arbitrary", "arbitrary"),
        name="in_proj",
    )(x, g, w_in_t, w_dt_t, *cast)


def _mask_chunk_weights(ws_ref, wm_ref):
    row = lax.broadcasted_iota(jnp.int32, (CHUNK_LEN, CHUNK_LEN), 0)
    col = lax.broadcasted_iota(jnp.int32, (CHUNK_LEN, CHUNK_LEN), 1)
    for h in range(CHUNK_HEADS):
        wm_ref[:, h * CHUNK_LEN:(h + 1) * CHUNK_LEN] = jnp.where(row >= col, ws_ref[h], 0.0).astype(BF16)


def _chunk_gate(u_ref, v_ref, gv_ref, wm_ref, bst_ref, go_ref, store_y):
    vb = _rms(jax.nn.gelu(v_ref[...]), gv_ref[...]).astype(BF16)
    yield
    u = jax.nn.gelu(u_ref[...])
    yield
    zeros = jnp.zeros((CHUNK_LEN, CHUNK_HEAD_DIM), BF16)
    parts = []
    for h0 in range(0, CHUNK_HEADS, 2):
        w = wm_ref[:, h0 * CHUNK_LEN:(h0 + 2) * CHUNK_LEN]
        va = vb[:, h0 * CHUNK_HEAD_DIM:(h0 + 1) * CHUNK_HEAD_DIM]
        vb2 = vb[:, (h0 + 1) * CHUNK_HEAD_DIM:(h0 + 2) * CHUNK_HEAD_DIM]
        rhs = jnp.concatenate([jnp.concatenate([va, zeros], axis=1),
                               jnp.concatenate([zeros, vb2], axis=1)], axis=0)
        mixed = _dot(w, rhs)
        for k, h in enumerate((h0, h0 + 1)):
            sl = slice(h * CHUNK_HEAD_DIM, (h + 1) * CHUNK_HEAD_DIM)
            mh = mixed[:, k * CHUNK_HEAD_DIM:(k + 1) * CHUNK_HEAD_DIM] + bst_ref[:, h:h + 1]
            parts.append(u[:, sl] * mh)
        yield
    y = jnp.concatenate(parts, axis=1)
    store_y(_rms(y, go_ref[...]).astype(BF16))


CONV_PAD = 8


def _group_norm_out(y, z, g):
    y = y * _silu(z)
    parts = []
    for grp in range(SSM_GROUPS):
        yg = y[:, grp * GROUP_WIDTH:(grp + 1) * GROUP_WIDTH]
        parts.append(yg * lax.rsqrt(jnp.mean(yg * yg, axis=-1, keepdims=True) + EPS))
    return (jnp.concatenate(parts, axis=1) * g).astype(BF16)


def _split3(x):
    p1 = x.astype(BF16)
    r1 = x - p1.astype(F32)
    p2 = r1.astype(BF16)
    p3 = (r1 - p2.astype(F32)).astype(BF16)
    return p1, p2, p3


def _select_rows(sel3, x):
    return _dot(sel3, jnp.concatenate(_split3(x), axis=0))


def _select_cols(x, sel3):
    return _dot(jnp.concatenate(_split3(x), axis=1), sel3)


def _ssd_chunk(first, xbc_ref, z_ref, dt_ref, cw_ref, cb_ref, dtb_ref, alog_ref, drep_ref, ng_ref,
               tril3_ref, nconv_ref, nssm_ref, tail_ref, ht_ref, store_y):
    q = SSD_CHUNK

    xbc = xbc_ref[...]
    tail = jnp.where(first, 0.0, tail_ref[...])
    ht_prev = jnp.where(first, 0.0, ht_ref[...])
    sub = lax.broadcasted_iota(jnp.int32, (CONV_PAD, CONV_DIM), 0)
    conv = cb_ref[...] + xbc * cw_ref[CONV_W - 1:CONV_W, :]
    for k in range(1, CONV_W):
        shifted = pltpu.roll(xbc, k, 0)
        head = jnp.where(sub < k, pltpu.roll(tail, k, 0), shifted[0:CONV_PAD, :])
        shifted = jnp.concatenate([head, shifted[CONV_PAD:, :]], axis=0)
        conv = conv + shifted * cw_ref[CONV_W - 1 - k:CONV_W - k, :]
    tail_ref[...] = xbc[q - CONV_PAD:, :]
    nconv_ref[...] = xbc[q - (CONV_W - 1):, :]
    yield

    act = _silu(conv)
    xs = act[:, :D_SSM]
    bmat = act[:, D_SSM:D_SSM + SSM_GROUPS * D_STATE]
    cmat = act[:, D_SSM + SSM_GROUPS * D_STATE:]
    yield

    dt = _softplus(dt_ref[...] + dtb_ref[...])
    a = -jnp.exp(alog_ref[...])
    acum = _select_rows(tril3_ref[...], dt * a)
    acum_t = acum.T
    low_lanes = lax.broadcasted_iota(jnp.int32, (q, 2 * SSM_HEAD_DIM), 1) < SSM_HEAD_DIM

    def expand(cols):
        pairs = []
        for h0 in range(0, SSM_HEADS, 2):
            a = jnp.broadcast_to(cols[:, h0:h0 + 1], (q, 2 * SSM_HEAD_DIM))
            b = jnp.broadcast_to(cols[:, h0 + 1:h0 + 2], (q, 2 * SSM_HEAD_DIM))
            pairs.append(jnp.where(low_lanes, a, b))
        return jnp.concatenate(pairs, axis=1)

    dt_rep = expand(dt)
    yield
    acum_rep = expand(acum)
    yield
    xdt = xs * dt_rep
    exp_a = jnp.exp(acum_rep)
    xw = xdt * jnp.exp(acum_rep[q - 1:q, :] - acum_rep)
    yield

    row = lax.broadcasted_iota(jnp.int32, (q, q), 0)
    col = lax.broadcasted_iota(jnp.int32, (q, q), 1)
    causal = row >= col
    low_half = col < SSM_HEAD_DIM
    ydiag, yoff, states = [], [], []
    for grp in range(SSM_GROUPS):
        bg_f = bmat[:, grp * D_STATE:(grp + 1) * D_STATE]
        bg = bg_f.astype(BF16)
        cg = cmat[:, grp * D_STATE:(grp + 1) * D_STATE].astype(BF16)
        cb = _dot_nt(cg, bg)
        ch = slice(grp * GROUP_WIDTH, (grp + 1) * GROUP_WIDTH)
        yoff.append(_dot(cg, ht_prev[:, ch].astype(BF16)))
        yield
        for quad in range(HEADS_PER_GROUP // 4):
            h0 = grp * HEADS_PER_GROUP + 4 * quad
            masks, blocks = [], []
            for k, h in enumerate(range(h0, h0 + 4)):
                diff = acum[:, h:h + 1] - acum_t[h:h + 1, :]
                masks.append((cb * jnp.exp(jnp.where(causal, diff, -jnp.inf))).astype(BF16))
                xp = xdt[:, (h - k % 2) * SSM_HEAD_DIM:(h - k % 2 + 2) * SSM_HEAD_DIM]
                own = jnp.where(low_half == (k % 2 == 0), xp, 0.0).astype(BF16)
                zero = jnp.zeros_like(own)
                blocks.append(jnp.concatenate([own, zero] if k < 2 else [zero, own], axis=1))
                if k % 2 == 1:
                    yield
            ydiag.append(_dot(jnp.concatenate(masks, axis=1), jnp.concatenate(blocks, axis=0)))
            yield
        states.append(_dot(bg_f.T.astype(BF16), xw[:, ch].astype(BF16)))
        yield
    y = drep_ref[...] * xs + jnp.concatenate(ydiag, axis=1) + jnp.concatenate(yoff, axis=1) * exp_a

    ht_new = ht_prev * exp_a[q - 1:q, :] + jnp.concatenate(states, axis=1)
    ht_ref[...] = ht_new
    nssm_ref[...] = ht_new.T
    yield
    store_y(_group_norm_out(y, z_ref[...], ng_ref[...]))


def _sample_rows_kernel(u_ref, v_ref, xbc_ref, dt_ref, sc_ref, gv_ref, w0_ref, b0_ref, go_ref,
                        cw_ref, cb_ref, dtb_ref, alog_ref, drep_ref, rep_ref, rep8_ref,
                        ya_ref, vrows_ref, nconv_ref, split_ref, b_ref, c_ref, dx_ref):
    vr = _rms(jax.nn.gelu(v_ref[...]), gv_ref[...])
    vrows_ref[...] = vr
    mixed = vr * w0_ref[...] + b0_ref[...]
    ya_ref[...] = _rms(jax.nn.gelu(u_ref[...]) * mixed, go_ref[...]).astype(BF16)

    xbc = xbc_ref[...]
    conv = cb_ref[...]
    for k in range(CONV_W - 1):
        conv = conv + sc_ref[:, k * CONV_DIM:(k + 1) * CONV_DIM] * cw_ref[k:k + 1, :]
    conv = conv + xbc * cw_ref[CONV_W - 1:CONV_W, :]
    nconv_ref[:, 0:(CONV_W - 2) * CONV_DIM] = sc_ref[:, CONV_DIM:(CONV_W - 1) * CONV_DIM]
    nconv_ref[:, (CONV_W - 2) * CONV_DIM:] = xbc

    act = _silu(conv)
    xs = act[:, :D_SSM]
    b_ref[...] = act[:, D_SSM:D_SSM + SSM_GROUPS * D_STATE]
    c_ref[...] = act[:, D_SSM + SSM_GROUPS * D_STATE:]
    dx_ref[...] = drep_ref[...] * xs

    dt = _softplus(dt_ref[...] + dtb_ref[...])
    decay = jnp.exp(dt * (-jnp.exp(alog_ref[...])))
    xdt = xs * _dot_exact(dt, rep_ref[...])
    cols = jnp.concatenate([xdt, _dot_exact(decay, rep8_ref[...])], axis=1).T
    for p in range(N_SPLIT):
        piece = cols.astype(BF16)
        split_ref[:, p * LANES:(p + 1) * LANES] = piece
        cols = cols - piece.astype(F32)


SPLIT_ROWS = D_SSM + HEAD_PAD


def _sample_rows(proj, dt_raw, sconv, gv, w0, b0, go, cw, cb, dtb, alog, drep, rep, rep8):
    n = proj.shape[0]
    full = lambda shape: pl.BlockSpec(shape, lambda i: (0,) * len(shape))
    return pl.pallas_call(
        _sample_rows_kernel,
        grid=(1,),
        in_specs=[
            pl.BlockSpec((n, D_CHUNK), lambda i: (0, U_BLK)),
            pl.BlockSpec((n, D_CHUNK), lambda i: (0, V_BLK)),
            pl.BlockSpec((n, CONV_DIM), lambda i: (0, XBC_BLK)),
            full((n, HEAD_PAD)),
            full((n, (CONV_W - 1) * CONV_DIM)),
            full((1, D_CHUNK)), full((1, D_CHUNK)), full((1, D_CHUNK)), full((1, D_CHUNK)),
            full((CONV_W, CONV_DIM)), full((1, CONV_DIM)),
            full((1, HEAD_PAD)), full((1, HEAD_PAD)), full((1, D_SSM)),
            full((HEAD_PAD, D_SSM)), full((HEAD_PAD, HEAD_PAD)),
        ],
        out_specs=[
            full((n, D_CHUNK)), full((n, D_CHUNK)), full((n, (CONV_W - 1) * CONV_DIM)),
            full((SPLIT_ROWS, N_SPLIT * LANES)),
            full((n, SSM_GROUPS * D_STATE)), full((n, SSM_GROUPS * D_STATE)), full((n, D_SSM)),
        ],
        out_shape=[
            jax.ShapeDtypeStruct((n, D_CHUNK), BF16),
            jax.ShapeDtypeStruct((n, D_CHUNK), F32),
            jax.ShapeDtypeStruct((n, (CONV_W - 1) * CONV_DIM), F32),
            jax.ShapeDtypeStruct((SPLIT_ROWS, N_SPLIT * LANES), BF16),
            jax.ShapeDtypeStruct((n, SSM_GROUPS * D_STATE), F32),
            jax.ShapeDtypeStruct((n, SSM_GROUPS * D_STATE), F32),
            jax.ShapeDtypeStruct((n, D_SSM), F32),
        ],
        compiler_params=_params("arbitrary"),
        name="sample_rows",
    )(proj, proj, proj, dt_raw, sconv, gv, w0, b0, go, cw, cb, dtb, alog, drep, rep, rep8)


SAMPLE_TB = 8


def _sample_state_kernel(split_ref, b_ref, c_ref, h0_ref, dx_ref, z_ref, ng_ref,
                         hn_ref, y_ref, yt_ref):
    i = pl.program_id(0)
    n_tok = b_ref.shape[0]

    @pl.when(i == 0)
    def _():
        yt_ref[...] = jnp.zeros_like(yt_ref)

    tok = lax.broadcasted_iota(jnp.int32, (N_SPLIT * LANES, D_STATE), 0) % LANES
    lane = lax.broadcasted_iota(jnp.int32, (D_SSM, n_tok), 1)

    def per_group(row, grp):
        return jnp.broadcast_to(row[:, grp * D_STATE:(grp + 1) * D_STATE], (GROUP_WIDTH, D_STATE))

    for k in range(SAMPLE_TB):
        t = i * SAMPLE_TB + k
        onehot = (tok == t).astype(BF16)
        bcast = _dot(split_ref[...], onehot)
        decay = jnp.broadcast_to(
            bcast[D_SSM:, :].reshape(SSM_HEADS, 1, SUBLANES, D_STATE),
            (SSM_HEADS, SSM_HEAD_DIM // SUBLANES, SUBLANES, D_STATE)).reshape(D_SSM, D_STATE)
        brow = b_ref[pl.ds(t, 1), :]
        crow = c_ref[pl.ds(t, 1), :]
        bfull = jnp.concatenate([per_group(brow, g) for g in range(SSM_GROUPS)], axis=0)
        cfull = jnp.concatenate([per_group(crow, g) for g in range(SSM_GROUPS)], axis=0)
        h_new = h0_ref[k] * decay + bcast[:D_SSM, :] * bfull
        hn_ref[k] = h_new
        ysum = jnp.sum(h_new * cfull, axis=-1, keepdims=True)
        yt_ref[...] = jnp.where(lane == t, ysum, yt_ref[...])

    @pl.when(i == pl.num_programs(0) - 1)
    def _():
        y = yt_ref[...].T + dx_ref[...]
        y_ref[...] = _group_norm_out(y, z_ref[...], ng_ref[...])


def _sample_state(split, bmat, cmat, h0, dx, proj, ng):
    n = h0.shape[0]
    const2 = lambda i: (0, 0)
    return pl.pallas_call(
        _sample_state_kernel,
        grid=(n // SAMPLE_TB,),
        in_specs=[
            pl.BlockSpec((SPLIT_ROWS, N_SPLIT * LANES), const2),
            pl.BlockSpec((n, SSM_GROUPS * D_STATE), const2),
            pl.BlockSpec((n, SSM_GROUPS * D_STATE), const2),
            pl.BlockSpec((SAMPLE_TB, D_SSM, D_STATE), lambda i: (i, 0, 0)),
            pl.BlockSpec((n, D_SSM), const2),
            pl.BlockSpec((n, D_SSM), lambda i: (0, Z_BLK)),
            pl.BlockSpec((1, D_SSM), const2),
        ],
        out_specs=[
            pl.BlockSpec((SAMPLE_TB, D_SSM, D_STATE), lambda i: (i, 0, 0)),
            pl.BlockSpec((n, D_SSM), const2),
        ],
        out_shape=[
            jax.ShapeDtypeStruct((n, D_SSM, D_STATE), F32),
            jax.ShapeDtypeStruct((n, D_SSM), BF16),
        ],
        scratch_shapes=[pltpu.VMEM((D_SSM, n), F32)],
        compiler_params=_params("arbitrary"),
        name="sample_state",
    )(split, bmat, cmat, h0, dx, proj, ng)


OUT_TN = 1024


def _out_proj_kernel(x_ref, ya_ref, yb_ref, w_ref, o_ref, wres_ref):
    i = pl.program_id(0)
    j = pl.program_id(1)

    @pl.when(i == 0)
    def _():
        wres_ref[j] = w_ref[...].astype(BF16)

    y = jnp.concatenate([ya_ref[...], yb_ref[...]], axis=1)
    o_ref[...] = x_ref[...] + _dot(y, wres_ref[j])


def _out_proj(x, ya, yb, w_out, layer, tm):
    m = x.shape[0]
    n_j = D_MODEL // OUT_TN
    return pl.pallas_call(
        _out_proj_kernel,
        grid=(m // tm, n_j),
        in_specs=[
            pl.BlockSpec((tm, OUT_TN), lambda i, j: (i, j)),
            pl.BlockSpec((tm, D_CHUNK), lambda i, j: (i, 0)),
            pl.BlockSpec((tm, D_SSM), lambda i, j: (i, 0)),
            pl.BlockSpec((None, D_MODEL, OUT_TN), lambda i, j: (layer, 0, jnp.where(i == 0, j, n_j - 1))),
        ],
        out_specs=pl.BlockSpec((tm, OUT_TN), lambda i, j: (i, j)),
        out_shape=jax.ShapeDtypeStruct((m, D_MODEL), F32),
        scratch_shapes=[pltpu.VMEM((n_j, D_MODEL, OUT_TN), BF16)],
        compiler_params=_params("arbitrary", "arbitrary"),
        name="out_proj",
    )(x, ya, yb, w_out)


FFN_TF = 512


def _ffn_kernel(h_ref, g_ref, wg_ref, wu_ref, wd_ref, gf_ref, o_ref, m_ref):
    f = pl.program_id(1)

    @pl.when(f == 0)
    def _():
        h = h_ref[...]
        m_ref[...] = _rms(h, g_ref[...]).astype(BF16)
        o_ref[...] = h

    m = m_ref[...]
    act = (_silu(_dot(m, wg_ref[...])) * _dot(m, wu_ref[...])).astype(BF16)
    o_ref[...] += _dot(act, wd_ref[...])

    @pl.when(f == pl.num_programs(1) - 1)
    def _():
        o_ref[...] = _rms(o_ref[...], gf_ref[...])


def _ffn(h, g, w_gate, w_up, w_down, g_final, tm):
    m = h.shape[0]
    return pl.pallas_call(
        _ffn_kernel,
        grid=(m // tm, D_FF // FFN_TF),
        in_specs=[
            pl.BlockSpec((tm, D_MODEL), lambda i, f: (i, 0)),
            pl.BlockSpec((1, D_MODEL), lambda i, f: (0, 0)),
            pl.BlockSpec((D_MODEL, FFN_TF), lambda i, f: (0, f)),
            pl.BlockSpec((D_MODEL, FFN_TF), lambda i, f: (0, f)),
            pl.BlockSpec((FFN_TF, D_MODEL), lambda i, f: (f, 0)),
            pl.BlockSpec((1, D_MODEL), lambda i, f: (0, 0)),
        ],
        out_specs=pl.BlockSpec((tm, D_MODEL), lambda i, f: (i, 0)),
        out_shape=jax.ShapeDtypeStruct((m, D_MODEL), F32),
        scratch_shapes=[pltpu.VMEM((tm, D_MODEL), BF16)],
        compiler_params=_params("arbitrary", "arbitrary"),
        name="ffn",
    )(h, g, w_gate, w_up, w_down, g_final)


MIX_TM = 512
CHUNKS_PER_TILE = MIX_TM // SSD_CHUNK
FFN_UP_PIECE = 256
FFN_UP_K = 2048
FFN_DOWN_PIECE = 512
MIX_PHASES_PER_FFN_PIECE = 5


def _interleave(main, side, side_per_main):
    for _ in main:
        for _ in range(side_per_main):
            next(side, None)
    for _ in side:
        pass


def _mix_ffn_kernel(x_ref, wo_ref, g_ref, wg_ref, wu_ref, wd_ref, gf_ref,
                    u_ref, v_ref, gv_ref, ws_ref, bst_ref, go_ref,
                    xbc_ref, z_ref, dt_ref, cw_ref, cb_ref, dtb_ref, alog_ref, drep_ref, ng_ref,
                    tril3_ref,
                    o_ref, nconv_ref, nssm_ref,
                    m_ref, ya_ref, yb_ref, tail_ref, ht_ref, wm_ref, *, chunks_per_seq):
    s = pl.program_id(0)
    f = pl.program_id(1)
    has_ffn = s >= 1
    has_mix = (s < pl.num_programs(0) - 1) & (f < CHUNKS_PER_TILE)

    @pl.when((s == 0) & (f == 0))
    def _():
        _mask_chunk_weights(ws_ref, wm_ref)

    @pl.when(has_ffn & (f == 0))
    def _():
        y = jnp.concatenate([ya_ref[...], yb_ref[...]], axis=1)
        h = x_ref[...] + _dot(y, wo_ref[...])
        m_ref[...] = _rms(h, g_ref[...]).astype(BF16)
        o_ref[...] = h

    def ffn_step():
        acts = []
        for c in range(FFN_TF // FFN_UP_PIECE):
            cols = slice(c * FFN_UP_PIECE, (c + 1) * FFN_UP_PIECE)
            halves = []
            for w_ref in (wg_ref, wu_ref):
                acc = None
                for k in range(D_MODEL // FFN_UP_K):
                    rows = slice(k * FFN_UP_K, (k + 1) * FFN_UP_K)
                    part = _dot(m_ref[:, rows], w_ref[rows, cols])
                    acc = part if acc is None else acc + part
                    yield
                halves.append(acc)
            acts.append((_silu(halves[0]) * halves[1]).astype(BF16))
        act = jnp.concatenate(acts, axis=1)
        for c in range(D_MODEL // FFN_DOWN_PIECE):
            cols = slice(c * FFN_DOWN_PIECE, (c + 1) * FFN_DOWN_PIECE)
            o_ref[:, cols] += _dot(act, wd_ref[:, cols])
            yield

    chunk_rows = pl.ds(pl.multiple_of(f * SSD_CHUNK, SSD_CHUNK), SSD_CHUNK)

    def store_ya(ya):
        ya_ref[chunk_rows, :] = ya

    def store_yb(yb):
        yb_ref[chunk_rows, :] = yb

    def mix_step():
        first = (s * CHUNKS_PER_TILE + f) % chunks_per_seq == 0
        yield from _chunk_gate(u_ref, v_ref, gv_ref, wm_ref, bst_ref, go_ref, store_ya)
        yield
        yield from _ssd_chunk(first, xbc_ref, z_ref, dt_ref, cw_ref, cb_ref, dtb_ref, alog_ref, drep_ref,
                              ng_ref, tril3_ref, nconv_ref, nssm_ref, tail_ref, ht_ref, store_yb)

    @pl.when(has_ffn & has_mix)
    def _():
        _interleave(ffn_step(), mix_step(), MIX_PHASES_PER_FFN_PIECE)

    @pl.when(has_ffn & jnp.logical_not(has_mix))
    def _():
        _interleave(ffn_step(), iter(()), 0)

    @pl.when(jnp.logical_not(has_ffn) & has_mix)
    def _():
        _interleave(iter(()), mix_step(), 0)

    @pl.when(has_ffn & (f == pl.num_programs(1) - 1))
    def _():
        o_ref[...] = _rms(o_ref[...], gf_ref[...])


def _mix_ffn(x, proj, dt_raw, w_out_b, g, w_gate, w_up, w_down, g_final, gv, ws, bst, go,
             cw, cb, dtb, alog, drep, ng, tril3, batch, seq):
    m = x.shape[0]
    n_tiles = m // MIX_TM
    n_chunks = m // SSD_CHUNK
    chunks_per_seq = seq // SSD_CHUNK
    assert MIX_TM % SSD_CHUNK == 0 and seq % MIX_TM == 0 and CHUNKS_PER_TILE <= D_FF // FFN_TF

    def tile(s, f):
        return jnp.maximum(s - 1, 0)

    def chunk(s, f):
        return jnp.where(s >= n_tiles, n_chunks - 1,
                         s * CHUNKS_PER_TILE + jnp.minimum(f, CHUNKS_PER_TILE - 1))

    def seq_of(s, f):
        return chunk(s, f) // chunks_per_seq

    const2 = lambda s, f: (0, 0)
    return pl.pallas_call(
        functools.partial(_mix_ffn_kernel, chunks_per_seq=chunks_per_seq),
        grid=(n_tiles + 1, D_FF // FFN_TF),
        in_specs=[
            pl.BlockSpec((MIX_TM, D_MODEL), lambda s, f: (tile(s, f), 0)),
            pl.BlockSpec((D_MODEL, D_MODEL), const2, pipeline_mode=pl.Buffered(1)),
            pl.BlockSpec((1, D_MODEL), const2),
            pl.BlockSpec((D_MODEL, FFN_TF), lambda s, f: (0, f)),
            pl.BlockSpec((D_MODEL, FFN_TF), lambda s, f: (0, f)),
            pl.BlockSpec((FFN_TF, D_MODEL), lambda s, f: (f, 0)),
            pl.BlockSpec((1, D_MODEL), const2),
            pl.BlockSpec((CHUNK_LEN, D_CHUNK), lambda s, f: (chunk(s, f), U_BLK)),
            pl.BlockSpec((CHUNK_LEN, D_CHUNK), lambda s, f: (chunk(s, f), V_BLK)),
            pl.BlockSpec((1, D_CHUNK), const2),
            pl.BlockSpec((CHUNK_HEADS, CHUNK_LEN, CHUNK_LEN), lambda s, f: (0, 0, 0)),
            pl.BlockSpec((CHUNK_LEN, CHUNK_HEADS), const2),
            pl.BlockSpec((1, D_CHUNK), const2),
            pl.BlockSpec((SSD_CHUNK, CONV_DIM), lambda s, f: (chunk(s, f), XBC_BLK)),
            pl.BlockSpec((SSD_CHUNK, D_SSM), lambda s, f: (chunk(s, f), Z_BLK)),
            pl.BlockSpec((SSD_CHUNK, HEAD_PAD), lambda s, f: (chunk(s, f), 0)),
            pl.BlockSpec((CONV_W, CONV_DIM), const2),
            pl.BlockSpec((1, CONV_DIM), const2),
            pl.BlockSpec((1, HEAD_PAD), const2),
            pl.BlockSpec((1, HEAD_PAD), const2),
            pl.BlockSpec((1, D_SSM), const2),
            pl.BlockSpec((1, D_SSM), const2),
            pl.BlockSpec((SSD_CHUNK, N_SPLIT * SSD_CHUNK), const2),
        ],
        out_specs=[
            pl.BlockSpec((MIX_TM, D_MODEL), lambda s, f: (tile(s, f), 0)),
            pl.BlockSpec((None, CONV_W - 1, CONV_DIM), lambda s, f: (seq_of(s, f), 0, 0)),
            pl.BlockSpec((None, D_SSM, D_STATE), lambda s, f: (seq_of(s, f), 0, 0)),
        ],
        out_shape=[
            jax.ShapeDtypeStruct((m, D_MODEL), F32),
            jax.ShapeDtypeStruct((batch, CONV_W - 1, CONV_DIM), F32),
            jax.ShapeDtypeStruct((batch, D_SSM, D_STATE), F32),
        ],
        scratch_shapes=[
            pltpu.VMEM((MIX_TM, D_MODEL), BF16),
            pltpu.VMEM((MIX_TM, D_CHUNK), BF16),
            pltpu.VMEM((MIX_TM, D_SSM), BF16),
            pltpu.VMEM((CONV_PAD, CONV_DIM), F32),
            pltpu.VMEM((D_STATE, D_SSM), F32),
            pltpu.VMEM((CHUNK_LEN, CHUNK_HEADS * CHUNK_LEN), BF16),
        ],
        compiler_params=_params("arbitrary", "arbitrary"),
        name="mix_ffn",
    )(x, w_out_b, g, w_gate, w_up, w_down, g_final, proj, proj, gv, ws, bst, go, proj, proj, dt_raw,
      cw, cb, dtb, alog, drep, ng, tril3)


def _row(v, width=None):
    v = v.astype(F32).reshape(1, -1)
    if width is not None and v.shape[1] < width:
        v = jnp.pad(v, ((0, 0), (0, width - v.shape[1])))
    return v


def kernel(x_prompt, x_sample, state_conv, state_ssm, norm_mix_g, w_in, chunk_v_norm_g, chunk_w_s,
           chunk_b_s, chunk_out_norm_g, ssd_conv_w, ssd_conv_b, ssd_dt_bias, ssd_a_log, ssd_d,
           ssd_norm_g, w_out, norm_ffn_g, w_gate, w_up, w_down, norm_final_g):
    depth = w_in.shape[0]
    batch, seq, _ = x_prompt.shape
    n_s = x_sample.shape[0]
    assert x_sample.shape[1] == 1 and seq % SSD_CHUNK == 0 and depth == 1

    rep = (lax.broadcasted_iota(jnp.int32, (HEAD_PAD, D_SSM), 0)
           == lax.broadcasted_iota(jnp.int32, (HEAD_PAD, D_SSM), 1) // SSM_HEAD_DIM).astype(F32)
    rep8 = (lax.broadcasted_iota(jnp.int32, (HEAD_PAD, HEAD_PAD), 0)
            == lax.broadcasted_iota(jnp.int32, (HEAD_PAD, HEAD_PAD), 1) // SUBLANES).astype(F32)
    tril3 = jnp.tile(jnp.tril(jnp.ones((SSD_CHUNK, SSD_CHUNK), F32)), (1, N_SPLIT)).astype(BF16)

    hp = x_prompt.reshape(batch * seq, D_MODEL)
    hs = x_sample.reshape(n_s, D_MODEL)
    l = 0
    w_in_t = jnp.swapaxes(w_in, 1, 2)
    w_dt_t = jnp.pad(w_in_t[l, D_MAIN:, :], ((0, HEAD_PAD - SSM_HEADS), (0, 0))).astype(BF16)
    g_mix = _row(norm_mix_g[l])
    gv = _row(chunk_v_norm_g[l])
    go = _row(chunk_out_norm_g[l])
    cw = ssd_conv_w[l].astype(F32)
    cb = _row(ssd_conv_b[l])
    dtb = _row(ssd_dt_bias[l], HEAD_PAD)
    alog = _row(ssd_a_log[l], HEAD_PAD)
    drep = _row(jnp.repeat(ssd_d[l], SSM_HEAD_DIM))
    ng = _row(ssd_norm_g[l])
    g_ffn = _row(norm_ffn_g[l])
    g_final = _row(norm_final_g)

    proj_p, dt_p, w_gate_b, w_up_b, w_down_b, w_out_b = _in_proj(
        hp, g_mix, w_in_t, l, w_dt_t, tm=1024, cast=(w_gate, w_up, w_down, w_out))
    y_p, conv_p, ssm_p = _mix_ffn(hp, proj_p, dt_p, w_out_b, g_ffn, w_gate_b, w_up_b, w_down_b, g_final,
                                  gv, chunk_w_s[l].astype(F32), chunk_b_s[l].astype(F32).T, go,
                                  cw, cb, dtb, alog, drep, ng, tril3, batch, seq)

    proj_s, dt_s = _in_proj(hs, g_mix, w_in_t, l, w_dt_t, tm=n_s)
    w0 = _row(jnp.repeat(chunk_w_s[l, :, 0, 0], CHUNK_HEAD_DIM))
    b0 = _row(jnp.repeat(chunk_b_s[l, :, 0], CHUNK_HEAD_DIM))
    ya_s, v_s, conv_s, split, b_s, c_s, dx_s = _sample_rows(
        proj_s, dt_s, state_conv[l].reshape(n_s, (CONV_W - 1) * CONV_DIM), gv, w0, b0, go,
        cw, cb, dtb, alog, drep, rep, rep8)
    ssm_s, yb_s = _sample_state(split, b_s, c_s, state_ssm[l].reshape(n_s, D_SSM, D_STATE),
                                dx_s, proj_s, ng)
    h_s = _out_proj(hs, ya_s, yb_s, w_out, l, tm=n_s)
    y_s = _ffn(h_s, g_ffn, w_gate_b, w_up_b, w_down_b, g_final, tm=n_s)

    return (
        y_p.reshape(batch, seq, D_MODEL),
        y_s.reshape(n_s, 1, D_MODEL),
        conv_p[None],
        ssm_p.reshape(1, batch, SSM_HEADS, SSM_HEAD_DIM, D_STATE),
        conv_s.reshape(1, n_s, CONV_W - 1, CONV_DIM),
        ssm_s.reshape(1, n_s, SSM_HEADS, SSM_HEAD_DIM, D_STATE),
        v_s.reshape(1, n_s, 1, D_CHUNK),
    )
```

```python
import functools

import jax
import jax.numpy as jnp
from jax import lax
from jax.experimental import pallas as pl
from jax.experimental.pallas import tpu as pltpu

F32 = jnp.float32
BF16 = jnp.bfloat16
HIGHEST = lax.Precision.HIGHEST

D_MODEL = 2048
D_CHUNK = 1024
CHUNK_HEADS = 8
CHUNK_HEAD_DIM = 128
CHUNK_LEN = 128
D_SSM = 1024
SSM_HEAD_DIM = 64
SSM_HEADS = 16
SSM_GROUPS = 2
HEADS_PER_GROUP = SSM_HEADS // SSM_GROUPS
GROUP_WIDTH = D_SSM // SSM_GROUPS
D_STATE = 128
CONV_W = 4
CONV_DIM = D_SSM + 2 * SSM_GROUPS * D_STATE
SSD_CHUNK = 128
D_MAIN = 2 * D_CHUNK + D_SSM + CONV_DIM
D_FF = 5632
EPS = 1e-6

LANES = 128
SUBLANES = 8
BF16_SUBLANES = 16
HEAD_PAD = LANES
N_SPLIT = 3
VMEM_LIMIT = 56 * 1024 * 1024

U_BLK, V_BLK, Z_BLK = 0, 1, 2
XBC_BLK = 2


def _rms(x, g):
    return x * lax.rsqrt(jnp.mean(x * x, axis=-1, keepdims=True) + EPS) * g


def _silu(x):
    return x * jax.nn.sigmoid(x)


def _softplus(x):
    return jnp.maximum(x, 0.0) + jnp.log1p(jnp.exp(-jnp.abs(x)))


def _dot(a, b):
    return jnp.dot(a, b, preferred_element_type=F32)


def _dot_nt(a, b):
    return lax.dot_general(a, b, (((1,), (1,)), ((), ())), preferred_element_type=F32)


def _dot_exact(a, b):
    return jnp.dot(a, b, precision=HIGHEST, preferred_element_type=F32)


def _params(*semantics):
    return pltpu.CompilerParams(dimension_semantics=semantics, vmem_limit_bytes=VMEM_LIMIT)


IN_TN = 512


def _in_proj_kernel(x_hbm, g_ref, w_ref, wdt_ref, *refs, tm, n_cast):
    cast_in = refs[:n_cast]
    o_ref, dt_ref = refs[n_cast:n_cast + 2]
    cast_out = refs[n_cast + 2:2 * n_cast + 2]
    xn_ref, wres_ref, xbuf_ref, sem = refs[2 * n_cast + 2:]
    i = pl.program_id(0)
    j = pl.program_id(1)

    for src, dst in zip(cast_in, cast_out, strict=True):
        dst[...] = src[...].astype(BF16)

    def x_copy(tile):
        return pltpu.make_async_copy(x_hbm.at[pl.ds(tile * tm, tm), :], xbuf_ref, sem)

    @pl.when((i == 0) & (j == 0))
    def _():
        x_copy(0).start()

    @pl.when(i == 0)
    def _():
        wres_ref[j] = w_ref[...].T.astype(BF16)

    @pl.when(j == 0)
    def _():
        x_copy(i).wait()
        nb = _rms(xbuf_ref[...], g_ref[...]).astype(BF16)
        xn_ref[...] = nb
        dt_ref[...] = _dot_nt(nb, wdt_ref[...])

    @pl.when((j == 1) & (i + 1 < pl.num_programs(0)))
    def _():
        x_copy(i + 1).start()

    o_ref[...] = _dot(xn_ref[...], wres_ref[j])


def _in_proj(x, g, w_in_t, layer, w_dt_t, tm, cast=()):
    m = x.shape[0]
    n_i = m // tm
    n_j = D_MAIN // IN_TN
    assert n_j >= 2
    cast_in, cast_out, cast_shapes = [], [], []
    n_steps = n_i * n_j
    for w in cast:
        _, w_rows, w_cols = w.shape
        slab_rows = next(r for r in range(BF16_SUBLANES, w_rows + 1, BF16_SUBLANES)
                         if w_rows % r == 0 and w_rows // r <= n_steps)
        n_slabs = w_rows // slab_rows
        hold = n_steps // n_slabs
        slab = lambda i, j, n_slabs=n_slabs, hold=hold: jnp.minimum((i * n_j + j) // hold, n_slabs - 1)
        cast_in.append(pl.BlockSpec((None, slab_rows, w_cols), lambda i, j, slab=slab: (layer, slab(i, j), 0)))
        cast_out.append(pl.BlockSpec((slab_rows, w_cols), lambda i, j, slab=slab: (slab(i, j), 0)))
        cast_shapes.append(jax.ShapeDtypeStruct((w_rows, w_cols), BF16))
    return pl.pallas_call(
        functools.partial(_in_proj_kernel, tm=tm, n_cast=len(cast)),
        grid=(n_i, n_j),
        in_specs=[
            pl.BlockSpec(memory_space=pl.ANY),
            pl.BlockSpec((1, D_MODEL), lambda i, j: (0, 0)),
            pl.BlockSpec((None, IN_TN, D_MODEL), lambda i, j: (layer, jnp.where(i == 0, j, n_j - 1), 0)),
            pl.BlockSpec((HEAD_PAD, D_MODEL), lambda i, j: (0, 0)),
            *cast_in,
        ],
        out_specs=[
            pl.BlockSpec((tm, IN_TN), lambda i, j: (i, j)),
            pl.BlockSpec((tm, HEAD_PAD), lambda i, j: (i, 0)),
            *cast_out,
        ],
        out_shape=[
            jax.ShapeDtypeStruct((m, D_MAIN), F32),
            jax.ShapeDtypeStruct((m, HEAD_PAD), F32),
            *cast_shapes,
        ],
        scratch_shapes=[
            pltpu.VMEM((tm, D_MODEL), BF16),
            pltpu.VMEM((n_j, D_MODEL, IN_TN), BF16),
            pltpu.VMEM((tm, D_MODEL), F32),
            pltpu.SemaphoreType.DMA(()),
        ],
        compiler_params=_params("arbitrary", "arbitrary"),
        name="in_proj",
    )(x, g, w_in_t, w_dt_t, *cast)


def _mask_chunk_weights(ws_ref, wm_ref):
    row = lax.broadcasted_iota(jnp.int32, (CHUNK_LEN, CHUNK_LEN), 0)
    col = lax.broadcasted_iota(jnp.int32, (CHUNK_LEN, CHUNK_LEN), 1)
    for h in range(CHUNK_HEADS):
        wm_ref[:, h * CHUNK_LEN:(h + 1) * CHUNK_LEN] = jnp.where(row >= col, ws_ref[h], 0.0).astype(BF16)


def _chunk_gate(u_ref, v_ref, gv_ref, wm_ref, bst_ref, go_ref, store_y):
    vb = _rms(jax.nn.gelu(v_ref[...]), gv_ref[...]).astype(BF16)
    yield
    u = jax.nn.gelu(u_ref[...])
    yield
    zeros = jnp.zeros((CHUNK_LEN, CHUNK_HEAD_DIM), BF16)
    parts = []
    for h0 in range(0, CHUNK_HEADS, 2):
        w = wm_ref[:, h0 * CHUNK_LEN:(h0 + 2) * CHUNK_LEN]
        va = vb[:, h0 * CHUNK_HEAD_DIM:(h0 + 1) * CHUNK_HEAD_DIM]
        vb2 = vb[:, (h0 + 1) * CHUNK_HEAD_DIM:(h0 + 2) * CHUNK_HEAD_DIM]
        rhs = jnp.concatenate([jnp.concatenate([va, zeros], axis=1),
                               jnp.concatenate([zeros, vb2], axis=1)], axis=0)
        mixed = _dot(w, rhs)
        for k, h in enumerate((h0, h0 + 1)):
            sl = slice(h * CHUNK_HEAD_DIM, (h + 1) * CHUNK_HEAD_DIM)
            mh = mixed[:, k * CHUNK_HEAD_DIM:(k + 1) * CHUNK_HEAD_DIM] + bst_ref[:, h:h + 1]
            parts.append(u[:, sl] * mh)
        yield
    y = jnp.concatenate(parts, axis=1)
    store_y(_rms(y, go_ref[...]).astype(BF16))


CONV_PAD = 8


def _group_norm_out(y, z, g):
    y = y * _silu(z)
    parts = []
    for grp in range(SSM_GROUPS):
        yg = y[:, grp * GROUP_WIDTH:(grp + 1) * GROUP_WIDTH]
        parts.append(yg * lax.rsqrt(jnp.mean(yg * yg, axis=-1, keepdims=True) + EPS))
    return (jnp.concatenate(parts, axis=1) * g).astype(BF16)


def _split3(x):
    p1 = x.astype(BF16)
    r1 = x - p1.astype(F32)
    p2 = r1.astype(BF16)
    p3 = (r1 - p2.astype(F32)).astype(BF16)
    return p1, p2, p3


def _select_rows(sel3, x):
    return _dot(sel3, jnp.concatenate(_split3(x), axis=0))


def _select_cols(x, sel3):
    return _dot(jnp.concatenate(_split3(x), axis=1), sel3)


def _ssd_chunk(first, xbc_ref, z_ref, dt_ref, cw_ref, cb_ref, dtb_ref, alog_ref, drep_ref, ng_ref,
               tril3_ref, nconv_ref, nssm_ref, tail_ref, ht_ref, store_y):
    q = SSD_CHUNK

    xbc = xbc_ref[...]
    tail = jnp.where(first, 0.0, tail_ref[...])
    ht_prev = jnp.where(first, 0.0, ht_ref[...])
    sub = lax.broadcasted_iota(jnp.int32, (CONV_PAD, CONV_DIM), 0)
    conv = cb_ref[...] + xbc * cw_ref[CONV_W - 1:CONV_W, :]
    for k in range(1, CONV_W):
        shifted = pltpu.roll(xbc, k, 0)
        head = jnp.where(sub < k, pltpu.roll(tail, k, 0), shifted[0:CONV_PAD, :])
        shifted = jnp.concatenate([head, shifted[CONV_PAD:, :]], axis=0)
        conv = conv + shifted * cw_ref[CONV_W - 1 - k:CONV_W - k, :]
    tail_ref[...] = xbc[q - CONV_PAD:, :]
    nconv_ref[...] = xbc[q - (CONV_W - 1):, :]
    yield

    act = _silu(conv)
    xs = act[:, :D_SSM]
    bmat = act[:, D_SSM:D_SSM + SSM_GROUPS * D_STATE]
    cmat = act[:, D_SSM + SSM_GROUPS * D_STATE:]
    yield

    dt = _softplus(dt_ref[...] + dtb_ref[...])
    a = -jnp.exp(alog_ref[...])
    acum = _select_rows(tril3_ref[...], dt * a)
    acum_t = acum.T
    low_lanes = lax.broadcasted_iota(jnp.int32, (q, 2 * SSM_HEAD_DIM), 1) < SSM_HEAD_DIM

    def expand(cols):
        pairs = []
        for h0 in range(0, SSM_HEADS, 2):
            a = jnp.broadcast_to(cols[:, h0:h0 + 1], (q, 2 * SSM_HEAD_DIM))
            b = jnp.broadcast_to(cols[:, h0 + 1:h0 + 2], (q, 2 * SSM_HEAD_DIM))
            pairs.append(jnp.where(low_lanes, a, b))
        return jnp.concatenate(pairs, axis=1)

    dt_rep = expand(dt)
    yield
    acum_rep = expand(acum)
    yield
    xdt = xs * dt_rep
    exp_a = jnp.exp(acum_rep)
    xw = xdt * jnp.exp(acum_rep[q - 1:q, :] - acum_rep)
    yield

    row = lax.broadcasted_iota(jnp.int32, (q, q), 0)
    col = lax.broadcasted_iota(jnp.int32, (q, q), 1)
    causal = row >= col
    low_half = col < SSM_HEAD_DIM
    ydiag, yoff, states = [], [], []
    for grp in range(SSM_GROUPS):
        bg_f = bmat[:, grp * D_STATE:(grp + 1) * D_STATE]
        bg = bg_f.astype(BF16)
        cg = cmat[:, grp * D_STATE:(grp + 1) * D_STATE].astype(BF16)
        cb = _dot_nt(cg, bg)
        ch = slice(grp * GROUP_WIDTH, (grp + 1) * GROUP_WIDTH)
        yoff.append(_dot(cg, ht_prev[:, ch].astype(BF16)))
        yield
        for quad in range(HEADS_PER_GROUP // 4):
            h0 = grp * HEADS_PER_GROUP + 4 * quad
            masks, blocks = [], []
            for k, h in enumerate(range(h0, h0 + 4)):
                diff = acum[:, h:h + 1] - acum_t[h:h + 1, :]
                masks.append((cb * jnp.exp(jnp.where(causal, diff, -jnp.inf))).astype(BF16))
                xp = xdt[:, (h - k % 2) * SSM_HEAD_DIM:(h - k % 2 + 2) * SSM_HEAD_DIM]
                own = jnp.where(low_half == (k % 2 == 0), xp, 0.0).astype(BF16)
                zero = jnp.zeros_like(own)
                blocks.append(jnp.concatenate([own, zero] if k < 2 else [zero, own], axis=1))
                if k % 2 == 1:
                    yield
            ydiag.append(_dot(jnp.concatenate(masks, axis=1), jnp.concatenate(blocks, axis=0)))
            yield
        states.append(_dot(bg_f.T.astype(BF16), xw[:, ch].astype(BF16)))
        yield
    y = drep_ref[...] * xs + jnp.concatenate(ydiag, axis=1) + jnp.concatenate(yoff, axis=1) * exp_a

    ht_new = ht_prev * exp_a[q - 1:q, :] + jnp.concatenate(states, axis=1)
    ht_ref[...] = ht_new
    nssm_ref[...] = ht_new.T
    yield
    store_y(_group_norm_out(y, z_ref[...], ng_ref[...]))


def _sample_rows_kernel(u_ref, v_ref, xbc_ref, dt_ref, sc_ref, gv_ref, w0_ref, b0_ref, go_ref,
                        cw_ref, cb_ref, dtb_ref, alog_ref, drep_ref, rep_ref, rep8_ref,
                        ya_ref, vrows_ref, nconv_ref, split_ref, b_ref, c_ref, dx_ref):
    vr = _rms(jax.nn.gelu(v_ref[...]), gv_ref[...])
    vrows_ref[...] = vr
    mixed = vr * w0_ref[...] + b0_ref[...]
    ya_ref[...] = _rms(jax.nn.gelu(u_ref[...]) * mixed, go_ref[...]).astype(BF16)

    xbc = xbc_ref[...]
    conv = cb_ref[...]
    for k in range(CONV_W - 1):
        conv = conv + sc_ref[:, k * CONV_DIM:(k + 1) * CONV_DIM] * cw_ref[k:k + 1, :]
    conv = conv + xbc * cw_ref[CONV_W - 1:CONV_W, :]
    nconv_ref[:, 0:(CONV_W - 2) * CONV_DIM] = sc_ref[:, CONV_DIM:(CONV_W - 1) * CONV_DIM]
    nconv_ref[:, (CONV_W - 2) * CONV_DIM:] = xbc

    act = _silu(conv)
    xs = act[:, :D_SSM]
    b_ref[...] = act[:, D_SSM:D_SSM + SSM_GROUPS * D_STATE]
    c_ref[...] = act[:, D_SSM + SSM_GROUPS * D_STATE:]
    dx_ref[...] = drep_ref[...] * xs

    dt = _softplus(dt_ref[...] + dtb_ref[...])
    decay = jnp.exp(dt * (-jnp.exp(alog_ref[...])))
    xdt = xs * _dot_exact(dt, rep_ref[...])
    cols = jnp.concatenate([xdt, _dot_exact(decay, rep8_ref[...])], axis=1).T
    for p in range(N_SPLIT):
        piece = cols.astype(BF16)
        split_ref[:, p * LANES:(p + 1) * LANES] = piece
        cols = cols - piece.astype(F32)


SPLIT_ROWS = D_SSM + HEAD_PAD


def _sample_rows(proj, dt_raw, sconv, gv, w0, b0, go, cw, cb, dtb, alog, drep, rep, rep8):
    n = proj.shape[0]
    full = lambda shape: pl.BlockSpec(shape, lambda i: (0,) * len(shape))
    return pl.pallas_call(
        _sample_rows_kernel,
        grid=(1,),
        in_specs=[
            pl.BlockSpec((n, D_CHUNK), lambda i: (0, U_BLK)),
            pl.BlockSpec((n, D_CHUNK), lambda i: (0, V_BLK)),
            pl.BlockSpec((n, CONV_DIM), lambda i: (0, XBC_BLK)),
            full((n, HEAD_PAD)),
            full((n, (CONV_W - 1) * CONV_DIM)),
            full((1, D_CHUNK)), full((1, D_CHUNK)), full((1, D_CHUNK)), full((1, D_CHUNK)),
            full((CONV_W, CONV_DIM)), full((1, CONV_DIM)),
            full((1, HEAD_PAD)), full((1, HEAD_PAD)), full((1, D_SSM)),
            full((HEAD_PAD, D_SSM)), full((HEAD_PAD, HEAD_PAD)),
        ],
        out_specs=[
            full((n, D_CHUNK)), full((n, D_CHUNK)), full((n, (CONV_W - 1) * CONV_DIM)),
            full((SPLIT_ROWS, N_SPLIT * LANES)),
            full((n, SSM_GROUPS * D_STATE)), full((n, SSM_GROUPS * D_STATE)), full((n, D_SSM)),
        ],
        out_shape=[
            jax.ShapeDtypeStruct((n, D_CHUNK), BF16),
            jax.ShapeDtypeStruct((n, D_CHUNK), F32),
            jax.ShapeDtypeStruct((n, (CONV_W - 1) * CONV_DIM), F32),
            jax.ShapeDtypeStruct((SPLIT_ROWS, N_SPLIT * LANES), BF16),
            jax.ShapeDtypeStruct((n, SSM_GROUPS * D_STATE), F32),
            jax.ShapeDtypeStruct((n, SSM_GROUPS * D_STATE), F32),
            jax.ShapeDtypeStruct((n, D_SSM), F32),
        ],
        compiler_params=_params("arbitrary"),
        name="sample_rows",
    )(proj, proj, proj, dt_raw, sconv, gv, w0, b0, go, cw, cb, dtb, alog, drep, rep, rep8)


SAMPLE_TB = 8


def _sample_state_kernel(split_ref, b_ref, c_ref, h0_ref, dx_ref, z_ref, ng_ref,
                         hn_ref, y_ref, yt_ref):
    i = pl.program_id(0)
    n_tok = b_ref.shape[0]

    @pl.when(i == 0)
    def _():
        yt_ref[...] = jnp.zeros_like(yt_ref)

    tok = lax.broadcasted_iota(jnp.int32, (N_SPLIT * LANES, D_STATE), 0) % LANES
    lane = lax.broadcasted_iota(jnp.int32, (D_SSM, n_tok), 1)

    def per_group(row, grp):
        return jnp.broadcast_to(row[:, grp * D_STATE:(grp + 1) * D_STATE], (GROUP_WIDTH, D_STATE))

    for k in range(SAMPLE_TB):
        t = i * SAMPLE_TB + k
        onehot = (tok == t).astype(BF16)
        bcast = _dot(split_ref[...], onehot)
        decay = jnp.broadcast_to(
            bcast[D_SSM:, :].reshape(SSM_HEADS, 1, SUBLANES, D_STATE),
            (SSM_HEADS, SSM_HEAD_DIM // SUBLANES, SUBLANES, D_STATE)).reshape(D_SSM, D_STATE)
        brow = b_ref[pl.ds(t, 1), :]
        crow = c_ref[pl.ds(t, 1), :]
        bfull = jnp.concatenate([per_group(brow, g) for g in range(SSM_GROUPS)], axis=0)
        cfull = jnp.concatenate([per_group(crow, g) for g in range(SSM_GROUPS)], axis=0)
        h_new = h0_ref[k] * decay + bcast[:D_SSM, :] * bfull
        hn_ref[k] = h_new
        ysum = jnp.sum(h_new * cfull, axis=-1, keepdims=True)
        yt_ref[...] = jnp.where(lane == t, ysum, yt_ref[...])

    @pl.when(i == pl.num_programs(0) - 1)
    def _():
        y = yt_ref[...].T + dx_ref[...]
        y_ref[...] = _group_norm_out(y, z_ref[...], ng_ref[...])


def _sample_state(split, bmat, cmat, h0, dx, proj, ng):
    n = h0.shape[0]
    const2 = lambda i: (0, 0)
    return pl.pallas_call(
        _sample_state_kernel,
        grid=(n // SAMPLE_TB,),
        in_specs=[
            pl.BlockSpec((SPLIT_ROWS, N_SPLIT * LANES), const2),
            pl.BlockSpec((n, SSM_GROUPS * D_STATE), const2),
            pl.BlockSpec((n, SSM_GROUPS * D_STATE), const2),
            pl.BlockSpec((SAMPLE_TB, D_SSM, D_STATE), lambda i: (i, 0, 0)),
            pl.BlockSpec((n, D_SSM), const2),
            pl.BlockSpec((n, D_SSM), lambda i: (0, Z_BLK)),
            pl.BlockSpec((1, D_SSM), const2),
        ],
        out_specs=[
            pl.BlockSpec((SAMPLE_TB, D_SSM, D_STATE), lambda i: (i, 0, 0)),
            pl.BlockSpec((n, D_SSM), const2),
        ],
        out_shape=[
            jax.ShapeDtypeStruct((n, D_SSM, D_STATE), F32),
            jax.ShapeDtypeStruct((n, D_SSM), BF16),
        ],
        scratch_shapes=[pltpu.VMEM((D_SSM, n), F32)],
        compiler_params=_params("arbitrary"),
        name="sample_state",
    )(split, bmat, cmat, h0, dx, proj, ng)


OUT_TN = 1024


def _out_proj_kernel(x_ref, ya_ref, yb_ref, w_ref, o_ref, wres_ref):
    i = pl.program_id(0)
    j = pl.program_id(1)

    @pl.when(i == 0)
    def _():
        wres_ref[j] = w_ref[...].astype(BF16)

    y = jnp.concatenate([ya_ref[...], yb_ref[...]], axis=1)
    o_ref[...] = x_ref[...] + _dot(y, wres_ref[j])


def _out_proj(x, ya, yb, w_out, layer, tm):
    m = x.shape[0]
    n_j = D_MODEL // OUT_TN
    return pl.pallas_call(
        _out_proj_kernel,
        grid=(m // tm, n_j),
        in_specs=[
            pl.BlockSpec((tm, OUT_TN), lambda i, j: (i, j)),
            pl.BlockSpec((tm, D_CHUNK), lambda i, j: (i, 0)),
            pl.BlockSpec((tm, D_SSM), lambda i, j: (i, 0)),
            pl.BlockSpec((None, D_MODEL, OUT_TN), lambda i, j: (layer, 0, jnp.where(i == 0, j, n_j - 1))),
        ],
        out_specs=pl.BlockSpec((tm, OUT_TN), lambda i, j: (i, j)),
        out_shape=jax.ShapeDtypeStruct((m, D_MODEL), F32),
        scratch_shapes=[pltpu.VMEM((n_j, D_MODEL, OUT_TN), BF16)],
        compiler_params=_params("arbitrary", "arbitrary"),
        name="out_proj",
    )(x, ya, yb, w_out)


FFN_TF = 512


def _ffn_kernel(h_ref, g_ref, wg_ref, wu_ref, wd_ref, gf_ref, o_ref, m_ref):
    f = pl.program_id(1)

    @pl.when(f == 0)
    def _():
        h = h_ref[...]
        m_ref[...] = _rms(h, g_ref[...]).astype(BF16)
        o_ref[...] = h

    m = m_ref[...]
    act = (_silu(_dot(m, wg_ref[...])) * _dot(m, wu_ref[...])).astype(BF16)
    o_ref[...] += _dot(act, wd_ref[...])

    @pl.when(f == pl.num_programs(1) - 1)
    def _():
        o_ref[...] = _rms(o_ref[...], gf_ref[...])


def _ffn(h, g, w_gate, w_up, w_down, g_final, tm):
    m = h.shape[0]
    return pl.pallas_call(
        _ffn_kernel,
        grid=(m // tm, D_FF // FFN_TF),
        in_specs=[
            pl.BlockSpec((tm, D_MODEL), lambda i, f: (i, 0)),
            pl.BlockSpec((1, D_MODEL), lambda i, f: (0, 0)),
            pl.BlockSpec((D_MODEL, FFN_TF), lambda i, f: (0, f)),
            pl.BlockSpec((D_MODEL, FFN_TF), lambda i, f: (0, f)),
            pl.BlockSpec((FFN_TF, D_MODEL), lambda i, f: (f, 0)),
            pl.BlockSpec((1, D_MODEL), lambda i, f: (0, 0)),
        ],
        out_specs=pl.BlockSpec((tm, D_MODEL), lambda i, f: (i, 0)),
        out_shape=jax.ShapeDtypeStruct((m, D_MODEL), F32),
        scratch_shapes=[pltpu.VMEM((tm, D_MODEL), BF16)],
        compiler_params=_params("arbitrary", "arbitrary"),
        name="ffn",
    )(h, g, w_gate, w_up, w_down, g_final)


MIX_TM = 512
CHUNKS_PER_TILE = MIX_TM // SSD_CHUNK
FFN_UP_PIECE = 256
FFN_UP_K = 2048
FFN_DOWN_PIECE = 512
MIX_PHASES_PER_FFN_PIECE = 5


def _interleave(main, side, side_per_main):
    for _ in main:
        for _ in range(side_per_main):
            next(side, None)
    for _ in side:
        pass


def _mix_ffn_kernel(x_ref, wo_ref, g_ref, wg_ref, wu_ref, wd_ref, gf_ref,
                    u_ref, v_ref, gv_ref, ws_ref, bst_ref, go_ref,
                    xbc_ref, z_ref, dt_ref, cw_ref, cb_ref, dtb_ref, alog_ref, drep_ref, ng_ref,
                    tril3_ref,
                    o_ref, nconv_ref, nssm_ref,
                    m_ref, ya_ref, yb_ref, tail_ref, ht_ref, wm_ref, *, chunks_per_seq):
    s = pl.program_id(0)
    f = pl.program_id(1)
    has_ffn = s >= 1
    has_mix = (s < pl.num_programs(0) - 1) & (f < CHUNKS_PER_TILE)

    @pl.when((s == 0) & (f == 0))
    def _():
        _mask_chunk_weights(ws_ref, wm_ref)

    @pl.when(has_ffn & (f == 0))
    def _():
        y = jnp.concatenate([ya_ref[...], yb_ref[...]], axis=1)
        h = x_ref[...] + _dot(y, wo_ref[...])
        m_ref[...] = _rms(h, g_ref[...]).astype(BF16)
        o_ref[...] = h

    def ffn_step():
        acts = []
        for c in range(FFN_TF // FFN_UP_PIECE):
            cols = slice(c * FFN_UP_PIECE, (c + 1) * FFN_UP_PIECE)
            halves = []
            for w_ref in (wg_ref, wu_ref):
                acc = None
                for k in range(D_MODEL // FFN_UP_K):
                    rows = slice(k * FFN_UP_K, (k + 1) * FFN_UP_K)
                    part = _dot(m_ref[:, rows], w_ref[rows, cols])
                    acc = part if acc is None else acc + part
                    yield
                halves.append(acc)
            acts.append((_silu(halves[0]) * halves[1]).astype(BF16))
        act = jnp.concatenate(acts, axis=1)
        for c in range(D_MODEL // FFN_DOWN_PIECE):
            cols = slice(c * FFN_DOWN_PIECE, (c + 1) * FFN_DOWN_PIECE)
            o_ref[:, cols] += _dot(act, wd_ref[:, cols])
            yield

    chunk_rows = pl.ds(pl.multiple_of(f * SSD_CHUNK, SSD_CHUNK), SSD_CHUNK)

    def store_ya(ya):
        ya_ref[chunk_rows, :] = ya

    def store_yb(yb):
        yb_ref[chunk_rows, :] = yb

    def mix_step():
        first = (s * CHUNKS_PER_TILE + f) % chunks_per_seq == 0
        yield from _chunk_gate(u_ref, v_ref, gv_ref, wm_ref, bst_ref, go_ref, store_ya)
        yield
        yield from _ssd_chunk(first, xbc_ref, z_ref, dt_ref, cw_ref, cb_ref, dtb_ref, alog_ref, drep_ref,
                              ng_ref, tril3_ref, nconv_ref, nssm_ref, tail_ref, ht_ref, store_yb)

    @pl.when(has_ffn & has_mix)
    def _():
        _interleave(ffn_step(), mix_step(), MIX_PHASES_PER_FFN_PIECE)

    @pl.when(has_ffn & jnp.logical_not(has_mix))
    def _():
        _interleave(ffn_step(), iter(()), 0)

    @pl.when(jnp.logical_not(has_ffn) & has_mix)
    def _():
        _interleave(iter(()), mix_step(), 0)

    @pl.when(has_ffn & (f == pl.num_programs(1) - 1))
    def _():
        o_ref[...] = _rms(o_ref[...], gf_ref[...])


def _mix_ffn(x, proj, dt_raw, w_out_b, g, w_gate, w_up, w_down, g_final, gv, ws, bst, go,
             cw, cb, dtb, alog, drep, ng, tril3, batch, seq):
    m = x.shape[0]
    n_tiles = m // MIX_TM
    n_chunks = m // SSD_CHUNK
    chunks_per_seq = seq // SSD_CHUNK
    assert MIX_TM % SSD_CHUNK == 0 and seq % MIX_TM == 0 and CHUNKS_PER_TILE <= D_FF // FFN_TF

    def tile(s, f):
        return jnp.maximum(s - 1, 0)

    def chunk(s, f):
        return jnp.where(s >= n_tiles, n_chunks - 1,
                         s * CHUNKS_PER_TILE + jnp.minimum(f, CHUNKS_PER_TILE - 1))

    def seq_of(s, f):
        return chunk(s, f) // chunks_per_seq

    const2 = lambda s, f: (0, 0)
    return pl.pallas_call(
        functools.partial(_mix_ffn_kernel, chunks_per_seq=chunks_per_seq),
        grid=(n_tiles + 1, D_FF // FFN_TF),
        in_specs=[
            pl.BlockSpec((MIX_TM, D_MODEL), lambda s, f: (tile(s, f), 0)),
            pl.BlockSpec((D_MODEL, D_MODEL), const2, pipeline_mode=pl.Buffered(1)),
            pl.BlockSpec((1, D_MODEL), const2),
            pl.BlockSpec((D_MODEL, FFN_TF), lambda s, f: (0, f)),
            pl.BlockSpec((D_MODEL, FFN_TF), lambda s, f: (0, f)),
            pl.BlockSpec((FFN_TF, D_MODEL), lambda s, f: (f, 0)),
            pl.BlockSpec((1, D_MODEL), const2),
            pl.BlockSpec((CHUNK_LEN, D_CHUNK), lambda s, f: (chunk(s, f), U_BLK)),
            pl.BlockSpec((CHUNK_LEN, D_CHUNK), lambda s, f: (chunk(s, f), V_BLK)),
            pl.BlockSpec((1, D_CHUNK), const2),
            pl.BlockSpec((CHUNK_HEADS, CHUNK_LEN, CHUNK_LEN), lambda s, f: (0, 0, 0)),
            pl.BlockSpec((CHUNK_LEN, CHUNK_HEADS), const2),
            pl.BlockSpec((1, D_CHUNK), const2),
            pl.BlockSpec((SSD_CHUNK, CONV_DIM), lambda s, f: (chunk(s, f), XBC_BLK)),
            pl.BlockSpec((SSD_CHUNK, D_SSM), lambda s, f: (chunk(s, f), Z_BLK)),
            pl.BlockSpec((SSD_CHUNK, HEAD_PAD), lambda s, f: (chunk(s, f), 0)),
            pl.BlockSpec((CONV_W, CONV_DIM), const2),
            pl.BlockSpec((1, CONV_DIM), const2),
            pl.BlockSpec((1, HEAD_PAD), const2),
            pl.BlockSpec((1, HEAD_PAD), const2),
            pl.BlockSpec((1, D_SSM), const2),
            pl.BlockSpec((1, D_SSM), const2),
            pl.BlockSpec((SSD_CHUNK, N_SPLIT * SSD_CHUNK), const2),
        ],
        out_specs=[
            pl.BlockSpec((MIX_TM, D_MODEL), lambda s, f: (tile(s, f), 0)),
            pl.BlockSpec((None, CONV_W - 1, CONV_DIM), lambda s, f: (seq_of(s, f), 0, 0)),
            pl.BlockSpec((None, D_SSM, D_STATE), lambda s, f: (seq_of(s, f), 0, 0)),
        ],
        out_shape=[
            jax.ShapeDtypeStruct((m, D_MODEL), F32),
            jax.ShapeDtypeStruct((batch, CONV_W - 1, CONV_DIM), F32),
            jax.ShapeDtypeStruct((batch, D_SSM, D_STATE), F32),
        ],
        scratch_shapes=[
            pltpu.VMEM((MIX_TM, D_MODEL), BF16),
            pltpu.VMEM((MIX_TM, D_CHUNK), BF16),
            pltpu.VMEM((MIX_TM, D_SSM), BF16),
            pltpu.VMEM((CONV_PAD, CONV_DIM), F32),
            pltpu.VMEM((D_STATE, D_SSM), F32),
            pltpu.VMEM((CHUNK_LEN, CHUNK_HEADS * CHUNK_LEN), BF16),
        ],
        compiler_params=_params("arbitrary", "arbitrary"),
        name="mix_ffn",
    )(x, w_out_b, g, w_gate, w_up, w_down, g_final, proj, proj, gv, ws, bst, go, proj, proj, dt_raw,
      cw, cb, dtb, alog, drep, ng, tril3)


def _row(v, width=None):
    v = v.astype(F32).reshape(1, -1)
    if width is not None and v.shape[1] < width:
        v = jnp.pad(v, ((0, 0), (0, width - v.shape[1])))
    return v


def kernel(x_prompt, x_sample, state_conv, state_ssm, norm_mix_g, w_in, chunk_v_norm_g, chunk_w_s,
           chunk_b_s, chunk_out_norm_g, ssd_conv_w, ssd_conv_b, ssd_dt_bias, ssd_a_log, ssd_d,
           ssd_norm_g, w_out, norm_ffn_g, w_gate, w_up, w_down, norm_final_g):
    depth = w_in.shape[0]
    batch, seq, _ = x_prompt.shape
    n_s = x_sample.shape[0]
    assert x_sample.shape[1] == 1 and seq % SSD_CHUNK == 0 and depth == 1

    rep = (lax.broadcasted_iota(jnp.int32, (HEAD_PAD, D_SSM), 0)
           == lax.broadcasted_iota(jnp.int32, (HEAD_PAD, D_SSM), 1) // SSM_HEAD_DIM).astype(F32)
    rep8 = (lax.broadcasted_iota(jnp.int32, (HEAD_PAD, HEAD_PAD), 0)
            == lax.broadcasted_iota(jnp.int32, (HEAD_PAD, HEAD_PAD), 1) // SUBLANES).astype(F32)
    tril3 = jnp.tile(jnp.tril(jnp.ones((SSD_CHUNK, SSD_CHUNK), F32)), (1, N_SPLIT)).astype(BF16)

    hp = x_prompt.reshape(batch * seq, D_MODEL)
    hs = x_sample.reshape(n_s, D_MODEL)
    l = 0
    w_in_t = jnp.swapaxes(w_in, 1, 2)
    w_dt_t = jnp.pad(w_in_t[l, D_MAIN:, :], ((0, HEAD_PAD - SSM_HEADS), (0, 0))).astype(BF16)
    g_mix = _row(norm_mix_g[l])
    gv = _row(chunk_v_norm_g[l])
    go = _row(chunk_out_norm_g[l])
    cw = ssd_conv_w[l].astype(F32)
    cb = _row(ssd_conv_b[l])
    dtb = _row(ssd_dt_bias[l], HEAD_PAD)
    alog = _row(ssd_a_log[l], HEAD_PAD)
    drep = _row(jnp.repeat(ssd_d[l], SSM_HEAD_DIM))
    ng = _row(ssd_norm_g[l])
    g_ffn = _row(norm_ffn_g[l])
    g_final = _row(norm_final_g)

    proj_p, dt_p, w_gate_b, w_up_b, w_down_b, w_out_b = _in_proj(
        hp, g_mix, w_in_t, l, w_dt_t, tm=1024, cast=(w_gate, w_up, w_down, w_out))
    y_p, conv_p, ssm_p = _mix_ffn(hp, proj_p, dt_p, w_out_b, g_ffn, w_gate_b, w_up_b, w_down_b, g_final,
                                  gv, chunk_w_s[l].astype(F32), chunk_b_s[l].astype(F32).T, go,
                                  cw, cb, dtb, alog, drep, ng, tril3, batch, seq)

    proj_s, dt_s = _in_proj(hs, g_mix, w_in_t, l, w_dt_t, tm=n_s)
    w0 = _row(jnp.repeat(chunk_w_s[l, :, 0, 0], CHUNK_HEAD_DIM))
    b0 = _row(jnp.repeat(chunk_b_s[l, :, 0], CHUNK_HEAD_DIM))
    ya_s, v_s, conv_s, split, b_s, c_s, dx_s = _sample_rows(
        proj_s, dt_s, state_conv[l].reshape(n_s, (CONV_W - 1) * CONV_DIM), gv, w0, b0, go,
        cw, cb, dtb, alog, drep, rep, rep8)
    ssm_s, yb_s = _sample_state(split, b_s, c_s, state_ssm[l].reshape(n_s, D_SSM, D_STATE),
                                dx_s, proj_s, ng)
    h_s = _out_proj(hs, ya_s, yb_s, w_out, l, tm=n_s)
    y_s = _ffn(h_s, g_ffn, w_gate_b, w_up_b, w_down_b, g_final, tm=n_s)

    return (
        y_p.reshape(batch, seq, D_MODEL),
        y_s.reshape(n_s, 1, D_MODEL),
        conv_p[None],
        ssm_p.reshape(1, batch, SSM_HEADS, SSM_HEAD_DIM, D_STATE),
        conv_s.reshape(1, n_s, CONV_W - 1, CONV_DIM),
        ssm_s.reshape(1, n_s, SSM_HEADS, SSM_HEAD_DIM, D_STATE),
        v_s.reshape(1, n_s, 1, D_CHUNK),
    )
```

```python
import functools

import jax
import jax.numpy as jnp
from jax import lax
from jax.experimental import pallas as pl
from jax.experimental.pallas import tpu as pltpu

F32 = jnp.float32
BF16 = jnp.bfloat16
HIGHEST = lax.Precision.HIGHEST

D_MODEL = 2048
D_CHUNK = 1024
CHUNK_HEADS = 8
CHUNK_HEAD_DIM = 128
CHUNK_LEN = 128
D_SSM = 1024
SSM_HEAD_DIM = 64
SSM_HEADS = 16
SSM_GROUPS = 2
HEADS_PER_GROUP = SSM_HEADS // SSM_GROUPS
GROUP_WIDTH = D_SSM // SSM_GROUPS
D_STATE = 128
CONV_W = 4
CONV_DIM = D_SSM + 2 * SSM_GROUPS * D_STATE
SSD_CHUNK = 128
D_MAIN = 2 * D_CHUNK + D_SSM + CONV_DIM
D_FF = 5632
EPS = 1e-6

LANES = 128
SUBLANES = 8
BF16_SUBLANES = 16
HEAD_PAD = LANES
N_SPLIT = 3
VMEM_LIMIT = 56 * 1024 * 1024

U_BLK, V_BLK, Z_BLK = 0, 1, 2
XBC_BLK = 2


def _rms(x, g):
    return x * lax.rsqrt(jnp.mean(x * x, axis=-1, keepdims=True) + EPS) * g


def _silu(x):
    return x * jax.nn.sigmoid(x)


def _softplus(x):
    return jnp.maximum(x, 0.0) + jnp.log1p(jnp.exp(-jnp.abs(x)))


def _dot(a, b):
    return jnp.dot(a, b, preferred_element_type=F32)


def _dot_nt(a, b):
    return lax.dot_general(a, b, (((1,), (1,)), ((), ())), preferred_element_type=F32)


def _dot_exact(a, b):
    return jnp.dot(a, b, precision=HIGHEST, preferred_element_type=F32)


def _params(*semantics):
    return pltpu.CompilerParams(dimension_semantics=semantics, vmem_limit_bytes=VMEM_LIMIT)


IN_TN = 512


def _in_proj_kernel(x_hbm, g_ref, w_ref, wdt_ref, *refs, tm, n_cast):
    cast_in = refs[:n_cast]
    o_ref, dt_ref = refs[n_cast:n_cast + 2]
    cast_out = refs[n_cast + 2:2 * n_cast + 2]
    xn_ref, wres_ref, xbuf_ref, sem = refs[2 * n_cast + 2:]
    i = pl.program_id(0)
    j = pl.program_id(1)

    for src, dst in zip(cast_in, cast_out, strict=True):
        dst[...] = src[...].astype(BF16)

    def x_copy(tile):
        return pltpu.make_async_copy(x_hbm.at[pl.ds(tile * tm, tm), :], xbuf_ref, sem)

    @pl.when((i == 0) & (j == 0))
    def _():
        x_copy(0).start()

    @pl.when(i == 0)
    def _():
        wres_ref[j] = w_ref[...].T.astype(BF16)

    @pl.when(j == 0)
    def _():
        x_copy(i).wait()
        nb = _rms(xbuf_ref[...], g_ref[...]).astype(BF16)
        xn_ref[...] = nb
        dt_ref[...] = _dot_nt(nb, wdt_ref[...])

    @pl.when((j == 1) & (i + 1 < pl.num_programs(0)))
    def _():
        x_copy(i + 1).start()

    o_ref[...] = _dot(xn_ref[...], wres_ref[j])


def _in_proj(x, g, w_in_t, layer, w_dt_t, tm, cast=()):
    m = x.shape[0]
    n_i = m // tm
    n_j = D_MAIN // IN_TN
    assert n_j >= 2
    cast_in, cast_out, cast_shapes = [], [], []
    n_steps = n_i * n_j
    for w in cast:
        _, w_rows, w_cols = w.shape
        slab_rows = next(r for r in range(BF16_SUBLANES, w_rows + 1, BF16_SUBLANES)
                         if w_rows % r == 0 and w_rows // r <= n_steps)
        n_slabs = w_rows // slab_rows
        hold = n_steps // n_slabs
        slab = lambda i, j, n_slabs=n_slabs, hold=hold: jnp.minimum((i * n_j + j) // hold, n_slabs - 1)
        cast_in.append(pl.BlockSpec((None, slab_rows, w_cols), lambda i, j, slab=slab: (layer, slab(i, j), 0)))
        cast_out.append(pl.BlockSpec((slab_rows, w_cols), lambda i, j, slab=slab: (slab(i, j), 0)))
        cast_shapes.append(jax.ShapeDtypeStruct((w_rows, w_cols), BF16))
    return pl.pallas_call(
        functools.partial(_in_proj_kernel, tm=tm, n_cast=len(cast)),
        grid=(n_i, n_j),
        in_specs=[
            pl.BlockSpec(memory_space=pl.ANY),
            pl.BlockSpec((1, D_MODEL), lambda i, j: (0, 0)),
            pl.BlockSpec((None, IN_TN, D_MODEL), lambda i, j: (layer, jnp.where(i == 0, j, n_j - 1), 0)),
            pl.BlockSpec((HEAD_PAD, D_MODEL), lambda i, j: (0, 0)),
            *cast_in,
        ],
        out_specs=[
            pl.BlockSpec((tm, IN_TN), lambda i, j: (i, j)),
            pl.BlockSpec((tm, HEAD_PAD), lambda i, j: (i, 0)),
            *cast_out,
        ],
        out_shape=[
            jax.ShapeDtypeStruct((m, D_MAIN), F32),
            jax.ShapeDtypeStruct((m, HEAD_PAD), F32),
            *cast_shapes,
        ],
        scratch_shapes=[
            pltpu.VMEM((tm, D_MODEL), BF16),
            pltpu.VMEM((n_j, D_MODEL, IN_TN), BF16),
            pltpu.VMEM((tm, D_MODEL), F32),
            pltpu.SemaphoreType.DMA(()),
        ],
        compiler_params=_params("arbitrary", "arbitrary"),
        name="in_proj",
    )(x, g, w_in_t, w_dt_t, *cast)


def _mask_chunk_weights(ws_ref, wm_ref):
    row = lax.broadcasted_iota(jnp.int32, (CHUNK_LEN, CHUNK_LEN), 0)
    col = lax.broadcasted_iota(jnp.int32, (CHUNK_LEN, CHUNK_LEN), 1)
    for h in range(CHUNK_HEADS):
        wm_ref[:, h * CHUNK_LEN:(h + 1) * CHUNK_LEN] = jnp.where(row >= col, ws_ref[h], 0.0).astype(BF16)


def _chunk_gate(u_ref, v_ref, gv_ref, wm_ref, bst_ref, go_ref, store_y):
    vb = _rms(jax.nn.gelu(v_ref[...]), gv_ref[...]).astype(BF16)
    yield
    u = jax.nn.gelu(u_ref[...])
    yield
    zeros = jnp.zeros((CHUNK_LEN, CHUNK_HEAD_DIM), BF16)
    parts = []
    for h0 in range(0, CHUNK_HEADS, 2):
        w = wm_ref[:, h0 * CHUNK_LEN:(h0 + 2) * CHUNK_LEN]
        va = vb[:, h0 * CHUNK_HEAD_DIM:(h0 + 1) * CHUNK_HEAD_DIM]
        vb2 = vb[:, (h0 + 1) * CHUNK_HEAD_DIM:(h0 + 2) * CHUNK_HEAD_DIM]
        rhs = jnp.concatenate([jnp.concatenate([va, zeros], axis=1),
                               jnp.concatenate([zeros, vb2], axis=1)], axis=0)
        mixed = _dot(w, rhs)
        for k, h in enumerate((h0, h0 + 1)):
            sl = slice(h * CHUNK_HEAD_DIM, (h + 1) * CHUNK_HEAD_DIM)
            mh = mixed[:, k * CHUNK_HEAD_DIM:(k + 1) * CHUNK_HEAD_DIM] + bst_ref[:, h:h + 1]
            parts.append(u[:, sl] * mh)
        yield
    y = jnp.concatenate(parts, axis=1)
    store_y(_rms(y, go_ref[...]).astype(BF16))


CONV_PAD = 8


def _group_norm_out(y, z, g):
    y = y * _silu(z)
    parts = []
    for grp in range(SSM_GROUPS):
        yg = y[:, grp * GROUP_WIDTH:(grp + 1) * GROUP_WIDTH]
        parts.append(yg * lax.rsqrt(jnp.mean(yg * yg, axis=-1, keepdims=True) + EPS))
    return (jnp.concatenate(parts, axis=1) * g).astype(BF16)


def _split3(x):
    p1 = x.astype(BF16)
    r1 = x - p1.astype(F32)
    p2 = r1.astype(BF16)
    p3 = (r1 - p2.astype(F32)).astype(BF16)
    return p1, p2, p3


def _select_rows(sel3, x):
    return _dot(sel3, jnp.concatenate(_split3(x), axis=0))


def _ssd_chunk(first, xbc_ref, z_ref, dt_ref, cw_ref, cb_ref, dtb_ref, alog_ref, drep_ref, ng_ref,
               tril3_ref, nconv_ref, nssm_ref, tail_ref, ht_ref, store_y):
    q = SSD_CHUNK

    xbc = xbc_ref[...]
    tail = jnp.where(first, 0.0, tail_ref[...])
    ht_prev = jnp.where(first, 0.0, ht_ref[...])
    sub = lax.broadcasted_iota(jnp.int32, (CONV_PAD, CONV_DIM), 0)
    conv = cb_ref[...] + xbc * cw_ref[CONV_W - 1:CONV_W, :]
    for k in range(1, CONV_W):
        shifted = pltpu.roll(xbc, k, 0)
        head = jnp.where(sub < k, pltpu.roll(tail, k, 0), shifted[0:CONV_PAD, :])
        shifted = jnp.concatenate([head, shifted[CONV_PAD:, :]], axis=0)
        conv = conv + shifted * cw_ref[CONV_W - 1 - k:CONV_W - k, :]
    tail_ref[...] = xbc[q - CONV_PAD:, :]
    nconv_ref[...] = xbc[q - (CONV_W - 1):, :]
    yield

    act = _silu(conv)
    xs = act[:, :D_SSM]
    bmat = act[:, D_SSM:D_SSM + SSM_GROUPS * D_STATE]
    cmat = act[:, D_SSM + SSM_GROUPS * D_STATE:]
    yield

    dt = _softplus(dt_ref[...] + dtb_ref[...])
    a = -jnp.exp(alog_ref[...])
    acum = _select_rows(tril3_ref[...], dt * a)
    acum_t = acum.T
    low_lanes = lax.broadcasted_iota(jnp.int32, (q, 2 * SSM_HEAD_DIM), 1) < SSM_HEAD_DIM

    def expand(cols):
        pairs = []
        for h0 in range(0, SSM_HEADS, 2):
            a = jnp.broadcast_to(cols[:, h0:h0 + 1], (q, 2 * SSM_HEAD_DIM))
            b = jnp.broadcast_to(cols[:, h0 + 1:h0 + 2], (q, 2 * SSM_HEAD_DIM))
            pairs.append(jnp.where(low_lanes, a, b))
        return jnp.concatenate(pairs, axis=1)

    dt_rep = expand(dt)
    yield
    acum_rep = expand(acum)
    yield
    xdt = xs * dt_rep
    exp_a = jnp.exp(acum_rep)
    xw = xdt * jnp.exp(acum_rep[q - 1:q, :] - acum_rep)
    yield

    row = lax.broadcasted_iota(jnp.int32, (q, q), 0)
    col = lax.broadcasted_iota(jnp.int32, (q, q), 1)
    causal = row >= col
    low_half = col < SSM_HEAD_DIM
    ydiag, yoff, states = [], [], []
    for grp in range(SSM_GROUPS):
        bg_f = bmat[:, grp * D_STATE:(grp + 1) * D_STATE]
        bg = bg_f.astype(BF16)
        cg = cmat[:, grp * D_STATE:(grp + 1) * D_STATE].astype(BF16)
        cb = _dot_nt(cg, bg)
        ch = slice(grp * GROUP_WIDTH, (grp + 1) * GROUP_WIDTH)
        yoff.append(_dot(cg, ht_prev[:, ch].astype(BF16)))
        yield
        for quad in range(HEADS_PER_GROUP // 4):
            h0 = grp * HEADS_PER_GROUP + 4 * quad
            masks, blocks = [], []
            for k, h in enumerate(range(h0, h0 + 4)):
                diff = acum[:, h:h + 1] - acum_t[h:h + 1, :]
                masks.append((cb * jnp.exp(jnp.where(causal, diff, -jnp.inf))).astype(BF16))
                xp = xdt[:, (h - k % 2) * SSM_HEAD_DIM:(h - k % 2 + 2) * SSM_HEAD_DIM]
                own = jnp.where(low_half == (k % 2 == 0), xp, 0.0).astype(BF16)
                zero = jnp.zeros_like(own)
                blocks.append(jnp.concatenate([own, zero] if k < 2 else [zero, own], axis=1))
                if k % 2 == 1:
                    yield
            ydiag.append(_dot(jnp.concatenate(masks, axis=1), jnp.concatenate(blocks, axis=0)))
            yield
        states.append(_dot(bg_f.T.astype(BF16), xw[:, ch].astype(BF16)))
        yield
    y = drep_ref[...] * xs + jnp.concatenate(ydiag, axis=1) + jnp.concatenate(yoff, axis=1) * exp_a

    ht_new = ht_prev * exp_a[q - 1:q, :] + jnp.concatenate(states, axis=1)
    ht_ref[...] = ht_new
    nssm_ref[...] = ht_new.T
    yield
    store_y(_group_norm_out(y, z_ref[...], ng_ref[...]))


def _sample_rows_kernel(u_ref, v_ref, xbc_ref, dt_ref, sc_ref, gv_ref, w0_ref, b0_ref, go_ref,
                        cw_ref, cb_ref, dtb_ref, alog_ref, drep_ref, rep_ref, rep8_ref,
                        ya_ref, vrows_ref, nconv_ref, split_ref, b_ref, c_ref, dx_ref):
    vr = _rms(jax.nn.gelu(v_ref[...]), gv_ref[...])
    vrows_ref[...] = vr
    mixed = vr * w0_ref[...] + b0_ref[...]
    ya_ref[...] = _rms(jax.nn.gelu(u_ref[...]) * mixed, go_ref[...]).astype(BF16)

    xbc = xbc_ref[...]
    conv = cb_ref[...]
    for k in range(CONV_W - 1):
        conv = conv + sc_ref[:, k * CONV_DIM:(k + 1) * CONV_DIM] * cw_ref[k:k + 1, :]
    conv = conv + xbc * cw_ref[CONV_W - 1:CONV_W, :]
    nconv_ref[:, 0:(CONV_W - 2) * CONV_DIM] = sc_ref[:, CONV_DIM:(CONV_W - 1) * CONV_DIM]
    nconv_ref[:, (CONV_W - 2) * CONV_DIM:] = xbc

    act = _silu(conv)
    xs = act[:, :D_SSM]
    b_ref[...] = act[:, D_SSM:D_SSM + SSM_GROUPS * D_STATE]
    c_ref[...] = act[:, D_SSM + SSM_GROUPS * D_STATE:]
    dx_ref[...] = drep_ref[...] * xs

    dt = _softplus(dt_ref[...] + dtb_ref[...])
    decay = jnp.exp(dt * (-jnp.exp(alog_ref[...])))
    xdt = xs * _dot_exact(dt, rep_ref[...])
    cols = jnp.concatenate([xdt, _dot_exact(decay, rep8_ref[...])], axis=1).T
    for p in range(N_SPLIT):
        piece = cols.astype(BF16)
        split_ref[:, p * LANES:(p + 1) * LANES] = piece
        cols = cols - piece.astype(F32)


SPLIT_ROWS = D_SSM + HEAD_PAD


def _sample_rows(proj, dt_raw, sconv, gv, w0, b0, go, cw, cb, dtb, alog, drep, rep, rep8):
    n = proj.shape[0]
    full = lambda shape: pl.BlockSpec(shape, lambda i: (0,) * len(shape))
    return pl.pallas_call(
        _sample_rows_kernel,
        grid=(1,),
        in_specs=[
            pl.BlockSpec((n, D_CHUNK), lambda i: (0, U_BLK)),
            pl.BlockSpec((n, D_CHUNK), lambda i: (0, V_BLK)),
            pl.BlockSpec((n, CONV_DIM), lambda i: (0, XBC_BLK)),
            full((n, HEAD_PAD)),
            full((n, (CONV_W - 1) * CONV_DIM)),
            full((1, D_CHUNK)), full((1, D_CHUNK)), full((1, D_CHUNK)), full((1, D_CHUNK)),
            full((CONV_W, CONV_DIM)), full((1, CONV_DIM)),
            full((1, HEAD_PAD)), full((1, HEAD_PAD)), full((1, D_SSM)),
            full((HEAD_PAD, D_SSM)), full((HEAD_PAD, HEAD_PAD)),
        ],
        out_specs=[
            full((n, D_CHUNK)), full((n, D_CHUNK)), full((n, (CONV_W - 1) * CONV_DIM)),
            full((SPLIT_ROWS, N_SPLIT * LANES)),
            full((n, SSM_GROUPS * D_STATE)), full((n, SSM_GROUPS * D_STATE)), full((n, D_SSM)),
        ],
        out_shape=[
            jax.ShapeDtypeStruct((n, D_CHUNK), BF16),
            jax.ShapeDtypeStruct((n, D_CHUNK), F32),
            jax.ShapeDtypeStruct((n, (CONV_W - 1) * CONV_DIM), F32),
            jax.ShapeDtypeStruct((SPLIT_ROWS, N_SPLIT * LANES), BF16),
            jax.ShapeDtypeStruct((n, SSM_GROUPS * D_STATE), F32),
            jax.ShapeDtypeStruct((n, SSM_GROUPS * D_STATE), F32),
            jax.ShapeDtypeStruct((n, D_SSM), F32),
        ],
        compiler_params=_params("arbitrary"),
        name="sample_rows",
    )(proj, proj, proj, dt_raw, sconv, gv, w0, b0, go, cw, cb, dtb, alog, drep, rep, rep8)


SAMPLE_TB = 16


def _sample_state_kernel(split_ref, b_ref, c_ref, h0_ref, dx_ref, z_ref, ng_ref,
                         hn_ref, y_ref, yt_ref):
    i = pl.program_id(0)
    n_tok = b_ref.shape[0]

    @pl.when(i == 0)
    def _():
        yt_ref[...] = jnp.zeros_like(yt_ref)

    tok = lax.broadcasted_iota(jnp.int32, (N_SPLIT * LANES, D_STATE), 0) % LANES
    lane = lax.broadcasted_iota(jnp.int32, (D_SSM, n_tok), 1)

    def per_group(row, grp):
        return jnp.broadcast_to(row[:, grp * D_STATE:(grp + 1) * D_STATE], (GROUP_WIDTH, D_STATE))

    for k in range(SAMPLE_TB):
        t = i * SAMPLE_TB + k
        onehot = (tok == t).astype(BF16)
        bcast = _dot(split_ref[...], onehot)
        decay = jnp.broadcast_to(
            bcast[D_SSM:, :].reshape(SSM_HEADS, 1, SUBLANES, D_STATE),
            (SSM_HEADS, SSM_HEAD_DIM // SUBLANES, SUBLANES, D_STATE)).reshape(D_SSM, D_STATE)
        brow = b_ref[pl.ds(t, 1), :]
        crow = c_ref[pl.ds(t, 1), :]
        bfull = jnp.concatenate([per_group(brow, g) for g in range(SSM_GROUPS)], axis=0)
        cfull = jnp.concatenate([per_group(crow, g) for g in range(SSM_GROUPS)], axis=0)
        h_new = h0_ref[k] * decay + bcast[:D_SSM, :] * bfull
        hn_ref[k] = h_new
        ysum = jnp.sum(h_new * cfull, axis=-1, keepdims=True)
        yt_ref[...] = jnp.where(lane == t, ysum, yt_ref[...])

    @pl.when(i == pl.num_programs(0) - 1)
    def _():
        y = yt_ref[...].T + dx_ref[...]
        y_ref[...] = _group_norm_out(y, z_ref[...], ng_ref[...])


def _sample_state(split, bmat, cmat, h0, dx, proj, ng):
    n = h0.shape[0]
    const2 = lambda i: (0, 0)
    return pl.pallas_call(
        _sample_state_kernel,
        grid=(n // SAMPLE_TB,),
        in_specs=[
            pl.BlockSpec((SPLIT_ROWS, N_SPLIT * LANES), const2),
            pl.BlockSpec((n, SSM_GROUPS * D_STATE), const2),
            pl.BlockSpec((n, SSM_GROUPS * D_STATE), const2),
            pl.BlockSpec((SAMPLE_TB, D_SSM, D_STATE), lambda i: (i, 0, 0)),
            pl.BlockSpec((n, D_SSM), const2),
            pl.BlockSpec((n, D_SSM), lambda i: (0, Z_BLK)),
            pl.BlockSpec((1, D_SSM), const2),
        ],
        out_specs=[
            pl.BlockSpec((SAMPLE_TB, D_SSM, D_STATE), lambda i: (i, 0, 0)),
            pl.BlockSpec((n, D_SSM), const2),
        ],
        out_shape=[
            jax.ShapeDtypeStruct((n, D_SSM, D_STATE), F32),
            jax.ShapeDtypeStruct((n, D_SSM), BF16),
        ],
        scratch_shapes=[pltpu.VMEM((D_SSM, n), F32)],
        compiler_params=_params("arbitrary"),
        name="sample_state",
    )(split, bmat, cmat, h0, dx, proj, ng)


FFN_TF = 512


def _proj_ffn_kernel(x_ref, ya_ref, yb_ref, wo_ref, g_ref, wg_ref, wu_ref, wd_ref, gf_ref, o_ref, m_ref):
    f = pl.program_id(1)

    @pl.when(f == 0)
    def _():
        y = jnp.concatenate([ya_ref[...], yb_ref[...]], axis=1)
        h = x_ref[...] + _dot(y, wo_ref[...])
        m_ref[...] = _rms(h, g_ref[...]).astype(BF16)
        o_ref[...] = h

    m = m_ref[...]
    act = (_silu(_dot(m, wg_ref[...])) * _dot(m, wu_ref[...])).astype(BF16)
    o_ref[...] += _dot(act, wd_ref[...])

    @pl.when(f == pl.num_programs(1) - 1)
    def _():
        o_ref[...] = _rms(o_ref[...], gf_ref[...])


def _proj_ffn(x, ya, yb, w_out_b, g, w_gate, w_up, w_down, g_final, tm):
    m = x.shape[0]
    return pl.pallas_call(
        _proj_ffn_kernel,
        grid=(m // tm, D_FF // FFN_TF),
        in_specs=[
            pl.BlockSpec((tm, D_MODEL), lambda i, f: (i, 0)),
            pl.BlockSpec((tm, D_CHUNK), lambda i, f: (i, 0)),
            pl.BlockSpec((tm, D_SSM), lambda i, f: (i, 0)),
            pl.BlockSpec((D_MODEL, D_MODEL), lambda i, f: (0, 0)),
            pl.BlockSpec((1, D_MODEL), lambda i, f: (0, 0)),
            pl.BlockSpec((D_MODEL, FFN_TF), lambda i, f: (0, f)),
            pl.BlockSpec((D_MODEL, FFN_TF), lambda i, f: (0, f)),
            pl.BlockSpec((FFN_TF, D_MODEL), lambda i, f: (f, 0)),
            pl.BlockSpec((1, D_MODEL), lambda i, f: (0, 0)),
        ],
        out_specs=pl.BlockSpec((tm, D_MODEL), lambda i, f: (i, 0)),
        out_shape=jax.ShapeDtypeStruct((m, D_MODEL), F32),
        scratch_shapes=[pltpu.VMEM((tm, D_MODEL), BF16)],
        compiler_params=_params("arbitrary", "arbitrary"),
        name="proj_ffn",
    )(x, ya, yb, w_out_b, g, w_gate, w_up, w_down, g_final)


MIX_TM = 512
CHUNKS_PER_TILE = MIX_TM // SSD_CHUNK
FFN_UP_PIECE = 256
FFN_UP_K = 2048
FFN_DOWN_PIECE = 512
MIX_PHASES_PER_FFN_PIECE = 5
MIX_LEAD_PIECES = 1


def _interleave(main, side, side_per_main, lead=0):
    for k, _ in enumerate(main):
        if k >= lead:
            for _ in range(side_per_main):
                next(side, None)
    for _ in side:
        pass


def _mix_ffn_kernel(x_ref, wo_ref, g_ref, wg_ref, wu_ref, wd_ref, gf_ref,
                    u_ref, v_ref, gv_ref, ws_ref, bst_ref, go_ref,
                    xbc_ref, z_ref, dt_ref, cw_ref, cb_ref, dtb_ref, alog_ref, drep_ref, ng_ref,
                    tril3_ref,
                    o_ref, nconv_ref, nssm_ref,
                    m_ref, ya_ref, yb_ref, tail_ref, ht_ref, wm_ref, *, chunks_per_seq):
    s = pl.program_id(0)
    f = pl.program_id(1)
    has_ffn = s >= 1
    has_mix = (s < pl.num_programs(0) - 1) & (f < CHUNKS_PER_TILE)

    @pl.when((s == 0) & (f == 0))
    def _():
        _mask_chunk_weights(ws_ref, wm_ref)

    @pl.when(has_ffn & (f == 0))
    def _():
        y = jnp.concatenate([ya_ref[...], yb_ref[...]], axis=1)
        h = x_ref[...] + _dot(y, wo_ref[...])
        m_ref[...] = _rms(h, g_ref[...]).astype(BF16)
        o_ref[...] = h

    def ffn_step():
        acts = []
        for c in range(FFN_TF // FFN_UP_PIECE):
            cols = slice(c * FFN_UP_PIECE, (c + 1) * FFN_UP_PIECE)
            halves = []
            for w_ref in (wg_ref, wu_ref):
                acc = None
                for k in range(D_MODEL // FFN_UP_K):
                    rows = slice(k * FFN_UP_K, (k + 1) * FFN_UP_K)
                    part = _dot(m_ref[:, rows], w_ref[rows, cols])
                    acc = part if acc is None else acc + part
                    yield
                halves.append(acc)
            acts.append((_silu(halves[0]) * halves[1]).astype(BF16))
        act = jnp.concatenate(acts, axis=1)
        for c in range(D_MODEL // FFN_DOWN_PIECE):
            cols = slice(c * FFN_DOWN_PIECE, (c + 1) * FFN_DOWN_PIECE)
            o_ref[:, cols] += _dot(act, wd_ref[:, cols])
            yield

    chunk_rows = pl.ds(pl.multiple_of(f * SSD_CHUNK, SSD_CHUNK), SSD_CHUNK)

    def store_ya(ya):
        ya_ref[chunk_rows, :] = ya

    def store_yb(yb):
        yb_ref[chunk_rows, :] = yb

    def mix_step():
        first = (s * CHUNKS_PER_TILE + f) % chunks_per_seq == 0
        yield from _chunk_gate(u_ref, v_ref, gv_ref, wm_ref, bst_ref, go_ref, store_ya)
        yield
        yield from _ssd_chunk(first, xbc_ref, z_ref, dt_ref, cw_ref, cb_ref, dtb_ref, alog_ref, drep_ref,
                              ng_ref, tril3_ref, nconv_ref, nssm_ref, tail_ref, ht_ref, store_yb)

    @pl.when(has_ffn & has_mix)
    def _():
        _interleave(ffn_step(), mix_step(), MIX_PHASES_PER_FFN_PIECE, MIX_LEAD_PIECES)

    @pl.when(has_ffn & jnp.logical_not(has_mix))
    def _():
        _interleave(ffn_step(), iter(()), 0)

    @pl.when(jnp.logical_not(has_ffn) & has_mix)
    def _():
        _interleave(iter(()), mix_step(), 0)

    @pl.when(has_ffn & (f == pl.num_programs(1) - 1))
    def _():
        o_ref[...] = _rms(o_ref[...], gf_ref[...])


def _mix_ffn(x, proj, dt_raw, w_out_b, g, w_gate, w_up, w_down, g_final, gv, ws, bst, go,
             cw, cb, dtb, alog, drep, ng, tril3, batch, seq):
    m = x.shape[0]
    n_tiles = m // MIX_TM
    n_chunks = m // SSD_CHUNK
    chunks_per_seq = seq // SSD_CHUNK
    assert MIX_TM % SSD_CHUNK == 0 and seq % MIX_TM == 0 and CHUNKS_PER_TILE <= D_FF // FFN_TF

    def tile(s, f):
        return jnp.maximum(s - 1, 0)

    def chunk(s, f):
        return jnp.where(s >= n_tiles, n_chunks - 1,
                         s * CHUNKS_PER_TILE + jnp.minimum(f, CHUNKS_PER_TILE - 1))

    def seq_of(s, f):
        return chunk(s, f) // chunks_per_seq

    const2 = lambda s, f: (0, 0)
    return pl.pallas_call(
        functools.partial(_mix_ffn_kernel, chunks_per_seq=chunks_per_seq),
        grid=(n_tiles + 1, D_FF // FFN_TF),
        in_specs=[
            pl.BlockSpec((MIX_TM, D_MODEL), lambda s, f: (tile(s, f), 0)),
            pl.BlockSpec((D_MODEL, D_MODEL), const2, pipeline_mode=pl.Buffered(1)),
            pl.BlockSpec((1, D_MODEL), const2),
            pl.BlockSpec((D_MODEL, FFN_TF), lambda s, f: (0, f)),
            pl.BlockSpec((D_MODEL, FFN_TF), lambda s, f: (0, f)),
            pl.BlockSpec((FFN_TF, D_MODEL), lambda s, f: (f, 0)),
            pl.BlockSpec((1, D_MODEL), const2),
            pl.BlockSpec((CHUNK_LEN, D_CHUNK), lambda s, f: (chunk(s, f), U_BLK)),
            pl.BlockSpec((CHUNK_LEN, D_CHUNK), lambda s, f: (chunk(s, f), V_BLK)),
            pl.BlockSpec((1, D_CHUNK), const2),
            pl.BlockSpec((CHUNK_HEADS, CHUNK_LEN, CHUNK_LEN), lambda s, f: (0, 0, 0)),
            pl.BlockSpec((CHUNK_LEN, CHUNK_HEADS), const2),
            pl.BlockSpec((1, D_CHUNK), const2),
            pl.BlockSpec((SSD_CHUNK, CONV_DIM), lambda s, f: (chunk(s, f), XBC_BLK)),
            pl.BlockSpec((SSD_CHUNK, D_SSM), lambda s, f: (chunk(s, f), Z_BLK)),
            pl.BlockSpec((SSD_CHUNK, HEAD_PAD), lambda s, f: (chunk(s, f), 0)),
            pl.BlockSpec((CONV_W, CONV_DIM), const2),
            pl.BlockSpec((1, CONV_DIM), const2),
            pl.BlockSpec((1, HEAD_PAD), const2),
            pl.BlockSpec((1, HEAD_PAD), const2),
            pl.BlockSpec((1, D_SSM), const2),
            pl.BlockSpec((1, D_SSM), const2),
            pl.BlockSpec((SSD_CHUNK, N_SPLIT * SSD_CHUNK), const2),
        ],
        out_specs=[
            pl.BlockSpec((MIX_TM, D_MODEL), lambda s, f: (tile(s, f), 0)),
            pl.BlockSpec((None, CONV_W - 1, CONV_DIM), lambda s, f: (seq_of(s, f), 0, 0)),
            pl.BlockSpec((None, D_SSM, D_STATE), lambda s, f: (seq_of(s, f), 0, 0)),
        ],
        out_shape=[
            jax.ShapeDtypeStruct((m, D_MODEL), F32),
            jax.ShapeDtypeStruct((batch, CONV_W - 1, CONV_DIM), F32),
            jax.ShapeDtypeStruct((batch, D_SSM, D_STATE), F32),
        ],
        scratch_shapes=[
            pltpu.VMEM((MIX_TM, D_MODEL), BF16),
            pltpu.VMEM((MIX_TM, D_CHUNK), BF16),
            pltpu.VMEM((MIX_TM, D_SSM), BF16),
            pltpu.VMEM((CONV_PAD, CONV_DIM), F32),
            pltpu.VMEM((D_STATE, D_SSM), F32),
            pltpu.VMEM((CHUNK_LEN, CHUNK_HEADS * CHUNK_LEN), BF16),
        ],
        compiler_params=_params("arbitrary", "arbitrary"),
        name="mix_ffn",
    )(x, w_out_b, g, w_gate, w_up, w_down, g_final, proj, proj, gv, ws, bst, go, proj, proj, dt_raw,
      cw, cb, dtb, alog, drep, ng, tril3)


def _row(v, width=None):
    v = v.astype(F32).reshape(1, -1)
    if width is not None and v.shape[1] < width:
        v = jnp.pad(v, ((0, 0), (0, width - v.shape[1])))
    return v


def kernel(x_prompt, x_sample, state_conv, state_ssm, norm_mix_g, w_in, chunk_v_norm_g, chunk_w_s,
           chunk_b_s, chunk_out_norm_g, ssd_conv_w, ssd_conv_b, ssd_dt_bias, ssd_a_log, ssd_d,
           ssd_norm_g, w_out, norm_ffn_g, w_gate, w_up, w_down, norm_final_g):
    depth = w_in.shape[0]
    batch, seq, _ = x_prompt.shape
    n_s = x_sample.shape[0]
    assert x_sample.shape[1] == 1 and seq % SSD_CHUNK == 0 and depth == 1

    rep = (lax.broadcasted_iota(jnp.int32, (HEAD_PAD, D_SSM), 0)
           == lax.broadcasted_iota(jnp.int32, (HEAD_PAD, D_SSM), 1) // SSM_HEAD_DIM).astype(F32)
    rep8 = (lax.broadcasted_iota(jnp.int32, (HEAD_PAD, HEAD_PAD), 0)
            == lax.broadcasted_iota(jnp.int32, (HEAD_PAD, HEAD_PAD), 1) // SUBLANES).astype(F32)
    tril3 = jnp.tile(jnp.tril(jnp.ones((SSD_CHUNK, SSD_CHUNK), F32)), (1, N_SPLIT)).astype(BF16)

    hp = x_prompt.reshape(batch * seq, D_MODEL)
    hs = x_sample.reshape(n_s, D_MODEL)
    l = 0
    w_in_t = jnp.swapaxes(w_in, 1, 2)
    w_dt_t = jnp.pad(w_in_t[l, D_MAIN:, :], ((0, HEAD_PAD - SSM_HEADS), (0, 0))).astype(BF16)
    g_mix = _row(norm_mix_g[l])
    gv = _row(chunk_v_norm_g[l])
    go = _row(chunk_out_norm_g[l])
    cw = ssd_conv_w[l].astype(F32)
    cb = _row(ssd_conv_b[l])
    dtb = _row(ssd_dt_bias[l], HEAD_PAD)
    alog = _row(ssd_a_log[l], HEAD_PAD)
    drep = _row(jnp.repeat(ssd_d[l], SSM_HEAD_DIM))
    ng = _row(ssd_norm_g[l])
    g_ffn = _row(norm_ffn_g[l])
    g_final = _row(norm_final_g)

    proj_p, dt_p, w_gate_b, w_up_b, w_down_b, w_out_b = _in_proj(
        hp, g_mix, w_in_t, l, w_dt_t, tm=1024, cast=(w_gate, w_up, w_down, w_out))
    y_p, conv_p, ssm_p = _mix_ffn(hp, proj_p, dt_p, w_out_b, g_ffn, w_gate_b, w_up_b, w_down_b, g_final,
                                  gv, chunk_w_s[l].astype(F32), chunk_b_s[l].astype(F32).T, go,
                                  cw, cb, dtb, alog, drep, ng, tril3, batch, seq)

    proj_s, dt_s = _in_proj(hs, g_mix, w_in_t, l, w_dt_t, tm=n_s)
    w0 = _row(jnp.repeat(chunk_w_s[l, :, 0, 0], CHUNK_HEAD_DIM))
    b0 = _row(jnp.repeat(chunk_b_s[l, :, 0], CHUNK_HEAD_DIM))
    ya_s, v_s, conv_s, split, b_s, c_s, dx_s = _sample_rows(
        proj_s, dt_s, state_conv[l].reshape(n_s, (CONV_W - 1) * CONV_DIM), gv, w0, b0, go,
        cw, cb, dtb, alog, drep, rep, rep8)
    ssm_s, yb_s = _sample_state(split, b_s, c_s, state_ssm[l].reshape(n_s, D_SSM, D_STATE),
                                dx_s, proj_s, ng)
    y_s = _proj_ffn(hs, ya_s, yb_s, w_out_b, g_ffn, w_gate_b, w_up_b, w_down_b, g_final, tm=n_s)

    return (
        y_p.reshape(batch, seq, D_MODEL),
        y_s.reshape(n_s, 1, D_MODEL),
        conv_p[None],
        ssm_p.reshape(1, batch, SSM_HEADS, SSM_HEAD_DIM, D_STATE),
        conv_s.reshape(1, n_s, CONV_W - 1, CONV_DIM),
        ssm_s.reshape(1, n_s, SSM_HEADS, SSM_HEAD_DIM, D_STATE),
        v_s.reshape(1, n_s, 1, D_CHUNK),
    )
```

```python
import functools

import jax
import jax.numpy as jnp
from jax import lax
from jax.experimental import pallas as pl
from jax.experimental.pallas import tpu as pltpu

F32 = jnp.float32
BF16 = jnp.bfloat16
HIGHEST = lax.Precision.HIGHEST

D_MODEL = 2048
D_CHUNK = 1024
CHUNK_HEADS = 8
CHUNK_HEAD_DIM = 128
CHUNK_LEN = 128
D_SSM = 1024
SSM_HEAD_DIM = 64
SSM_HEADS = 16
SSM_GROUPS = 2
HEADS_PER_GROUP = SSM_HEADS // SSM_GROUPS
GROUP_WIDTH = D_SSM // SSM_GROUPS
D_STATE = 128
CONV_W = 4
CONV_DIM = D_SSM + 2 * SSM_GROUPS * D_STATE
SSD_CHUNK = 128
D_MAIN = 2 * D_CHUNK + D_SSM + CONV_DIM
D_FF = 5632
EPS = 1e-6

LANES = 128
SUBLANES = 8
BF16_SUBLANES = 16
HEAD_PAD = LANES
N_SPLIT = 3
VMEM_LIMIT = 56 * 1024 * 1024

U_BLK, V_BLK, Z_BLK = 0, 1, 2
XBC_BLK = 2


def _rms(x, g):
    return x * lax.rsqrt(jnp.mean(x * x, axis=-1, keepdims=True) + EPS) * g


def _silu(x):
    return x * jax.nn.sigmoid(x)


def _softplus(x):
    return jnp.maximum(x, 0.0) + jnp.log1p(jnp.exp(-jnp.abs(x)))


def _dot(a, b):
    return jnp.dot(a, b, preferred_element_type=F32)


def _dot_nt(a, b):
    return lax.dot_general(a, b, (((1,), (1,)), ((), ())), preferred_element_type=F32)


def _dot_exact(a, b):
    return jnp.dot(a, b, precision=HIGHEST, preferred_element_type=F32)


def _params(*semantics):
    return pltpu.CompilerParams(dimension_semantics=semantics, vmem_limit_bytes=VMEM_LIMIT)


IN_TN = 512


def _in_proj_kernel(x_hbm, xs_hbm, g_ref, w_ref, wdt_ref, *refs, tm, n_cast):
    cast_in = refs[:n_cast]
    o_ref, dt_ref, os_ref, dts_ref = refs[n_cast:n_cast + 4]
    cast_out = refs[n_cast + 4:2 * n_cast + 4]
    xn_ref, wres_ref, xbuf_ref, sem = refs[2 * n_cast + 4:]
    i = pl.program_id(0)
    j = pl.program_id(1)
    n_i = pl.num_programs(0) - 1
    n_s = xs_hbm.shape[0]

    for src, dst in zip(cast_in, cast_out, strict=True):
        dst[...] = src[...].astype(BF16)

    def x_copy(tile):
        return pltpu.make_async_copy(x_hbm.at[pl.ds(tile * tm, tm), :], xbuf_ref, sem)

    def xs_copy():
        return pltpu.make_async_copy(xs_hbm, xbuf_ref.at[pl.ds(0, n_s), :], sem)

    @pl.when((i == 0) & (j == 0))
    def _():
        x_copy(0).start()

    @pl.when(i == 0)
    def _():
        wres_ref[j] = w_ref[...].T.astype(BF16)

    @pl.when((j == 0) & (i < n_i))
    def _():
        x_copy(i).wait()
        nb = _rms(xbuf_ref[...], g_ref[...]).astype(BF16)
        xn_ref[...] = nb
        dt_ref[...] = _dot_nt(nb, wdt_ref[...])

    @pl.when((j == 0) & (i == n_i))
    def _():
        xs_copy().wait()
        nb = _rms(xbuf_ref[0:n_s, :], g_ref[...]).astype(BF16)
        xn_ref[0:n_s, :] = nb
        dts_ref[...] = _dot_nt(nb, wdt_ref[...])

    @pl.when((j == 1) & (i + 1 < n_i))
    def _():
        x_copy(i + 1).start()

    @pl.when((j == 1) & (i + 1 == n_i))
    def _():
        xs_copy().start()

    @pl.when(i < n_i)
    def _():
        o_ref[...] = _dot(xn_ref[...], wres_ref[j])

    @pl.when(i == n_i)
    def _():
        os_ref[...] = _dot(xn_ref[0:n_s, :], wres_ref[j])


def _in_proj(x, xs, g, w_in_t, layer, w_dt_t, tm, cast=()):
    m = x.shape[0]
    n_s = xs.shape[0]
    n_i = m // tm
    n_j = D_MAIN // IN_TN
    assert n_j >= 2 and n_s <= tm and n_s % BF16_SUBLANES == 0
    cast_in, cast_out, cast_shapes = [], [], []
    n_steps = n_i * n_j

    def row_tile(i, j):
        return jnp.minimum(i, n_i - 1)

    def col_prompt(i, j):
        return jnp.where(i == n_i, n_j - 1, j)

    def col_sample(i, j):
        return jnp.where(i == n_i, j, 0)

    for w in cast:
        _, w_rows, w_cols = w.shape
        slab_rows = next(r for r in range(BF16_SUBLANES, w_rows + 1, BF16_SUBLANES)
                         if w_rows % r == 0 and w_rows // r <= n_steps)
        n_slabs = w_rows // slab_rows
        hold = n_steps // n_slabs
        slab = lambda i, j, n_slabs=n_slabs, hold=hold: jnp.minimum((i * n_j + j) // hold, n_slabs - 1)
        cast_in.append(pl.BlockSpec((None, slab_rows, w_cols), lambda i, j, slab=slab: (layer, slab(i, j), 0)))
        cast_out.append(pl.BlockSpec((slab_rows, w_cols), lambda i, j, slab=slab: (slab(i, j), 0)))
        cast_shapes.append(jax.ShapeDtypeStruct((w_rows, w_cols), BF16))
    return pl.pallas_call(
        functools.partial(_in_proj_kernel, tm=tm, n_cast=len(cast)),
        grid=(n_i + 1, n_j),
        in_specs=[
            pl.BlockSpec(memory_space=pl.ANY),
            pl.BlockSpec(memory_space=pl.ANY),
            pl.BlockSpec((1, D_MODEL), lambda i, j: (0, 0)),
            pl.BlockSpec((None, IN_TN, D_MODEL), lambda i, j: (layer, jnp.where(i == 0, j, n_j - 1), 0)),
            pl.BlockSpec((HEAD_PAD, D_MODEL), lambda i, j: (0, 0)),
            *cast_in,
        ],
        out_specs=[
            pl.BlockSpec((tm, IN_TN), lambda i, j: (row_tile(i, j), col_prompt(i, j))),
            pl.BlockSpec((tm, HEAD_PAD), lambda i, j: (row_tile(i, j), 0)),
            pl.BlockSpec((n_s, IN_TN), lambda i, j: (0, col_sample(i, j))),
            pl.BlockSpec((n_s, HEAD_PAD), lambda i, j: (0, 0)),
            *cast_out,
        ],
        out_shape=[
            jax.ShapeDtypeStruct((m, D_MAIN), F32),
            jax.ShapeDtypeStruct((m, HEAD_PAD), F32),
            jax.ShapeDtypeStruct((n_s, D_MAIN), F32),
            jax.ShapeDtypeStruct((n_s, HEAD_PAD), F32),
            *cast_shapes,
        ],
        scratch_shapes=[
            pltpu.VMEM((tm, D_MODEL), BF16),
            pltpu.VMEM((n_j, D_MODEL, IN_TN), BF16),
            pltpu.VMEM((tm, D_MODEL), F32),
            pltpu.SemaphoreType.DMA(()),
        ],
        compiler_params=_params("arbitrary", "arbitrary"),
        name="in_proj",
    )(x, xs, g, w_in_t, w_dt_t, *cast)


def _mask_chunk_weights(ws_ref, wm_ref):
    row = lax.broadcasted_iota(jnp.int32, (CHUNK_LEN, CHUNK_LEN), 0)
    col = lax.broadcasted_iota(jnp.int32, (CHUNK_LEN, CHUNK_LEN), 1)
    for h in range(CHUNK_HEADS):
        wm_ref[:, h * CHUNK_LEN:(h + 1) * CHUNK_LEN] = jnp.where(row >= col, ws_ref[h], 0.0).astype(BF16)


def _chunk_gate(u_ref, v_ref, gv_ref, wm_ref, bst_ref, go_ref, store_y):
    vb = _rms(jax.nn.gelu(v_ref[...]), gv_ref[...]).astype(BF16)
    yield
    u = jax.nn.gelu(u_ref[...])
    yield
    zeros = jnp.zeros((CHUNK_LEN, CHUNK_HEAD_DIM), BF16)
    parts = []
    for h0 in range(0, CHUNK_HEADS, 2):
        w = wm_ref[:, h0 * CHUNK_LEN:(h0 + 2) * CHUNK_LEN]
        va = vb[:, h0 * CHUNK_HEAD_DIM:(h0 + 1) * CHUNK_HEAD_DIM]
        vb2 = vb[:, (h0 + 1) * CHUNK_HEAD_DIM:(h0 + 2) * CHUNK_HEAD_DIM]
        rhs = jnp.concatenate([jnp.concatenate([va, zeros], axis=1),
                               jnp.concatenate([zeros, vb2], axis=1)], axis=0)
        mixed = _dot(w, rhs)
        for k, h in enumerate((h0, h0 + 1)):
            sl = slice(h * CHUNK_HEAD_DIM, (h + 1) * CHUNK_HEAD_DIM)
            mh = mixed[:, k * CHUNK_HEAD_DIM:(k + 1) * CHUNK_HEAD_DIM] + bst_ref[:, h:h + 1]
            parts.append(u[:, sl] * mh)
        yield
    y = jnp.concatenate(parts, axis=1)
    store_y(_rms(y, go_ref[...]).astype(BF16))


CONV_PAD = 8


def _group_norm_out(y, z, g):
    y = y * _silu(z)
    parts = []
    for grp in range(SSM_GROUPS):
        yg = y[:, grp * GROUP_WIDTH:(grp + 1) * GROUP_WIDTH]
        parts.append(yg * lax.rsqrt(jnp.mean(yg * yg, axis=-1, keepdims=True) + EPS))
    return (jnp.concatenate(parts, axis=1) * g).astype(BF16)


def _split3(x):
    p1 = x.astype(BF16)
    r1 = x - p1.astype(F32)
    p2 = r1.astype(BF16)
    p3 = (r1 - p2.astype(F32)).astype(BF16)
    return p1, p2, p3


def _select_rows(sel3, x):
    return _dot(sel3, jnp.concatenate(_split3(x), axis=0))


def _ssd_chunk(first, xbc_ref, z_ref, dt_ref, cw_ref, cb_ref, dtb_ref, alog_ref, drep_ref, ng_ref,
               tril3_ref, nconv_ref, nssm_ref, tail_ref, ht_ref, store_y):
    q = SSD_CHUNK

    xbc = xbc_ref[...]
    tail = jnp.where(first, 0.0, tail_ref[...])
    ht_prev = jnp.where(first, 0.0, ht_ref[...])
    sub = lax.broadcasted_iota(jnp.int32, (CONV_PAD, CONV_DIM), 0)
    conv = cb_ref[...] + xbc * cw_ref[CONV_W - 1:CONV_W, :]
    for k in range(1, CONV_W):
        shifted = pltpu.roll(xbc, k, 0)
        head = jnp.where(sub < k, pltpu.roll(tail, k, 0), shifted[0:CONV_PAD, :])
        shifted = jnp.concatenate([head, shifted[CONV_PAD:, :]], axis=0)
        conv = conv + shifted * cw_ref[CONV_W - 1 - k:CONV_W - k, :]
    tail_ref[...] = xbc[q - CONV_PAD:, :]
    nconv_ref[...] = xbc[q - (CONV_W - 1):, :]
    yield

    act = _silu(conv)
    xs = act[:, :D_SSM]
    bmat = act[:, D_SSM:D_SSM + SSM_GROUPS * D_STATE]
    cmat = act[:, D_SSM + SSM_GROUPS * D_STATE:]
    yield

    dt = _softplus(dt_ref[...] + dtb_ref[...])
    a = -jnp.exp(alog_ref[...])
    acum = _select_rows(tril3_ref[...], dt * a)
    acum_t = acum.T
    low_lanes = lax.broadcasted_iota(jnp.int32, (q, 2 * SSM_HEAD_DIM), 1) < SSM_HEAD_DIM

    def expand(cols):
        pairs = []
        for h0 in range(0, SSM_HEADS, 2):
            a = jnp.broadcast_to(cols[:, h0:h0 + 1], (q, 2 * SSM_HEAD_DIM))
            b = jnp.broadcast_to(cols[:, h0 + 1:h0 + 2], (q, 2 * SSM_HEAD_DIM))
            pairs.append(jnp.where(low_lanes, a, b))
        return jnp.concatenate(pairs, axis=1)

    dt_rep = expand(dt)
    yield
    acum_rep = expand(acum)
    yield
    xdt = xs * dt_rep
    exp_a = jnp.exp(acum_rep)
    xw = xdt * jnp.exp(acum_rep[q - 1:q, :] - acum_rep)
    yield

    row = lax.broadcasted_iota(jnp.int32, (q, q), 0)
    col = lax.broadcasted_iota(jnp.int32, (q, q), 1)
    causal = row >= col
    low_half = col < SSM_HEAD_DIM
    ydiag, yoff, states = [], [], []
    for grp in range(SSM_GROUPS):
        bg_f = bmat[:, grp * D_STATE:(grp + 1) * D_STATE]
        bg = bg_f.astype(BF16)
        cg = cmat[:, grp * D_STATE:(grp + 1) * D_STATE].astype(BF16)
        cb = _dot_nt(cg, bg)
        ch = slice(grp * GROUP_WIDTH, (grp + 1) * GROUP_WIDTH)
        yoff.append(_dot(cg, ht_prev[:, ch].astype(BF16)))
        yield
        for quad in range(HEADS_PER_GROUP // 4):
            h0 = grp * HEADS_PER_GROUP + 4 * quad
            masks, blocks = [], []
            for k, h in enumerate(range(h0, h0 + 4)):
                diff = acum[:, h:h + 1] - acum_t[h:h + 1, :]
                masks.append((cb * jnp.exp(jnp.where(causal, diff, -jnp.inf))).astype(BF16))
                xp = xdt[:, (h - k % 2) * SSM_HEAD_DIM:(h - k % 2 + 2) * SSM_HEAD_DIM]
                own = jnp.where(low_half == (k % 2 == 0), xp, 0.0).astype(BF16)
                zero = jnp.zeros_like(own)
                blocks.append(jnp.concatenate([own, zero] if k < 2 else [zero, own], axis=1))
                if k % 2 == 1:
                    yield
            ydiag.append(_dot(jnp.concatenate(masks, axis=1), jnp.concatenate(blocks, axis=0)))
            yield
        states.append(_dot(bg_f.T.astype(BF16), xw[:, ch].astype(BF16)))
        yield
    y = drep_ref[...] * xs + jnp.concatenate(ydiag, axis=1) + jnp.concatenate(yoff, axis=1) * exp_a

    ht_new = ht_prev * exp_a[q - 1:q, :] + jnp.concatenate(states, axis=1)
    ht_ref[...] = ht_new
    nssm_ref[...] = ht_new.T
    yield
    store_y(_group_norm_out(y, z_ref[...], ng_ref[...]))


def _sample_rows_kernel(u_ref, v_ref, xbc_ref, dt_ref, sc_ref, gv_ref, w0_ref, b0_ref, go_ref,
                        cw_ref, cb_ref, dtb_ref, alog_ref, drep_ref, rep_ref, rep8_ref,
                        ya_ref, vrows_ref, nconv_ref, split_ref, b_ref, c_ref, dx_ref):
    vr = _rms(jax.nn.gelu(v_ref[...]), gv_ref[...])
    vrows_ref[...] = vr
    mixed = vr * w0_ref[...] + b0_ref[...]
    ya_ref[...] = _rms(jax.nn.gelu(u_ref[...]) * mixed, go_ref[...]).astype(BF16)

    xbc = xbc_ref[...]
    conv = cb_ref[...]
    for k in range(CONV_W - 1):
        conv = conv + sc_ref[:, k * CONV_DIM:(k + 1) * CONV_DIM] * cw_ref[k:k + 1, :]
    conv = conv + xbc * cw_ref[CONV_W - 1:CONV_W, :]
    nconv_ref[:, 0:(CONV_W - 2) * CONV_DIM] = sc_ref[:, CONV_DIM:(CONV_W - 1) * CONV_DIM]
    nconv_ref[:, (CONV_W - 2) * CONV_DIM:] = xbc

    act = _silu(conv)
    xs = act[:, :D_SSM]
    b_ref[...] = act[:, D_SSM:D_SSM + SSM_GROUPS * D_STATE]
    c_ref[...] = act[:, D_SSM + SSM_GROUPS * D_STATE:]
    dx_ref[...] = drep_ref[...] * xs

    dt = _softplus(dt_ref[...] + dtb_ref[...])
    decay = jnp.exp(dt * (-jnp.exp(alog_ref[...])))
    xdt = xs * _dot_exact(dt, rep_ref[...])
    cols = jnp.concatenate([xdt, _dot_exact(decay, rep8_ref[...])], axis=1).T
    for p in range(N_SPLIT):
        piece = cols.astype(BF16)
        split_ref[:, p * LANES:(p + 1) * LANES] = piece
        cols = cols - piece.astype(F32)


SPLIT_ROWS = D_SSM + HEAD_PAD


def _sample_rows(proj, dt_raw, sconv, gv, w0, b0, go, cw, cb, dtb, alog, drep, rep, rep8):
    n = proj.shape[0]
    full = lambda shape: pl.BlockSpec(shape, lambda i: (0,) * len(shape))
    return pl.pallas_call(
        _sample_rows_kernel,
        grid=(1,),
        in_specs=[
            pl.BlockSpec((n, D_CHUNK), lambda i: (0, U_BLK)),
            pl.BlockSpec((n, D_CHUNK), lambda i: (0, V_BLK)),
            pl.BlockSpec((n, CONV_DIM), lambda i: (0, XBC_BLK)),
            full((n, HEAD_PAD)),
            full((n, (CONV_W - 1) * CONV_DIM)),
            full((1, D_CHUNK)), full((1, D_CHUNK)), full((1, D_CHUNK)), full((1, D_CHUNK)),
            full((CONV_W, CONV_DIM)), full((1, CONV_DIM)),
            full((1, HEAD_PAD)), full((1, HEAD_PAD)), full((1, D_SSM)),
            full((HEAD_PAD, D_SSM)), full((HEAD_PAD, HEAD_PAD)),
        ],
        out_specs=[
            full((n, D_CHUNK)), full((n, D_CHUNK)), full((n, (CONV_W - 1) * CONV_DIM)),
            full((SPLIT_ROWS, N_SPLIT * LANES)),
            full((n, SSM_GROUPS * D_STATE)), full((n, SSM_GROUPS * D_STATE)), full((n, D_SSM)),
        ],
        out_shape=[
            jax.ShapeDtypeStruct((n, D_CHUNK), BF16),
            jax.ShapeDtypeStruct((n, D_CHUNK), F32),
            jax.ShapeDtypeStruct((n, (CONV_W - 1) * CONV_DIM), F32),
            jax.ShapeDtypeStruct((SPLIT_ROWS, N_SPLIT * LANES), BF16),
            jax.ShapeDtypeStruct((n, SSM_GROUPS * D_STATE), F32),
            jax.ShapeDtypeStruct((n, SSM_GROUPS * D_STATE), F32),
            jax.ShapeDtypeStruct((n, D_SSM), F32),
        ],
        compiler_params=_params("arbitrary"),
        name="sample_rows",
    )(proj, proj, proj, dt_raw, sconv, gv, w0, b0, go, cw, cb, dtb, alog, drep, rep, rep8)


SAMPLE_TB = 16


def _sample_state_kernel(split_ref, b_ref, c_ref, h0_ref, dx_ref, z_ref, ng_ref,
                         hn_ref, y_ref, yt_ref):
    i = pl.program_id(0)
    n_tok = b_ref.shape[0]

    @pl.when(i == 0)
    def _():
        yt_ref[...] = jnp.zeros_like(yt_ref)

    tok = lax.broadcasted_iota(jnp.int32, (N_SPLIT * LANES, D_STATE), 0) % LANES
    lane = lax.broadcasted_iota(jnp.int32, (D_SSM, n_tok), 1)

    def per_group(row, grp):
        return jnp.broadcast_to(row[:, grp * D_STATE:(grp + 1) * D_STATE], (GROUP_WIDTH, D_STATE))

    for k in range(SAMPLE_TB):
        t = i * SAMPLE_TB + k
        onehot = (tok == t).astype(BF16)
        bcast = _dot(split_ref[...], onehot)
        decay = jnp.broadcast_to(
            bcast[D_SSM:, :].reshape(SSM_HEADS, 1, SUBLANES, D_STATE),
            (SSM_HEADS, SSM_HEAD_DIM // SUBLANES, SUBLANES, D_STATE)).reshape(D_SSM, D_STATE)
        brow = b_ref[pl.ds(t, 1), :]
        crow = c_ref[pl.ds(t, 1), :]
        bfull = jnp.concatenate([per_group(brow, g) for g in range(SSM_GROUPS)], axis=0)
        cfull = jnp.concatenate([per_group(crow, g) for g in range(SSM_GROUPS)], axis=0)
        h_new = h0_ref[k] * decay + bcast[:D_SSM, :] * bfull
        hn_ref[k] = h_new
        ysum = jnp.sum(h_new * cfull, axis=-1, keepdims=True)
        yt_ref[...] = jnp.where(lane == t, ysum, yt_ref[...])

    @pl.when(i == pl.num_programs(0) - 1)
    def _():
        y = yt_ref[...].T + dx_ref[...]
        y_ref[...] = _group_norm_out(y, z_ref[...], ng_ref[...])


def _sample_state(split, bmat, cmat, h0, dx, proj, ng):
    n = h0.shape[0]
    const2 = lambda i: (0, 0)
    return pl.pallas_call(
        _sample_state_kernel,
        grid=(n // SAMPLE_TB,),
        in_specs=[
            pl.BlockSpec((SPLIT_ROWS, N_SPLIT * LANES), const2),
            pl.BlockSpec((n, SSM_GROUPS * D_STATE), const2),
            pl.BlockSpec((n, SSM_GROUPS * D_STATE), const2),
            pl.BlockSpec((SAMPLE_TB, D_SSM, D_STATE), lambda i: (i, 0, 0)),
            pl.BlockSpec((n, D_SSM), const2),
            pl.BlockSpec((n, D_SSM), lambda i: (0, Z_BLK)),
            pl.BlockSpec((1, D_SSM), const2),
        ],
        out_specs=[
            pl.BlockSpec((SAMPLE_TB, D_SSM, D_STATE), lambda i: (i, 0, 0)),
            pl.BlockSpec((n, D_SSM), const2),
        ],
        out_shape=[
            jax.ShapeDtypeStruct((n, D_SSM, D_STATE), F32),
            jax.ShapeDtypeStruct((n, D_SSM), BF16),
        ],
        scratch_shapes=[pltpu.VMEM((D_SSM, n), F32)],
        compiler_params=_params("arbitrary"),
        name="sample_state",
    )(split, bmat, cmat, h0, dx, proj, ng)


FFN_TF = 512


def _proj_ffn_kernel(x_ref, ya_ref, yb_ref, wo_ref, g_ref, wg_ref, wu_ref, wd_ref, gf_ref, o_ref, m_ref):
    f = pl.program_id(1)

    @pl.when(f == 0)
    def _():
        y = jnp.concatenate([ya_ref[...], yb_ref[...]], axis=1)
        h = x_ref[...] + _dot(y, wo_ref[...])
        m_ref[...] = _rms(h, g_ref[...]).astype(BF16)
        o_ref[...] = h

    m = m_ref[...]
    act = (_silu(_dot(m, wg_ref[...])) * _dot(m, wu_ref[...])).astype(BF16)
    o_ref[...] += _dot(act, wd_ref[...])

    @pl.when(f == pl.num_programs(1) - 1)
    def _():
        o_ref[...] = _rms(o_ref[...], gf_ref[...])


def _proj_ffn(x, ya, yb, w_out_b, g, w_gate, w_up, w_down, g_final, tm):
    m = x.shape[0]
    return pl.pallas_call(
        _proj_ffn_kernel,
        grid=(m // tm, D_FF // FFN_TF),
        in_specs=[
            pl.BlockSpec((tm, D_MODEL), lambda i, f: (i, 0)),
            pl.BlockSpec((tm, D_CHUNK), lambda i, f: (i, 0)),
            pl.BlockSpec((tm, D_SSM), lambda i, f: (i, 0)),
            pl.BlockSpec((D_MODEL, D_MODEL), lambda i, f: (0, 0)),
            pl.BlockSpec((1, D_MODEL), lambda i, f: (0, 0)),
            pl.BlockSpec((D_MODEL, FFN_TF), lambda i, f: (0, f)),
            pl.BlockSpec((D_MODEL, FFN_TF), lambda i, f: (0, f)),
            pl.BlockSpec((FFN_TF, D_MODEL), lambda i, f: (f, 0)),
            pl.BlockSpec((1, D_MODEL), lambda i, f: (0, 0)),
        ],
        out_specs=pl.BlockSpec((tm, D_MODEL), lambda i, f: (i, 0)),
        out_shape=jax.ShapeDtypeStruct((m, D_MODEL), F32),
        scratch_shapes=[pltpu.VMEM((tm, D_MODEL), BF16)],
        compiler_params=_params("arbitrary", "arbitrary"),
        name="proj_ffn",
    )(x, ya, yb, w_out_b, g, w_gate, w_up, w_down, g_final)


MIX_TM = 512
CHUNKS_PER_TILE = MIX_TM // SSD_CHUNK
FFN_UP_PIECE = 256
FFN_UP_K = 2048
FFN_DOWN_PIECE = 512
MIX_PHASES_PER_FFN_PIECE = 5
MIX_LEAD_PIECES = 0


def _interleave(main, side, side_per_main, lead=0):
    for k, _ in enumerate(main):
        if k >= lead:
            for _ in range(side_per_main):
                next(side, None)
    for _ in side:
        pass


def _mix_ffn_kernel(x_ref, wo_ref, g_ref, wg_ref, wu_ref, wd_ref, gf_ref,
                    u_ref, v_ref, gv_ref, ws_ref, bst_ref, go_ref,
                    xbc_ref, z_ref, dt_ref, cw_ref, cb_ref, dtb_ref, alog_ref, drep_ref, ng_ref,
                    tril3_ref,
                    o_ref, nconv_ref, nssm_ref,
                    m_ref, ya_ref, yb_ref, tail_ref, ht_ref, wm_ref, *, chunks_per_seq):
    s = pl.program_id(0)
    f = pl.program_id(1)
    has_ffn = s >= 1
    has_mix = (s < pl.num_programs(0) - 1) & (f < CHUNKS_PER_TILE)

    @pl.when((s == 0) & (f == 0))
    def _():
        _mask_chunk_weights(ws_ref, wm_ref)

    @pl.when(has_ffn & (f == 0))
    def _():
        y = jnp.concatenate([ya_ref[...], yb_ref[...]], axis=1)
        h = x_ref[...] + _dot(y, wo_ref[...])
        m_ref[...] = _rms(h, g_ref[...]).astype(BF16)
        o_ref[...] = h

    def ffn_step():
        acts = []
        for c in range(FFN_TF // FFN_UP_PIECE):
            cols = slice(c * FFN_UP_PIECE, (c + 1) * FFN_UP_PIECE)
            halves = []
            for w_ref in (wg_ref, wu_ref):
                acc = None
                for k in range(D_MODEL // FFN_UP_K):
                    rows = slice(k * FFN_UP_K, (k + 1) * FFN_UP_K)
                    part = _dot(m_ref[:, rows], w_ref[rows, cols])
                    acc = part if acc is None else acc + part
                    yield
                halves.append(acc)
            acts.append((_silu(halves[0]) * halves[1]).astype(BF16))
        act = jnp.concatenate(acts, axis=1)
        for c in range(D_MODEL // FFN_DOWN_PIECE):
            cols = slice(c * FFN_DOWN_PIECE, (c + 1) * FFN_DOWN_PIECE)
            o_ref[:, cols] += _dot(act, wd_ref[:, cols])
            yield

    chunk_rows = pl.ds(pl.multiple_of(f * SSD_CHUNK, SSD_CHUNK), SSD_CHUNK)

    def store_ya(ya):
        ya_ref[chunk_rows, :] = ya

    def store_yb(yb):
        yb_ref[chunk_rows, :] = yb

    def mix_step():
        first = (s * CHUNKS_PER_TILE + f) % chunks_per_seq == 0
        yield from _chunk_gate(u_ref, v_ref, gv_ref, wm_ref, bst_ref, go_ref, store_ya)
        yield
        yield from _ssd_chunk(first, xbc_ref, z_ref, dt_ref, cw_ref, cb_ref, dtb_ref, alog_ref, drep_ref,
                              ng_ref, tril3_ref, nconv_ref, nssm_ref, tail_ref, ht_ref, store_yb)

    @pl.when(has_ffn & has_mix)
    def _():
        _interleave(ffn_step(), mix_step(), MIX_PHASES_PER_FFN_PIECE, MIX_LEAD_PIECES)

    @pl.when(has_ffn & jnp.logical_not(has_mix))
    def _():
        _interleave(ffn_step(), iter(()), 0)

    @pl.when(jnp.logical_not(has_ffn) & has_mix)
    def _():
        _interleave(iter(()), mix_step(), 0)

    @pl.when(has_ffn & (f == pl.num_programs(1) - 1))
    def _():
        o_ref[...] = _rms(o_ref[...], gf_ref[...])


def _mix_ffn(x, proj, dt_raw, w_out_b, g, w_gate, w_up, w_down, g_final, gv, ws, bst, go,
             cw, cb, dtb, alog, drep, ng, tril3, batch, seq):
    m = x.shape[0]
    n_tiles = m // MIX_TM
    n_chunks = m // SSD_CHUNK
    chunks_per_seq = seq // SSD_CHUNK
    assert MIX_TM % SSD_CHUNK == 0 and seq % MIX_TM == 0 and CHUNKS_PER_TILE <= D_FF // FFN_TF

    def tile(s, f):
        return jnp.maximum(s - 1, 0)

    def chunk(s, f):
        return jnp.where(s >= n_tiles, n_chunks - 1,
                         s * CHUNKS_PER_TILE + jnp.minimum(f, CHUNKS_PER_TILE - 1))

    def seq_of(s, f):
        return chunk(s, f) // chunks_per_seq

    const2 = lambda s, f: (0, 0)
    return pl.pallas_call(
        functools.partial(_mix_ffn_kernel, chunks_per_seq=chunks_per_seq),
        grid=(n_tiles + 1, D_FF // FFN_TF),
        in_specs=[
            pl.BlockSpec((MIX_TM, D_MODEL), lambda s, f: (tile(s, f), 0)),
            pl.BlockSpec((D_MODEL, D_MODEL), const2, pipeline_mode=pl.Buffered(1)),
            pl.BlockSpec((1, D_MODEL), const2),
            pl.BlockSpec((D_MODEL, FFN_TF), lambda s, f: (0, f)),
            pl.BlockSpec((D_MODEL, FFN_TF), lambda s, f: (0, f)),
            pl.BlockSpec((FFN_TF, D_MODEL), lambda s, f: (f, 0)),
            pl.BlockSpec((1, D_MODEL), const2),
            pl.BlockSpec((CHUNK_LEN, D_CHUNK), lambda s, f: (chunk(s, f), U_BLK)),
            pl.BlockSpec((CHUNK_LEN, D_CHUNK), lambda s, f: (chunk(s, f), V_BLK)),
            pl.BlockSpec((1, D_CHUNK), const2),
            pl.BlockSpec((CHUNK_HEADS, CHUNK_LEN, CHUNK_LEN), lambda s, f: (0, 0, 0)),
            pl.BlockSpec((CHUNK_LEN, CHUNK_HEADS), const2),
            pl.BlockSpec((1, D_CHUNK), const2),
            pl.BlockSpec((SSD_CHUNK, CONV_DIM), lambda s, f: (chunk(s, f), XBC_BLK)),
            pl.BlockSpec((SSD_CHUNK, D_SSM), lambda s, f: (chunk(s, f), Z_BLK)),
            pl.BlockSpec((SSD_CHUNK, HEAD_PAD), lambda s, f: (chunk(s, f), 0)),
            pl.BlockSpec((CONV_W, CONV_DIM), const2),
            pl.BlockSpec((1, CONV_DIM), const2),
            pl.BlockSpec((1, HEAD_PAD), const2),
            pl.BlockSpec((1, HEAD_PAD), const2),
            pl.BlockSpec((1, D_SSM), const2),
            pl.BlockSpec((1, D_SSM), const2),
            pl.BlockSpec((SSD_CHUNK, N_SPLIT * SSD_CHUNK), const2),
        ],
        out_specs=[
            pl.BlockSpec((MIX_TM, D_MODEL), lambda s, f: (tile(s, f), 0)),
            pl.BlockSpec((None, CONV_W - 1, CONV_DIM), lambda s, f: (seq_of(s, f), 0, 0)),
            pl.BlockSpec((None, D_SSM, D_STATE), lambda s, f: (seq_of(s, f), 0, 0)),
        ],
        out_shape=[
            jax.ShapeDtypeStruct((m, D_MODEL), F32),
            jax.ShapeDtypeStruct((batch, CONV_W - 1, CONV_DIM), F32),
            jax.ShapeDtypeStruct((batch, D_SSM, D_STATE), F32),
        ],
        scratch_shapes=[
            pltpu.VMEM((MIX_TM, D_MODEL), BF16),
            pltpu.VMEM((MIX_TM, D_CHUNK), BF16),
            pltpu.VMEM((MIX_TM, D_SSM), BF16),
            pltpu.VMEM((CONV_PAD, CONV_DIM), F32),
            pltpu.VMEM((D_STATE, D_SSM), F32),
            pltpu.VMEM((CHUNK_LEN, CHUNK_HEADS * CHUNK_LEN), BF16),
        ],
        compiler_params=_params("arbitrary", "arbitrary"),
        name="mix_ffn",
    )(x, w_out_b, g, w_gate, w_up, w_down, g_final, proj, proj, gv, ws, bst, go, proj, proj, dt_raw,
      cw, cb, dtb, alog, drep, ng, tril3)


def _row(v, width=None):
    v = v.astype(F32).reshape(1, -1)
    if width is not None and v.shape[1] < width:
        v = jnp.pad(v, ((0, 0), (0, width - v.shape[1])))
    return v


def kernel(x_prompt, x_sample, state_conv, state_ssm, norm_mix_g, w_in, chunk_v_norm_g, chunk_w_s,
           chunk_b_s, chunk_out_norm_g, ssd_conv_w, ssd_conv_b, ssd_dt_bias, ssd_a_log, ssd_d,
           ssd_norm_g, w_out, norm_ffn_g, w_gate, w_up, w_down, norm_final_g):
    depth = w_in.shape[0]
    batch, seq, _ = x_prompt.shape
    n_s = x_sample.shape[0]
    assert x_sample.shape[1] == 1 and seq % SSD_CHUNK == 0 and depth == 1

    rep = (lax.broadcasted_iota(jnp.int32, (HEAD_PAD, D_SSM), 0)
           == lax.broadcasted_iota(jnp.int32, (HEAD_PAD, D_SSM), 1) // SSM_HEAD_DIM).astype(F32)
    rep8 = (lax.broadcasted_iota(jnp.int32, (HEAD_PAD, HEAD_PAD), 0)
            == lax.broadcasted_iota(jnp.int32, (HEAD_PAD, HEAD_PAD), 1) // SUBLANES).astype(F32)
    tril3 = jnp.tile(jnp.tril(jnp.ones((SSD_CHUNK, SSD_CHUNK), F32)), (1, N_SPLIT)).astype(BF16)

    hp = x_prompt.reshape(batch * seq, D_MODEL)
    hs = x_sample.reshape(n_s, D_MODEL)
    l = 0
    w_in_t = jnp.swapaxes(w_in, 1, 2)
    w_dt_t = jnp.pad(w_in_t[l, D_MAIN:, :], ((0, HEAD_PAD - SSM_HEADS), (0, 0))).astype(BF16)
    g_mix = _row(norm_mix_g[l])
    gv = _row(chunk_v_norm_g[l])
    go = _row(chunk_out_norm_g[l])
    cw = ssd_conv_w[l].astype(F32)
    cb = _row(ssd_conv_b[l])
    dtb = _row(ssd_dt_bias[l], HEAD_PAD)
    alog = _row(ssd_a_log[l], HEAD_PAD)
    drep = _row(jnp.repeat(ssd_d[l], SSM_HEAD_DIM))
    ng = _row(ssd_norm_g[l])
    g_ffn = _row(norm_ffn_g[l])
    g_final = _row(norm_final_g)

    proj_p, dt_p, proj_s, dt_s, w_gate_b, w_up_b, w_down_b, w_out_b = _in_proj(
        hp, hs, g_mix, w_in_t, l, w_dt_t, tm=1024, cast=(w_gate, w_up, w_down, w_out))

    y_p, conv_p, ssm_p = _mix_ffn(hp, proj_p, dt_p, w_out_b, g_ffn, w_gate_b, w_up_b, w_down_b, g_final,
                                  gv, chunk_w_s[l].astype(F32), chunk_b_s[l].astype(F32).T, go,
                                  cw, cb, dtb, alog, drep, ng, tril3, batch, seq)

    w0 = _row(jnp.repeat(chunk_w_s[l, :, 0, 0], CHUNK_HEAD_DIM))
    b0 = _row(jnp.repeat(chunk_b_s[l, :, 0], CHUNK_HEAD_DIM))
    ya_s, v_s, conv_s, split, b_s, c_s, dx_s = _sample_rows(
        proj_s, dt_s, state_conv[l].reshape(n_s, (CONV_W - 1) * CONV_DIM), gv, w0, b0, go,
        cw, cb, dtb, alog, drep, rep, rep8)
    ssm_s, yb_s = _sample_state(split, b_s, c_s, state_ssm[l].reshape(n_s, D_SSM, D_STATE),
                                dx_s, proj_s, ng)
    y_s = _proj_ffn(hs, ya_s, yb_s, w_out_b, g_ffn, w_gate_b, w_up_b, w_down_b, g_final, tm=n_s)

    return (
        y_p.reshape(batch, seq, D_MODEL),
        y_s.reshape(n_s, 1, D_MODEL),
        conv_p[None],
        ssm_p.reshape(1, batch, SSM_HEADS, SSM_HEAD_DIM, D_STATE),
        conv_s.reshape(1, n_s, CONV_W - 1, CONV_DIM),
        ssm_s.reshape(1, n_s, SSM_HEADS, SSM_HEAD_DIM, D_STATE),
        v_s.reshape(1, n_s, 1, D_CHUNK),
    )
```

```python
import functools

import jax
import jax.numpy as jnp
from jax import lax
from jax.experimental import pallas as pl
from jax.experimental.pallas import tpu as pltpu

F32 = jnp.float32
BF16 = jnp.bfloat16
HIGHEST = lax.Precision.HIGHEST

D_MODEL = 2048
D_CHUNK = 1024
CHUNK_HEADS = 8
CHUNK_HEAD_DIM = 128
CHUNK_LEN = 128
D_SSM = 1024
SSM_HEAD_DIM = 64
SSM_HEADS = 16
SSM_GROUPS = 2
HEADS_PER_GROUP = SSM_HEADS // SSM_GROUPS
GROUP_WIDTH = D_SSM // SSM_GROUPS
D_STATE = 128
CONV_W = 4
CONV_DIM = D_SSM + 2 * SSM_GROUPS * D_STATE
SSD_CHUNK = 128
D_MAIN = 2 * D_CHUNK + D_SSM + CONV_DIM
D_FF = 5632
EPS = 1e-6

LANES = 128
SUBLANES = 8
BF16_SUBLANES = 16
HEAD_PAD = LANES
N_SPLIT = 3
VMEM_LIMIT = 56 * 1024 * 1024

U_BLK, V_BLK, Z_BLK = 0, 1, 2
XBC_BLK = 2


def _rms(x, g):
    return x * lax.rsqrt(jnp.mean(x * x, axis=-1, keepdims=True) + EPS) * g


def _silu(x):
    return x * jax.nn.sigmoid(x)


def _softplus(x):
    return jnp.maximum(x, 0.0) + jnp.log1p(jnp.exp(-jnp.abs(x)))


def _dot(a, b):
    return jnp.dot(a, b, preferred_element_type=F32)


def _dot_nt(a, b):
    return lax.dot_general(a, b, (((1,), (1,)), ((), ())), preferred_element_type=F32)


def _dot_exact(a, b):
    return jnp.dot(a, b, precision=HIGHEST, preferred_element_type=F32)


def _params(*semantics):
    return pltpu.CompilerParams(dimension_semantics=semantics, vmem_limit_bytes=VMEM_LIMIT)


IN_TN = 512


def _in_proj_kernel(x_hbm, xs_hbm, g_ref, w_ref, wdt_ref, *refs, tm, n_cast):
    cast_in = refs[:n_cast]
    o_ref, dt_ref, os_ref, dts_ref = refs[n_cast:n_cast + 4]
    cast_out = refs[n_cast + 4:2 * n_cast + 4]
    xn_ref, wres_ref, xbuf_ref, sem = refs[2 * n_cast + 4:]
    i = pl.program_id(0)
    j = pl.program_id(1)
    n_i = pl.num_programs(0) - 1
    n_s = xs_hbm.shape[0]

    for src, dst in zip(cast_in, cast_out, strict=True):
        dst[...] = src[...].astype(BF16)

    def x_copy(tile):
        return pltpu.make_async_copy(x_hbm.at[pl.ds(tile * tm, tm), :], xbuf_ref, sem)

    def xs_copy():
        return pltpu.make_async_copy(xs_hbm, xbuf_ref.at[pl.ds(0, n_s), :], sem)

    @pl.when((i == 0) & (j == 0))
    def _():
        x_copy(0).start()

    @pl.when(i == 0)
    def _():
        wres_ref[j] = w_ref[...].T.astype(BF16)

    @pl.when((j == 0) & (i < n_i))
    def _():
        x_copy(i).wait()
        nb = _rms(xbuf_ref[...], g_ref[...]).astype(BF16)
        xn_ref[...] = nb
        dt_ref[...] = _dot_nt(nb, wdt_ref[...])

    @pl.when((j == 0) & (i == n_i))
    def _():
        xs_copy().wait()
        nb = _rms(xbuf_ref[0:n_s, :], g_ref[...]).astype(BF16)
        xn_ref[0:n_s, :] = nb
        dts_ref[...] = _dot_nt(nb, wdt_ref[...])

    @pl.when((j == 1) & (i + 1 < n_i))
    def _():
        x_copy(i + 1).start()

    @pl.when((j == 1) & (i + 1 == n_i))
    def _():
        xs_copy().start()

    @pl.when(i < n_i)
    def _():
        o_ref[...] = _dot(xn_ref[...], wres_ref[j])

    @pl.when(i == n_i)
    def _():
        os_ref[...] = _dot(xn_ref[0:n_s, :], wres_ref[j])


def _in_proj(x, xs, g, w_in_t, layer, w_dt_t, tm, cast=()):
    m = x.shape[0]
    n_s = xs.shape[0]
    n_i = m // tm
    n_j = D_MAIN // IN_TN
    assert n_j >= 2 and n_s <= tm and n_s % BF16_SUBLANES == 0
    cast_in, cast_out, cast_shapes = [], [], []
    n_steps = n_i * n_j

    def row_tile(i, j):
        return jnp.minimum(i, n_i - 1)

    def col_prompt(i, j):
        return jnp.where(i == n_i, n_j - 1, j)

    def col_sample(i, j):
        return jnp.where(i == n_i, j, 0)

    for w in cast:
        _, w_rows, w_cols = w.shape
        slab_rows = next(r for r in range(BF16_SUBLANES, w_rows + 1, BF16_SUBLANES)
                         if w_rows % r == 0 and w_rows // r <= n_steps)
        n_slabs = w_rows // slab_rows
        hold = n_steps // n_slabs
        slab = lambda i, j, n_slabs=n_slabs, hold=hold: jnp.minimum((i * n_j + j) // hold, n_slabs - 1)
        cast_in.append(pl.BlockSpec((None, slab_rows, w_cols), lambda i, j, slab=slab: (layer, slab(i, j), 0)))
        cast_out.append(pl.BlockSpec((slab_rows, w_cols), lambda i, j, slab=slab: (slab(i, j), 0)))
        cast_shapes.append(jax.ShapeDtypeStruct((w_rows, w_cols), BF16))
    return pl.pallas_call(
        functools.partial(_in_proj_kernel, tm=tm, n_cast=len(cast)),
        grid=(n_i + 1, n_j),
        in_specs=[
            pl.BlockSpec(memory_space=pl.ANY),
            pl.BlockSpec(memory_space=pl.ANY),
            pl.BlockSpec((1, D_MODEL), lambda i, j: (0, 0)),
            pl.BlockSpec((None, IN_TN, D_MODEL), lambda i, j: (layer, jnp.where(i == 0, j, n_j - 1), 0)),
            pl.BlockSpec((HEAD_PAD, D_MODEL), lambda i, j: (0, 0)),
            *cast_in,
        ],
        out_specs=[
            pl.BlockSpec((tm, IN_TN), lambda i, j: (row_tile(i, j), col_prompt(i, j))),
            pl.BlockSpec((tm, HEAD_PAD), lambda i, j: (row_tile(i, j), 0)),
            pl.BlockSpec((n_s, IN_TN), lambda i, j: (0, col_sample(i, j))),
            pl.BlockSpec((n_s, HEAD_PAD), lambda i, j: (0, 0)),
            *cast_out,
        ],
        out_shape=[
            jax.ShapeDtypeStruct((m, D_MAIN), F32),
            jax.ShapeDtypeStruct((m, HEAD_PAD), F32),
            jax.ShapeDtypeStruct((n_s, D_MAIN), F32),
            jax.ShapeDtypeStruct((n_s, HEAD_PAD), F32),
            *cast_shapes,
        ],
        scratch_shapes=[
            pltpu.VMEM((tm, D_MODEL), BF16),
            pltpu.VMEM((n_j, D_MODEL, IN_TN), BF16),
            pltpu.VMEM((tm, D_MODEL), F32),
            pltpu.SemaphoreType.DMA(()),
        ],
        compiler_params=_params("arbitrary", "arbitrary"),
        name="in_proj",
    )(x, xs, g, w_in_t, w_dt_t, *cast)


def _mask_chunk_weights(ws_ref, wm_ref):
    row = lax.broadcasted_iota(jnp.int32, (CHUNK_LEN, CHUNK_LEN), 0)
    col = lax.broadcasted_iota(jnp.int32, (CHUNK_LEN, CHUNK_LEN), 1)
    for h in range(CHUNK_HEADS):
        wm_ref[:, h * CHUNK_LEN:(h + 1) * CHUNK_LEN] = jnp.where(row >= col, ws_ref[h], 0.0).astype(BF16)


def _chunk_gate(u_ref, v_ref, gv_ref, wm_ref, bst_ref, go_ref, store_y):
    vb = _rms(jax.nn.gelu(v_ref[...]), gv_ref[...]).astype(BF16)
    yield
    u = jax.nn.gelu(u_ref[...])
    yield
    zeros = jnp.zeros((CHUNK_LEN, CHUNK_HEAD_DIM), BF16)
    parts = []
    for h0 in range(0, CHUNK_HEADS, 2):
        w = wm_ref[:, h0 * CHUNK_LEN:(h0 + 2) * CHUNK_LEN]
        va = vb[:, h0 * CHUNK_HEAD_DIM:(h0 + 1) * CHUNK_HEAD_DIM]
        vb2 = vb[:, (h0 + 1) * CHUNK_HEAD_DIM:(h0 + 2) * CHUNK_HEAD_DIM]
        rhs = jnp.concatenate([jnp.concatenate([va, zeros], axis=1),
                               jnp.concatenate([zeros, vb2], axis=1)], axis=0)
        mixed = _dot(w, rhs)
        for k, h in enumerate((h0, h0 + 1)):
            sl = slice(h * CHUNK_HEAD_DIM, (h + 1) * CHUNK_HEAD_DIM)
            mh = mixed[:, k * CHUNK_HEAD_DIM:(k + 1) * CHUNK_HEAD_DIM] + bst_ref[:, h:h + 1]
            parts.append(u[:, sl] * mh)
        yield
    y = jnp.concatenate(parts, axis=1)
    store_y(_rms(y, go_ref[...]).astype(BF16))


CONV_PAD = 8


def _group_norm_out(y, z, g):
    y = y * _silu(z)
    parts = []
    for grp in range(SSM_GROUPS):
        yg = y[:, grp * GROUP_WIDTH:(grp + 1) * GROUP_WIDTH]
        parts.append(yg * lax.rsqrt(jnp.mean(yg * yg, axis=-1, keepdims=True) + EPS))
    return (jnp.concatenate(parts, axis=1) * g).astype(BF16)


def _split3(x):
    p1 = x.astype(BF16)
    r1 = x - p1.astype(F32)
    p2 = r1.astype(BF16)
    p3 = (r1 - p2.astype(F32)).astype(BF16)
    return p1, p2, p3


def _select_rows(sel3, x):
    return _dot(sel3, jnp.concatenate(_split3(x), axis=0))


def _ssd_chunk(first, xbc_ref, z_ref, dt_ref, cw_ref, cb_ref, dtb_ref, alog_ref, drep_ref, ng_ref,
               tril3_ref, nconv_ref, nssm_ref, tail_ref, ht_ref, store_y):
    q = SSD_CHUNK

    xbc = xbc_ref[...]
    tail = jnp.where(first, 0.0, tail_ref[...])
    ht_prev = jnp.where(first, 0.0, ht_ref[...])
    sub = lax.broadcasted_iota(jnp.int32, (CONV_PAD, CONV_DIM), 0)
    conv = cb_ref[...] + xbc * cw_ref[CONV_W - 1:CONV_W, :]
    for k in range(1, CONV_W):
        shifted = pltpu.roll(xbc, k, 0)
        head = jnp.where(sub < k, pltpu.roll(tail, k, 0), shifted[0:CONV_PAD, :])
        shifted = jnp.concatenate([head, shifted[CONV_PAD:, :]], axis=0)
        conv = conv + shifted * cw_ref[CONV_W - 1 - k:CONV_W - k, :]
    tail_ref[...] = xbc[q - CONV_PAD:, :]
    nconv_ref[...] = xbc[q - (CONV_W - 1):, :]
    yield

    act = _silu(conv)
    xs = act[:, :D_SSM]
    bmat = act[:, D_SSM:D_SSM + SSM_GROUPS * D_STATE]
    cmat = act[:, D_SSM + SSM_GROUPS * D_STATE:]
    yield

    dt = _softplus(dt_ref[...] + dtb_ref[...])
    a = -jnp.exp(alog_ref[...])
    acum = _select_rows(tril3_ref[...], dt * a)
    acum_t = acum.T
    low_lanes = lax.broadcasted_iota(jnp.int32, (q, 2 * SSM_HEAD_DIM), 1) < SSM_HEAD_DIM

    def expand(cols):
        pairs = []
        for h0 in range(0, SSM_HEADS, 2):
            a = jnp.broadcast_to(cols[:, h0:h0 + 1], (q, 2 * SSM_HEAD_DIM))
            b = jnp.broadcast_to(cols[:, h0 + 1:h0 + 2], (q, 2 * SSM_HEAD_DIM))
            pairs.append(jnp.where(low_lanes, a, b))
        return jnp.concatenate(pairs, axis=1)

    dt_rep = expand(dt)
    yield
    acum_rep = expand(acum)
    yield
    xdt = xs * dt_rep
    exp_a = jnp.exp(acum_rep)
    xw = xdt * jnp.exp(acum_rep[q - 1:q, :] - acum_rep)
    yield

    row = lax.broadcasted_iota(jnp.int32, (q, q), 0)
    col = lax.broadcasted_iota(jnp.int32, (q, q), 1)
    causal = row >= col
    low_half = col < SSM_HEAD_DIM
    ydiag, yoff, states = [], [], []
    for grp in range(SSM_GROUPS):
        bg_f = bmat[:, grp * D_STATE:(grp + 1) * D_STATE]
        bg = bg_f.astype(BF16)
        cg = cmat[:, grp * D_STATE:(grp + 1) * D_STATE].astype(BF16)
        cb = _dot_nt(cg, bg)
        ch = slice(grp * GROUP_WIDTH, (grp + 1) * GROUP_WIDTH)
        yoff.append(_dot(cg, ht_prev[:, ch].astype(BF16)))
        yield
        for quad in range(HEADS_PER_GROUP // 4):
            h0 = grp * HEADS_PER_GROUP + 4 * quad
            masks, blocks = [], []
            for k, h in enumerate(range(h0, h0 + 4)):
                diff = acum[:, h:h + 1] - acum_t[h:h + 1, :]
                masks.append((cb * jnp.exp(jnp.where(causal, diff, -jnp.inf))).astype(BF16))
                xp = xdt[:, (h - k % 2) * SSM_HEAD_DIM:(h - k % 2 + 2) * SSM_HEAD_DIM]
                own = jnp.where(low_half == (k % 2 == 0), xp, 0.0).astype(BF16)
                zero = jnp.zeros_like(own)
                blocks.append(jnp.concatenate([own, zero] if k < 2 else [zero, own], axis=1))
                if k % 2 == 1:
                    yield
            ydiag.append(_dot(jnp.concatenate(masks, axis=1), jnp.concatenate(blocks, axis=0)))
            yield
        states.append(_dot(bg_f.T.astype(BF16), xw[:, ch].astype(BF16)))
        yield
    y = drep_ref[...] * xs + jnp.concatenate(ydiag, axis=1) + jnp.concatenate(yoff, axis=1) * exp_a

    ht_new = ht_prev * exp_a[q - 1:q, :] + jnp.concatenate(states, axis=1)
    ht_ref[...] = ht_new
    nssm_ref[...] = ht_new.T
    yield
    store_y(_group_norm_out(y, z_ref[...], ng_ref[...]))


def _sample_rows_kernel(u_ref, v_ref, xbc_ref, dt_ref, sc_ref, gv_ref, w0_ref, b0_ref, go_ref,
                        cw_ref, cb_ref, dtb_ref, alog_ref, drep_ref, rep_ref, rep8_ref,
                        ya_ref, vrows_ref, nconv_ref, split_ref, b_ref, c_ref, dx_ref):
    vr = _rms(jax.nn.gelu(v_ref[...]), gv_ref[...])
    vrows_ref[...] = vr
    mixed = vr * w0_ref[...] + b0_ref[...]
    ya_ref[...] = _rms(jax.nn.gelu(u_ref[...]) * mixed, go_ref[...]).astype(BF16)

    xbc = xbc_ref[...]
    conv = cb_ref[...]
    for k in range(CONV_W - 1):
        conv = conv + sc_ref[:, k * CONV_DIM:(k + 1) * CONV_DIM] * cw_ref[k:k + 1, :]
    conv = conv + xbc * cw_ref[CONV_W - 1:CONV_W, :]
    nconv_ref[:, 0:(CONV_W - 2) * CONV_DIM] = sc_ref[:, CONV_DIM:(CONV_W - 1) * CONV_DIM]
    nconv_ref[:, (CONV_W - 2) * CONV_DIM:] = xbc

    act = _silu(conv)
    xs = act[:, :D_SSM]
    b_ref[...] = act[:, D_SSM:D_SSM + SSM_GROUPS * D_STATE]
    c_ref[...] = act[:, D_SSM + SSM_GROUPS * D_STATE:]
    dx_ref[...] = drep_ref[...] * xs

    dt = _softplus(dt_ref[...] + dtb_ref[...])
    decay = jnp.exp(dt * (-jnp.exp(alog_ref[...])))
    xdt = xs * _dot_exact(dt, rep_ref[...])
    cols = jnp.concatenate([xdt, _dot_exact(decay, rep8_ref[...])], axis=1).T
    for p in range(N_SPLIT):
        piece = cols.astype(BF16)
        split_ref[:, p * LANES:(p + 1) * LANES] = piece
        cols = cols - piece.astype(F32)


SPLIT_ROWS = D_SSM + HEAD_PAD


def _sample_rows(proj, dt_raw, sconv, gv, w0, b0, go, cw, cb, dtb, alog, drep, rep, rep8):
    n = proj.shape[0]
    full = lambda shape: pl.BlockSpec(shape, lambda i: (0,) * len(shape))
    return pl.pallas_call(
        _sample_rows_kernel,
        grid=(1,),
        in_specs=[
            pl.BlockSpec((n, D_CHUNK), lambda i: (0, U_BLK)),
            pl.BlockSpec((n, D_CHUNK), lambda i: (0, V_BLK)),
            pl.BlockSpec((n, CONV_DIM), lambda i: (0, XBC_BLK)),
            full((n, HEAD_PAD)),
            full((n, (CONV_W - 1) * CONV_DIM)),
            full((1, D_CHUNK)), full((1, D_CHUNK)), full((1, D_CHUNK)), full((1, D_CHUNK)),
            full((CONV_W, CONV_DIM)), full((1, CONV_DIM)),
            full((1, HEAD_PAD)), full((1, HEAD_PAD)), full((1, D_SSM)),
            full((HEAD_PAD, D_SSM)), full((HEAD_PAD, HEAD_PAD)),
        ],
        out_specs=[
            full((n, D_CHUNK)), full((n, D_CHUNK)), full((n, (CONV_W - 1) * CONV_DIM)),
            full((SPLIT_ROWS, N_SPLIT * LANES)),
            full((n, SSM_GROUPS * D_STATE)), full((n, SSM_GROUPS * D_STATE)), full((n, D_SSM)),
        ],
        out_shape=[
            jax.ShapeDtypeStruct((n, D_CHUNK), BF16),
            jax.ShapeDtypeStruct((n, D_CHUNK), F32),
            jax.ShapeDtypeStruct((n, (CONV_W - 1) * CONV_DIM), F32),
            jax.ShapeDtypeStruct((SPLIT_ROWS, N_SPLIT * LANES), BF16),
            jax.ShapeDtypeStruct((n, SSM_GROUPS * D_STATE), F32),
            jax.ShapeDtypeStruct((n, SSM_GROUPS * D_STATE), F32),
            jax.ShapeDtypeStruct((n, D_SSM), F32),
        ],
        compiler_params=_params("arbitrary"),
        name="sample_rows",
    )(proj, proj, proj, dt_raw, sconv, gv, w0, b0, go, cw, cb, dtb, alog, drep, rep, rep8)


SAMPLE_TB = 16


def _sample_state_kernel(split_ref, b_ref, c_ref, h0_ref, dx_ref, z_ref, ng_ref,
                         hn_ref, y_ref, yt_ref):
    i = pl.program_id(0)
    n_tok = b_ref.shape[0]

    @pl.when(i == 0)
    def _():
        yt_ref[...] = jnp.zeros_like(yt_ref)

    tok = lax.broadcasted_iota(jnp.int32, (N_SPLIT * LANES, D_STATE), 0) % LANES
    lane = lax.broadcasted_iota(jnp.int32, (D_SSM, n_tok), 1)

    def per_group(row, grp):
        return jnp.broadcast_to(row[:, grp * D_STATE:(grp + 1) * D_STATE], (GROUP_WIDTH, D_STATE))

    for k in range(SAMPLE_TB):
        t = i * SAMPLE_TB + k
        onehot = (tok == t).astype(BF16)
        bcast = _dot(split_ref[...], onehot)
        decay = jnp.broadcast_to(
            bcast[D_SSM:, :].reshape(SSM_HEADS, 1, SUBLANES, D_STATE),
            (SSM_HEADS, SSM_HEAD_DIM // SUBLANES, SUBLANES, D_STATE)).reshape(D_SSM, D_STATE)
        brow = b_ref[pl.ds(t, 1), :]
        crow = c_ref[pl.ds(t, 1), :]
        bfull = jnp.concatenate([per_group(brow, g) for g in range(SSM_GROUPS)], axis=0)
        cfull = jnp.concatenate([per_group(crow, g) for g in range(SSM_GROUPS)], axis=0)
        h_new = h0_ref[k] * decay + bcast[:D_SSM, :] * bfull
        hn_ref[k] = h_new
        ysum = jnp.sum(h_new * cfull, axis=-1, keepdims=True)
        yt_ref[...] = jnp.where(lane == t, ysum, yt_ref[...])

    @pl.when(i == pl.num_programs(0) - 1)
    def _():
        y = yt_ref[...].T + dx_ref[...]
        y_ref[...] = _group_norm_out(y, z_ref[...], ng_ref[...])


def _sample_state(split, bmat, cmat, h0, dx, proj, ng):
    n = h0.shape[0]
    const2 = lambda i: (0, 0)
    return pl.pallas_call(
        _sample_state_kernel,
        grid=(n // SAMPLE_TB,),
        in_specs=[
            pl.BlockSpec((SPLIT_ROWS, N_SPLIT * LANES), const2),
            pl.BlockSpec((n, SSM_GROUPS * D_STATE), const2),
            pl.BlockSpec((n, SSM_GROUPS * D_STATE), const2),
            pl.BlockSpec((SAMPLE_TB, D_SSM, D_STATE), lambda i: (i, 0, 0)),
            pl.BlockSpec((n, D_SSM), const2),
            pl.BlockSpec((n, D_SSM), lambda i: (0, Z_BLK)),
            pl.BlockSpec((1, D_SSM), const2),
        ],
        out_specs=[
            pl.BlockSpec((SAMPLE_TB, D_SSM, D_STATE), lambda i: (i, 0, 0)),
            pl.BlockSpec((n, D_SSM), const2),
        ],
        out_shape=[
            jax.ShapeDtypeStruct((n, D_SSM, D_STATE), F32),
            jax.ShapeDtypeStruct((n, D_SSM), BF16),
        ],
        scratch_shapes=[pltpu.VMEM((D_SSM, n), F32)],
        compiler_params=_params("arbitrary"),
        name="sample_state",
    )(split, bmat, cmat, h0, dx, proj, ng)


FFN_TF = 512
MIX_TM = 512
CHUNKS_PER_TILE = MIX_TM // SSD_CHUNK
FFN_UP_PIECE = 256
FFN_DOWN_PIECE = 512
MIX_PHASES_PER_FFN_PIECE = 5
MIX_LEAD_PIECES = 0


def _interleave(main, side, side_per_main, lead=0):
    for k, _ in enumerate(main):
        if k >= lead:
            for _ in range(side_per_main):
                next(side, None)
    for _ in side:
        pass


def _mix_ffn_kernel(x_ref, wo_ref, g_ref, wg_ref, wu_ref, wd_ref, gf_ref,
                    u_ref, v_ref, gv_ref, ws_ref, bst_ref, go_ref,
                    xbc_ref, z_ref, dt_ref, cw_ref, cb_ref, dtb_ref, alog_ref, drep_ref, ng_ref,
                    tril3_ref, xs_ref, yas_ref, ybs_ref,
                    o_ref, nconv_ref, nssm_ref, os_ref,
                    m_ref, ya_ref, yb_ref, tail_ref, ht_ref, wm_ref, ms_ref, *, chunks_per_seq):
    s = pl.program_id(0)
    f = pl.program_id(1)
    has_ffn = s >= 1
    is_last = s == pl.num_programs(0) - 1
    has_mix = jnp.logical_not(is_last) & (f < CHUNKS_PER_TILE)

    @pl.when((s == 0) & (f == 0))
    def _():
        _mask_chunk_weights(ws_ref, wm_ref)

    def project(x, ya, yb, m_out, o_out):
        h = x + _dot(jnp.concatenate([ya, yb], axis=1), wo_ref[...])
        m_out[...] = _rms(h, g_ref[...]).astype(BF16)
        o_out[...] = h

    @pl.when(has_ffn & (f == 0))
    def _():
        project(x_ref[...], ya_ref[...], yb_ref[...], m_ref, o_ref)

    @pl.when(is_last & (f == 0))
    def _():
        project(xs_ref[...], yas_ref[...], ybs_ref[...], ms_ref, os_ref)

    def ffn_step(with_sample=False):
        m = jnp.concatenate([m_ref[...], ms_ref[...]], axis=0) if with_sample else m_ref[...]
        acts = []
        for c in range(FFN_TF // FFN_UP_PIECE):
            cols = slice(c * FFN_UP_PIECE, (c + 1) * FFN_UP_PIECE)
            halves = []
            for w_ref in (wg_ref, wu_ref):
                halves.append(_dot(m, w_ref[:, cols]))
                yield
            acts.append((_silu(halves[0]) * halves[1]).astype(BF16))
        act = jnp.concatenate(acts, axis=1)
        for c in range(D_MODEL // FFN_DOWN_PIECE):
            cols = slice(c * FFN_DOWN_PIECE, (c + 1) * FFN_DOWN_PIECE)
            res = _dot(act, wd_ref[:, cols])
            o_ref[:, cols] += res[:MIX_TM]
            if with_sample:
                os_ref[:, cols] += res[MIX_TM:]
            yield

    chunk_rows = pl.ds(pl.multiple_of(f * SSD_CHUNK, SSD_CHUNK), SSD_CHUNK)

    def store_ya(ya):
        ya_ref[chunk_rows, :] = ya

    def store_yb(yb):
        yb_ref[chunk_rows, :] = yb

    def mix_step():
        first = (s * CHUNKS_PER_TILE + f) % chunks_per_seq == 0
        yield from _chunk_gate(u_ref, v_ref, gv_ref, wm_ref, bst_ref, go_ref, store_ya)
        yield
        yield from _ssd_chunk(first, xbc_ref, z_ref, dt_ref, cw_ref, cb_ref, dtb_ref, alog_ref, drep_ref,
                              ng_ref, tril3_ref, nconv_ref, nssm_ref, tail_ref, ht_ref, store_yb)

    @pl.when(has_ffn & has_mix)
    def _():
        _interleave(ffn_step(), mix_step(), MIX_PHASES_PER_FFN_PIECE, MIX_LEAD_PIECES)

    @pl.when(has_ffn & jnp.logical_not(has_mix) & jnp.logical_not(is_last))
    def _():
        _interleave(ffn_step(), iter(()), 0)

    @pl.when(is_last)
    def _():
        _interleave(ffn_step(with_sample=True), iter(()), 0)

    @pl.when(jnp.logical_not(has_ffn) & has_mix)
    def _():
        _interleave(iter(()), mix_step(), 0)

    @pl.when(has_ffn & (f == pl.num_programs(1) - 1))
    def _():
        o_ref[...] = _rms(o_ref[...], gf_ref[...])

    @pl.when(is_last & (f == pl.num_programs(1) - 1))
    def _():
        os_ref[...] = _rms(os_ref[...], gf_ref[...])


def _mix_ffn(x, proj, dt_raw, w_out_b, g, w_gate, w_up, w_down, g_final, gv, ws, bst, go,
             cw, cb, dtb, alog, drep, ng, tril3, xs, ya_s, yb_s, batch, seq):
    m = x.shape[0]
    n_s = xs.shape[0]
    n_tiles = m // MIX_TM
    n_chunks = m // SSD_CHUNK
    chunks_per_seq = seq // SSD_CHUNK
    assert MIX_TM % SSD_CHUNK == 0 and seq % MIX_TM == 0 and CHUNKS_PER_TILE <= D_FF // FFN_TF

    def tile(s, f):
        return jnp.maximum(s - 1, 0)

    def chunk(s, f):
        return jnp.where(s >= n_tiles, n_chunks - 1,
                         s * CHUNKS_PER_TILE + jnp.minimum(f, CHUNKS_PER_TILE - 1))

    def seq_of(s, f):
        return chunk(s, f) // chunks_per_seq

    const2 = lambda s, f: (0, 0)
    return pl.pallas_call(
        functools.partial(_mix_ffn_kernel, chunks_per_seq=chunks_per_seq),
        grid=(n_tiles + 1, D_FF // FFN_TF),
        in_specs=[
            pl.BlockSpec((MIX_TM, D_MODEL), lambda s, f: (tile(s, f), 0)),
            pl.BlockSpec((D_MODEL, D_MODEL), const2, pipeline_mode=pl.Buffered(1)),
            pl.BlockSpec((1, D_MODEL), const2),
            pl.BlockSpec((D_MODEL, FFN_TF), lambda s, f: (0, f)),
            pl.BlockSpec((D_MODEL, FFN_TF), lambda s, f: (0, f)),
            pl.BlockSpec((FFN_TF, D_MODEL), lambda s, f: (f, 0)),
            pl.BlockSpec((1, D_MODEL), const2),
            pl.BlockSpec((CHUNK_LEN, D_CHUNK), lambda s, f: (chunk(s, f), U_BLK)),
            pl.BlockSpec((CHUNK_LEN, D_CHUNK), lambda s, f: (chunk(s, f), V_BLK)),
            pl.BlockSpec((1, D_CHUNK), const2),
            pl.BlockSpec((CHUNK_HEADS, CHUNK_LEN, CHUNK_LEN), lambda s, f: (0, 0, 0)),
            pl.BlockSpec((CHUNK_LEN, CHUNK_HEADS), const2),
            pl.BlockSpec((1, D_CHUNK), const2),
            pl.BlockSpec((SSD_CHUNK, CONV_DIM), lambda s, f: (chunk(s, f), XBC_BLK)),
            pl.BlockSpec((SSD_CHUNK, D_SSM), lambda s, f: (chunk(s, f), Z_BLK)),
            pl.BlockSpec((SSD_CHUNK, HEAD_PAD), lambda s, f: (chunk(s, f), 0)),
            pl.BlockSpec((CONV_W, CONV_DIM), const2),
            pl.BlockSpec((1, CONV_DIM), const2),
            pl.BlockSpec((1, HEAD_PAD), const2),
            pl.BlockSpec((1, HEAD_PAD), const2),
            pl.BlockSpec((1, D_SSM), const2),
            pl.BlockSpec((1, D_SSM), const2),
            pl.BlockSpec((SSD_CHUNK, N_SPLIT * SSD_CHUNK), const2),
            pl.BlockSpec((n_s, D_MODEL), const2, pipeline_mode=pl.Buffered(1)),
            pl.BlockSpec((n_s, D_CHUNK), const2, pipeline_mode=pl.Buffered(1)),
            pl.BlockSpec((n_s, D_SSM), const2, pipeline_mode=pl.Buffered(1)),
        ],
        out_specs=[
            pl.BlockSpec((MIX_TM, D_MODEL), lambda s, f: (tile(s, f), 0)),
            pl.BlockSpec((None, CONV_W - 1, CONV_DIM), lambda s, f: (seq_of(s, f), 0, 0)),
            pl.BlockSpec((None, D_SSM, D_STATE), lambda s, f: (seq_of(s, f), 0, 0)),
            pl.BlockSpec((n_s, D_MODEL), const2),
        ],
        out_shape=[
            jax.ShapeDtypeStruct((m, D_MODEL), F32),
            jax.ShapeDtypeStruct((batch, CONV_W - 1, CONV_DIM), F32),
            jax.ShapeDtypeStruct((batch, D_SSM, D_STATE), F32),
            jax.ShapeDtypeStruct((n_s, D_MODEL), F32),
        ],
        scratch_shapes=[
            pltpu.VMEM((MIX_TM, D_MODEL), BF16),
            pltpu.VMEM((MIX_TM, D_CHUNK), BF16),
            pltpu.VMEM((MIX_TM, D_SSM), BF16),
            pltpu.VMEM((CONV_PAD, CONV_DIM), F32),
            pltpu.VMEM((D_STATE, D_SSM), F32),
            pltpu.VMEM((CHUNK_LEN, CHUNK_HEADS * CHUNK_LEN), BF16),
            pltpu.VMEM((n_s, D_MODEL), BF16),
        ],
        compiler_params=_params("arbitrary", "arbitrary"),
        name="mix_ffn",
    )(x, w_out_b, g, w_gate, w_up, w_down, g_final, proj, proj, gv, ws, bst, go, proj, proj, dt_raw,
      cw, cb, dtb, alog, drep, ng, tril3, xs, ya_s, yb_s)


def _row(v, width=None):
    v = v.astype(F32).reshape(1, -1)
    if width is not None and v.shape[1] < width:
        v = jnp.pad(v, ((0, 0), (0, width - v.shape[1])))
    return v


def kernel(x_prompt, x_sample, state_conv, state_ssm, norm_mix_g, w_in, chunk_v_norm_g, chunk_w_s,
           chunk_b_s, chunk_out_norm_g, ssd_conv_w, ssd_conv_b, ssd_dt_bias, ssd_a_log, ssd_d,
           ssd_norm_g, w_out, norm_ffn_g, w_gate, w_up, w_down, norm_final_g):
    depth = w_in.shape[0]
    batch, seq, _ = x_prompt.shape
    n_s = x_sample.shape[0]
    assert x_sample.shape[1] == 1 and seq % SSD_CHUNK == 0 and depth == 1

    rep = (lax.broadcasted_iota(jnp.int32, (HEAD_PAD, D_SSM), 0)
           == lax.broadcasted_iota(jnp.int32, (HEAD_PAD, D_SSM), 1) // SSM_HEAD_DIM).astype(F32)
    rep8 = (lax.broadcasted_iota(jnp.int32, (HEAD_PAD, HEAD_PAD), 0)
            == lax.broadcasted_iota(jnp.int32, (HEAD_PAD, HEAD_PAD), 1) // SUBLANES).astype(F32)
    tril3 = jnp.tile(jnp.tril(jnp.ones((SSD_CHUNK, SSD_CHUNK), F32)), (1, N_SPLIT)).astype(BF16)

    hp = x_prompt.reshape(batch * seq, D_MODEL)
    hs = x_sample.reshape(n_s, D_MODEL)
    l = 0
    w_in_t = jnp.swapaxes(w_in, 1, 2)
    w_dt_t = jnp.pad(w_in_t[l, D_MAIN:, :], ((0, HEAD_PAD - SSM_HEADS), (0, 0))).astype(BF16)
    g_mix = _row(norm_mix_g[l])
    gv = _row(chunk_v_norm_g[l])
    go = _row(chunk_out_norm_g[l])
    cw = ssd_conv_w[l].astype(F32)
    cb = _row(ssd_conv_b[l])
    dtb = _row(ssd_dt_bias[l], HEAD_PAD)
    alog = _row(ssd_a_log[l], HEAD_PAD)
    drep = _row(jnp.repeat(ssd_d[l], SSM_HEAD_DIM))
    ng = _row(ssd_norm_g[l])
    g_ffn = _row(norm_ffn_g[l])
    g_final = _row(norm_final_g)

    proj_p, dt_p, proj_s, dt_s, w_gate_b, w_up_b, w_down_b, w_out_b = _in_proj(
        hp, hs, g_mix, w_in_t, l, w_dt_t, tm=1024, cast=(w_gate, w_up, w_down, w_out))

    w0 = _row(jnp.repeat(chunk_w_s[l, :, 0, 0], CHUNK_HEAD_DIM))
    b0 = _row(jnp.repeat(chunk_b_s[l, :, 0], CHUNK_HEAD_DIM))
    ya_s, v_s, conv_s, split, b_s, c_s, dx_s = _sample_rows(
        proj_s, dt_s, state_conv[l].reshape(n_s, (CONV_W - 1) * CONV_DIM), gv, w0, b0, go,
        cw, cb, dtb, alog, drep, rep, rep8)
    ssm_s, yb_s = _sample_state(split, b_s, c_s, state_ssm[l].reshape(n_s, D_SSM, D_STATE),
                                dx_s, proj_s, ng)

    y_p, conv_p, ssm_p, y_s = _mix_ffn(
        hp, proj_p, dt_p, w_out_b, g_ffn, w_gate_b, w_up_b, w_down_b, g_final,
        gv, chunk_w_s[l].astype(F32), chunk_b_s[l].astype(F32).T, go,
        cw, cb, dtb, alog, drep, ng, tril3, hs, ya_s, yb_s, batch, seq)

    return (
        y_p.reshape(batch, seq, D_MODEL),
        y_s.reshape(n_s, 1, D_MODEL),
        conv_p[None],
        ssm_p.reshape(1, batch, SSM_HEADS, SSM_HEAD_DIM, D_STATE),
        conv_s.reshape(1, n_s, CONV_W - 1, CONV_DIM),
        ssm_s.reshape(1, n_s, SSM_HEADS, SSM_HEAD_DIM, D_STATE),
        v_s.reshape(1, n_s, 1, D_CHUNK),
    )
```

```python
import functools

import jax
import jax.numpy as jnp
from jax import lax
from jax.experimental import pallas as pl
from jax.experimental.pallas import tpu as pltpu

F32 = jnp.float32
BF16 = jnp.bfloat16
HIGHEST = lax.Precision.HIGHEST

D_MODEL = 2048
D_CHUNK = 1024
CHUNK_HEADS = 8
CHUNK_HEAD_DIM = 128
CHUNK_LEN = 128
D_SSM = 1024
SSM_HEAD_DIM = 64
SSM_HEADS = 16
SSM_GROUPS = 2
HEADS_PER_GROUP = SSM_HEADS // SSM_GROUPS
GROUP_WIDTH = D_SSM // SSM_GROUPS
D_STATE = 128
CONV_W = 4
CONV_DIM = D_SSM + 2 * SSM_GROUPS * D_STATE
SSD_CHUNK = 128
D_MAIN = 2 * D_CHUNK + D_SSM + CONV_DIM
D_FF = 5632
EPS = 1e-6

LANES = 128
SUBLANES = 8
BF16_SUBLANES = 16
HEAD_PAD = LANES
N_SPLIT = 3
VMEM_LIMIT = 56 * 1024 * 1024

U_BLK, V_BLK, Z_BLK = 0, 1, 2
XBC_BLK = 2


def _rms(x, g):
    return x * lax.rsqrt(jnp.mean(x * x, axis=-1, keepdims=True) + EPS) * g


def _silu(x):
    return x * jax.nn.sigmoid(x)


def _softplus(x):
    return jnp.maximum(x, 0.0) + jnp.log1p(jnp.exp(-jnp.abs(x)))


def _dot(a, b):
    return jnp.dot(a, b, preferred_element_type=F32)


def _dot_nt(a, b):
    return lax.dot_general(a, b, (((1,), (1,)), ((), ())), preferred_element_type=F32)


def _dot_exact(a, b):
    return jnp.dot(a, b, precision=HIGHEST, preferred_element_type=F32)


def _params(*semantics):
    return pltpu.CompilerParams(dimension_semantics=semantics, vmem_limit_bytes=VMEM_LIMIT)


IN_TN = 512


def _in_proj_kernel(x_hbm, xs_hbm, g_ref, w_ref, wdt_ref, *refs, tm, n_cast):
    cast_in = refs[:n_cast]
    o_ref, dt_ref, os_ref, dts_ref = refs[n_cast:n_cast + 4]
    cast_out = refs[n_cast + 4:2 * n_cast + 4]
    xn_ref, wres_ref, xbuf_ref, sem = refs[2 * n_cast + 4:]
    i = pl.program_id(0)
    j = pl.program_id(1)
    last = pl.num_programs(0) - 1
    n_s = xs_hbm.shape[0]

    for src, dst in zip(cast_in, cast_out, strict=True):
        dst[...] = src[...].astype(BF16)

    def x_copy(tile):
        return pltpu.make_async_copy(x_hbm.at[pl.ds(tile * tm, tm), :], xbuf_ref, sem)

    def xs_copy():
        return pltpu.make_async_copy(xs_hbm, xbuf_ref.at[pl.ds(0, n_s), :], sem)

    @pl.when((i == 0) & (j == 0))
    def _():
        xs_copy().start()
        xs_copy().wait()
        nb = _rms(xbuf_ref[0:n_s, :], g_ref[...]).astype(BF16)
        xn_ref[tm:tm + n_s, :] = nb
        dts_ref[...] = _dot_nt(nb, wdt_ref[...])
        x_copy(0).start()

    @pl.when(i == 0)
    def _():
        wres_ref[j] = w_ref[...].T.astype(BF16)

    @pl.when(j == 0)
    def _():
        x_copy(i).wait()
        nb = _rms(xbuf_ref[...], g_ref[...]).astype(BF16)
        xn_ref[0:tm, :] = nb
        dt_ref[...] = _dot_nt(nb, wdt_ref[...])

    @pl.when((j == 1) & (i < last))
    def _():
        x_copy(i + 1).start()

    @pl.when(i < last)
    def _():
        o_ref[...] = _dot(xn_ref[0:tm, :], wres_ref[j])

    @pl.when(i == last)
    def _():
        res = _dot(xn_ref[...], wres_ref[j])
        o_ref[...] = res[0:tm, :]
        os_ref[...] = res[tm:, :]


def _in_proj(x, xs, g, w_in_t, layer, w_dt_t, tm, cast=()):
    m = x.shape[0]
    n_s = xs.shape[0]
    n_i = m // tm
    n_j = D_MAIN // IN_TN
    assert n_j >= 2 and n_s <= tm and n_s % BF16_SUBLANES == 0
    cast_in, cast_out, cast_shapes = [], [], []
    n_steps = n_i * n_j

    def col_sample(i, j):
        return jnp.where(i == n_i - 1, j, 0)

    for w in cast:
        _, w_rows, w_cols = w.shape
        slab_rows = next(r for r in range(BF16_SUBLANES, w_rows + 1, BF16_SUBLANES)
                         if w_rows % r == 0 and w_rows // r <= n_steps)
        n_slabs = w_rows // slab_rows
        hold = n_steps // n_slabs
        slab = lambda i, j, n_slabs=n_slabs, hold=hold: jnp.minimum((i * n_j + j) // hold, n_slabs - 1)
        cast_in.append(pl.BlockSpec((None, slab_rows, w_cols), lambda i, j, slab=slab: (layer, slab(i, j), 0)))
        cast_out.append(pl.BlockSpec((slab_rows, w_cols), lambda i, j, slab=slab: (slab(i, j), 0)))
        cast_shapes.append(jax.ShapeDtypeStruct((w_rows, w_cols), BF16))
    return pl.pallas_call(
        functools.partial(_in_proj_kernel, tm=tm, n_cast=len(cast)),
        grid=(n_i, n_j),
        in_specs=[
            pl.BlockSpec(memory_space=pl.ANY),
            pl.BlockSpec(memory_space=pl.ANY),
            pl.BlockSpec((1, D_MODEL), lambda i, j: (0, 0)),
            pl.BlockSpec((None, IN_TN, D_MODEL), lambda i, j: (layer, jnp.where(i == 0, j, n_j - 1), 0)),
            pl.BlockSpec((HEAD_PAD, D_MODEL), lambda i, j: (0, 0)),
            *cast_in,
        ],
        out_specs=[
            pl.BlockSpec((tm, IN_TN), lambda i, j: (i, j)),
            pl.BlockSpec((tm, HEAD_PAD), lambda i, j: (i, 0)),
            pl.BlockSpec((n_s, IN_TN), lambda i, j: (0, col_sample(i, j))),
            pl.BlockSpec((n_s, HEAD_PAD), lambda i, j: (0, 0)),
            *cast_out,
        ],
        out_shape=[
            jax.ShapeDtypeStruct((m, D_MAIN), F32),
            jax.ShapeDtypeStruct((m, HEAD_PAD), F32),
            jax.ShapeDtypeStruct((n_s, D_MAIN), F32),
            jax.ShapeDtypeStruct((n_s, HEAD_PAD), F32),
            *cast_shapes,
        ],
        scratch_shapes=[
            pltpu.VMEM((tm + n_s, D_MODEL), BF16),
            pltpu.VMEM((n_j, D_MODEL, IN_TN), BF16),
            pltpu.VMEM((tm, D_MODEL), F32),
            pltpu.SemaphoreType.DMA(()),
        ],
        compiler_params=_params("arbitrary", "arbitrary"),
        name="in_proj",
    )(x, xs, g, w_in_t, w_dt_t, *cast)


def _mask_chunk_weights(ws_ref, wm_ref):
    row = lax.broadcasted_iota(jnp.int32, (CHUNK_LEN, CHUNK_LEN), 0)
    col = lax.broadcasted_iota(jnp.int32, (CHUNK_LEN, CHUNK_LEN), 1)
    for h in range(CHUNK_HEADS):
        wm_ref[:, h * CHUNK_LEN:(h + 1) * CHUNK_LEN] = jnp.where(row >= col, ws_ref[h], 0.0).astype(BF16)


def _chunk_gate(u_ref, v_ref, gv_ref, wm_ref, bst_ref, go_ref, store_y):
    vb = _rms(jax.nn.gelu(v_ref[...]), gv_ref[...]).astype(BF16)
    yield
    u = jax.nn.gelu(u_ref[...])
    yield
    zeros = jnp.zeros((CHUNK_LEN, CHUNK_HEAD_DIM), BF16)
    parts = []
    for h0 in range(0, CHUNK_HEADS, 2):
        w = wm_ref[:, h0 * CHUNK_LEN:(h0 + 2) * CHUNK_LEN]
        va = vb[:, h0 * CHUNK_HEAD_DIM:(h0 + 1) * CHUNK_HEAD_DIM]
        vb2 = vb[:, (h0 + 1) * CHUNK_HEAD_DIM:(h0 + 2) * CHUNK_HEAD_DIM]
        rhs = jnp.concatenate([jnp.concatenate([va, zeros], axis=1),
                               jnp.concatenate([zeros, vb2], axis=1)], axis=0)
        mixed = _dot(w, rhs)
        for k, h in enumerate((h0, h0 + 1)):
            sl = slice(h * CHUNK_HEAD_DIM, (h + 1) * CHUNK_HEAD_DIM)
            mh = mixed[:, k * CHUNK_HEAD_DIM:(k + 1) * CHUNK_HEAD_DIM] + bst_ref[:, h:h + 1]
            parts.append(u[:, sl] * mh)
        yield
    y = jnp.concatenate(parts, axis=1)
    store_y(_rms(y, go_ref[...]).astype(BF16))


CONV_PAD = 8


def _group_norm_out(y, z, g):
    y = y * _silu(z)
    parts = []
    for grp in range(SSM_GROUPS):
        yg = y[:, grp * GROUP_WIDTH:(grp + 1) * GROUP_WIDTH]
        parts.append(yg * lax.rsqrt(jnp.mean(yg * yg, axis=-1, keepdims=True) + EPS))
    return (jnp.concatenate(parts, axis=1) * g).astype(BF16)


def _split3(x):
    p1 = x.astype(BF16)
    r1 = x - p1.astype(F32)
    p2 = r1.astype(BF16)
    p3 = (r1 - p2.astype(F32)).astype(BF16)
    return p1, p2, p3


def _select_rows(sel3, x):
    return _dot(sel3, jnp.concatenate(_split3(x), axis=0))


def _ssd_chunk(first, xbc_ref, z_ref, dt_ref, cw_ref, cb_ref, dtb_ref, alog_ref, drep_ref, ng_ref,
               tril3_ref, nconv_ref, nssm_ref, tail_ref, ht_ref, store_y):
    q = SSD_CHUNK

    xbc = xbc_ref[...]
    tail = jnp.where(first, 0.0, tail_ref[...])
    ht_prev = jnp.where(first, 0.0, ht_ref[...])
    sub = lax.broadcasted_iota(jnp.int32, (CONV_PAD, CONV_DIM), 0)
    conv = cb_ref[...] + xbc * cw_ref[CONV_W - 1:CONV_W, :]
    for k in range(1, CONV_W):
        shifted = pltpu.roll(xbc, k, 0)
        head = jnp.where(sub < k, pltpu.roll(tail, k, 0), shifted[0:CONV_PAD, :])
        shifted = jnp.concatenate([head, shifted[CONV_PAD:, :]], axis=0)
        conv = conv + shifted * cw_ref[CONV_W - 1 - k:CONV_W - k, :]
    tail_ref[...] = xbc[q - CONV_PAD:, :]
    nconv_ref[...] = xbc[q - (CONV_W - 1):, :]
    yield

    act = _silu(conv)
    xs = act[:, :D_SSM]
    bmat = act[:, D_SSM:D_SSM + SSM_GROUPS * D_STATE]
    cmat = act[:, D_SSM + SSM_GROUPS * D_STATE:]
    yield

    dt = _softplus(dt_ref[...] + dtb_ref[...])
    a = -jnp.exp(alog_ref[...])
    acum = _select_rows(tril3_ref[...], dt * a)
    acum_t = acum.T
    low_lanes = lax.broadcasted_iota(jnp.int32, (q, 2 * SSM_HEAD_DIM), 1) < SSM_HEAD_DIM

    def expand(cols):
        pairs = []
        for h0 in range(0, SSM_HEADS, 2):
            a = jnp.broadcast_to(cols[:, h0:h0 + 1], (q, 2 * SSM_HEAD_DIM))
            b = jnp.broadcast_to(cols[:, h0 + 1:h0 + 2], (q, 2 * SSM_HEAD_DIM))
            pairs.append(jnp.where(low_lanes, a, b))
        return jnp.concatenate(pairs, axis=1)

    dt_rep = expand(dt)
    yield
    acum_rep = expand(acum)
    yield
    xdt = xs * dt_rep
    exp_a = jnp.exp(acum_rep)
    xw = xdt * jnp.exp(acum_rep[q - 1:q, :] - acum_rep)
    yield

    row = lax.broadcasted_iota(jnp.int32, (q, q), 0)
    col = lax.broadcasted_iota(jnp.int32, (q, q), 1)
    causal = row >= col
    low_half = col < SSM_HEAD_DIM
    ydiag, yoff, states = [], [], []
    for grp in range(SSM_GROUPS):
        bg_f = bmat[:, grp * D_STATE:(grp + 1) * D_STATE]
        bg = bg_f.astype(BF16)
        cg = cmat[:, grp * D_STATE:(grp + 1) * D_STATE].astype(BF16)
        cb = _dot_nt(cg, bg)
        ch = slice(grp * GROUP_WIDTH, (grp + 1) * GROUP_WIDTH)
        yoff.append(_dot(cg, ht_prev[:, ch].astype(BF16)))
        yield
        for quad in range(HEADS_PER_GROUP // 4):
            h0 = grp * HEADS_PER_GROUP + 4 * quad
            masks, blocks = [], []
            for k, h in enumerate(range(h0, h0 + 4)):
                diff = acum[:, h:h + 1] - acum_t[h:h + 1, :]
                masks.append((cb * jnp.exp(jnp.where(causal, diff, -jnp.inf))).astype(BF16))
                xp = xdt[:, (h - k % 2) * SSM_HEAD_DIM:(h - k % 2 + 2) * SSM_HEAD_DIM]
                own = jnp.where(low_half == (k % 2 == 0), xp, 0.0).astype(BF16)
                zero = jnp.zeros_like(own)
                blocks.append(jnp.concatenate([own, zero] if k < 2 else [zero, own], axis=1))
                if k % 2 == 1:
                    yield
            ydiag.append(_dot(jnp.concatenate(masks, axis=1), jnp.concatenate(blocks, axis=0)))
            yield
        states.append(_dot(bg_f.T.astype(BF16), xw[:, ch].astype(BF16)))
        yield
    y = drep_ref[...] * xs + jnp.concatenate(ydiag, axis=1) + jnp.concatenate(yoff, axis=1) * exp_a

    ht_new = ht_prev * exp_a[q - 1:q, :] + jnp.concatenate(states, axis=1)
    ht_ref[...] = ht_new
    nssm_ref[...] = ht_new.T
    yield
    store_y(_group_norm_out(y, z_ref[...], ng_ref[...]))


def _sample_rows_kernel(u_ref, v_ref, xbc_ref, dt_ref, sc_ref, gv_ref, w0_ref, b0_ref, go_ref,
                        cw_ref, cb_ref, dtb_ref, alog_ref, drep_ref, rep_ref, rep8_ref,
                        ya_ref, vrows_ref, nconv_ref, split_ref, b_ref, c_ref, dx_ref):
    vr = _rms(jax.nn.gelu(v_ref[...]), gv_ref[...])
    vrows_ref[...] = vr
    mixed = vr * w0_ref[...] + b0_ref[...]
    ya_ref[...] = _rms(jax.nn.gelu(u_ref[...]) * mixed, go_ref[...]).astype(BF16)

    xbc = xbc_ref[...]
    conv = cb_ref[...]
    for k in range(CONV_W - 1):
        conv = conv + sc_ref[:, k * CONV_DIM:(k + 1) * CONV_DIM] * cw_ref[k:k + 1, :]
    conv = conv + xbc * cw_ref[CONV_W - 1:CONV_W, :]
    nconv_ref[:, 0:(CONV_W - 2) * CONV_DIM] = sc_ref[:, CONV_DIM:(CONV_W - 1) * CONV_DIM]
    nconv_ref[:, (CONV_W - 2) * CONV_DIM:] = xbc

    act = _silu(conv)
    xs = act[:, :D_SSM]
    b_ref[...] = act[:, D_SSM:D_SSM + SSM_GROUPS * D_STATE]
    c_ref[...] = act[:, D_SSM + SSM_GROUPS * D_STATE:]
    dx_ref[...] = drep_ref[...] * xs

    dt = _softplus(dt_ref[...] + dtb_ref[...])
    decay = jnp.exp(dt * (-jnp.exp(alog_ref[...])))
    xdt = xs * _dot_exact(dt, rep_ref[...])
    cols = jnp.concatenate([xdt, _dot_exact(decay, rep8_ref[...])], axis=1).T
    for p in range(N_SPLIT):
        piece = cols.astype(BF16)
        split_ref[:, p * LANES:(p + 1) * LANES] = piece
        cols = cols - piece.astype(F32)


SPLIT_ROWS = D_SSM + HEAD_PAD


def _sample_rows(proj, dt_raw, sconv, gv, w0, b0, go, cw, cb, dtb, alog, drep, rep, rep8):
    n = proj.shape[0]
    full = lambda shape: pl.BlockSpec(shape, lambda i: (0,) * len(shape))
    return pl.pallas_call(
        _sample_rows_kernel,
        grid=(1,),
        in_specs=[
            pl.BlockSpec((n, D_CHUNK), lambda i: (0, U_BLK)),
            pl.BlockSpec((n, D_CHUNK), lambda i: (0, V_BLK)),
            pl.BlockSpec((n, CONV_DIM), lambda i: (0, XBC_BLK)),
            full((n, HEAD_PAD)),
            full((n, (CONV_W - 1) * CONV_DIM)),
            full((1, D_CHUNK)), full((1, D_CHUNK)), full((1, D_CHUNK)), full((1, D_CHUNK)),
            full((CONV_W, CONV_DIM)), full((1, CONV_DIM)),
            full((1, HEAD_PAD)), full((1, HEAD_PAD)), full((1, D_SSM)),
            full((HEAD_PAD, D_SSM)), full((HEAD_PAD, HEAD_PAD)),
        ],
        out_specs=[
            full((n, D_CHUNK)), full((n, D_CHUNK)), full((n, (CONV_W - 1) * CONV_DIM)),
            full((SPLIT_ROWS, N_SPLIT * LANES)),
            full((n, SSM_GROUPS * D_STATE)), full((n, SSM_GROUPS * D_STATE)), full((n, D_SSM)),
        ],
        out_shape=[
            jax.ShapeDtypeStruct((n, D_CHUNK), BF16),
            jax.ShapeDtypeStruct((n, D_CHUNK), F32),
            jax.ShapeDtypeStruct((n, (CONV_W - 1) * CONV_DIM), F32),
            jax.ShapeDtypeStruct((SPLIT_ROWS, N_SPLIT * LANES), BF16),
            jax.ShapeDtypeStruct((n, SSM_GROUPS * D_STATE), F32),
            jax.ShapeDtypeStruct((n, SSM_GROUPS * D_STATE), F32),
            jax.ShapeDtypeStruct((n, D_SSM), F32),
        ],
        compiler_params=_params("arbitrary"),
        name="sample_rows",
    )(proj, proj, proj, dt_raw, sconv, gv, w0, b0, go, cw, cb, dtb, alog, drep, rep, rep8)


SAMPLE_TB = 16


def _sample_state_kernel(split_ref, b_ref, c_ref, h0_ref, dx_ref, z_ref, ng_ref,
                         hn_ref, y_ref, yt_ref):
    i = pl.program_id(0)
    n_tok = b_ref.shape[0]

    @pl.when(i == 0)
    def _():
        yt_ref[...] = jnp.zeros_like(yt_ref)

    tok = lax.broadcasted_iota(jnp.int32, (N_SPLIT * LANES, D_STATE), 0) % LANES
    lane = lax.broadcasted_iota(jnp.int32, (D_SSM, n_tok), 1)

    def per_group(row, grp):
        return jnp.broadcast_to(row[:, grp * D_STATE:(grp + 1) * D_STATE], (GROUP_WIDTH, D_STATE))

    for k in range(SAMPLE_TB):
        t = i * SAMPLE_TB + k
        onehot = (tok == t).astype(BF16)
        bcast = _dot(split_ref[...], onehot)
        decay = jnp.broadcast_to(
            bcast[D_SSM:, :].reshape(SSM_HEADS, 1, SUBLANES, D_STATE),
            (SSM_HEADS, SSM_HEAD_DIM // SUBLANES, SUBLANES, D_STATE)).reshape(D_SSM, D_STATE)
        brow = b_ref[pl.ds(t, 1), :]
        crow = c_ref[pl.ds(t, 1), :]
        bfull = jnp.concatenate([per_group(brow, g) for g in range(SSM_GROUPS)], axis=0)
        cfull = jnp.concatenate([per_group(crow, g) for g in range(SSM_GROUPS)], axis=0)
        h_new = h0_ref[k] * decay + bcast[:D_SSM, :] * bfull
        hn_ref[k] = h_new
        ysum = jnp.sum(h_new * cfull, axis=-1, keepdims=True)
        yt_ref[...] = jnp.where(lane == t, ysum, yt_ref[...])

    @pl.when(i == pl.num_programs(0) - 1)
    def _():
        y = yt_ref[...].T + dx_ref[...]
        y_ref[...] = _group_norm_out(y, z_ref[...], ng_ref[...])


def _sample_state(split, bmat, cmat, h0, dx, proj, ng):
    n = h0.shape[0]
    const2 = lambda i: (0, 0)
    return pl.pallas_call(
        _sample_state_kernel,
        grid=(n // SAMPLE_TB,),
        in_specs=[
            pl.BlockSpec((SPLIT_ROWS, N_SPLIT * LANES), const2),
            pl.BlockSpec((n, SSM_GROUPS * D_STATE), const2),
            pl.BlockSpec((n, SSM_GROUPS * D_STATE), const2),
            pl.BlockSpec((SAMPLE_TB, D_SSM, D_STATE), lambda i: (i, 0, 0)),
            pl.BlockSpec((n, D_SSM), const2),
            pl.BlockSpec((n, D_SSM), lambda i: (0, Z_BLK)),
            pl.BlockSpec((1, D_SSM), const2),
        ],
        out_specs=[
            pl.BlockSpec((SAMPLE_TB, D_SSM, D_STATE), lambda i: (i, 0, 0)),
            pl.BlockSpec((n, D_SSM), const2),
        ],
        out_shape=[
            jax.ShapeDtypeStruct((n, D_SSM, D_STATE), F32),
            jax.ShapeDtypeStruct((n, D_SSM), BF16),
        ],
        scratch_shapes=[pltpu.VMEM((D_SSM, n), F32)],
        compiler_params=_params("arbitrary"),
        name="sample_state",
    )(split, bmat, cmat, h0, dx, proj, ng)


FFN_TF = 512
MIX_TM = 512
CHUNKS_PER_TILE = MIX_TM // SSD_CHUNK
FFN_UP_PIECE = 256
FFN_DOWN_PIECE = 512
MIX_PHASES_PER_FFN_PIECE = 5
MIX_LEAD_PIECES = 0


def _interleave(main, side, side_per_main, lead=0):
    for k, _ in enumerate(main):
        if k >= lead:
            for _ in range(side_per_main):
                next(side, None)
    for _ in side:
        pass


def _mix_ffn_kernel(x_ref, wo_ref, g_ref, wg_ref, wu_ref, wd_ref, gf_ref,
                    u_ref, v_ref, gv_ref, ws_ref, bst_ref, go_ref,
                    xbc_ref, z_ref, dt_ref, cw_ref, cb_ref, dtb_ref, alog_ref, drep_ref, ng_ref,
                    tril3_ref, xs_ref, yas_ref, ybs_ref,
                    o_ref, nconv_ref, nssm_ref, os_ref,
                    m_ref, ya_ref, yb_ref, tail_ref, ht_ref, wm_ref, ms_ref, *, chunks_per_seq):
    s = pl.program_id(0)
    f = pl.program_id(1)
    has_ffn = s >= 1
    is_last = s == pl.num_programs(0) - 1
    has_mix = jnp.logical_not(is_last) & (f < CHUNKS_PER_TILE)

    @pl.when((s == 0) & (f == 0))
    def _():
        _mask_chunk_weights(ws_ref, wm_ref)

    def project(x, ya, yb, m_out, o_out):
        h = x + _dot(jnp.concatenate([ya, yb], axis=1), wo_ref[...])
        m_out[...] = _rms(h, g_ref[...]).astype(BF16)
        o_out[...] = h

    @pl.when(has_ffn & (f == 0))
    def _():
        project(x_ref[...], ya_ref[...], yb_ref[...], m_ref, o_ref)

    @pl.when(is_last & (f == 0))
    def _():
        project(xs_ref[...], yas_ref[...], ybs_ref[...], ms_ref, os_ref)

    def ffn_step(with_sample=False):
        m = jnp.concatenate([m_ref[...], ms_ref[...]], axis=0) if with_sample else m_ref[...]
        acts = []
        for c in range(FFN_TF // FFN_UP_PIECE):
            cols = slice(c * FFN_UP_PIECE, (c + 1) * FFN_UP_PIECE)
            halves = []
            for w_ref in (wg_ref, wu_ref):
                halves.append(_dot(m, w_ref[:, cols]))
                yield
            acts.append((_silu(halves[0]) * halves[1]).astype(BF16))
        act = jnp.concatenate(acts, axis=1)
        for c in range(D_MODEL // FFN_DOWN_PIECE):
            cols = slice(c * FFN_DOWN_PIECE, (c + 1) * FFN_DOWN_PIECE)
            res = _dot(act, wd_ref[:, cols])
            o_ref[:, cols] += res[:MIX_TM]
            if with_sample:
                os_ref[:, cols] += res[MIX_TM:]
            yield

    chunk_rows = pl.ds(pl.multiple_of(f * SSD_CHUNK, SSD_CHUNK), SSD_CHUNK)

    def store_ya(ya):
        ya_ref[chunk_rows, :] = ya

    def store_yb(yb):
        yb_ref[chunk_rows, :] = yb

    def mix_step():
        first = (s * CHUNKS_PER_TILE + f) % chunks_per_seq == 0
        yield from _chunk_gate(u_ref, v_ref, gv_ref, wm_ref, bst_ref, go_ref, store_ya)
        yield
        yield from _ssd_chunk(first, xbc_ref, z_ref, dt_ref, cw_ref, cb_ref, dtb_ref, alog_ref, drep_ref,
                              ng_ref, tril3_ref, nconv_ref, nssm_ref, tail_ref, ht_ref, store_yb)

    @pl.when(has_ffn & has_mix)
    def _():
        _interleave(ffn_step(), mix_step(), MIX_PHASES_PER_FFN_PIECE, MIX_LEAD_PIECES)

    @pl.when(has_ffn & jnp.logical_not(has_mix) & jnp.logical_not(is_last))
    def _():
        _interleave(ffn_step(), iter(()), 0)

    @pl.when(is_last)
    def _():
        _interleave(ffn_step(with_sample=True), iter(()), 0)

    @pl.when(jnp.logical_not(has_ffn) & has_mix)
    def _():
        _interleave(iter(()), mix_step(), 0)

    @pl.when(has_ffn & (f == pl.num_programs(1) - 1))
    def _():
        o_ref[...] = _rms(o_ref[...], gf_ref[...])

    @pl.when(is_last & (f == pl.num_programs(1) - 1))
    def _():
        os_ref[...] = _rms(os_ref[...], gf_ref[...])


def _mix_ffn(x, proj, dt_raw, w_out_b, g, w_gate, w_up, w_down, g_final, gv, ws, bst, go,
             cw, cb, dtb, alog, drep, ng, tril3, xs, ya_s, yb_s, batch, seq):
    m = x.shape[0]
    n_s = xs.shape[0]
    n_tiles = m // MIX_TM
    n_chunks = m // SSD_CHUNK
    chunks_per_seq = seq // SSD_CHUNK
    assert MIX_TM % SSD_CHUNK == 0 and seq % MIX_TM == 0 and CHUNKS_PER_TILE <= D_FF // FFN_TF

    def tile(s, f):
        return jnp.maximum(s - 1, 0)

    def chunk(s, f):
        return jnp.where(s >= n_tiles, n_chunks - 1,
                         s * CHUNKS_PER_TILE + jnp.minimum(f, CHUNKS_PER_TILE - 1))

    def seq_of(s, f):
        return chunk(s, f) // chunks_per_seq

    const2 = lambda s, f: (0, 0)
    return pl.pallas_call(
        functools.partial(_mix_ffn_kernel, chunks_per_seq=chunks_per_seq),
        grid=(n_tiles + 1, D_FF // FFN_TF),
        in_specs=[
            pl.BlockSpec((MIX_TM, D_MODEL), lambda s, f: (tile(s, f), 0)),
            pl.BlockSpec((D_MODEL, D_MODEL), const2, pipeline_mode=pl.Buffered(1)),
            pl.BlockSpec((1, D_MODEL), const2),
            pl.BlockSpec((D_MODEL, FFN_TF), lambda s, f: (0, f)),
            pl.BlockSpec((D_MODEL, FFN_TF), lambda s, f: (0, f)),
            pl.BlockSpec((FFN_TF, D_MODEL), lambda s, f: (f, 0)),
            pl.BlockSpec((1, D_MODEL), const2),
            pl.BlockSpec((CHUNK_LEN, D_CHUNK), lambda s, f: (chunk(s, f), U_BLK)),
            pl.BlockSpec((CHUNK_LEN, D_CHUNK), lambda s, f: (chunk(s, f), V_BLK)),
            pl.BlockSpec((1, D_CHUNK), const2),
            pl.BlockSpec((CHUNK_HEADS, CHUNK_LEN, CHUNK_LEN), lambda s, f: (0, 0, 0)),
            pl.BlockSpec((CHUNK_LEN, CHUNK_HEADS), const2),
            pl.BlockSpec((1, D_CHUNK), const2),
            pl.BlockSpec((SSD_CHUNK, CONV_DIM), lambda s, f: (chunk(s, f), XBC_BLK)),
            pl.BlockSpec((SSD_CHUNK, D_SSM), lambda s, f: (chunk(s, f), Z_BLK)),
            pl.BlockSpec((SSD_CHUNK, HEAD_PAD), lambda s, f: (chunk(s, f), 0)),
            pl.BlockSpec((CONV_W, CONV_DIM), const2),
            pl.BlockSpec((1, CONV_DIM), const2),
            pl.BlockSpec((1, HEAD_PAD), const2),
            pl.BlockSpec((1, HEAD_PAD), const2),
            pl.BlockSpec((1, D_SSM), const2),
            pl.BlockSpec((1, D_SSM), const2),
            pl.BlockSpec((SSD_CHUNK, N_SPLIT * SSD_CHUNK), const2),
            pl.BlockSpec((n_s, D_MODEL), const2, pipeline_mode=pl.Buffered(1)),
            pl.BlockSpec((n_s, D_CHUNK), const2, pipeline_mode=pl.Buffered(1)),
            pl.BlockSpec((n_s, D_SSM), const2, pipeline_mode=pl.Buffered(1)),
        ],
        out_specs=[
            pl.BlockSpec((MIX_TM, D_MODEL), lambda s, f: (tile(s, f), 0)),
            pl.BlockSpec((None, CONV_W - 1, CONV_DIM), lambda s, f: (seq_of(s, f), 0, 0)),
            pl.BlockSpec((None, D_SSM, D_STATE), lambda s, f: (seq_of(s, f), 0, 0)),
            pl.BlockSpec((n_s, D_MODEL), const2),
        ],
        out_shape=[
            jax.ShapeDtypeStruct((m, D_MODEL), F32),
            jax.ShapeDtypeStruct((batch, CONV_W - 1, CONV_DIM), F32),
            jax.ShapeDtypeStruct((batch, D_SSM, D_STATE), F32),
            jax.ShapeDtypeStruct((n_s, D_MODEL), F32),
        ],
        scratch_shapes=[
            pltpu.VMEM((MIX_TM, D_MODEL), BF16),
            pltpu.VMEM((MIX_TM, D_CHUNK), BF16),
            pltpu.VMEM((MIX_TM, D_SSM), BF16),
            pltpu.VMEM((CONV_PAD, CONV_DIM), F32),
            pltpu.VMEM((D_STATE, D_SSM), F32),
            pltpu.VMEM((CHUNK_LEN, CHUNK_HEADS * CHUNK_LEN), BF16),
            pltpu.VMEM((n_s, D_MODEL), BF16),
        ],
        compiler_params=_params("arbitrary", "arbitrary"),
        name="mix_ffn",
    )(x, w_out_b, g, w_gate, w_up, w_down, g_final, proj, proj, gv, ws, bst, go, proj, proj, dt_raw,
      cw, cb, dtb, alog, drep, ng, tril3, xs, ya_s, yb_s)


def _row(v, width=None):
    v = v.astype(F32).reshape(1, -1)
    if width is not None and v.shape[1] < width:
        v = jnp.pad(v, ((0, 0), (0, width - v.shape[1])))
    return v


def kernel(x_prompt, x_sample, state_conv, state_ssm, norm_mix_g, w_in, chunk_v_norm_g, chunk_w_s,
           chunk_b_s, chunk_out_norm_g, ssd_conv_w, ssd_conv_b, ssd_dt_bias, ssd_a_log, ssd_d,
           ssd_norm_g, w_out, norm_ffn_g, w_gate, w_up, w_down, norm_final_g):
    depth = w_in.shape[0]
    batch, seq, _ = x_prompt.shape
    n_s = x_sample.shape[0]
    assert x_sample.shape[1] == 1 and seq % SSD_CHUNK == 0 and depth == 1

    rep = (lax.broadcasted_iota(jnp.int32, (HEAD_PAD, D_SSM), 0)
           == lax.broadcasted_iota(jnp.int32, (HEAD_PAD, D_SSM), 1) // SSM_HEAD_DIM).astype(F32)
    rep8 = (lax.broadcasted_iota(jnp.int32, (HEAD_PAD, HEAD_PAD), 0)
            == lax.broadcasted_iota(jnp.int32, (HEAD_PAD, HEAD_PAD), 1) // SUBLANES).astype(F32)
    tril3 = jnp.tile(jnp.tril(jnp.ones((SSD_CHUNK, SSD_CHUNK), F32)), (1, N_SPLIT)).astype(BF16)

    hp = x_prompt.reshape(batch * seq, D_MODEL)
    hs = x_sample.reshape(n_s, D_MODEL)
    l = 0
    w_in_t = jnp.swapaxes(w_in, 1, 2)
    w_dt_t = jnp.pad(w_in_t[l, D_MAIN:, :], ((0, HEAD_PAD - SSM_HEADS), (0, 0))).astype(BF16)
    g_mix = _row(norm_mix_g[l])
    gv = _row(chunk_v_norm_g[l])
    go = _row(chunk_out_norm_g[l])
    cw = ssd_conv_w[l].astype(F32)
    cb = _row(ssd_conv_b[l])
    dtb = _row(ssd_dt_bias[l], HEAD_PAD)
    alog = _row(ssd_a_log[l], HEAD_PAD)
    drep = _row(jnp.repeat(ssd_d[l], SSM_HEAD_DIM))
    ng = _row(ssd_norm_g[l])
    g_ffn = _row(norm_ffn_g[l])
    g_final = _row(norm_final_g)

    proj_p, dt_p, proj_s, dt_s, w_gate_b, w_up_b, w_down_b, w_out_b = _in_proj(
        hp, hs, g_mix, w_in_t, l, w_dt_t, tm=1024, cast=(w_gate, w_up, w_down, w_out))

    w0 = _row(jnp.repeat(chunk_w_s[l, :, 0, 0], CHUNK_HEAD_DIM))
    b0 = _row(jnp.repeat(chunk_b_s[l, :, 0], CHUNK_HEAD_DIM))
    ya_s, v_s, conv_s, split, b_s, c_s, dx_s = _sample_rows(
        proj_s, dt_s, state_conv[l].reshape(n_s, (CONV_W - 1) * CONV_DIM), gv, w0, b0, go,
        cw, cb, dtb, alog, drep, rep, rep8)
    ssm_s, yb_s = _sample_state(split, b_s, c_s, state_ssm[l].reshape(n_s, D_SSM, D_STATE),
                                dx_s, proj_s, ng)

    y_p, conv_p, ssm_p, y_s = _mix_ffn(
        hp, proj_p, dt_p, w_out_b, g_ffn, w_gate_b, w_up_b, w_down_b, g_final,
        gv, chunk_w_s[l].astype(F32), chunk_b_s[l].astype(F32).T, go,
        cw, cb, dtb, alog, drep, ng, tril3, hs, ya_s, yb_s, batch, seq)

    return (
        y_p.reshape(batch, seq, D_MODEL),
        y_s.reshape(n_s, 1, D_MODEL),
        conv_p[None],
        ssm_p.reshape(1, batch, SSM_HEADS, SSM_HEAD_DIM, D_STATE),
        conv_s.reshape(1, n_s, CONV_W - 1, CONV_DIM),
        ssm_s.reshape(1, n_s, SSM_HEADS, SSM_HEAD_DIM, D_STATE),
        v_s.reshape(1, n_s, 1, D_CHUNK),
    )
```

```python
import functools

import jax
import jax.numpy as jnp
from jax import lax
from jax.experimental import pallas as pl
from jax.experimental.pallas import tpu as pltpu

F32 = jnp.float32
BF16 = jnp.bfloat16
HIGHEST = lax.Precision.HIGHEST

D_MODEL = 2048
D_CHUNK = 1024
CHUNK_HEADS = 8
CHUNK_HEAD_DIM = 128
CHUNK_LEN = 128
D_SSM = 1024
SSM_HEAD_DIM = 64
SSM_HEADS = 16
SSM_GROUPS = 2
HEADS_PER_GROUP = SSM_HEADS // SSM_GROUPS
GROUP_WIDTH = D_SSM // SSM_GROUPS
D_STATE = 128
CONV_W = 4
CONV_DIM = D_SSM + 2 * SSM_GROUPS * D_STATE
SSD_CHUNK = 128
D_MAIN = 2 * D_CHUNK + D_SSM + CONV_DIM
D_FF = 5632
EPS = 1e-6

LANES = 128
SUBLANES = 8
BF16_SUBLANES = 16
HEAD_PAD = LANES
N_SPLIT = 3
VMEM_LIMIT = 56 * 1024 * 1024

U_BLK, V_BLK, Z_BLK = 0, 1, 2
XBC_BLK = 2


def _rms(x, g):
    return x * lax.rsqrt(jnp.mean(x * x, axis=-1, keepdims=True) + EPS) * g


def _silu(x):
    return x * jax.nn.sigmoid(x)


def _softplus(x):
    return jnp.maximum(x, 0.0) + jnp.log1p(jnp.exp(-jnp.abs(x)))


def _dot(a, b):
    return jnp.dot(a, b, preferred_element_type=F32)


def _dot_nt(a, b):
    return lax.dot_general(a, b, (((1,), (1,)), ((), ())), preferred_element_type=F32)


def _dot_exact(a, b):
    return jnp.dot(a, b, precision=HIGHEST, preferred_element_type=F32)


def _params(*semantics):
    return pltpu.CompilerParams(dimension_semantics=semantics, vmem_limit_bytes=VMEM_LIMIT)


IN_TN = 512


def _in_proj_kernel(x_hbm, xs_hbm, g_ref, w_ref, wdt_ref, *refs, tm, n_cast):
    cast_in = refs[:n_cast]
    o_ref, dt_ref, os_ref, dts_ref = refs[n_cast:n_cast + 4]
    cast_out = refs[n_cast + 4:2 * n_cast + 4]
    xn_ref, wres_ref, xbuf_ref, sem = refs[2 * n_cast + 4:]
    i = pl.program_id(0)
    j = pl.program_id(1)
    last = pl.num_programs(0) - 1
    n_s = xs_hbm.shape[0]

    for src, dst in zip(cast_in, cast_out, strict=True):
        dst[...] = src[...].astype(BF16)

    def x_copy(tile):
        return pltpu.make_async_copy(x_hbm.at[pl.ds(tile * tm, tm), :], xbuf_ref, sem)

    def xs_copy():
        return pltpu.make_async_copy(xs_hbm, xbuf_ref.at[pl.ds(0, n_s), :], sem)

    @pl.when((i == 0) & (j == 0))
    def _():
        xs_copy().start()
        xs_copy().wait()
        nb = _rms(xbuf_ref[0:n_s, :], g_ref[...]).astype(BF16)
        xn_ref[tm:tm + n_s, :] = nb
        dts_ref[...] = _dot_nt(nb, wdt_ref[...])
        x_copy(0).start()

    @pl.when(i == 0)
    def _():
        wres_ref[j] = w_ref[...].T.astype(BF16)

    @pl.when(j == 0)
    def _():
        x_copy(i).wait()
        nb = _rms(xbuf_ref[...], g_ref[...]).astype(BF16)
        xn_ref[0:tm, :] = nb
        dt_ref[...] = _dot_nt(nb, wdt_ref[...])

    @pl.when((j == 1) & (i < last))
    def _():
        x_copy(i + 1).start()

    @pl.when(i < last)
    def _():
        o_ref[...] = _dot(xn_ref[0:tm, :], wres_ref[j])

    @pl.when(i == last)
    def _():
        res = _dot(xn_ref[...], wres_ref[j])
        o_ref[...] = res[0:tm, :]
        os_ref[...] = res[tm:, :]


def _in_proj(x, xs, g, w_in_t, layer, w_dt_t, tm, cast=()):
    m = x.shape[0]
    n_s = xs.shape[0]
    n_i = m // tm
    n_j = D_MAIN // IN_TN
    assert n_j >= 2 and n_s <= tm and n_s % BF16_SUBLANES == 0
    cast_in, cast_out, cast_shapes = [], [], []
    n_steps = n_i * n_j

    def col_sample(i, j):
        return jnp.where(i == n_i - 1, j, 0)

    for w in cast:
        _, w_rows, w_cols = w.shape
        slab_rows = next(r for r in range(BF16_SUBLANES, w_rows + 1, BF16_SUBLANES)
                         if w_rows % r == 0 and w_rows // r <= n_steps)
        n_slabs = w_rows // slab_rows
        hold = n_steps // n_slabs
        slab = lambda i, j, n_slabs=n_slabs, hold=hold: jnp.minimum((i * n_j + j) // hold, n_slabs - 1)
        cast_in.append(pl.BlockSpec((None, slab_rows, w_cols), lambda i, j, slab=slab: (layer, slab(i, j), 0)))
        cast_out.append(pl.BlockSpec((slab_rows, w_cols), lambda i, j, slab=slab: (slab(i, j), 0)))
        cast_shapes.append(jax.ShapeDtypeStruct((w_rows, w_cols), BF16))
    return pl.pallas_call(
        functools.partial(_in_proj_kernel, tm=tm, n_cast=len(cast)),
        grid=(n_i, n_j),
        in_specs=[
            pl.BlockSpec(memory_space=pl.ANY),
            pl.BlockSpec(memory_space=pl.ANY),
            pl.BlockSpec((1, D_MODEL), lambda i, j: (0, 0)),
            pl.BlockSpec((None, IN_TN, D_MODEL), lambda i, j: (layer, jnp.where(i == 0, j, n_j - 1), 0)),
            pl.BlockSpec((HEAD_PAD, D_MODEL), lambda i, j: (0, 0)),
            *cast_in,
        ],
        out_specs=[
            pl.BlockSpec((tm, IN_TN), lambda i, j: (i, j)),
            pl.BlockSpec((tm, HEAD_PAD), lambda i, j: (i, 0)),
            pl.BlockSpec((n_s, IN_TN), lambda i, j: (0, col_sample(i, j))),
            pl.BlockSpec((n_s, HEAD_PAD), lambda i, j: (0, 0)),
            *cast_out,
        ],
        out_shape=[
            jax.ShapeDtypeStruct((m, D_MAIN), F32),
            jax.ShapeDtypeStruct((m, HEAD_PAD), F32),
            jax.ShapeDtypeStruct((n_s, D_MAIN), F32),
            jax.ShapeDtypeStruct((n_s, HEAD_PAD), F32),
            *cast_shapes,
        ],
        scratch_shapes=[
            pltpu.VMEM((tm + n_s, D_MODEL), BF16),
            pltpu.VMEM((n_j, D_MODEL, IN_TN), BF16),
            pltpu.VMEM((tm, D_MODEL), F32),
            pltpu.SemaphoreType.DMA(()),
        ],
        compiler_params=_params("arbitrary", "arbitrary"),
        name="in_proj",
    )(x, xs, g, w_in_t, w_dt_t, *cast)


def _mask_chunk_weights(ws_ref, wm_ref):
    row = lax.broadcasted_iota(jnp.int32, (CHUNK_LEN, CHUNK_LEN), 0)
    col = lax.broadcasted_iota(jnp.int32, (CHUNK_LEN, CHUNK_LEN), 1)
    for h in range(CHUNK_HEADS):
        wm_ref[:, h * CHUNK_LEN:(h + 1) * CHUNK_LEN] = jnp.where(row >= col, ws_ref[h], 0.0).astype(BF16)


def _chunk_gate(u_ref, v_ref, gv_ref, wm_ref, bst_ref, go_ref, store_y):
    vb = _rms(jax.nn.gelu(v_ref[...]), gv_ref[...]).astype(BF16)
    yield
    u = jax.nn.gelu(u_ref[...])
    yield
    zeros = jnp.zeros((CHUNK_LEN, CHUNK_HEAD_DIM), BF16)
    parts = []
    for h0 in range(0, CHUNK_HEADS, 2):
        w = wm_ref[:, h0 * CHUNK_LEN:(h0 + 2) * CHUNK_LEN]
        va = vb[:, h0 * CHUNK_HEAD_DIM:(h0 + 1) * CHUNK_HEAD_DIM]
        vb2 = vb[:, (h0 + 1) * CHUNK_HEAD_DIM:(h0 + 2) * CHUNK_HEAD_DIM]
        rhs = jnp.concatenate([jnp.concatenate([va, zeros], axis=1),
                               jnp.concatenate([zeros, vb2], axis=1)], axis=0)
        mixed = _dot(w, rhs)
        for k, h in enumerate((h0, h0 + 1)):
            sl = slice(h * CHUNK_HEAD_DIM, (h + 1) * CHUNK_HEAD_DIM)
            mh = mixed[:, k * CHUNK_HEAD_DIM:(k + 1) * CHUNK_HEAD_DIM] + bst_ref[:, h:h + 1]
            parts.append(u[:, sl] * mh)
        yield
    y = jnp.concatenate(parts, axis=1)
    store_y(_rms(y, go_ref[...]).astype(BF16))


CONV_PAD = 8


def _group_norm_out(y, z, g):
    y = y * _silu(z)
    parts = []
    for grp in range(SSM_GROUPS):
        yg = y[:, grp * GROUP_WIDTH:(grp + 1) * GROUP_WIDTH]
        parts.append(yg * lax.rsqrt(jnp.mean(yg * yg, axis=-1, keepdims=True) + EPS))
    return (jnp.concatenate(parts, axis=1) * g).astype(BF16)


def _split3(x):
    p1 = x.astype(BF16)
    r1 = x - p1.astype(F32)
    p2 = r1.astype(BF16)
    p3 = (r1 - p2.astype(F32)).astype(BF16)
    return p1, p2, p3


def _select_rows(sel3, x):
    return _dot(sel3, jnp.concatenate(_split3(x), axis=0))


def _ssd_chunk(first, xbc_ref, z_ref, dt_ref, cw_ref, cb_ref, dtb_ref, alog_ref, drep_ref, ng_ref,
               tril3_ref, nconv_ref, nssm_ref, tail_ref, ht_ref, store_y):
    q = SSD_CHUNK

    xbc = xbc_ref[...]
    tail = jnp.where(first, 0.0, tail_ref[...])
    ht_prev = jnp.where(first, 0.0, ht_ref[...])
    sub = lax.broadcasted_iota(jnp.int32, (CONV_PAD, CONV_DIM), 0)
    conv = cb_ref[...] + xbc * cw_ref[CONV_W - 1:CONV_W, :]
    for k in range(1, CONV_W):
        shifted = pltpu.roll(xbc, k, 0)
        head = jnp.where(sub < k, pltpu.roll(tail, k, 0), shifted[0:CONV_PAD, :])
        shifted = jnp.concatenate([head, shifted[CONV_PAD:, :]], axis=0)
        conv = conv + shifted * cw_ref[CONV_W - 1 - k:CONV_W - k, :]
    tail_ref[...] = xbc[q - CONV_PAD:, :]
    nconv_ref[...] = xbc[q - (CONV_W - 1):, :]
    yield

    act = _silu(conv)
    xs = act[:, :D_SSM]
    bmat = act[:, D_SSM:D_SSM + SSM_GROUPS * D_STATE]
    cmat = act[:, D_SSM + SSM_GROUPS * D_STATE:]
    yield

    dt = _softplus(dt_ref[...] + dtb_ref[...])
    a = -jnp.exp(alog_ref[...])
    acum = _select_rows(tril3_ref[...], dt * a)
    acum_t = acum.T
    low_lanes = lax.broadcasted_iota(jnp.int32, (q, 2 * SSM_HEAD_DIM), 1) < SSM_HEAD_DIM

    def expand(cols):
        pairs = []
        for h0 in range(0, SSM_HEADS, 2):
            a = jnp.broadcast_to(cols[:, h0:h0 + 1], (q, 2 * SSM_HEAD_DIM))
            b = jnp.broadcast_to(cols[:, h0 + 1:h0 + 2], (q, 2 * SSM_HEAD_DIM))
            pairs.append(jnp.where(low_lanes, a, b))
        return jnp.concatenate(pairs, axis=1)

    dt_rep = expand(dt)
    yield
    acum_rep = expand(acum)
    yield
    xdt = xs * dt_rep
    exp_a = jnp.exp(acum_rep)
    xw = xdt * jnp.exp(acum_rep[q - 1:q, :] - acum_rep)
    yield

    row = lax.broadcasted_iota(jnp.int32, (q, q), 0)
    col = lax.broadcasted_iota(jnp.int32, (q, q), 1)
    causal = row >= col
    low_half = col < SSM_HEAD_DIM
    ydiag, yoff, states = [], [], []
    for grp in range(SSM_GROUPS):
        bg_f = bmat[:, grp * D_STATE:(grp + 1) * D_STATE]
        bg = bg_f.astype(BF16)
        cg = cmat[:, grp * D_STATE:(grp + 1) * D_STATE].astype(BF16)
        cb = _dot_nt(cg, bg)
        ch = slice(grp * GROUP_WIDTH, (grp + 1) * GROUP_WIDTH)
        yoff.append(_dot(cg, ht_prev[:, ch].astype(BF16)))
        yield
        for quad in range(HEADS_PER_GROUP // 4):
            h0 = grp * HEADS_PER_GROUP + 4 * quad
            masks, blocks = [], []
            for k, h in enumerate(range(h0, h0 + 4)):
                diff = acum[:, h:h + 1] - acum_t[h:h + 1, :]
                masks.append((cb * jnp.exp(jnp.where(causal, diff, -jnp.inf))).astype(BF16))
                xp = xdt[:, (h - k % 2) * SSM_HEAD_DIM:(h - k % 2 + 2) * SSM_HEAD_DIM]
                own = jnp.where(low_half == (k % 2 == 0), xp, 0.0).astype(BF16)
                zero = jnp.zeros_like(own)
                blocks.append(jnp.concatenate([own, zero] if k < 2 else [zero, own], axis=1))
                if k % 2 == 1:
                    yield
            ydiag.append(_dot(jnp.concatenate(masks, axis=1), jnp.concatenate(blocks, axis=0)))
            yield
        states.append(_dot(bg_f.T.astype(BF16), xw[:, ch].astype(BF16)))
        yield
    y = drep_ref[...] * xs + jnp.concatenate(ydiag, axis=1) + jnp.concatenate(yoff, axis=1) * exp_a

    ht_new = ht_prev * exp_a[q - 1:q, :] + jnp.concatenate(states, axis=1)
    ht_ref[...] = ht_new
    nssm_ref[...] = ht_new.T
    yield
    store_y(_group_norm_out(y, z_ref[...], ng_ref[...]))


def _sample_rows_kernel(u_ref, v_ref, xbc_ref, dt_ref, sc_ref, gv_ref, w0_ref, b0_ref, go_ref,
                        cw_ref, cb_ref, dtb_ref, alog_ref, drep_ref, rep_ref, rep8_ref,
                        ya_ref, vrows_ref, nconv_ref, split_ref, b_ref, c_ref, dx_ref):
    vr = _rms(jax.nn.gelu(v_ref[...]), gv_ref[...])
    vrows_ref[...] = vr
    mixed = vr * w0_ref[...] + b0_ref[...]
    ya_ref[...] = _rms(jax.nn.gelu(u_ref[...]) * mixed, go_ref[...]).astype(BF16)

    xbc = xbc_ref[...]
    conv = cb_ref[...]
    for k in range(CONV_W - 1):
        conv = conv + sc_ref[:, k * CONV_DIM:(k + 1) * CONV_DIM] * cw_ref[k:k + 1, :]
    conv = conv + xbc * cw_ref[CONV_W - 1:CONV_W, :]
    nconv_ref[:, 0:(CONV_W - 2) * CONV_DIM] = sc_ref[:, CONV_DIM:(CONV_W - 1) * CONV_DIM]
    nconv_ref[:, (CONV_W - 2) * CONV_DIM:] = xbc

    act = _silu(conv)
    xs = act[:, :D_SSM]
    b_ref[...] = act[:, D_SSM:D_SSM + SSM_GROUPS * D_STATE]
    c_ref[...] = act[:, D_SSM + SSM_GROUPS * D_STATE:]
    dx_ref[...] = drep_ref[...] * xs

    dt = _softplus(dt_ref[...] + dtb_ref[...])
    decay = jnp.exp(dt * (-jnp.exp(alog_ref[...])))
    xdt = xs * _dot_exact(dt, rep_ref[...])
    cols = jnp.concatenate([xdt, _dot_exact(decay, rep8_ref[...])], axis=1).T
    for p in range(N_SPLIT):
        piece = cols.astype(BF16)
        split_ref[:, p * LANES:(p + 1) * LANES] = piece
        cols = cols - piece.astype(F32)


SPLIT_ROWS = D_SSM + HEAD_PAD


def _sample_rows(proj, dt_raw, sconv, gv, w0, b0, go, cw, cb, dtb, alog, drep, rep, rep8):
    n = proj.shape[0]
    full = lambda shape: pl.BlockSpec(shape, lambda i: (0,) * len(shape))
    return pl.pallas_call(
        _sample_rows_kernel,
        grid=(1,),
        in_specs=[
            pl.BlockSpec((n, D_CHUNK), lambda i: (0, U_BLK)),
            pl.BlockSpec((n, D_CHUNK), lambda i: (0, V_BLK)),
            pl.BlockSpec((n, CONV_DIM), lambda i: (0, XBC_BLK)),
            full((n, HEAD_PAD)),
            full((n, (CONV_W - 1) * CONV_DIM)),
            full((1, D_CHUNK)), full((1, D_CHUNK)), full((1, D_CHUNK)), full((1, D_CHUNK)),
            full((CONV_W, CONV_DIM)), full((1, CONV_DIM)),
            full((1, HEAD_PAD)), full((1, HEAD_PAD)), full((1, D_SSM)),
            full((HEAD_PAD, D_SSM)), full((HEAD_PAD, HEAD_PAD)),
        ],
        out_specs=[
            full((n, D_CHUNK)), full((n, D_CHUNK)), full((n, (CONV_W - 1) * CONV_DIM)),
            full((SPLIT_ROWS, N_SPLIT * LANES)),
            full((n, SSM_GROUPS * D_STATE)), full((n, SSM_GROUPS * D_STATE)), full((n, D_SSM)),
        ],
        out_shape=[
            jax.ShapeDtypeStruct((n, D_CHUNK), BF16),
            jax.ShapeDtypeStruct((n, D_CHUNK), F32),
            jax.ShapeDtypeStruct((n, (CONV_W - 1) * CONV_DIM), F32),
            jax.ShapeDtypeStruct((SPLIT_ROWS, N_SPLIT * LANES), BF16),
            jax.ShapeDtypeStruct((n, SSM_GROUPS * D_STATE), F32),
            jax.ShapeDtypeStruct((n, SSM_GROUPS * D_STATE), F32),
            jax.ShapeDtypeStruct((n, D_SSM), F32),
        ],
        compiler_params=_params("arbitrary"),
        name="sample_rows",
    )(proj, proj, proj, dt_raw, sconv, gv, w0, b0, go, cw, cb, dtb, alog, drep, rep, rep8)


SAMPLE_TB = 16


def _sample_state_kernel(split_ref, b_ref, c_ref, h0_ref, dx_ref, z_ref, ng_ref,
                         hn_ref, y_ref, yt_ref):
    i = pl.program_id(0)
    n_tok = b_ref.shape[0]

    @pl.when(i == 0)
    def _():
        yt_ref[...] = jnp.zeros_like(yt_ref)

    tok = lax.broadcasted_iota(jnp.int32, (N_SPLIT * LANES, D_STATE), 0) % LANES
    lane = lax.broadcasted_iota(jnp.int32, (D_SSM, n_tok), 1)

    def per_group(row, grp):
        return jnp.broadcast_to(row[:, grp * D_STATE:(grp + 1) * D_STATE], (GROUP_WIDTH, D_STATE))

    for k in range(SAMPLE_TB):
        t = i * SAMPLE_TB + k
        onehot = (tok == t).astype(BF16)
        bcast = _dot(split_ref[...], onehot)
        decay = jnp.broadcast_to(
            bcast[D_SSM:, :].reshape(SSM_HEADS, 1, SUBLANES, D_STATE),
            (SSM_HEADS, SSM_HEAD_DIM // SUBLANES, SUBLANES, D_STATE)).reshape(D_SSM, D_STATE)
        brow = b_ref[pl.ds(t, 1), :]
        crow = c_ref[pl.ds(t, 1), :]
        bfull = jnp.concatenate([per_group(brow, g) for g in range(SSM_GROUPS)], axis=0)
        cfull = jnp.concatenate([per_group(crow, g) for g in range(SSM_GROUPS)], axis=0)
        h_new = h0_ref[k] * decay + bcast[:D_SSM, :] * bfull
        hn_ref[k] = h_new
        ysum = jnp.sum(h_new * cfull, axis=-1, keepdims=True)
        yt_ref[...] = jnp.where(lane == t, ysum, yt_ref[...])

    @pl.when(i == pl.num_programs(0) - 1)
    def _():
        y = yt_ref[...].T + dx_ref[...]
        y_ref[...] = _group_norm_out(y, z_ref[...], ng_ref[...])


def _sample_state(split, bmat, cmat, h0, dx, proj, ng):
    n = h0.shape[0]
    const2 = lambda i: (0, 0)
    return pl.pallas_call(
        _sample_state_kernel,
        grid=(n // SAMPLE_TB,),
        in_specs=[
            pl.BlockSpec((SPLIT_ROWS, N_SPLIT * LANES), const2),
            pl.BlockSpec((n, SSM_GROUPS * D_STATE), const2),
            pl.BlockSpec((n, SSM_GROUPS * D_STATE), const2),
            pl.BlockSpec((SAMPLE_TB, D_SSM, D_STATE), lambda i: (i, 0, 0)),
            pl.BlockSpec((n, D_SSM), const2),
            pl.BlockSpec((n, D_SSM), lambda i: (0, Z_BLK)),
            pl.BlockSpec((1, D_SSM), const2),
        ],
        out_specs=[
            pl.BlockSpec((SAMPLE_TB, D_SSM, D_STATE), lambda i: (i, 0, 0)),
            pl.BlockSpec((n, D_SSM), const2),
        ],
        out_shape=[
            jax.ShapeDtypeStruct((n, D_SSM, D_STATE), F32),
            jax.ShapeDtypeStruct((n, D_SSM), BF16),
        ],
        scratch_shapes=[pltpu.VMEM((D_SSM, n), F32)],
        compiler_params=_params("arbitrary"),
        name="sample_state",
    )(split, bmat, cmat, h0, dx, proj, ng)


FFN_TF = 512
MIX_TM = 512
CHUNKS_PER_TILE = MIX_TM // SSD_CHUNK
FFN_UP_PIECE = 256
FFN_DOWN_PIECE = 512
MIX_PHASES_PER_FFN_PIECE = 5
MIX_LEAD_PIECES = 0


def _interleave(main, side, side_per_main, lead=0):
    for k, _ in enumerate(main):
        if k >= lead:
            for _ in range(side_per_main):
                next(side, None)
    for _ in side:
        pass


def _mix_ffn_kernel(x_ref, wo_ref, g_ref, wg_ref, wu_ref, wd_ref, gf_ref,
                    u_ref, v_ref, gv_ref, ws_ref, bst_ref, go_ref,
                    xbc_ref, z_ref, dt_ref, cw_ref, cb_ref, dtb_ref, alog_ref, drep_ref, ng_ref,
                    tril3_ref, xs_ref, yas_ref, ybs_ref,
                    o_ref, nconv_ref, nssm_ref, os_ref,
                    m_ref, ya_ref, yb_ref, tail_ref, ht_ref, wm_ref, ms_ref, *, chunks_per_seq):
    s = pl.program_id(0)
    f = pl.program_id(1)
    has_ffn = s >= 1
    is_last = s == pl.num_programs(0) - 1
    has_mix = jnp.logical_not(is_last) & (f < CHUNKS_PER_TILE)

    @pl.when((s == 0) & (f == 0))
    def _():
        _mask_chunk_weights(ws_ref, wm_ref)

    def project(x, ya, yb, m_out, o_out):
        h = x + _dot(jnp.concatenate([ya, yb], axis=1), wo_ref[...])
        m_out[...] = _rms(h, g_ref[...]).astype(BF16)
        o_out[...] = h

    @pl.when(has_ffn & (f == 0))
    def _():
        project(x_ref[...], ya_ref[...], yb_ref[...], m_ref, o_ref)

    @pl.when(is_last & (f == 0))
    def _():
        project(xs_ref[...], yas_ref[...], ybs_ref[...], ms_ref, os_ref)

    def ffn_step(with_sample=False):
        m = jnp.concatenate([m_ref[...], ms_ref[...]], axis=0) if with_sample else m_ref[...]
        acts = []
        for c in range(FFN_TF // FFN_UP_PIECE):
            cols = slice(c * FFN_UP_PIECE, (c + 1) * FFN_UP_PIECE)
            halves = []
            for w_ref in (wg_ref, wu_ref):
                halves.append(_dot(m, w_ref[:, cols]))
                yield
            acts.append((_silu(halves[0]) * halves[1]).astype(BF16))
        act = jnp.concatenate(acts, axis=1)
        for c in range(D_MODEL // FFN_DOWN_PIECE):
            cols = slice(c * FFN_DOWN_PIECE, (c + 1) * FFN_DOWN_PIECE)
            res = _dot(act, wd_ref[:, cols])
            o_ref[:, cols] += res[:MIX_TM]
            if with_sample:
                os_ref[:, cols] += res[MIX_TM:]
            yield

    chunk_rows = pl.ds(pl.multiple_of(f * SSD_CHUNK, SSD_CHUNK), SSD_CHUNK)

    def store_ya(ya):
        ya_ref[chunk_rows, :] = ya

    def store_yb(yb):
        yb_ref[chunk_rows, :] = yb

    def mix_step():
        first = (s * CHUNKS_PER_TILE + f) % chunks_per_seq == 0
        yield from _chunk_gate(u_ref, v_ref, gv_ref, wm_ref, bst_ref, go_ref, store_ya)
        yield
        yield from _ssd_chunk(first, xbc_ref, z_ref, dt_ref, cw_ref, cb_ref, dtb_ref, alog_ref, drep_ref,
                              ng_ref, tril3_ref, nconv_ref, nssm_ref, tail_ref, ht_ref, store_yb)

    @pl.when(has_ffn & has_mix)
    def _():
        _interleave(ffn_step(), mix_step(), MIX_PHASES_PER_FFN_PIECE, MIX_LEAD_PIECES)

    @pl.when(has_ffn & jnp.logical_not(has_mix) & jnp.logical_not(is_last))
    def _():
        _interleave(ffn_step(), iter(()), 0)

    @pl.when(is_last)
    def _():
        _interleave(ffn_step(with_sample=True), iter(()), 0)

    @pl.when(jnp.logical_not(has_ffn) & has_mix)
    def _():
        _interleave(iter(()), mix_step(), 0)

    @pl.when(has_ffn & (f == pl.num_programs(1) - 1))
    def _():
        o_ref[...] = _rms(o_ref[...], gf_ref[...])

    @pl.when(is_last & (f == pl.num_programs(1) - 1))
    def _():
        os_ref[...] = _rms(os_ref[...], gf_ref[...])


def _mix_ffn(x, proj, dt_raw, w_out_b, g, w_gate, w_up, w_down, g_final, gv, ws, bst, go,
             cw, cb, dtb, alog, drep, ng, tril3, xs, ya_s, yb_s, batch, seq):
    m = x.shape[0]
    n_s = xs.shape[0]
    n_tiles = m // MIX_TM
    n_chunks = m // SSD_CHUNK
    chunks_per_seq = seq // SSD_CHUNK
    assert MIX_TM % SSD_CHUNK == 0 and seq % MIX_TM == 0 and CHUNKS_PER_TILE <= D_FF // FFN_TF

    def tile(s, f):
        return jnp.maximum(s - 1, 0)

    def chunk(s, f):
        return jnp.where(s >= n_tiles, n_chunks - 1,
                         s * CHUNKS_PER_TILE + jnp.minimum(f, CHUNKS_PER_TILE - 1))

    def seq_of(s, f):
        return chunk(s, f) // chunks_per_seq

    def ffn_block(s, f):
        return jnp.where(s == 0, 0, f)

    const2 = lambda s, f: (0, 0)
    return pl.pallas_call(
        functools.partial(_mix_ffn_kernel, chunks_per_seq=chunks_per_seq),
        grid=(n_tiles + 1, D_FF // FFN_TF),
        in_specs=[
            pl.BlockSpec((MIX_TM, D_MODEL), lambda s, f: (tile(s, f), 0)),
            pl.BlockSpec((D_MODEL, D_MODEL), const2, pipeline_mode=pl.Buffered(1)),
            pl.BlockSpec((1, D_MODEL), const2),
            pl.BlockSpec((D_MODEL, FFN_TF), lambda s, f: (0, ffn_block(s, f))),
            pl.BlockSpec((D_MODEL, FFN_TF), lambda s, f: (0, ffn_block(s, f))),
            pl.BlockSpec((FFN_TF, D_MODEL), lambda s, f: (ffn_block(s, f), 0)),
            pl.BlockSpec((1, D_MODEL), const2),
            pl.BlockSpec((CHUNK_LEN, D_CHUNK), lambda s, f: (chunk(s, f), U_BLK)),
            pl.BlockSpec((CHUNK_LEN, D_CHUNK), lambda s, f: (chunk(s, f), V_BLK)),
            pl.BlockSpec((1, D_CHUNK), const2),
            pl.BlockSpec((CHUNK_HEADS, CHUNK_LEN, CHUNK_LEN), lambda s, f: (0, 0, 0)),
            pl.BlockSpec((CHUNK_LEN, CHUNK_HEADS), const2),
            pl.BlockSpec((1, D_CHUNK), const2),
            pl.BlockSpec((SSD_CHUNK, CONV_DIM), lambda s, f: (chunk(s, f), XBC_BLK)),
            pl.BlockSpec((SSD_CHUNK, D_SSM), lambda s, f: (chunk(s, f), Z_BLK)),
            pl.BlockSpec((SSD_CHUNK, HEAD_PAD), lambda s, f: (chunk(s, f), 0)),
            pl.BlockSpec((CONV_W, CONV_DIM), const2),
            pl.BlockSpec((1, CONV_DIM), const2),
            pl.BlockSpec((1, HEAD_PAD), const2),
            pl.BlockSpec((1, HEAD_PAD), const2),
            pl.BlockSpec((1, D_SSM), const2),
            pl.BlockSpec((1, D_SSM), const2),
            pl.BlockSpec((SSD_CHUNK, N_SPLIT * SSD_CHUNK), const2),
            pl.BlockSpec((n_s, D_MODEL), const2, pipeline_mode=pl.Buffered(1)),
            pl.BlockSpec((n_s, D_CHUNK), const2, pipeline_mode=pl.Buffered(1)),
            pl.BlockSpec((n_s, D_SSM), const2, pipeline_mode=pl.Buffered(1)),
        ],
        out_specs=[
            pl.BlockSpec((MIX_TM, D_MODEL), lambda s, f: (tile(s, f), 0)),
            pl.BlockSpec((None, CONV_W - 1, CONV_DIM), lambda s, f: (seq_of(s, f), 0, 0)),
            pl.BlockSpec((None, D_SSM, D_STATE), lambda s, f: (seq_of(s, f), 0, 0)),
            pl.BlockSpec((n_s, D_MODEL), const2),
        ],
        out_shape=[
            jax.ShapeDtypeStruct((m, D_MODEL), F32),
            jax.ShapeDtypeStruct((batch, CONV_W - 1, CONV_DIM), F32),
            jax.ShapeDtypeStruct((batch, D_SSM, D_STATE), F32),
            jax.ShapeDtypeStruct((n_s, D_MODEL), F32),
        ],
        scratch_shapes=[
            pltpu.VMEM((MIX_TM, D_MODEL), BF16),
            pltpu.VMEM((MIX_TM, D_CHUNK), BF16),
            pltpu.VMEM((MIX_TM, D_SSM), BF16),
            pltpu.VMEM((CONV_PAD, CONV_DIM), F32),
            pltpu.VMEM((D_STATE, D_SSM), F32),
            pltpu.VMEM((CHUNK_LEN, CHUNK_HEADS * CHUNK_LEN), BF16),
            pltpu.VMEM((n_s, D_MODEL), BF16),
        ],
        compiler_params=_params("arbitrary", "arbitrary"),
        name="mix_ffn",
    )(x, w_out_b, g, w_gate, w_up, w_down, g_final, proj, proj, gv, ws, bst, go, proj, proj, dt_raw,
      cw, cb, dtb, alog, drep, ng, tril3, xs, ya_s, yb_s)


def _row(v, width=None):
    v = v.astype(F32).reshape(1, -1)
    if width is not None and v.shape[1] < width:
        v = jnp.pad(v, ((0, 0), (0, width - v.shape[1])))
    return v


def kernel(x_prompt, x_sample, state_conv, state_ssm, norm_mix_g, w_in, chunk_v_norm_g, chunk_w_s,
           chunk_b_s, chunk_out_norm_g, ssd_conv_w, ssd_conv_b, ssd_dt_bias, ssd_a_log, ssd_d,
           ssd_norm_g, w_out, norm_ffn_g, w_gate, w_up, w_down, norm_final_g):
    depth = w_in.shape[0]
    batch, seq, _ = x_prompt.shape
    n_s = x_sample.shape[0]
    assert x_sample.shape[1] == 1 and seq % SSD_CHUNK == 0 and depth == 1

    rep = (lax.broadcasted_iota(jnp.int32, (HEAD_PAD, D_SSM), 0)
           == lax.broadcasted_iota(jnp.int32, (HEAD_PAD, D_SSM), 1) // SSM_HEAD_DIM).astype(F32)
    rep8 = (lax.broadcasted_iota(jnp.int32, (HEAD_PAD, HEAD_PAD), 0)
            == lax.broadcasted_iota(jnp.int32, (HEAD_PAD, HEAD_PAD), 1) // SUBLANES).astype(F32)
    tril3 = jnp.tile(jnp.tril(jnp.ones((SSD_CHUNK, SSD_CHUNK), F32)), (1, N_SPLIT)).astype(BF16)

    hp = x_prompt.reshape(batch * seq, D_MODEL)
    hs = x_sample.reshape(n_s, D_MODEL)
    l = 0
    w_in_t = jnp.swapaxes(w_in, 1, 2)
    w_dt_t = jnp.pad(w_in_t[l, D_MAIN:, :], ((0, HEAD_PAD - SSM_HEADS), (0, 0))).astype(BF16)
    g_mix = _row(norm_mix_g[l])
    gv = _row(chunk_v_norm_g[l])
    go = _row(chunk_out_norm_g[l])
    cw = ssd_conv_w[l].astype(F32)
    cb = _row(ssd_conv_b[l])
    dtb = _row(ssd_dt_bias[l], HEAD_PAD)
    alog = _row(ssd_a_log[l], HEAD_PAD)
    drep = _row(jnp.repeat(ssd_d[l], SSM_HEAD_DIM))
    ng = _row(ssd_norm_g[l])
    g_ffn = _row(norm_ffn_g[l])
    g_final = _row(norm_final_g)

    proj_p, dt_p, proj_s, dt_s, w_gate_b, w_up_b, w_down_b, w_out_b = _in_proj(
        hp, hs, g_mix, w_in_t, l, w_dt_t, tm=1024, cast=(w_gate, w_up, w_down, w_out))

    w0 = _row(jnp.repeat(chunk_w_s[l, :, 0, 0], CHUNK_HEAD_DIM))
    b0 = _row(jnp.repeat(chunk_b_s[l, :, 0], CHUNK_HEAD_DIM))
    ya_s, v_s, conv_s, split, b_s, c_s, dx_s = _sample_rows(
        proj_s, dt_s, state_conv[l].reshape(n_s, (CONV_W - 1) * CONV_DIM), gv, w0, b0, go,
        cw, cb, dtb, alog, drep, rep, rep8)
    ssm_s, yb_s = _sample_state(split, b_s, c_s, state_ssm[l].reshape(n_s, D_SSM, D_STATE),
                                dx_s, proj_s, ng)

    y_p, conv_p, ssm_p, y_s = _mix_ffn(
        hp, proj_p, dt_p, w_out_b, g_ffn, w_gate_b, w_up_b, w_down_b, g_final,
        gv, chunk_w_s[l].astype(F32), chunk_b_s[l].astype(F32).T, go,
        cw, cb, dtb, alog, drep, ng, tril3, hs, ya_s, yb_s, batch, seq)

    return (
        y_p.reshape(batch, seq, D_MODEL),
        y_s.reshape(n_s, 1, D_MODEL),
        conv_p[None],
        ssm_p.reshape(1, batch, SSM_HEADS, SSM_HEAD_DIM, D_STATE),
        conv_s.reshape(1, n_s, CONV_W - 1, CONV_DIM),
        ssm_s.reshape(1, n_s, SSM_HEADS, SSM_HEAD_DIM, D_STATE),
        v_s.reshape(1, n_s, 1, D_CHUNK),
    )
```

```python
import functools

import jax
import jax.numpy as jnp
from jax import lax
from jax.experimental import pallas as pl
from jax.experimental.pallas import tpu as pltpu

F32 = jnp.float32
BF16 = jnp.bfloat16
HIGHEST = lax.Precision.HIGHEST

D_MODEL = 2048
D_CHUNK = 1024
CHUNK_HEADS = 8
CHUNK_HEAD_DIM = 128
CHUNK_LEN = 128
D_SSM = 1024
SSM_HEAD_DIM = 64
SSM_HEADS = 16
SSM_GROUPS = 2
HEADS_PER_GROUP = SSM_HEADS // SSM_GROUPS
GROUP_WIDTH = D_SSM // SSM_GROUPS
D_STATE = 128
CONV_W = 4
CONV_DIM = D_SSM + 2 * SSM_GROUPS * D_STATE
SSD_CHUNK = 128
D_MAIN = 2 * D_CHUNK + D_SSM + CONV_DIM
D_FF = 5632
EPS = 1e-6

LANES = 128
SUBLANES = 8
BF16_SUBLANES = 16
HEAD_PAD = LANES
N_SPLIT = 3
VMEM_LIMIT = 56 * 1024 * 1024

U_BLK, V_BLK, Z_BLK = 0, 1, 2
XBC_BLK = 2


def _rms(x, g):
    return x * lax.rsqrt(jnp.mean(x * x, axis=-1, keepdims=True) + EPS) * g


def _silu(x):
    return x * jax.nn.sigmoid(x)


def _softplus(x):
    return jnp.maximum(x, 0.0) + jnp.log1p(jnp.exp(-jnp.abs(x)))


def _dot(a, b):
    return jnp.dot(a, b, preferred_element_type=F32)


def _dot_nt(a, b):
    return lax.dot_general(a, b, (((1,), (1,)), ((), ())), preferred_element_type=F32)


def _dot_exact(a, b):
    return jnp.dot(a, b, precision=HIGHEST, preferred_element_type=F32)


def _params(*semantics):
    return pltpu.CompilerParams(dimension_semantics=semantics, vmem_limit_bytes=VMEM_LIMIT)


PACK_ROWS = 16
PACK_LAYOUT = {
    "g_mix": (0, 1, 0, D_MODEL),
    "g_ffn": (1, 1, 0, D_MODEL),
    "g_final": (2, 1, 0, D_MODEL),
    "gv": (3, 1, 0, D_CHUNK),
    "go": (3, 1, D_CHUNK, D_CHUNK),
    "ng": (4, 1, 0, D_SSM),
    "drep": (4, 1, D_SSM, D_SSM),
    "cb": (5, 1, 0, CONV_DIM),
    "cw": (6, CONV_W, 0, CONV_DIM),
    "dtb": (10, 1, 0, HEAD_PAD),
    "alog": (10, 1, HEAD_PAD, HEAD_PAD),
    "w0": (11, 1, 0, D_CHUNK),
    "b0": (11, 1, D_CHUNK, D_CHUNK),
}


class _Packed:
    def __init__(self, ref, name):
        self.ref = ref
        self.row, self.rows, self.lane, self.lanes = PACK_LAYOUT[name]

    def __getitem__(self, idx):
        lanes = slice(self.lane, self.lane + self.lanes)
        if idx is Ellipsis:
            return self.ref[self.row:self.row + self.rows, lanes]
        rows, cols = idx
        assert cols == slice(None)
        return self.ref[self.row + rows.start:self.row + rows.stop, lanes]


def _pack_params(**vectors):
    segments = [[] for _ in range(PACK_ROWS)]
    for name, value in vectors.items():
        row, n_rows, lane, lanes = PACK_LAYOUT[name]
        value = value.astype(F32).reshape(n_rows, -1)
        assert value.shape[1] <= lanes
        for r in range(n_rows):
            segments[row + r].append((lane, lanes, value[r]))
    rows = []
    for segs in segments:
        parts, at = [], 0
        for lane, lanes, value in sorted(segs, key=lambda seg: seg[0]):
            parts += [jnp.zeros((lane - at,), F32), value, jnp.zeros((lanes - value.shape[0],), F32)]
            at = lane + lanes
        parts.append(jnp.zeros((D_MODEL - at,), F32))
        rows.append(jnp.concatenate(parts))
    return jnp.stack(rows)


def _head_expander(width, log2_per_head):
    row = lax.broadcasted_iota(jnp.int32, (HEAD_PAD, width), 0)
    col = lax.broadcasted_iota(jnp.int32, (HEAD_PAD, width), 1)
    return (row == jnp.right_shift(col, log2_per_head)).astype(F32)


IN_TN = 512


def _in_proj_kernel(x_hbm, xs_hbm, pack_ref, w_ref, wdt_ref, *refs, tm, n_cast):
    g_ref = _Packed(pack_ref, "g_mix")
    cast_in = refs[:n_cast]
    o_ref, dt_ref, os_ref, dts_ref = refs[n_cast:n_cast + 4]
    cast_out = refs[n_cast + 4:2 * n_cast + 4]
    xn_ref, wres_ref, xbuf_ref, sem = refs[2 * n_cast + 4:]
    i = pl.program_id(0)
    j = pl.program_id(1)
    last = pl.num_programs(0) - 1
    n_s = xs_hbm.shape[0]

    for src, dst in zip(cast_in, cast_out, strict=True):
        dst[...] = src[...].astype(BF16)

    def x_copy(tile):
        return pltpu.make_async_copy(x_hbm.at[pl.ds(tile * tm, tm), :], xbuf_ref, sem)

    def xs_copy():
        return pltpu.make_async_copy(xs_hbm, xbuf_ref.at[pl.ds(0, n_s), :], sem)

    @pl.when((i == 0) & (j == 0))
    def _():
        xs_copy().start()
        xs_copy().wait()
        nb = _rms(xbuf_ref[0:n_s, :], g_ref[...]).astype(BF16)
        xn_ref[tm:tm + n_s, :] = nb
        dts_ref[...] = _dot_nt(nb, wdt_ref[...])
        x_copy(0).start()

    @pl.when(i == 0)
    def _():
        wres_ref[j] = w_ref[...].T.astype(BF16)

    @pl.when(j == 0)
    def _():
        x_copy(i).wait()
        nb = _rms(xbuf_ref[...], g_ref[...]).astype(BF16)
        xn_ref[0:tm, :] = nb
        dt_ref[...] = _dot_nt(nb, wdt_ref[...])

    @pl.when((j == 1) & (i < last))
    def _():
        x_copy(i + 1).start()

    @pl.when(i < last)
    def _():
        o_ref[...] = _dot(xn_ref[0:tm, :], wres_ref[j])

    @pl.when(i == last)
    def _():
        res = _dot(xn_ref[...], wres_ref[j])
        o_ref[...] = res[0:tm, :]
        os_ref[...] = res[tm:, :]


def _in_proj(x, xs, pack, w_in_t, layer, w_dt_t, tm, cast=()):
    m = x.shape[0]
    n_s = xs.shape[0]
    n_i = m // tm
    n_j = D_MAIN // IN_TN
    assert n_j >= 2 and n_s <= tm and n_s % BF16_SUBLANES == 0
    cast_in, cast_out, cast_shapes = [], [], []
    n_steps = n_i * n_j

    def col_sample(i, j):
        return jnp.where(i == n_i - 1, j, 0)

    for w in cast:
        _, w_rows, w_cols = w.shape
        slab_rows = next(r for r in range(BF16_SUBLANES, w_rows + 1, BF16_SUBLANES)
                         if w_rows % r == 0 and w_rows // r <= n_steps)
        n_slabs = w_rows // slab_rows
        hold = n_steps // n_slabs
        slab = lambda i, j, n_slabs=n_slabs, hold=hold: jnp.minimum((i * n_j + j) // hold, n_slabs - 1)
        cast_in.append(pl.BlockSpec((None, slab_rows, w_cols), lambda i, j, slab=slab: (layer, slab(i, j), 0)))
        cast_out.append(pl.BlockSpec((slab_rows, w_cols), lambda i, j, slab=slab: (slab(i, j), 0)))
        cast_shapes.append(jax.ShapeDtypeStruct((w_rows, w_cols), BF16))
    return pl.pallas_call(
        functools.partial(_in_proj_kernel, tm=tm, n_cast=len(cast)),
        grid=(n_i, n_j),
        in_specs=[
            pl.BlockSpec(memory_space=pl.ANY),
            pl.BlockSpec(memory_space=pl.ANY),
            pl.BlockSpec((PACK_ROWS, D_MODEL), lambda i, j: (0, 0)),
            pl.BlockSpec((None, IN_TN, D_MODEL), lambda i, j: (layer, jnp.where(i == 0, j, n_j - 1), 0)),
            pl.BlockSpec((HEAD_PAD, D_MODEL), lambda i, j: (0, 0)),
            *cast_in,
        ],
        out_specs=[
            pl.BlockSpec((tm, IN_TN), lambda i, j: (i, j)),
            pl.BlockSpec((tm, HEAD_PAD), lambda i, j: (i, 0)),
            pl.BlockSpec((n_s, IN_TN), lambda i, j: (0, col_sample(i, j))),
            pl.BlockSpec((n_s, HEAD_PAD), lambda i, j: (0, 0)),
            *cast_out,
        ],
        out_shape=[
            jax.ShapeDtypeStruct((m, D_MAIN), F32),
            jax.ShapeDtypeStruct((m, HEAD_PAD), F32),
            jax.ShapeDtypeStruct((n_s, D_MAIN), F32),
            jax.ShapeDtypeStruct((n_s, HEAD_PAD), F32),
            *cast_shapes,
        ],
        scratch_shapes=[
            pltpu.VMEM((tm + n_s, D_MODEL), BF16),
            pltpu.VMEM((n_j, D_MODEL, IN_TN), BF16),
            pltpu.VMEM((tm, D_MODEL), F32),
            pltpu.SemaphoreType.DMA(()),
        ],
        compiler_params=_params("arbitrary", "arbitrary"),
        name="in_proj",
    )(x, xs, pack, w_in_t, w_dt_t, *cast)


def _mask_chunk_weights(ws_ref, wm_ref):
    row = lax.broadcasted_iota(jnp.int32, (CHUNK_LEN, CHUNK_LEN), 0)
    col = lax.broadcasted_iota(jnp.int32, (CHUNK_LEN, CHUNK_LEN), 1)
    for h in range(CHUNK_HEADS):
        wm_ref[:, h * CHUNK_LEN:(h + 1) * CHUNK_LEN] = jnp.where(row >= col, ws_ref[h], 0.0).astype(BF16)


def _chunk_gate(u_ref, v_ref, gv_ref, wm_ref, bst_ref, go_ref, store_y):
    vb = _rms(jax.nn.gelu(v_ref[...]), gv_ref[...]).astype(BF16)
    yield
    u = jax.nn.gelu(u_ref[...])
    yield
    zeros = jnp.zeros((CHUNK_LEN, CHUNK_HEAD_DIM), BF16)
    parts = []
    for h0 in range(0, CHUNK_HEADS, 2):
        w = wm_ref[:, h0 * CHUNK_LEN:(h0 + 2) * CHUNK_LEN]
        va = vb[:, h0 * CHUNK_HEAD_DIM:(h0 + 1) * CHUNK_HEAD_DIM]
        vb2 = vb[:, (h0 + 1) * CHUNK_HEAD_DIM:(h0 + 2) * CHUNK_HEAD_DIM]
        rhs = jnp.concatenate([jnp.concatenate([va, zeros], axis=1),
                               jnp.concatenate([zeros, vb2], axis=1)], axis=0)
        mixed = _dot(w, rhs)
        for k, h in enumerate((h0, h0 + 1)):
            sl = slice(h * CHUNK_HEAD_DIM, (h + 1) * CHUNK_HEAD_DIM)
            mh = mixed[:, k * CHUNK_HEAD_DIM:(k + 1) * CHUNK_HEAD_DIM] + bst_ref[:, h:h + 1]
            parts.append(u[:, sl] * mh)
        yield
    y = jnp.concatenate(parts, axis=1)
    store_y(_rms(y, go_ref[...]).astype(BF16))


CONV_PAD = 8


def _group_norm_out(y, z, g):
    y = y * _silu(z)
    parts = []
    for grp in range(SSM_GROUPS):
        yg = y[:, grp * GROUP_WIDTH:(grp + 1) * GROUP_WIDTH]
        parts.append(yg * lax.rsqrt(jnp.mean(yg * yg, axis=-1, keepdims=True) + EPS))
    return (jnp.concatenate(parts, axis=1) * g).astype(BF16)


def _split3(x):
    p1 = x.astype(BF16)
    r1 = x - p1.astype(F32)
    p2 = r1.astype(BF16)
    p3 = (r1 - p2.astype(F32)).astype(BF16)
    return p1, p2, p3


def _select_rows(sel3, x):
    return _dot(sel3, jnp.concatenate(_split3(x), axis=0))


def _ssd_chunk(first, xbc_ref, z_ref, dt_ref, cw_ref, cb_ref, dtb_ref, alog_ref, drep_ref, ng_ref,
               tril3_ref, nconv_ref, nssm_ref, tail_ref, ht_ref, store_y):
    q = SSD_CHUNK

    xbc = xbc_ref[...]
    tail = jnp.where(first, 0.0, tail_ref[...])
    ht_prev = jnp.where(first, 0.0, ht_ref[...])
    sub = lax.broadcasted_iota(jnp.int32, (CONV_PAD, CONV_DIM), 0)
    conv = cb_ref[...] + xbc * cw_ref[CONV_W - 1:CONV_W, :]
    for k in range(1, CONV_W):
        shifted = pltpu.roll(xbc, k, 0)
        head = jnp.where(sub < k, pltpu.roll(tail, k, 0), shifted[0:CONV_PAD, :])
        shifted = jnp.concatenate([head, shifted[CONV_PAD:, :]], axis=0)
        conv = conv + shifted * cw_ref[CONV_W - 1 - k:CONV_W - k, :]
    tail_ref[...] = xbc[q - CONV_PAD:, :]
    nconv_ref[...] = xbc[q - (CONV_W - 1):, :]
    yield

    act = _silu(conv)
    xs = act[:, :D_SSM]
    bmat = act[:, D_SSM:D_SSM + SSM_GROUPS * D_STATE]
    cmat = act[:, D_SSM + SSM_GROUPS * D_STATE:]
    yield

    dt = _softplus(dt_ref[...] + dtb_ref[...])
    a = -jnp.exp(alog_ref[...])
    acum = _select_rows(tril3_ref[...], dt * a)
    acum_t = acum.T
    low_lanes = lax.broadcasted_iota(jnp.int32, (q, 2 * SSM_HEAD_DIM), 1) < SSM_HEAD_DIM

    def expand(cols):
        pairs = []
        for h0 in range(0, SSM_HEADS, 2):
            a = jnp.broadcast_to(cols[:, h0:h0 + 1], (q, 2 * SSM_HEAD_DIM))
            b = jnp.broadcast_to(cols[:, h0 + 1:h0 + 2], (q, 2 * SSM_HEAD_DIM))
            pairs.append(jnp.where(low_lanes, a, b))
        return jnp.concatenate(pairs, axis=1)

    dt_rep = expand(dt)
    yield
    acum_rep = expand(acum)
    yield
    xdt = xs * dt_rep
    exp_a = jnp.exp(acum_rep)
    xw = xdt * jnp.exp(acum_rep[q - 1:q, :] - acum_rep)
    yield

    row = lax.broadcasted_iota(jnp.int32, (q, q), 0)
    col = lax.broadcasted_iota(jnp.int32, (q, q), 1)
    causal = row >= col
    low_half = col < SSM_HEAD_DIM
    ydiag, yoff, states = [], [], []
    for grp in range(SSM_GROUPS):
        bg_f = bmat[:, grp * D_STATE:(grp + 1) * D_STATE]
        bg = bg_f.astype(BF16)
        cg = cmat[:, grp * D_STATE:(grp + 1) * D_STATE].astype(BF16)
        cb = _dot_nt(cg, bg)
        ch = slice(grp * GROUP_WIDTH, (grp + 1) * GROUP_WIDTH)
        yoff.append(_dot(cg, ht_prev[:, ch].astype(BF16)))
        yield
        for quad in range(HEADS_PER_GROUP // 4):
            h0 = grp * HEADS_PER_GROUP + 4 * quad
            masks, blocks = [], []
            for k, h in enumerate(range(h0, h0 + 4)):
                diff = acum[:, h:h + 1] - acum_t[h:h + 1, :]
                masks.append((cb * jnp.exp(jnp.where(causal, diff, -jnp.inf))).astype(BF16))
                xp = xdt[:, (h - k % 2) * SSM_HEAD_DIM:(h - k % 2 + 2) * SSM_HEAD_DIM]
                own = jnp.where(low_half == (k % 2 == 0), xp, 0.0).astype(BF16)
                zero = jnp.zeros_like(own)
                blocks.append(jnp.concatenate([own, zero] if k < 2 else [zero, own], axis=1))
                if k % 2 == 1:
                    yield
            ydiag.append(_dot(jnp.concatenate(masks, axis=1), jnp.concatenate(blocks, axis=0)))
            yield
        states.append(_dot(bg_f.T.astype(BF16), xw[:, ch].astype(BF16)))
        yield
    y = drep_ref[...] * xs + jnp.concatenate(ydiag, axis=1) + jnp.concatenate(yoff, axis=1) * exp_a

    ht_new = ht_prev * exp_a[q - 1:q, :] + jnp.concatenate(states, axis=1)
    ht_ref[...] = ht_new
    nssm_ref[...] = ht_new.T
    yield
    store_y(_group_norm_out(y, z_ref[...], ng_ref[...]))


def _sample_rows_kernel(u_ref, v_ref, xbc_ref, dt_ref, sc_ref, pack_ref,
                        ya_ref, vrows_ref, nconv_ref, split_ref, b_ref, c_ref, dx_ref):
    gv_ref, w0_ref, b0_ref, go_ref, cw_ref, cb_ref, dtb_ref, alog_ref, drep_ref = (
        _Packed(pack_ref, name) for name in ("gv", "w0", "b0", "go", "cw", "cb", "dtb", "alog", "drep"))
    vr =_rms(jax.nn.gelu(v_ref[...]), gv_ref[...])
    vrows_ref[...] = vr
    mixed = vr * w0_ref[...] + b0_ref[...]
    ya_ref[...] = _rms(jax.nn.gelu(u_ref[...]) * mixed, go_ref[...]).astype(BF16)

    xbc = xbc_ref[...]
    conv = cb_ref[...]
    for k in range(CONV_W - 1):
        conv = conv + sc_ref[:, k * CONV_DIM:(k + 1) * CONV_DIM] * cw_ref[k:k + 1, :]
    conv = conv + xbc * cw_ref[CONV_W - 1:CONV_W, :]
    nconv_ref[:, 0:(CONV_W - 2) * CONV_DIM] = sc_ref[:, CONV_DIM:(CONV_W - 1) * CONV_DIM]
    nconv_ref[:, (CONV_W - 2) * CONV_DIM:] = xbc

    act = _silu(conv)
    xs = act[:, :D_SSM]
    b_ref[...] = act[:, D_SSM:D_SSM + SSM_GROUPS * D_STATE]
    c_ref[...] = act[:, D_SSM + SSM_GROUPS * D_STATE:]
    dx_ref[...] = drep_ref[...] * xs

    dt = _softplus(dt_ref[...] + dtb_ref[...])
    decay = jnp.exp(dt * (-jnp.exp(alog_ref[...])))
    xdt = xs * _dot_exact(dt, _head_expander(D_SSM, SSM_HEAD_DIM.bit_length() - 1))
    decay8 = _dot_exact(decay, _head_expander(HEAD_PAD, SUBLANES.bit_length() - 1))
    cols = jnp.concatenate([xdt, decay8], axis=1).T
    for p in range(N_SPLIT):
        piece = cols.astype(BF16)
        split_ref[:, p * LANES:(p + 1) * LANES] = piece
        cols = cols - piece.astype(F32)


SPLIT_ROWS = D_SSM + HEAD_PAD


def _sample_rows(proj, dt_raw, sconv, pack):
    n = proj.shape[0]
    full = lambda shape: pl.BlockSpec(shape, lambda i: (0,) * len(shape))
    return pl.pallas_call(
        _sample_rows_kernel,
        grid=(1,),
        in_specs=[
            pl.BlockSpec((n, D_CHUNK), lambda i: (0, U_BLK)),
            pl.BlockSpec((n, D_CHUNK), lambda i: (0, V_BLK)),
            pl.BlockSpec((n, CONV_DIM), lambda i: (0, XBC_BLK)),
            full((n, HEAD_PAD)),
            full((n, (CONV_W - 1) * CONV_DIM)),
            full((PACK_ROWS, D_MODEL)),
        ],
        out_specs=[
            full((n, D_CHUNK)), full((n, D_CHUNK)), full((n, (CONV_W - 1) * CONV_DIM)),
            full((SPLIT_ROWS, N_SPLIT * LANES)),
            full((n, SSM_GROUPS * D_STATE)), full((n, SSM_GROUPS * D_STATE)), full((n, D_SSM)),
        ],
        out_shape=[
            jax.ShapeDtypeStruct((n, D_CHUNK), BF16),
            jax.ShapeDtypeStruct((n, D_CHUNK), F32),
            jax.ShapeDtypeStruct((n, (CONV_W - 1) * CONV_DIM), F32),
            jax.ShapeDtypeStruct((SPLIT_ROWS, N_SPLIT * LANES), BF16),
            jax.ShapeDtypeStruct((n, SSM_GROUPS * D_STATE), F32),
            jax.ShapeDtypeStruct((n, SSM_GROUPS * D_STATE), F32),
            jax.ShapeDtypeStruct((n, D_SSM), F32),
        ],
        compiler_params=_params("arbitrary"),
        name="sample_rows",
    )(proj, proj, proj, dt_raw, sconv, pack)


SAMPLE_TB = 16


def _sample_state_kernel(split_ref, b_ref, c_ref, h0_ref, dx_ref, z_ref, pack_ref,
                         hn_ref, y_ref, yt_ref):
    ng_ref = _Packed(pack_ref, "ng")
    i = pl.program_id(0)
    n_tok = b_ref.shape[0]

    @pl.when(i == 0)
    def _():
        yt_ref[...] = jnp.zeros_like(yt_ref)

    tok = lax.broadcasted_iota(jnp.int32, (N_SPLIT * LANES, D_STATE), 0) % LANES
    lane = lax.broadcasted_iota(jnp.int32, (D_SSM, n_tok), 1)

    def per_group(row, grp):
        return jnp.broadcast_to(row[:, grp * D_STATE:(grp + 1) * D_STATE], (GROUP_WIDTH, D_STATE))

    for k in range(SAMPLE_TB):
        t = i * SAMPLE_TB + k
        onehot = (tok == t).astype(BF16)
        bcast = _dot(split_ref[...], onehot)
        decay = jnp.broadcast_to(
            bcast[D_SSM:, :].reshape(SSM_HEADS, 1, SUBLANES, D_STATE),
            (SSM_HEADS, SSM_HEAD_DIM // SUBLANES, SUBLANES, D_STATE)).reshape(D_SSM, D_STATE)
        brow = b_ref[pl.ds(t, 1), :]
        crow = c_ref[pl.ds(t, 1), :]
        bfull = jnp.concatenate([per_group(brow, g) for g in range(SSM_GROUPS)], axis=0)
        cfull = jnp.concatenate([per_group(crow, g) for g in range(SSM_GROUPS)], axis=0)
        h_new = h0_ref[k] * decay + bcast[:D_SSM, :] * bfull
        hn_ref[k] = h_new
        ysum = jnp.sum(h_new * cfull, axis=-1, keepdims=True)
        yt_ref[...] = jnp.where(lane == t, ysum, yt_ref[...])

    @pl.when(i == pl.num_programs(0) - 1)
    def _():
        y = yt_ref[...].T + dx_ref[...]
        y_ref[...] = _group_norm_out(y, z_ref[...], ng_ref[...])


def _sample_state(split, bmat, cmat, h0, dx, proj, pack):
    n = h0.shape[0]
    const2 = lambda i: (0, 0)
    return pl.pallas_call(
        _sample_state_kernel,
        grid=(n // SAMPLE_TB,),
        in_specs=[
            pl.BlockSpec((SPLIT_ROWS, N_SPLIT * LANES), const2),
            pl.BlockSpec((n, SSM_GROUPS * D_STATE), const2),
            pl.BlockSpec((n, SSM_GROUPS * D_STATE), const2),
            pl.BlockSpec((SAMPLE_TB, D_SSM, D_STATE), lambda i: (i, 0, 0)),
            pl.BlockSpec((n, D_SSM), const2),
            pl.BlockSpec((n, D_SSM), lambda i: (0, Z_BLK)),
            pl.BlockSpec((PACK_ROWS, D_MODEL), const2),
        ],
        out_specs=[
            pl.BlockSpec((SAMPLE_TB, D_SSM, D_STATE), lambda i: (i, 0, 0)),
            pl.BlockSpec((n, D_SSM), const2),
        ],
        out_shape=[
            jax.ShapeDtypeStruct((n, D_SSM, D_STATE), F32),
            jax.ShapeDtypeStruct((n, D_SSM), BF16),
        ],
        scratch_shapes=[pltpu.VMEM((D_SSM, n), F32)],
        compiler_params=_params("arbitrary"),
        name="sample_state",
    )(split, bmat, cmat, h0, dx, proj, pack)


FFN_TF = 512
MIX_TM = 512
CHUNKS_PER_TILE = MIX_TM // SSD_CHUNK
FFN_UP_PIECE = 256
FFN_DOWN_PIECE = 512
MIX_PHASES_PER_FFN_PIECE = 5
MIX_LEAD_PIECES = 0


def _interleave(main, side, side_per_main, lead=0):
    for k, _ in enumerate(main):
        if k >= lead:
            for _ in range(side_per_main):
                next(side, None)
    for _ in side:
        pass


def _mix_ffn_kernel(x_ref, wo_ref, wg_ref, wu_ref, wd_ref, pack_ref, u_ref, v_ref, ws_ref, bst_ref,
                    xbc_ref, z_ref, dt_ref, xs_ref, yas_ref, ybs_ref,
                    o_ref, nconv_ref, nssm_ref, os_ref,
                    m_ref, ya_ref, yb_ref, tail_ref, ht_ref, wm_ref, ms_ref, tril3_ref, *, chunks_per_seq):
    g_ref, gf_ref, gv_ref, go_ref, cw_ref, cb_ref, dtb_ref, alog_ref, drep_ref, ng_ref = (
        _Packed(pack_ref, name)
        for name in ("g_ffn", "g_final", "gv", "go", "cw", "cb", "dtb", "alog", "drep", "ng"))
    s = pl.program_id(0)
    f = pl.program_id(1)
    has_ffn = s >= 1
    is_last = s == pl.num_programs(0) - 1
    has_mix = jnp.logical_not(is_last) & (f < CHUNKS_PER_TILE)

    @pl.when((s == 0) & (f == 0))
    def _():
        _mask_chunk_weights(ws_ref, wm_ref)
        row = lax.broadcasted_iota(jnp.int32, tril3_ref.shape, 0)
        col = lax.broadcasted_iota(jnp.int32, tril3_ref.shape, 1)
        tril3_ref[...] = (row >= jnp.bitwise_and(col, SSD_CHUNK - 1)).astype(BF16)

    def project(x, ya, yb, m_out, o_out):
        h = x + _dot(jnp.concatenate([ya, yb], axis=1), wo_ref[...])
        m_out[...] = _rms(h, g_ref[...]).astype(BF16)
        o_out[...] = h

    @pl.when(has_ffn & (f == 0))
    def _():
        project(x_ref[...], ya_ref[...], yb_ref[...], m_ref, o_ref)

    @pl.when(is_last & (f == 0))
    def _():
        project(xs_ref[...], yas_ref[...], ybs_ref[...], ms_ref, os_ref)

    def ffn_step(with_sample=False):
        m = jnp.concatenate([m_ref[...], ms_ref[...]], axis=0) if with_sample else m_ref[...]
        acts = []
        for c in range(FFN_TF // FFN_UP_PIECE):
            cols = slice(c * FFN_UP_PIECE, (c + 1) * FFN_UP_PIECE)
            halves = []
            for w_ref in (wg_ref, wu_ref):
                halves.append(_dot(m, w_ref[:, cols]))
                yield
            acts.append((_silu(halves[0]) * halves[1]).astype(BF16))
        act = jnp.concatenate(acts, axis=1)
        for c in range(D_MODEL // FFN_DOWN_PIECE):
            cols = slice(c * FFN_DOWN_PIECE, (c + 1) * FFN_DOWN_PIECE)
            res = _dot(act, wd_ref[:, cols])
            o_ref[:, cols] += res[:MIX_TM]
            if with_sample:
                os_ref[:, cols] += res[MIX_TM:]
            yield

    chunk_rows = pl.ds(pl.multiple_of(f * SSD_CHUNK, SSD_CHUNK), SSD_CHUNK)

    def store_ya(ya):
        ya_ref[chunk_rows, :] = ya

    def store_yb(yb):
        yb_ref[chunk_rows, :] = yb

    def mix_step():
        first = (s * CHUNKS_PER_TILE + f) % chunks_per_seq == 0
        yield from _chunk_gate(u_ref, v_ref, gv_ref, wm_ref, bst_ref, go_ref, store_ya)
        yield
        yield from _ssd_chunk(first, xbc_ref, z_ref, dt_ref, cw_ref, cb_ref, dtb_ref, alog_ref, drep_ref,
                              ng_ref, tril3_ref, nconv_ref, nssm_ref, tail_ref, ht_ref, store_yb)

    @pl.when(has_ffn & has_mix)
    def _():
        _interleave(ffn_step(), mix_step(), MIX_PHASES_PER_FFN_PIECE, MIX_LEAD_PIECES)

    @pl.when(has_ffn & jnp.logical_not(has_mix) & jnp.logical_not(is_last))
    def _():
        _interleave(ffn_step(), iter(()), 0)

    @pl.when(is_last)
    def _():
        _interleave(ffn_step(with_sample=True), iter(()), 0)

    @pl.when(jnp.logical_not(has_ffn) & has_mix)
    def _():
        _interleave(iter(()), mix_step(), 0)

    @pl.when(has_ffn & (f == pl.num_programs(1) - 1))
    def _():
        o_ref[...] = _rms(o_ref[...], gf_ref[...])

    @pl.when(is_last & (f == pl.num_programs(1) - 1))
    def _():
        os_ref[...] = _rms(os_ref[...], gf_ref[...])


def _mix_ffn(x, proj, dt_raw, w_out_b, w_gate, w_up, w_down, pack, ws, bst, xs, ya_s, yb_s, batch, seq):
    m = x.shape[0]
    n_s = xs.shape[0]
    n_tiles = m // MIX_TM
    n_chunks = m // SSD_CHUNK
    chunks_per_seq = seq // SSD_CHUNK
    assert MIX_TM % SSD_CHUNK == 0 and seq % MIX_TM == 0 and CHUNKS_PER_TILE <= D_FF // FFN_TF

    def tile(s, f):
        return jnp.maximum(s - 1, 0)

    def chunk(s, f):
        return jnp.where(s >= n_tiles, n_chunks - 1,
                         s * CHUNKS_PER_TILE + jnp.minimum(f, CHUNKS_PER_TILE - 1))

    def seq_of(s, f):
        return chunk(s, f) // chunks_per_seq

    def ffn_block(s, f):
        return jnp.where(s == 0, 0, f)

    const2 = lambda s, f: (0, 0)
    return pl.pallas_call(
        functools.partial(_mix_ffn_kernel, chunks_per_seq=chunks_per_seq),
        grid=(n_tiles + 1, D_FF // FFN_TF),
        in_specs=[
            pl.BlockSpec((MIX_TM, D_MODEL), lambda s, f: (tile(s, f), 0)),
            pl.BlockSpec((D_MODEL, D_MODEL), const2, pipeline_mode=pl.Buffered(1)),
            pl.BlockSpec((D_MODEL, FFN_TF), lambda s, f: (0, ffn_block(s, f))),
            pl.BlockSpec((D_MODEL, FFN_TF), lambda s, f: (0, ffn_block(s, f))),
            pl.BlockSpec((FFN_TF, D_MODEL), lambda s, f: (ffn_block(s, f), 0)),
            pl.BlockSpec((PACK_ROWS, D_MODEL), const2),
            pl.BlockSpec((CHUNK_LEN, D_CHUNK), lambda s, f: (chunk(s, f), U_BLK)),
            pl.BlockSpec((CHUNK_LEN, D_CHUNK), lambda s, f: (chunk(s, f), V_BLK)),
            pl.BlockSpec((CHUNK_HEADS, CHUNK_LEN, CHUNK_LEN), lambda s, f: (0, 0, 0)),
            pl.BlockSpec((CHUNK_LEN, CHUNK_HEADS), const2),
            pl.BlockSpec((SSD_CHUNK, CONV_DIM), lambda s, f: (chunk(s, f), XBC_BLK)),
            pl.BlockSpec((SSD_CHUNK, D_SSM), lambda s, f: (chunk(s, f), Z_BLK)),
            pl.BlockSpec((SSD_CHUNK, HEAD_PAD), lambda s, f: (chunk(s, f), 0)),
            pl.BlockSpec((n_s, D_MODEL), const2, pipeline_mode=pl.Buffered(1)),
            pl.BlockSpec((n_s, D_CHUNK), const2, pipeline_mode=pl.Buffered(1)),
            pl.BlockSpec((n_s, D_SSM), const2, pipeline_mode=pl.Buffered(1)),
        ],
        out_specs=[
            pl.BlockSpec((MIX_TM, D_MODEL), lambda s, f: (tile(s, f), 0)),
            pl.BlockSpec((None, CONV_W - 1, CONV_DIM), lambda s, f: (seq_of(s, f), 0, 0)),
            pl.BlockSpec((None, D_SSM, D_STATE), lambda s, f: (seq_of(s, f), 0, 0)),
            pl.BlockSpec((n_s, D_MODEL), const2),
        ],
        out_shape=[
            jax.ShapeDtypeStruct((m, D_MODEL), F32),
            jax.ShapeDtypeStruct((batch, CONV_W - 1, CONV_DIM), F32),
            jax.ShapeDtypeStruct((batch, D_SSM, D_STATE), F32),
            jax.ShapeDtypeStruct((n_s, D_MODEL), F32),
        ],
        scratch_shapes=[
            pltpu.VMEM((MIX_TM, D_MODEL), BF16),
            pltpu.VMEM((MIX_TM, D_CHUNK), BF16),
            pltpu.VMEM((MIX_TM, D_SSM), BF16),
            pltpu.VMEM((CONV_PAD, CONV_DIM), F32),
            pltpu.VMEM((D_STATE, D_SSM), F32),
            pltpu.VMEM((CHUNK_LEN, CHUNK_HEADS * CHUNK_LEN), BF16),
            pltpu.VMEM((n_s, D_MODEL), BF16),
            pltpu.VMEM((SSD_CHUNK, N_SPLIT * SSD_CHUNK), BF16),
        ],
        compiler_params=_params("arbitrary", "arbitrary"),
        name="mix_ffn",
    )(x, w_out_b, w_gate, w_up, w_down, pack, proj, proj, ws, bst, proj, proj, dt_raw, xs, ya_s, yb_s)


def kernel(x_prompt, x_sample, state_conv, state_ssm, norm_mix_g, w_in, chunk_v_norm_g, chunk_w_s,
           chunk_b_s, chunk_out_norm_g, ssd_conv_w, ssd_conv_b, ssd_dt_bias, ssd_a_log, ssd_d,
           ssd_norm_g, w_out, norm_ffn_g, w_gate, w_up, w_down, norm_final_g):
    depth = w_in.shape[0]
    batch, seq, _ = x_prompt.shape
    n_s = x_sample.shape[0]
    assert x_sample.shape[1] == 1 and seq % SSD_CHUNK == 0 and depth == 1

    hp = x_prompt.reshape(batch * seq, D_MODEL)
    hs = x_sample.reshape(n_s, D_MODEL)
    l = 0
    w_in_t = jnp.swapaxes(w_in, 1, 2)
    w_dt_t = jnp.pad(w_in_t[l, D_MAIN:, :], ((0, HEAD_PAD - SSM_HEADS), (0, 0))).astype(BF16)
    pack = _pack_params(
        g_mix=norm_mix_g[l], g_ffn=norm_ffn_g[l], g_final=norm_final_g,
        gv=chunk_v_norm_g[l], go=chunk_out_norm_g[l],
        ng=ssd_norm_g[l], drep=jnp.repeat(ssd_d[l], SSM_HEAD_DIM),
        cb=ssd_conv_b[l], cw=ssd_conv_w[l], dtb=ssd_dt_bias[l], alog=ssd_a_log[l],
        w0=jnp.repeat(chunk_w_s[l, :, 0, 0], CHUNK_HEAD_DIM),
        b0=jnp.repeat(chunk_b_s[l, :, 0], CHUNK_HEAD_DIM))

    proj_p, dt_p, proj_s, dt_s, w_gate_b, w_up_b, w_down_b, w_out_b = _in_proj(
        hp, hs, pack, w_in_t, l, w_dt_t, tm=1024, cast=(w_gate, w_up, w_down, w_out))

    ya_s, v_s, conv_s, split, b_s, c_s, dx_s = _sample_rows(
        proj_s, dt_s, state_conv[l].reshape(n_s, (CONV_W - 1) * CONV_DIM), pack)
    ssm_s, yb_s = _sample_state(split, b_s, c_s, state_ssm[l].reshape(n_s, D_SSM, D_STATE),
                                dx_s, proj_s, pack)

    y_p, conv_p, ssm_p, y_s = _mix_ffn(
        hp, proj_p, dt_p, w_out_b, w_gate_b, w_up_b, w_down_b, pack,
        chunk_w_s[l].astype(F32), chunk_b_s[l].astype(F32).T, hs, ya_s, yb_s, batch, seq)

    return (
        y_p.reshape(batch, seq, D_MODEL),
        y_s.reshape(n_s, 1, D_MODEL),
        conv_p[None],
        ssm_p.reshape(1, batch, SSM_HEADS, SSM_HEAD_DIM, D_STATE),
        conv_s.reshape(1, n_s, CONV_W - 1, CONV_DIM),
        ssm_s.reshape(1, n_s, SSM_HEADS, SSM_HEAD_DIM, D_STATE),
        v_s.reshape(1, n_s, 1, D_CHUNK),
    )
```

```python
import functools

import jax
import jax.numpy as jnp
from jax import lax
from jax.experimental import pallas as pl
from jax.experimental.pallas import tpu as pltpu

F32 = jnp.float32
BF16 = jnp.bfloat16
HIGHEST = lax.Precision.HIGHEST

D_MODEL = 2048
D_CHUNK = 1024
CHUNK_HEADS = 8
CHUNK_HEAD_DIM = 128
CHUNK_LEN = 128
D_SSM = 1024
SSM_HEAD_DIM = 64
SSM_HEADS = 16
SSM_GROUPS = 2
HEADS_PER_GROUP = SSM_HEADS // SSM_GROUPS
GROUP_WIDTH = D_SSM // SSM_GROUPS
D_STATE = 128
CONV_W = 4
CONV_DIM = D_SSM + 2 * SSM_GROUPS * D_STATE
SSD_CHUNK = 128
D_MAIN = 2 * D_CHUNK + D_SSM + CONV_DIM
D_FF = 5632
EPS = 1e-6

LANES = 128
SUBLANES = 8
BF16_SUBLANES = 16
HEAD_PAD = LANES
N_SPLIT = 3
VMEM_LIMIT = 58 * 1024 * 1024

U_BLK, V_BLK, Z_BLK = 0, 1, 2
XBC_BLK = 2


def _rms(x, g):
    return x * lax.rsqrt(jnp.mean(x * x, axis=-1, keepdims=True) + EPS) * g


def _silu(x):
    return x * jax.nn.sigmoid(x)


def _softplus(x):
    return jnp.maximum(x, 0.0) + jnp.log1p(jnp.exp(-jnp.abs(x)))


def _dot(a, b):
    return jnp.dot(a, b, preferred_element_type=F32)


def _dot_nt(a, b):
    return lax.dot_general(a, b, (((1,), (1,)), ((), ())), preferred_element_type=F32)


def _dot_exact(a, b):
    return jnp.dot(a, b, precision=HIGHEST, preferred_element_type=F32)


def _params(*semantics):
    return pltpu.CompilerParams(dimension_semantics=semantics, vmem_limit_bytes=VMEM_LIMIT)


PACK_ROWS = 16
PACK_LAYOUT = {
    "g_mix": (0, 1, 0, D_MODEL),
    "g_ffn": (1, 1, 0, D_MODEL),
    "g_final": (2, 1, 0, D_MODEL),
    "gv": (3, 1, 0, D_CHUNK),
    "go": (3, 1, D_CHUNK, D_CHUNK),
    "ng": (4, 1, 0, D_SSM),
    "drep": (4, 1, D_SSM, D_SSM),
    "cb": (5, 1, 0, CONV_DIM),
    "cw": (6, CONV_W, 0, CONV_DIM),
    "dtb": (10, 1, 0, HEAD_PAD),
    "alog": (10, 1, HEAD_PAD, HEAD_PAD),
    "w0": (11, 1, 0, D_CHUNK),
    "b0": (11, 1, D_CHUNK, D_CHUNK),
}


class _Packed:
    def __init__(self, ref, name):
        self.ref = ref
        self.row, self.rows, self.lane, self.lanes = PACK_LAYOUT[name]

    def __getitem__(self, idx):
        lanes = slice(self.lane, self.lane + self.lanes)
        if idx is Ellipsis:
            return self.ref[self.row:self.row + self.rows, lanes]
        rows, cols = idx
        assert cols == slice(None)
        return self.ref[self.row + rows.start:self.row + rows.stop, lanes]


def _pack_params(**vectors):
    segments = [[] for _ in range(PACK_ROWS)]
    for name, value in vectors.items():
        row, n_rows, lane, lanes = PACK_LAYOUT[name]
        value = value.astype(F32).reshape(n_rows, -1)
        assert value.shape[1] <= lanes
        for r in range(n_rows):
            segments[row + r].append((lane, lanes, value[r]))
    rows = []
    for segs in segments:
        parts, at = [], 0
        for lane, lanes, value in sorted(segs, key=lambda seg: seg[0]):
            parts += [jnp.zeros((lane - at,), F32), value, jnp.zeros((lanes - value.shape[0],), F32)]
            at = lane + lanes
        parts.append(jnp.zeros((D_MODEL - at,), F32))
        rows.append(jnp.concatenate(parts))
    return jnp.stack(rows)


def _head_expander(width, log2_per_head):
    row = lax.broadcasted_iota(jnp.int32, (HEAD_PAD, width), 0)
    col = lax.broadcasted_iota(jnp.int32, (HEAD_PAD, width), 1)
    return (row == jnp.right_shift(col, log2_per_head)).astype(F32)


IN_TN = 512


def _in_proj_kernel(x_hbm, xs_hbm, pack_ref, w_ref, wdt_ref, *refs, tm, n_cast):
    g_ref = _Packed(pack_ref, "g_mix")
    cast_in = refs[:n_cast]
    o_ref, dt_ref, os_ref, dts_ref = refs[n_cast:n_cast + 4]
    cast_out = refs[n_cast + 4:2 * n_cast + 4]
    xn_ref, wres_ref, xbuf_ref, sem = refs[2 * n_cast + 4:]
    i = pl.program_id(0)
    j = pl.program_id(1)
    last = pl.num_programs(0) - 1
    n_s = xs_hbm.shape[0]

    for src, dst in zip(cast_in, cast_out, strict=True):
        dst[...] = src[...].astype(BF16)

    def x_copy(tile):
        return pltpu.make_async_copy(x_hbm.at[pl.ds(tile * tm, tm), :], xbuf_ref, sem)

    def xs_copy():
        return pltpu.make_async_copy(xs_hbm, xbuf_ref.at[pl.ds(0, n_s), :], sem)

    @pl.when((i == 0) & (j == 0))
    def _():
        xs_copy().start()
        xs_copy().wait()
        nb = _rms(xbuf_ref[0:n_s, :], g_ref[...]).astype(BF16)
        xn_ref[tm:tm + n_s, :] = nb
        dts_ref[...] = _dot_nt(nb, wdt_ref[...])
        x_copy(0).start()

    @pl.when(i == 0)
    def _():
        wres_ref[j] = w_ref[...].T.astype(BF16)

    @pl.when(j == 0)
    def _():
        x_copy(i).wait()
        nb = _rms(xbuf_ref[...], g_ref[...]).astype(BF16)
        xn_ref[0:tm, :] = nb
        dt_ref[...] = _dot_nt(nb, wdt_ref[...])

    @pl.when((j == 1) & (i < last))
    def _():
        x_copy(i + 1).start()

    @pl.when(i < last)
    def _():
        o_ref[...] = _dot(xn_ref[0:tm, :], wres_ref[j])

    @pl.when(i == last)
    def _():
        res = _dot(xn_ref[...], wres_ref[j])
        o_ref[...] = res[0:tm, :]
        os_ref[...] = res[tm:, :]


def _in_proj(x, xs, pack, w_in_t, layer, w_dt_t, tm, cast=()):
    m = x.shape[0]
    n_s = xs.shape[0]
    n_i = m // tm
    n_j = D_MAIN // IN_TN
    assert n_j >= 2 and n_s <= tm and n_s % BF16_SUBLANES == 0
    cast_in, cast_out, cast_shapes = [], [], []
    n_steps = n_i * n_j

    def col_sample(i, j):
        return jnp.where(i == n_i - 1, j, 0)

    for w in cast:
        _, w_rows, w_cols = w.shape
        slab_rows = next(r for r in range(BF16_SUBLANES, w_rows + 1, BF16_SUBLANES)
                         if w_rows % r == 0 and w_rows // r <= n_steps)
        n_slabs = w_rows // slab_rows
        hold = n_steps // n_slabs
        slab = lambda i, j, n_slabs=n_slabs, hold=hold: jnp.minimum((i * n_j + j) // hold, n_slabs - 1)
        cast_in.append(pl.BlockSpec((None, slab_rows, w_cols), lambda i, j, slab=slab: (layer, slab(i, j), 0)))
        cast_out.append(pl.BlockSpec((slab_rows, w_cols), lambda i, j, slab=slab: (slab(i, j), 0)))
        cast_shapes.append(jax.ShapeDtypeStruct((w_rows, w_cols), BF16))
    return pl.pallas_call(
        functools.partial(_in_proj_kernel, tm=tm, n_cast=len(cast)),
        grid=(n_i, n_j),
        in_specs=[
            pl.BlockSpec(memory_space=pl.ANY),
            pl.BlockSpec(memory_space=pl.ANY),
            pl.BlockSpec((PACK_ROWS, D_MODEL), lambda i, j: (0, 0)),
            pl.BlockSpec((None, IN_TN, D_MODEL), lambda i, j: (layer, jnp.where(i == 0, j, n_j - 1), 0)),
            pl.BlockSpec((HEAD_PAD, D_MODEL), lambda i, j: (0, 0)),
            *cast_in,
        ],
        out_specs=[
            pl.BlockSpec((tm, IN_TN), lambda i, j: (i, j)),
            pl.BlockSpec((tm, HEAD_PAD), lambda i, j: (i, 0)),
            pl.BlockSpec((n_s, IN_TN), lambda i, j: (0, col_sample(i, j))),
            pl.BlockSpec((n_s, HEAD_PAD), lambda i, j: (0, 0)),
            *cast_out,
        ],
        out_shape=[
            jax.ShapeDtypeStruct((m, D_MAIN), F32),
            jax.ShapeDtypeStruct((m, HEAD_PAD), F32),
            jax.ShapeDtypeStruct((n_s, D_MAIN), F32),
            jax.ShapeDtypeStruct((n_s, HEAD_PAD), F32),
            *cast_shapes,
        ],
        scratch_shapes=[
            pltpu.VMEM((tm + n_s, D_MODEL), BF16),
            pltpu.VMEM((n_j, D_MODEL, IN_TN), BF16),
            pltpu.VMEM((tm, D_MODEL), F32),
            pltpu.SemaphoreType.DMA(()),
        ],
        compiler_params=_params("arbitrary", "arbitrary"),
        name="in_proj",
    )(x, xs, pack, w_in_t, w_dt_t, *cast)


def _mask_chunk_weights(ws_ref, wm_ref):
    row = lax.broadcasted_iota(jnp.int32, (CHUNK_LEN, CHUNK_LEN), 0)
    col = lax.broadcasted_iota(jnp.int32, (CHUNK_LEN, CHUNK_LEN), 1)
    for h in range(CHUNK_HEADS):
        wm_ref[:, h * CHUNK_LEN:(h + 1) * CHUNK_LEN] = jnp.where(row >= col, ws_ref[h], 0.0).astype(BF16)


def _chunk_gate(u_ref, v_ref, gv_ref, wm_ref, bst_ref, go_ref, store_y):
    vb = _rms(jax.nn.gelu(v_ref[...]), gv_ref[...]).astype(BF16)
    yield
    u = jax.nn.gelu(u_ref[...])
    yield
    zeros = jnp.zeros((CHUNK_LEN, CHUNK_HEAD_DIM), BF16)
    parts = []
    for h0 in range(0, CHUNK_HEADS, 2):
        w = wm_ref[:, h0 * CHUNK_LEN:(h0 + 2) * CHUNK_LEN]
        va = vb[:, h0 * CHUNK_HEAD_DIM:(h0 + 1) * CHUNK_HEAD_DIM]
        vb2 = vb[:, (h0 + 1) * CHUNK_HEAD_DIM:(h0 + 2) * CHUNK_HEAD_DIM]
        rhs = jnp.concatenate([jnp.concatenate([va, zeros], axis=1),
                               jnp.concatenate([zeros, vb2], axis=1)], axis=0)
        mixed = _dot(w, rhs)
        for k, h in enumerate((h0, h0 + 1)):
            sl = slice(h * CHUNK_HEAD_DIM, (h + 1) * CHUNK_HEAD_DIM)
            mh = mixed[:, k * CHUNK_HEAD_DIM:(k + 1) * CHUNK_HEAD_DIM] + bst_ref[:, h:h + 1]
            parts.append(u[:, sl] * mh)
        yield
    y = jnp.concatenate(parts, axis=1)
    store_y(_rms(y, go_ref[...]).astype(BF16))


CONV_PAD = 8


def _group_norm_out(y, z, g):
    y = y * _silu(z)
    parts = []
    for grp in range(SSM_GROUPS):
        yg = y[:, grp * GROUP_WIDTH:(grp + 1) * GROUP_WIDTH]
        parts.append(yg * lax.rsqrt(jnp.mean(yg * yg, axis=-1, keepdims=True) + EPS))
    return (jnp.concatenate(parts, axis=1) * g).astype(BF16)


def _split3(x):
    p1 = x.astype(BF16)
    r1 = x - p1.astype(F32)
    p2 = r1.astype(BF16)
    p3 = (r1 - p2.astype(F32)).astype(BF16)
    return p1, p2, p3


def _select_rows(sel3, x):
    return _dot(sel3, jnp.concatenate(_split3(x), axis=0))


def _ssd_chunk(first, xbc_ref, z_ref, dt_ref, cw_ref, cb_ref, dtb_ref, alog_ref, drep_ref, ng_ref,
               tril3_ref, nconv_ref, nssm_ref, tail_ref, ht_ref, store_y):
    q = SSD_CHUNK

    xbc = xbc_ref[...]
    tail = jnp.where(first, 0.0, tail_ref[...])
    ht_prev = jnp.where(first, 0.0, ht_ref[...])
    sub = lax.broadcasted_iota(jnp.int32, (CONV_PAD, CONV_DIM), 0)
    conv = cb_ref[...] + xbc * cw_ref[CONV_W - 1:CONV_W, :]
    for k in range(1, CONV_W):
        shifted = pltpu.roll(xbc, k, 0)
        head = jnp.where(sub < k, pltpu.roll(tail, k, 0), shifted[0:CONV_PAD, :])
        shifted = jnp.concatenate([head, shifted[CONV_PAD:, :]], axis=0)
        conv = conv + shifted * cw_ref[CONV_W - 1 - k:CONV_W - k, :]
    tail_ref[...] = xbc[q - CONV_PAD:, :]
    nconv_ref[...] = xbc[q - (CONV_W - 1):, :]
    yield

    act = _silu(conv)
    xs = act[:, :D_SSM]
    bmat = act[:, D_SSM:D_SSM + SSM_GROUPS * D_STATE]
    cmat = act[:, D_SSM + SSM_GROUPS * D_STATE:]
    yield

    dt = _softplus(dt_ref[...] + dtb_ref[...])
    a = -jnp.exp(alog_ref[...])
    acum = _select_rows(tril3_ref[...], dt * a)
    acum_t = acum.T
    low_lanes = lax.broadcasted_iota(jnp.int32, (q, 2 * SSM_HEAD_DIM), 1) < SSM_HEAD_DIM

    def expand(cols):
        pairs = []
        for h0 in range(0, SSM_HEADS, 2):
            a = jnp.broadcast_to(cols[:, h0:h0 + 1], (q, 2 * SSM_HEAD_DIM))
            b = jnp.broadcast_to(cols[:, h0 + 1:h0 + 2], (q, 2 * SSM_HEAD_DIM))
            pairs.append(jnp.where(low_lanes, a, b))
        return jnp.concatenate(pairs, axis=1)

    dt_rep = expand(dt)
    yield
    acum_rep = expand(acum)
    yield
    xdt = xs * dt_rep
    exp_a = jnp.exp(acum_rep)
    xw = xdt * jnp.exp(acum_rep[q - 1:q, :] - acum_rep)
    yield

    row = lax.broadcasted_iota(jnp.int32, (q, q), 0)
    col = lax.broadcasted_iota(jnp.int32, (q, q), 1)
    causal = row >= col
    low_half = col < SSM_HEAD_DIM
    ydiag, yoff, states = [], [], []
    for grp in range(SSM_GROUPS):
        bg_f = bmat[:, grp * D_STATE:(grp + 1) * D_STATE]
        bg = bg_f.astype(BF16)
        cg = cmat[:, grp * D_STATE:(grp + 1) * D_STATE].astype(BF16)
        cb = _dot_nt(cg, bg)
        ch = slice(grp * GROUP_WIDTH, (grp + 1) * GROUP_WIDTH)
        yoff.append(_dot(cg, ht_prev[:, ch].astype(BF16)))
        yield
        for quad in range(HEADS_PER_GROUP // 4):
            h0 = grp * HEADS_PER_GROUP + 4 * quad
            masks, blocks = [], []
            for k, h in enumerate(range(h0, h0 + 4)):
                diff = acum[:, h:h + 1] - acum_t[h:h + 1, :]
                masks.append((cb * jnp.exp(jnp.where(causal, diff, -jnp.inf))).astype(BF16))
                xp = xdt[:, (h - k % 2) * SSM_HEAD_DIM:(h - k % 2 + 2) * SSM_HEAD_DIM]
                own = jnp.where(low_half == (k % 2 == 0), xp, 0.0).astype(BF16)
                zero = jnp.zeros_like(own)
                blocks.append(jnp.concatenate([own, zero] if k < 2 else [zero, own], axis=1))
                if k % 2 == 1:
                    yield
            ydiag.append(_dot(jnp.concatenate(masks, axis=1), jnp.concatenate(blocks, axis=0)))
            yield
        states.append(_dot(bg_f.T.astype(BF16), xw[:, ch].astype(BF16)))
        yield
    y = drep_ref[...] * xs + jnp.concatenate(ydiag, axis=1) + jnp.concatenate(yoff, axis=1) * exp_a

    ht_new = ht_prev * exp_a[q - 1:q, :] + jnp.concatenate(states, axis=1)
    ht_ref[...] = ht_new
    nssm_ref[...] = ht_new.T
    yield
    store_y(_group_norm_out(y, z_ref[...], ng_ref[...]))


def _sample_rows_kernel(u_ref, v_ref, xbc_ref, dt_ref, sc_ref, pack_ref,
                        ya_ref, vrows_ref, nconv_ref, split_ref, b_ref, c_ref, dx_ref):
    gv_ref, w0_ref, b0_ref, go_ref, cw_ref, cb_ref, dtb_ref, alog_ref, drep_ref = (
        _Packed(pack_ref, name) for name in ("gv", "w0", "b0", "go", "cw", "cb", "dtb", "alog", "drep"))
    vr =_rms(jax.nn.gelu(v_ref[...]), gv_ref[...])
    vrows_ref[...] = vr
    mixed = vr * w0_ref[...] + b0_ref[...]
    ya_ref[...] = _rms(jax.nn.gelu(u_ref[...]) * mixed, go_ref[...]).astype(BF16)

    xbc = xbc_ref[...]
    conv = cb_ref[...]
    for k in range(CONV_W - 1):
        conv = conv + sc_ref[:, k * CONV_DIM:(k + 1) * CONV_DIM] * cw_ref[k:k + 1, :]
    conv = conv + xbc * cw_ref[CONV_W - 1:CONV_W, :]
    nconv_ref[:, 0:(CONV_W - 2) * CONV_DIM] = sc_ref[:, CONV_DIM:(CONV_W - 1) * CONV_DIM]
    nconv_ref[:, (CONV_W - 2) * CONV_DIM:] = xbc

    act = _silu(conv)
    xs = act[:, :D_SSM]
    b_ref[...] = act[:, D_SSM:D_SSM + SSM_GROUPS * D_STATE]
    c_ref[...] = act[:, D_SSM + SSM_GROUPS * D_STATE:]
    dx_ref[...] = drep_ref[...] * xs

    dt = _softplus(dt_ref[...] + dtb_ref[...])
    decay = jnp.exp(dt * (-jnp.exp(alog_ref[...])))
    xdt = xs * _dot_exact(dt, _head_expander(D_SSM, SSM_HEAD_DIM.bit_length() - 1))
    decay8 = _dot_exact(decay, _head_expander(HEAD_PAD, SUBLANES.bit_length() - 1))
    cols = jnp.concatenate([xdt, decay8], axis=1).T
    for p in range(N_SPLIT):
        piece = cols.astype(BF16)
        split_ref[:, p * LANES:(p + 1) * LANES] = piece
        cols = cols - piece.astype(F32)


SPLIT_ROWS = D_SSM + HEAD_PAD


def _sample_rows(proj, dt_raw, sconv, pack):
    n = proj.shape[0]
    full = lambda shape: pl.BlockSpec(shape, lambda i: (0,) * len(shape))
    return pl.pallas_call(
        _sample_rows_kernel,
        grid=(1,),
        in_specs=[
            pl.BlockSpec((n, D_CHUNK), lambda i: (0, U_BLK)),
            pl.BlockSpec((n, D_CHUNK), lambda i: (0, V_BLK)),
            pl.BlockSpec((n, CONV_DIM), lambda i: (0, XBC_BLK)),
            full((n, HEAD_PAD)),
            full((n, (CONV_W - 1) * CONV_DIM)),
            full((PACK_ROWS, D_MODEL)),
        ],
        out_specs=[
            full((n, D_CHUNK)), full((n, D_CHUNK)), full((n, (CONV_W - 1) * CONV_DIM)),
            full((SPLIT_ROWS, N_SPLIT * LANES)),
            full((n, SSM_GROUPS * D_STATE)), full((n, SSM_GROUPS * D_STATE)), full((n, D_SSM)),
        ],
        out_shape=[
            jax.ShapeDtypeStruct((n, D_CHUNK), BF16),
            jax.ShapeDtypeStruct((n, D_CHUNK), F32),
            jax.ShapeDtypeStruct((n, (CONV_W - 1) * CONV_DIM), F32),
            jax.ShapeDtypeStruct((SPLIT_ROWS, N_SPLIT * LANES), BF16),
            jax.ShapeDtypeStruct((n, SSM_GROUPS * D_STATE), F32),
            jax.ShapeDtypeStruct((n, SSM_GROUPS * D_STATE), F32),
            jax.ShapeDtypeStruct((n, D_SSM), F32),
        ],
        compiler_params=_params("arbitrary"),
        name="sample_rows",
    )(proj, proj, proj, dt_raw, sconv, pack)


SAMPLE_TB = 16


def _sample_state_kernel(split_ref, b_ref, c_ref, h0_ref, dx_ref, z_ref, pack_ref,
                         hn_ref, y_ref, yt_ref):
    ng_ref = _Packed(pack_ref, "ng")
    i = pl.program_id(0)
    n_tok = b_ref.shape[0]

    @pl.when(i == 0)
    def _():
        yt_ref[...] = jnp.zeros_like(yt_ref)

    tok = lax.broadcasted_iota(jnp.int32, (N_SPLIT * LANES, D_STATE), 0) % LANES
    lane = lax.broadcasted_iota(jnp.int32, (D_SSM, n_tok), 1)

    def per_group(row, grp):
        return jnp.broadcast_to(row[:, grp * D_STATE:(grp + 1) * D_STATE], (GROUP_WIDTH, D_STATE))

    for k in range(SAMPLE_TB):
        t = i * SAMPLE_TB + k
        onehot = (tok == t).astype(BF16)
        bcast = _dot(split_ref[...], onehot)
        decay = jnp.broadcast_to(
            bcast[D_SSM:, :].reshape(SSM_HEADS, 1, SUBLANES, D_STATE),
            (SSM_HEADS, SSM_HEAD_DIM // SUBLANES, SUBLANES, D_STATE)).reshape(D_SSM, D_STATE)
        brow = b_ref[pl.ds(t, 1), :]
        crow = c_ref[pl.ds(t, 1), :]
        bfull = jnp.concatenate([per_group(brow, g) for g in range(SSM_GROUPS)], axis=0)
        cfull = jnp.concatenate([per_group(crow, g) for g in range(SSM_GROUPS)], axis=0)
        h_new = h0_ref[k] * decay + bcast[:D_SSM, :] * bfull
        hn_ref[k] = h_new
        ysum = jnp.sum(h_new * cfull, axis=-1, keepdims=True)
        yt_ref[...] = jnp.where(lane == t, ysum, yt_ref[...])

    @pl.when(i == pl.num_programs(0) - 1)
    def _():
        y = yt_ref[...].T + dx_ref[...]
        y_ref[...] = _group_norm_out(y, z_ref[...], ng_ref[...])


def _sample_state(split, bmat, cmat, h0, dx, proj, pack):
    n = h0.shape[0]
    const2 = lambda i: (0, 0)
    return pl.pallas_call(
        _sample_state_kernel,
        grid=(n // SAMPLE_TB,),
        in_specs=[
            pl.BlockSpec((SPLIT_ROWS, N_SPLIT * LANES), const2),
            pl.BlockSpec((n, SSM_GROUPS * D_STATE), const2),
            pl.BlockSpec((n, SSM_GROUPS * D_STATE), const2),
            pl.BlockSpec((SAMPLE_TB, D_SSM, D_STATE), lambda i: (i, 0, 0)),
            pl.BlockSpec((n, D_SSM), const2),
            pl.BlockSpec((n, D_SSM), lambda i: (0, Z_BLK)),
            pl.BlockSpec((PACK_ROWS, D_MODEL), const2),
        ],
        out_specs=[
            pl.BlockSpec((SAMPLE_TB, D_SSM, D_STATE), lambda i: (i, 0, 0)),
            pl.BlockSpec((n, D_SSM), const2),
        ],
        out_shape=[
            jax.ShapeDtypeStruct((n, D_SSM, D_STATE), F32),
            jax.ShapeDtypeStruct((n, D_SSM), BF16),
        ],
        scratch_shapes=[pltpu.VMEM((D_SSM, n), F32)],
        compiler_params=_params("arbitrary"),
        name="sample_state",
    )(split, bmat, cmat, h0, dx, proj, pack)


FFN_TF = 512
MIX_TM = 512
CHUNKS_PER_TILE = MIX_TM // SSD_CHUNK
FFN_UP_PIECE = 256
FFN_DOWN_PIECE = 512
MIX_PHASES_PER_FFN_PIECE = 5
MIX_LEAD_PIECES = 0


def _interleave(main, side, side_per_main, lead=0):
    for k, _ in enumerate(main):
        if k >= lead:
            for _ in range(side_per_main):
                next(side, None)
    for _ in side:
        pass


def _mix_ffn_kernel(x_ref, wo_ref, wg_ref, wu_ref, wd_ref, pack_ref, u_ref, v_ref, ws_ref, bst_ref,
                    xbc_ref, z_ref, dt_ref, xs_ref, yas_ref, ybs_ref,
                    o_ref, nconv_ref, nssm_ref, os_ref,
                    m_ref, ya_ref, yb_ref, tail_ref, ht_ref, wm_ref, ms_ref, tril3_ref, *, chunks_per_seq):
    g_ref, gf_ref, gv_ref, go_ref, cw_ref, cb_ref, dtb_ref, alog_ref, drep_ref, ng_ref = (
        _Packed(pack_ref, name)
        for name in ("g_ffn", "g_final", "gv", "go", "cw", "cb", "dtb", "alog", "drep", "ng"))
    s = pl.program_id(0)
    f = pl.program_id(1)
    has_ffn = s >= 1
    is_last = s == pl.num_programs(0) - 1
    has_mix = jnp.logical_not(is_last) & (f < CHUNKS_PER_TILE)

    @pl.when((s == 0) & (f == 0))
    def _():
        _mask_chunk_weights(ws_ref, wm_ref)
        row = lax.broadcasted_iota(jnp.int32, tril3_ref.shape, 0)
        col = lax.broadcasted_iota(jnp.int32, tril3_ref.shape, 1)
        tril3_ref[...] = (row >= jnp.bitwise_and(col, SSD_CHUNK - 1)).astype(BF16)

    def project(x, ya, yb, m_out, o_out):
        h = x + _dot(jnp.concatenate([ya, yb], axis=1), wo_ref[...])
        m_out[...] = _rms(h, g_ref[...]).astype(BF16)
        o_out[...] = h

    @pl.when(has_ffn & (f == 0))
    def _():
        project(x_ref[...], ya_ref[...], yb_ref[...], m_ref, o_ref)

    @pl.when(is_last & (f == 0))
    def _():
        project(xs_ref[...], yas_ref[...], ybs_ref[...], ms_ref, os_ref)

    def ffn_step(with_sample=False):
        m = jnp.concatenate([m_ref[...], ms_ref[...]], axis=0) if with_sample else m_ref[...]
        acts = []
        for c in range(FFN_TF // FFN_UP_PIECE):
            cols = slice(c * FFN_UP_PIECE, (c + 1) * FFN_UP_PIECE)
            halves = []
            for w_ref in (wg_ref, wu_ref):
                halves.append(_dot(m, w_ref[:, cols]))
                yield
            acts.append((_silu(halves[0]) * halves[1]).astype(BF16))
        act = jnp.concatenate(acts, axis=1)
        for c in range(D_MODEL // FFN_DOWN_PIECE):
            cols = slice(c * FFN_DOWN_PIECE, (c + 1) * FFN_DOWN_PIECE)
            res = _dot(act, wd_ref[:, cols])
            o_ref[:, cols] += res[:MIX_TM]
            if with_sample:
                os_ref[:, cols] += res[MIX_TM:]
            yield

    chunk_rows = pl.ds(pl.multiple_of(f * SSD_CHUNK, SSD_CHUNK), SSD_CHUNK)

    def store_ya(ya):
        ya_ref[chunk_rows, :] = ya

    def store_yb(yb):
        yb_ref[chunk_rows, :] = yb

    def mix_step():
        first = (s * CHUNKS_PER_TILE + f) % chunks_per_seq == 0
        yield from _chunk_gate(u_ref, v_ref, gv_ref, wm_ref, bst_ref, go_ref, store_ya)
        yield
        yield from _ssd_chunk(first, xbc_ref, z_ref, dt_ref, cw_ref, cb_ref, dtb_ref, alog_ref, drep_ref,
                              ng_ref, tril3_ref, nconv_ref, nssm_ref, tail_ref, ht_ref, store_yb)

    @pl.when(has_ffn & has_mix)
    def _():
        _interleave(ffn_step(), mix_step(), MIX_PHASES_PER_FFN_PIECE, MIX_LEAD_PIECES)

    @pl.when(has_ffn & jnp.logical_not(has_mix) & jnp.logical_not(is_last))
    def _():
        _interleave(ffn_step(), iter(()), 0)

    @pl.when(is_last)
    def _():
        _interleave(ffn_step(with_sample=True), iter(()), 0)

    @pl.when(jnp.logical_not(has_ffn) & has_mix)
    def _():
        _interleave(iter(()), mix_step(), 0)

    @pl.when(has_ffn & (f == pl.num_programs(1) - 1))
    def _():
        o_ref[...] = _rms(o_ref[...], gf_ref[...])

    @pl.when(is_last & (f == pl.num_programs(1) - 1))
    def _():
        os_ref[...] = _rms(os_ref[...], gf_ref[...])


def _mix_ffn(x, proj, dt_raw, w_out_b, w_gate, w_up, w_down, pack, ws, bst, xs, ya_s, yb_s, batch, seq):
    m = x.shape[0]
    n_s = xs.shape[0]
    n_tiles = m // MIX_TM
    n_chunks = m // SSD_CHUNK
    chunks_per_seq = seq // SSD_CHUNK
    assert MIX_TM % SSD_CHUNK == 0 and seq % MIX_TM == 0 and CHUNKS_PER_TILE <= D_FF // FFN_TF

    def tile(s, f):
        return jnp.maximum(s - 1, 0)

    def chunk(s, f):
        return jnp.where(s >= n_tiles, n_chunks - 1,
                         s * CHUNKS_PER_TILE + jnp.minimum(f, CHUNKS_PER_TILE - 1))

    def seq_of(s, f):
        return chunk(s, f) // chunks_per_seq

    def ffn_block(s, f):
        return jnp.where(s == 0, 0, f)

    const2 = lambda s, f: (0, 0)
    return pl.pallas_call(
        functools.partial(_mix_ffn_kernel, chunks_per_seq=chunks_per_seq),
        grid=(n_tiles + 1, D_FF // FFN_TF),
        in_specs=[
            pl.BlockSpec((MIX_TM, D_MODEL), lambda s, f: (tile(s, f), 0)),
            pl.BlockSpec((D_MODEL, D_MODEL), const2, pipeline_mode=pl.Buffered(1)),
            pl.BlockSpec((D_MODEL, FFN_TF), lambda s, f: (0, ffn_block(s, f))),
            pl.BlockSpec((D_MODEL, FFN_TF), lambda s, f: (0, ffn_block(s, f))),
            pl.BlockSpec((FFN_TF, D_MODEL), lambda s, f: (ffn_block(s, f), 0)),
            pl.BlockSpec((PACK_ROWS, D_MODEL), const2),
            pl.BlockSpec((CHUNK_LEN, D_CHUNK), lambda s, f: (chunk(s, f), U_BLK)),
            pl.BlockSpec((CHUNK_LEN, D_CHUNK), lambda s, f: (chunk(s, f), V_BLK)),
            pl.BlockSpec((CHUNK_HEADS, CHUNK_LEN, CHUNK_LEN), lambda s, f: (0, 0, 0),
                         pipeline_mode=pl.Buffered(1)),
            pl.BlockSpec((CHUNK_LEN, CHUNK_HEADS), const2),
            pl.BlockSpec((SSD_CHUNK, CONV_DIM), lambda s, f: (chunk(s, f), XBC_BLK)),
            pl.BlockSpec((SSD_CHUNK, D_SSM), lambda s, f: (chunk(s, f), Z_BLK)),
            pl.BlockSpec((SSD_CHUNK, HEAD_PAD), lambda s, f: (chunk(s, f), 0)),
            pl.BlockSpec((n_s, D_MODEL), const2, pipeline_mode=pl.Buffered(1)),
            pl.BlockSpec((n_s, D_CHUNK), const2, pipeline_mode=pl.Buffered(1)),
            pl.BlockSpec((n_s, D_SSM), const2, pipeline_mode=pl.Buffered(1)),
        ],
        out_specs=[
            pl.BlockSpec((MIX_TM, D_MODEL), lambda s, f: (tile(s, f), 0)),
            pl.BlockSpec((None, CONV_W - 1, CONV_DIM), lambda s, f: (seq_of(s, f), 0, 0)),
            pl.BlockSpec((None, D_SSM, D_STATE), lambda s, f: (seq_of(s, f), 0, 0)),
            pl.BlockSpec((n_s, D_MODEL), const2),
        ],
        out_shape=[
            jax.ShapeDtypeStruct((m, D_MODEL), F32),
            jax.ShapeDtypeStruct((batch, CONV_W - 1, CONV_DIM), F32),
            jax.ShapeDtypeStruct((batch, D_SSM, D_STATE), F32),
            jax.ShapeDtypeStruct((n_s, D_MODEL), F32),
        ],
        scratch_shapes=[
            pltpu.VMEM((MIX_TM, D_MODEL), BF16),
            pltpu.VMEM((MIX_TM, D_CHUNK), BF16),
            pltpu.VMEM((MIX_TM, D_SSM), BF16),
            pltpu.VMEM((CONV_PAD, CONV_DIM), F32),
            pltpu.VMEM((D_STATE, D_SSM), F32),
            pltpu.VMEM((CHUNK_LEN, CHUNK_HEADS * CHUNK_LEN), BF16),
            pltpu.VMEM((n_s, D_MODEL), BF16),
            pltpu.VMEM((SSD_CHUNK, N_SPLIT * SSD_CHUNK), BF16),
        ],
        compiler_params=_params("arbitrary", "arbitrary"),
        name="mix_ffn",
    )(x, w_out_b, w_gate, w_up, w_down, pack, proj, proj, ws, bst, proj, proj, dt_raw, xs, ya_s, yb_s)


def kernel(x_prompt, x_sample, state_conv, state_ssm, norm_mix_g, w_in, chunk_v_norm_g, chunk_w_s,
           chunk_b_s, chunk_out_norm_g, ssd_conv_w, ssd_conv_b, ssd_dt_bias, ssd_a_log, ssd_d,
           ssd_norm_g, w_out, norm_ffn_g, w_gate, w_up, w_down, norm_final_g):
    depth = w_in.shape[0]
    batch, seq, _ = x_prompt.shape
    n_s = x_sample.shape[0]
    assert x_sample.shape[1] == 1 and seq % SSD_CHUNK == 0 and depth == 1

    hp = x_prompt.reshape(batch * seq, D_MODEL)
    hs = x_sample.reshape(n_s, D_MODEL)
    l = 0
    w_in_t = jnp.swapaxes(w_in, 1, 2)
    w_dt_t = jnp.pad(w_in_t[l, D_MAIN:, :], ((0, HEAD_PAD - SSM_HEADS), (0, 0))).astype(BF16)
    pack = _pack_params(
        g_mix=norm_mix_g[l], g_ffn=norm_ffn_g[l], g_final=norm_final_g,
        gv=chunk_v_norm_g[l], go=chunk_out_norm_g[l],
        ng=ssd_norm_g[l], drep=jnp.repeat(ssd_d[l], SSM_HEAD_DIM),
        cb=ssd_conv_b[l], cw=ssd_conv_w[l], dtb=ssd_dt_bias[l], alog=ssd_a_log[l],
        w0=jnp.repeat(chunk_w_s[l, :, 0, 0], CHUNK_HEAD_DIM),
        b0=jnp.repeat(chunk_b_s[l, :, 0], CHUNK_HEAD_DIM))

    proj_p, dt_p, proj_s, dt_s, w_gate_b, w_up_b, w_down_b, w_out_b = _in_proj(
        hp, hs, pack, w_in_t, l, w_dt_t, tm=1024, cast=(w_gate, w_up, w_down, w_out))

    ya_s, v_s, conv_s, split, b_s, c_s, dx_s = _sample_rows(
        proj_s, dt_s, state_conv[l].reshape(n_s, (CONV_W - 1) * CONV_DIM), pack)
    ssm_s, yb_s = _sample_state(split, b_s, c_s, state_ssm[l].reshape(n_s, D_SSM, D_STATE),
                                dx_s, proj_s, pack)

    y_p, conv_p, ssm_p, y_s = _mix_ffn(
        hp, proj_p, dt_p, w_out_b, w_gate_b, w_up_b, w_down_b, pack,
        chunk_w_s[l].astype(F32), chunk_b_s[l].astype(F32).T, hs, ya_s, yb_s, batch, seq)

    return (
        y_p.reshape(batch, seq, D_MODEL),
        y_s.reshape(n_s, 1, D_MODEL),
        conv_p[None],
        ssm_p.reshape(1, batch, SSM_HEADS, SSM_HEAD_DIM, D_STATE),
        conv_s.reshape(1, n_s, CONV_W - 1, CONV_DIM),
        ssm_s.reshape(1, n_s, SSM_HEADS, SSM_HEAD_DIM, D_STATE),
        v_s.reshape(1, n_s, 1, D_CHUNK),
    )
```

```python
import functools

import jax
import jax.numpy as jnp
from jax import lax
from jax.experimental import pallas as pl
from jax.experimental.pallas import tpu as pltpu

F32 = jnp.float32
BF16 = jnp.bfloat16
HIGHEST = lax.Precision.HIGHEST

D_MODEL = 2048
D_CHUNK = 1024
CHUNK_HEADS = 8
CHUNK_HEAD_DIM = 128
CHUNK_LEN = 128
D_SSM = 1024
SSM_HEAD_DIM = 64
SSM_HEADS = 16
SSM_GROUPS = 2
HEADS_PER_GROUP = SSM_HEADS // SSM_GROUPS
GROUP_WIDTH = D_SSM // SSM_GROUPS
D_STATE = 128
CONV_W = 4
CONV_DIM = D_SSM + 2 * SSM_GROUPS * D_STATE
SSD_CHUNK = 128
D_MAIN = 2 * D_CHUNK + D_SSM + CONV_DIM
D_FF = 5632
EPS = 1e-6

LANES = 128
SUBLANES = 8
BF16_SUBLANES = 16
HEAD_PAD = LANES
N_SPLIT = 3
VMEM_LIMIT = 58 * 1024 * 1024

U_BLK, V_BLK, Z_BLK = 0, 1, 2
XBC_BLK = 2


def _rms(x, g):
    return x * lax.rsqrt(jnp.mean(x * x, axis=-1, keepdims=True) + EPS) * g


def _silu(x):
    return x * jax.nn.sigmoid(x)


def _softplus(x):
    return jnp.maximum(x, 0.0) + jnp.log1p(jnp.exp(-jnp.abs(x)))


def _dot(a, b):
    return jnp.dot(a, b, preferred_element_type=F32)


def _dot_nt(a, b):
    return lax.dot_general(a, b, (((1,), (1,)), ((), ())), preferred_element_type=F32)


def _dot_exact(a, b):
    return jnp.dot(a, b, precision=HIGHEST, preferred_element_type=F32)


def _params(*semantics):
    return pltpu.CompilerParams(dimension_semantics=semantics, vmem_limit_bytes=VMEM_LIMIT)


PACK_ROWS = 16
PACK_LAYOUT = {
    "g_mix": (0, 1, 0, D_MODEL),
    "g_ffn": (1, 1, 0, D_MODEL),
    "g_final": (2, 1, 0, D_MODEL),
    "gv": (3, 1, 0, D_CHUNK),
    "go": (3, 1, D_CHUNK, D_CHUNK),
    "ng": (4, 1, 0, D_SSM),
    "drep": (4, 1, D_SSM, D_SSM),
    "cb": (5, 1, 0, CONV_DIM),
    "cw": (6, CONV_W, 0, CONV_DIM),
    "dtb": (10, 1, 0, HEAD_PAD),
    "alog": (10, 1, HEAD_PAD, HEAD_PAD),
    "w0": (11, 1, 0, D_CHUNK),
    "b0": (11, 1, D_CHUNK, D_CHUNK),
}


class _Packed:
    def __init__(self, ref, name):
        self.ref = ref
        self.row, self.rows, self.lane, self.lanes = PACK_LAYOUT[name]

    def __getitem__(self, idx):
        lanes = slice(self.lane, self.lane + self.lanes)
        if idx is Ellipsis:
            return self.ref[self.row:self.row + self.rows, lanes]
        rows, cols = idx
        assert cols == slice(None)
        return self.ref[self.row + rows.start:self.row + rows.stop, lanes]


def _pack_params(**vectors):
    segments = [[] for _ in range(PACK_ROWS)]
    for name, value in vectors.items():
        row, n_rows, lane, lanes = PACK_LAYOUT[name]
        value = value.astype(F32).reshape(n_rows, -1)
        assert value.shape[1] <= lanes
        for r in range(n_rows):
            segments[row + r].append((lane, lanes, value[r]))
    rows = []
    for segs in segments:
        parts, at = [], 0
        for lane, lanes, value in sorted(segs, key=lambda seg: seg[0]):
            parts += [jnp.zeros((lane - at,), F32), value, jnp.zeros((lanes - value.shape[0],), F32)]
            at = lane + lanes
        parts.append(jnp.zeros((D_MODEL - at,), F32))
        rows.append(jnp.concatenate(parts))
    return jnp.stack(rows)


def _head_expander(width, log2_per_head):
    row = lax.broadcasted_iota(jnp.int32, (HEAD_PAD, width), 0)
    col = lax.broadcasted_iota(jnp.int32, (HEAD_PAD, width), 1)
    return (row == jnp.right_shift(col, log2_per_head)).astype(F32)


IN_TN = 512


def _in_proj_kernel(x_hbm, xs_hbm, pack_ref, w_ref, wdt_ref, *refs, tm, n_cast):
    g_ref = _Packed(pack_ref, "g_mix")
    cast_in = refs[:n_cast]
    o_ref, dt_ref, os_ref, dts_ref = refs[n_cast:n_cast + 4]
    cast_out = refs[n_cast + 4:2 * n_cast + 4]
    xn_ref, wres_ref, xbuf_ref, sem = refs[2 * n_cast + 4:]
    i = pl.program_id(0)
    j = pl.program_id(1)
    last = pl.num_programs(0) - 1
    n_s = xs_hbm.shape[0]

    for src, dst in zip(cast_in, cast_out, strict=True):
        dst[...] = src[...].astype(BF16)

    def x_copy(tile):
        return pltpu.make_async_copy(x_hbm.at[pl.ds(tile * tm, tm), :], xbuf_ref, sem)

    def xs_copy():
        return pltpu.make_async_copy(xs_hbm, xbuf_ref.at[pl.ds(0, n_s), :], sem)

    @pl.when((i == 0) & (j == 0))
    def _():
        xs_copy().start()
        xs_copy().wait()
        nb = _rms(xbuf_ref[0:n_s, :], g_ref[...]).astype(BF16)
        xn_ref[tm:tm + n_s, :] = nb
        dts_ref[...] = _dot_nt(nb, wdt_ref[...])
        x_copy(0).start()

    @pl.when(i == 0)
    def _():
        wres_ref[j] = w_ref[...].T.astype(BF16)

    @pl.when(j == 0)
    def _():
        x_copy(i).wait()
        nb = _rms(xbuf_ref[...], g_ref[...]).astype(BF16)
        xn_ref[0:tm, :] = nb
        dt_ref[...] = _dot_nt(nb, wdt_ref[...])

    @pl.when((j == 1) & (i < last))
    def _():
        x_copy(i + 1).start()

    @pl.when(i < last)
    def _():
        o_ref[...] = _dot(xn_ref[0:tm, :], wres_ref[j])

    @pl.when(i == last)
    def _():
        res = _dot(xn_ref[...], wres_ref[j])
        o_ref[...] = res[0:tm, :]
        os_ref[...] = res[tm:, :]


def _in_proj(x, xs, pack, w_in_t, layer, w_dt_t, tm, cast=()):
    m = x.shape[0]
    n_s = xs.shape[0]
    n_i = m // tm
    n_j = D_MAIN // IN_TN
    assert n_j >= 2 and n_s <= tm and n_s % BF16_SUBLANES == 0
    cast_in, cast_out, cast_shapes = [], [], []
    n_steps = n_i * n_j

    def col_sample(i, j):
        return jnp.where(i == n_i - 1, j, 0)

    for w in cast:
        _, w_rows, w_cols = w.shape
        slab_rows = next(r for r in range(BF16_SUBLANES, w_rows + 1, BF16_SUBLANES)
                         if w_rows % r == 0 and w_rows // r <= n_steps)
        n_slabs = w_rows // slab_rows
        hold = n_steps // n_slabs
        slab = lambda i, j, n_slabs=n_slabs, hold=hold: jnp.minimum((i * n_j + j) // hold, n_slabs - 1)
        cast_in.append(pl.BlockSpec((None, slab_rows, w_cols), lambda i, j, slab=slab: (layer, slab(i, j), 0)))
        cast_out.append(pl.BlockSpec((slab_rows, w_cols), lambda i, j, slab=slab: (slab(i, j), 0)))
        cast_shapes.append(jax.ShapeDtypeStruct((w_rows, w_cols), BF16))
    return pl.pallas_call(
        functools.partial(_in_proj_kernel, tm=tm, n_cast=len(cast)),
        grid=(n_i, n_j),
        in_specs=[
            pl.BlockSpec(memory_space=pl.ANY),
            pl.BlockSpec(memory_space=pl.ANY),
            pl.BlockSpec((PACK_ROWS, D_MODEL), lambda i, j: (0, 0)),
            pl.BlockSpec((None, IN_TN, D_MODEL), lambda i, j: (layer, jnp.where(i == 0, j, n_j - 1), 0)),
            pl.BlockSpec((HEAD_PAD, D_MODEL), lambda i, j: (0, 0)),
            *cast_in,
        ],
        out_specs=[
            pl.BlockSpec((tm, IN_TN), lambda i, j: (i, j)),
            pl.BlockSpec((tm, HEAD_PAD), lambda i, j: (i, 0)),
            pl.BlockSpec((n_s, IN_TN), lambda i, j: (0, col_sample(i, j))),
            pl.BlockSpec((n_s, HEAD_PAD), lambda i, j: (0, 0)),
            *cast_out,
        ],
        out_shape=[
            jax.ShapeDtypeStruct((m, D_MAIN), F32),
            jax.ShapeDtypeStruct((m, HEAD_PAD), F32),
            jax.ShapeDtypeStruct((n_s, D_MAIN), F32),
            jax.ShapeDtypeStruct((n_s, HEAD_PAD), F32),
            *cast_shapes,
        ],
        scratch_shapes=[
            pltpu.VMEM((tm + n_s, D_MODEL), BF16),
            pltpu.VMEM((n_j, D_MODEL, IN_TN), BF16),
            pltpu.VMEM((tm, D_MODEL), F32),
            pltpu.SemaphoreType.DMA(()),
        ],
        compiler_params=_params("arbitrary", "arbitrary"),
        name="in_proj",
    )(x, xs, pack, w_in_t, w_dt_t, *cast)


def _mask_chunk_weights(ws_ref, wm_ref):
    row = lax.broadcasted_iota(jnp.int32, (CHUNK_LEN, CHUNK_LEN), 0)
    col = lax.broadcasted_iota(jnp.int32, (CHUNK_LEN, CHUNK_LEN), 1)
    for h in range(CHUNK_HEADS):
        wm_ref[:, h * CHUNK_LEN:(h + 1) * CHUNK_LEN] = jnp.where(row >= col, ws_ref[h], 0.0).astype(BF16)


def _chunk_gate(u_ref, v_ref, gv_ref, wm_ref, bst_ref, go_ref, store_y):
    vb = _rms(jax.nn.gelu(v_ref[...]), gv_ref[...]).astype(BF16)
    yield
    u = jax.nn.gelu(u_ref[...])
    yield
    zeros = jnp.zeros((CHUNK_LEN, CHUNK_HEAD_DIM), BF16)
    parts = []
    for h0 in range(0, CHUNK_HEADS, 2):
        w = wm_ref[:, h0 * CHUNK_LEN:(h0 + 2) * CHUNK_LEN]
        va = vb[:, h0 * CHUNK_HEAD_DIM:(h0 + 1) * CHUNK_HEAD_DIM]
        vb2 = vb[:, (h0 + 1) * CHUNK_HEAD_DIM:(h0 + 2) * CHUNK_HEAD_DIM]
        rhs = jnp.concatenate([jnp.concatenate([va, zeros], axis=1),
                               jnp.concatenate([zeros, vb2], axis=1)], axis=0)
        mixed = _dot(w, rhs)
        for k, h in enumerate((h0, h0 + 1)):
            sl = slice(h * CHUNK_HEAD_DIM, (h + 1) * CHUNK_HEAD_DIM)
            mh = mixed[:, k * CHUNK_HEAD_DIM:(k + 1) * CHUNK_HEAD_DIM] + bst_ref[:, h:h + 1]
            parts.append(u[:, sl] * mh)
        yield
    y = jnp.concatenate(parts, axis=1)
    store_y(_rms(y, go_ref[...]).astype(BF16))


CONV_PAD = 8


def _group_norm_out(y, z, g):
    y = y * _silu(z)
    parts = []
    for grp in range(SSM_GROUPS):
        yg = y[:, grp * GROUP_WIDTH:(grp + 1) * GROUP_WIDTH]
        parts.append(yg * lax.rsqrt(jnp.mean(yg * yg, axis=-1, keepdims=True) + EPS))
    return (jnp.concatenate(parts, axis=1) * g).astype(BF16)


def _split3(x):
    p1 = x.astype(BF16)
    r1 = x - p1.astype(F32)
    p2 = r1.astype(BF16)
    p3 = (r1 - p2.astype(F32)).astype(BF16)
    return p1, p2, p3


def _select_rows(sel3, x):
    return _dot(sel3, jnp.concatenate(_split3(x), axis=0))


def _ssd_chunk(first, xbc_ref, z_ref, dt_ref, cw_ref, cb_ref, dtb_ref, alog_ref, drep_ref, ng_ref,
               tril3_ref, nconv_ref, nssm_ref, tail_ref, ht_ref, store_y):
    q = SSD_CHUNK

    xbc = xbc_ref[...]
    tail = jnp.where(first, 0.0, tail_ref[...])
    ht_prev = jnp.where(first, 0.0, ht_ref[...])
    sub = lax.broadcasted_iota(jnp.int32, (CONV_PAD, CONV_DIM), 0)
    conv = cb_ref[...] + xbc * cw_ref[CONV_W - 1:CONV_W, :]
    for k in range(1, CONV_W):
        shifted = pltpu.roll(xbc, k, 0)
        head = jnp.where(sub < k, pltpu.roll(tail, k, 0), shifted[0:CONV_PAD, :])
        shifted = jnp.concatenate([head, shifted[CONV_PAD:, :]], axis=0)
        conv = conv + shifted * cw_ref[CONV_W - 1 - k:CONV_W - k, :]
    tail_ref[...] = xbc[q - CONV_PAD:, :]
    nconv_ref[...] = xbc[q - (CONV_W - 1):, :]
    yield

    act = _silu(conv)
    xs = act[:, :D_SSM]
    bmat = act[:, D_SSM:D_SSM + SSM_GROUPS * D_STATE]
    cmat = act[:, D_SSM + SSM_GROUPS * D_STATE:]
    yield

    dt = _softplus(dt_ref[...] + dtb_ref[...])
    a = -jnp.exp(alog_ref[...])
    acum = _select_rows(tril3_ref[...], dt * a)
    acum_t = acum.T
    low_lanes = lax.broadcasted_iota(jnp.int32, (q, 2 * SSM_HEAD_DIM), 1) < SSM_HEAD_DIM

    def expand(cols):
        pairs = []
        for h0 in range(0, SSM_HEADS, 2):
            a = jnp.broadcast_to(cols[:, h0:h0 + 1], (q, 2 * SSM_HEAD_DIM))
            b = jnp.broadcast_to(cols[:, h0 + 1:h0 + 2], (q, 2 * SSM_HEAD_DIM))
            pairs.append(jnp.where(low_lanes, a, b))
        return jnp.concatenate(pairs, axis=1)

    dt_rep = expand(dt)
    yield
    acum_rep = expand(acum)
    yield
    xdt = xs * dt_rep
    exp_a = jnp.exp(acum_rep)
    xw = xdt * jnp.exp(acum_rep[q - 1:q, :] - acum_rep)
    yield

    row = lax.broadcasted_iota(jnp.int32, (q, q), 0)
    col = lax.broadcasted_iota(jnp.int32, (q, q), 1)
    causal = row >= col
    low_half = col < SSM_HEAD_DIM
    ydiag, yoff, states = [], [], []
    for grp in range(SSM_GROUPS):
        bg_f = bmat[:, grp * D_STATE:(grp + 1) * D_STATE]
        bg = bg_f.astype(BF16)
        cg = cmat[:, grp * D_STATE:(grp + 1) * D_STATE].astype(BF16)
        cb = _dot_nt(cg, bg)
        ch = slice(grp * GROUP_WIDTH, (grp + 1) * GROUP_WIDTH)
        yoff.append(_dot(cg, ht_prev[:, ch].astype(BF16)))
        yield
        for quad in range(HEADS_PER_GROUP // 4):
            h0 = grp * HEADS_PER_GROUP + 4 * quad
            masks, blocks = [], []
            for k, h in enumerate(range(h0, h0 + 4)):
                diff = acum[:, h:h + 1] - acum_t[h:h + 1, :]
                masks.append((cb * jnp.exp(jnp.where(causal, diff, -jnp.inf))).astype(BF16))
                xp = xdt[:, (h - k % 2) * SSM_HEAD_DIM:(h - k % 2 + 2) * SSM_HEAD_DIM]
                own = jnp.where(low_half == (k % 2 == 0), xp, 0.0).astype(BF16)
                zero = jnp.zeros_like(own)
                blocks.append(jnp.concatenate([own, zero] if k < 2 else [zero, own], axis=1))
                if k % 2 == 1:
                    yield
            ydiag.append(_dot(jnp.concatenate(masks, axis=1), jnp.concatenate(blocks, axis=0)))
            yield
        states.append(_dot(bg_f.T.astype(BF16), xw[:, ch].astype(BF16)))
        yield
    y = drep_ref[...] * xs + jnp.concatenate(ydiag, axis=1) + jnp.concatenate(yoff, axis=1) * exp_a

    ht_new = ht_prev * exp_a[q - 1:q, :] + jnp.concatenate(states, axis=1)
    ht_ref[...] = ht_new
    nssm_ref[...] = ht_new.T
    yield
    store_y(_group_norm_out(y, z_ref[...], ng_ref[...]))


def _sample_rows_kernel(u_ref, v_ref, xbc_ref, dt_ref, sc_ref, pack_ref,
                        ya_ref, vrows_ref, nconv_ref, split_ref, b_ref, c_ref, dx_ref):
    gv_ref, w0_ref, b0_ref, go_ref, cw_ref, cb_ref, dtb_ref, alog_ref, drep_ref = (
        _Packed(pack_ref, name) for name in ("gv", "w0", "b0", "go", "cw", "cb", "dtb", "alog", "drep"))
    vr =_rms(jax.nn.gelu(v_ref[...]), gv_ref[...])
    vrows_ref[...] = vr
    mixed = vr * w0_ref[...] + b0_ref[...]
    ya_ref[...] = _rms(jax.nn.gelu(u_ref[...]) * mixed, go_ref[...]).astype(BF16)

    xbc = xbc_ref[...]
    conv = cb_ref[...]
    for k in range(CONV_W - 1):
        conv = conv + sc_ref[:, k * CONV_DIM:(k + 1) * CONV_DIM] * cw_ref[k:k + 1, :]
    conv = conv + xbc * cw_ref[CONV_W - 1:CONV_W, :]
    nconv_ref[:, 0:(CONV_W - 2) * CONV_DIM] = sc_ref[:, CONV_DIM:(CONV_W - 1) * CONV_DIM]
    nconv_ref[:, (CONV_W - 2) * CONV_DIM:] = xbc

    act = _silu(conv)
    xs = act[:, :D_SSM]
    b_ref[...] = act[:, D_SSM:D_SSM + SSM_GROUPS * D_STATE]
    c_ref[...] = act[:, D_SSM + SSM_GROUPS * D_STATE:]
    dx_ref[...] = drep_ref[...] * xs

    dt = _softplus(dt_ref[...] + dtb_ref[...])
    decay = jnp.exp(dt * (-jnp.exp(alog_ref[...])))
    xdt = xs * _dot_exact(dt, _head_expander(D_SSM, SSM_HEAD_DIM.bit_length() - 1))
    decay8 = _dot_exact(decay, _head_expander(HEAD_PAD, SUBLANES.bit_length() - 1))
    cols = jnp.concatenate([xdt, decay8], axis=1).T
    for p in range(N_SPLIT):
        piece = cols.astype(BF16)
        split_ref[:, p * LANES:(p + 1) * LANES] = piece
        cols = cols - piece.astype(F32)


SPLIT_ROWS = D_SSM + HEAD_PAD


def _sample_rows(proj, dt_raw, sconv, pack):
    n = proj.shape[0]
    full = lambda shape: pl.BlockSpec(shape, lambda i: (0,) * len(shape))
    return pl.pallas_call(
        _sample_rows_kernel,
        grid=(1,),
        in_specs=[
            pl.BlockSpec((n, D_CHUNK), lambda i: (0, U_BLK)),
            pl.BlockSpec((n, D_CHUNK), lambda i: (0, V_BLK)),
            pl.BlockSpec((n, CONV_DIM), lambda i: (0, XBC_BLK)),
            full((n, HEAD_PAD)),
            full((n, (CONV_W - 1) * CONV_DIM)),
            full((PACK_ROWS, D_MODEL)),
        ],
        out_specs=[
            full((n, D_CHUNK)), full((n, D_CHUNK)), full((n, (CONV_W - 1) * CONV_DIM)),
            full((SPLIT_ROWS, N_SPLIT * LANES)),
            full((n, SSM_GROUPS * D_STATE)), full((n, SSM_GROUPS * D_STATE)), full((n, D_SSM)),
        ],
        out_shape=[
            jax.ShapeDtypeStruct((n, D_CHUNK), BF16),
            jax.ShapeDtypeStruct((n, D_CHUNK), F32),
            jax.ShapeDtypeStruct((n, (CONV_W - 1) * CONV_DIM), F32),
            jax.ShapeDtypeStruct((SPLIT_ROWS, N_SPLIT * LANES), BF16),
            jax.ShapeDtypeStruct((n, SSM_GROUPS * D_STATE), F32),
            jax.ShapeDtypeStruct((n, SSM_GROUPS * D_STATE), F32),
            jax.ShapeDtypeStruct((n, D_SSM), F32),
        ],
        compiler_params=_params("arbitrary"),
        name="sample_rows",
    )(proj, proj, proj, dt_raw, sconv, pack)


SAMPLE_TB = 16


def _sample_state_kernel(split_ref, b_ref, c_ref, h0_ref, dx_ref, z_ref, pack_ref,
                         hn_ref, y_ref, yt_ref):
    ng_ref = _Packed(pack_ref, "ng")
    i = pl.program_id(0)
    n_tok = b_ref.shape[0]

    @pl.when(i == 0)
    def _():
        yt_ref[...] = jnp.zeros_like(yt_ref)

    tok = lax.broadcasted_iota(jnp.int32, (N_SPLIT * LANES, D_STATE), 0) % LANES
    lane = lax.broadcasted_iota(jnp.int32, (D_SSM, n_tok), 1)

    def per_group(row, grp):
        return jnp.broadcast_to(row[:, grp * D_STATE:(grp + 1) * D_STATE], (GROUP_WIDTH, D_STATE))

    for k in range(SAMPLE_TB):
        t = i * SAMPLE_TB + k
        onehot = (tok == t).astype(BF16)
        bcast = _dot(split_ref[...], onehot)
        decay = jnp.broadcast_to(
            bcast[D_SSM:, :].reshape(SSM_HEADS, 1, SUBLANES, D_STATE),
            (SSM_HEADS, SSM_HEAD_DIM // SUBLANES, SUBLANES, D_STATE)).reshape(D_SSM, D_STATE)
        brow = b_ref[pl.ds(t, 1), :]
        crow = c_ref[pl.ds(t, 1), :]
        bfull = jnp.concatenate([per_group(brow, g) for g in range(SSM_GROUPS)], axis=0)
        cfull = jnp.concatenate([per_group(crow, g) for g in range(SSM_GROUPS)], axis=0)
        h_new = h0_ref[k] * decay + bcast[:D_SSM, :] * bfull
        hn_ref[k] = h_new
        ysum = jnp.sum(h_new * cfull, axis=-1, keepdims=True)
        yt_ref[...] = jnp.where(lane == t, ysum, yt_ref[...])

    @pl.when(i == pl.num_programs(0) - 1)
    def _():
        y = yt_ref[...].T + dx_ref[...]
        y_ref[...] = _group_norm_out(y, z_ref[...], ng_ref[...])


def _sample_state(split, bmat, cmat, h0, dx, proj, pack):
    n = h0.shape[0]
    const2 = lambda i: (0, 0)
    return pl.pallas_call(
        _sample_state_kernel,
        grid=(n // SAMPLE_TB,),
        in_specs=[
            pl.BlockSpec((SPLIT_ROWS, N_SPLIT * LANES), const2),
            pl.BlockSpec((n, SSM_GROUPS * D_STATE), const2),
            pl.BlockSpec((n, SSM_GROUPS * D_STATE), const2),
            pl.BlockSpec((SAMPLE_TB, D_SSM, D_STATE), lambda i: (i, 0, 0)),
            pl.BlockSpec((n, D_SSM), const2),
            pl.BlockSpec((n, D_SSM), lambda i: (0, Z_BLK)),
            pl.BlockSpec((PACK_ROWS, D_MODEL), const2),
        ],
        out_specs=[
            pl.BlockSpec((SAMPLE_TB, D_SSM, D_STATE), lambda i: (i, 0, 0)),
            pl.BlockSpec((n, D_SSM), const2),
        ],
        out_shape=[
            jax.ShapeDtypeStruct((n, D_SSM, D_STATE), F32),
            jax.ShapeDtypeStruct((n, D_SSM), BF16),
        ],
        scratch_shapes=[pltpu.VMEM((D_SSM, n), F32)],
        compiler_params=_params("arbitrary"),
        name="sample_state",
    )(split, bmat, cmat, h0, dx, proj, pack)


FFN_TF = 512
MIX_TM = 512
CHUNKS_PER_TILE = MIX_TM // SSD_CHUNK
FFN_UP_PIECE = 256
FFN_DOWN_PIECE = 512
MIX_PHASES_PER_FFN_PIECE = 5
MIX_LEAD_PIECES = 0


def _interleave(main, side, side_per_main, lead=0):
    for k, _ in enumerate(main):
        if k >= lead:
            for _ in range(side_per_main):
                next(side, None)
    for _ in side:
        pass


def _mix_ffn_kernel(x_ref, wo_ref, wg_ref, wu_ref, wd_ref, pack_ref, u_ref, v_ref, ws_ref, bst_ref,
                    xbc_ref, z_ref, dt_ref, xs_ref, yas_ref, ybs_ref,
                    o_ref, nconv_ref, nssm_ref, os_ref,
                    m_ref, ya_ref, yb_ref, tail_ref, ht_ref, wm_ref, ms_ref, tril3_ref, *, chunks_per_seq):
    g_ref, gf_ref, gv_ref, go_ref, cw_ref, cb_ref, dtb_ref, alog_ref, drep_ref, ng_ref = (
        _Packed(pack_ref, name)
        for name in ("g_ffn", "g_final", "gv", "go", "cw", "cb", "dtb", "alog", "drep", "ng"))
    s = pl.program_id(0)
    f = pl.program_id(1)
    has_ffn = s >= 1
    is_last = s == pl.num_programs(0) - 1
    has_mix = jnp.logical_not(is_last) & (f < CHUNKS_PER_TILE)

    @pl.when((s == 0) & (f == 0))
    def _():
        _mask_chunk_weights(ws_ref, wm_ref)
        row = lax.broadcasted_iota(jnp.int32, tril3_ref.shape, 0)
        col = lax.broadcasted_iota(jnp.int32, tril3_ref.shape, 1)
        tril3_ref[...] = (row >= jnp.bitwise_and(col, SSD_CHUNK - 1)).astype(BF16)

    def project(x, ya, yb, m_out, o_out):
        h = x + _dot(jnp.concatenate([ya, yb], axis=1), wo_ref[...])
        m_out[...] = _rms(h, g_ref[...]).astype(BF16)
        o_out[...] = h

    @pl.when(has_ffn & (f == 0))
    def _():
        project(x_ref[...], ya_ref[...], yb_ref[...], m_ref, o_ref)

    @pl.when(is_last & (f == 0))
    def _():
        project(xs_ref[...], yas_ref[...], ybs_ref[...], ms_ref, os_ref)

    def ffn_step(with_sample=False):
        m = jnp.concatenate([m_ref[...], ms_ref[...]], axis=0) if with_sample else m_ref[...]
        acts = []
        for c in range(FFN_TF // FFN_UP_PIECE):
            cols = slice(c * FFN_UP_PIECE, (c + 1) * FFN_UP_PIECE)
            halves = []
            for w_ref in (wg_ref, wu_ref):
                halves.append(_dot(m, w_ref[:, cols]))
                yield
            acts.append((_silu(halves[0]) * halves[1]).astype(BF16))
        act = jnp.concatenate(acts, axis=1)
        for c in range(D_MODEL // FFN_DOWN_PIECE):
            cols = slice(c * FFN_DOWN_PIECE, (c + 1) * FFN_DOWN_PIECE)
            res = _dot(act, wd_ref[:, cols])
            o_ref[:, cols] += res[:MIX_TM]
            if with_sample:
                os_ref[:, cols] += res[MIX_TM:]
            yield

    chunk_rows = pl.ds(pl.multiple_of(f * SSD_CHUNK, SSD_CHUNK), SSD_CHUNK)

    def store_ya(ya):
        ya_ref[chunk_rows, :] = ya

    def store_yb(yb):
        yb_ref[chunk_rows, :] = yb

    def mix_step():
        first = (s * CHUNKS_PER_TILE + f) % chunks_per_seq == 0
        yield from _ssd_chunk(first, xbc_ref, z_ref, dt_ref, cw_ref, cb_ref, dtb_ref, alog_ref, drep_ref,
                              ng_ref, tril3_ref, nconv_ref, nssm_ref, tail_ref, ht_ref, store_yb)
        yield
        yield from _chunk_gate(u_ref, v_ref, gv_ref, wm_ref, bst_ref, go_ref, store_ya)

    @pl.when(has_ffn & has_mix)
    def _():
        _interleave(ffn_step(), mix_step(), MIX_PHASES_PER_FFN_PIECE, MIX_LEAD_PIECES)

    @pl.when(has_ffn & jnp.logical_not(has_mix) & jnp.logical_not(is_last))
    def _():
        _interleave(ffn_step(), iter(()), 0)

    @pl.when(is_last)
    def _():
        _interleave(ffn_step(with_sample=True), iter(()), 0)

    @pl.when(jnp.logical_not(has_ffn) & has_mix)
    def _():
        _interleave(iter(()), mix_step(), 0)

    @pl.when(has_ffn & (f == pl.num_programs(1) - 1))
    def _():
        o_ref[...] = _rms(o_ref[...], gf_ref[...])

    @pl.when(is_last & (f == pl.num_programs(1) - 1))
    def _():
        os_ref[...] = _rms(os_ref[...], gf_ref[...])


def _mix_ffn(x, proj, dt_raw, w_out_b, w_gate, w_up, w_down, pack, ws, bst, xs, ya_s, yb_s, batch, seq):
    m = x.shape[0]
    n_s = xs.shape[0]
    n_tiles = m // MIX_TM
    n_chunks = m // SSD_CHUNK
    chunks_per_seq = seq // SSD_CHUNK
    assert MIX_TM % SSD_CHUNK == 0 and seq % MIX_TM == 0 and CHUNKS_PER_TILE <= D_FF // FFN_TF

    def tile(s, f):
        return jnp.maximum(s - 1, 0)

    def chunk(s, f):
        return jnp.where(s >= n_tiles, n_chunks - 1,
                         s * CHUNKS_PER_TILE + jnp.minimum(f, CHUNKS_PER_TILE - 1))

    def seq_of(s, f):
        return chunk(s, f) // chunks_per_seq

    def ffn_block(s, f):
        return jnp.where(s == 0, 0, f)

    const2 = lambda s, f: (0, 0)
    return pl.pallas_call(
        functools.partial(_mix_ffn_kernel, chunks_per_seq=chunks_per_seq),
        grid=(n_tiles + 1, D_FF // FFN_TF),
        in_specs=[
            pl.BlockSpec((MIX_TM, D_MODEL), lambda s, f: (tile(s, f), 0)),
            pl.BlockSpec((D_MODEL, D_MODEL), const2, pipeline_mode=pl.Buffered(1)),
            pl.BlockSpec((D_MODEL, FFN_TF), lambda s, f: (0, ffn_block(s, f))),
            pl.BlockSpec((D_MODEL, FFN_TF), lambda s, f: (0, ffn_block(s, f))),
            pl.BlockSpec((FFN_TF, D_MODEL), lambda s, f: (ffn_block(s, f), 0)),
            pl.BlockSpec((PACK_ROWS, D_MODEL), const2),
            pl.BlockSpec((CHUNK_LEN, D_CHUNK), lambda s, f: (chunk(s, f), U_BLK)),
            pl.BlockSpec((CHUNK_LEN, D_CHUNK), lambda s, f: (chunk(s, f), V_BLK)),
            pl.BlockSpec((CHUNK_HEADS, CHUNK_LEN, CHUNK_LEN), lambda s, f: (0, 0, 0),
                         pipeline_mode=pl.Buffered(1)),
            pl.BlockSpec((CHUNK_LEN, CHUNK_HEADS), const2),
            pl.BlockSpec((SSD_CHUNK, CONV_DIM), lambda s, f: (chunk(s, f), XBC_BLK)),
            pl.BlockSpec((SSD_CHUNK, D_SSM), lambda s, f: (chunk(s, f), Z_BLK)),
            pl.BlockSpec((SSD_CHUNK, HEAD_PAD), lambda s, f: (chunk(s, f), 0)),
            pl.BlockSpec((n_s, D_MODEL), const2, pipeline_mode=pl.Buffered(1)),
            pl.BlockSpec((n_s, D_CHUNK), const2, pipeline_mode=pl.Buffered(1)),
            pl.BlockSpec((n_s, D_SSM), const2, pipeline_mode=pl.Buffered(1)),
        ],
        out_specs=[
            pl.BlockSpec((MIX_TM, D_MODEL), lambda s, f: (tile(s, f), 0)),
            pl.BlockSpec((None, CONV_W - 1, CONV_DIM), lambda s, f: (seq_of(s, f), 0, 0)),
            pl.BlockSpec((None, D_SSM, D_STATE), lambda s, f: (seq_of(s, f), 0, 0)),
            pl.BlockSpec((n_s, D_MODEL), const2),
        ],
        out_shape=[
            jax.ShapeDtypeStruct((m, D_MODEL), F32),
            jax.ShapeDtypeStruct((batch, CONV_W - 1, CONV_DIM), F32),
            jax.ShapeDtypeStruct((batch, D_SSM, D_STATE), F32),
            jax.ShapeDtypeStruct((n_s, D_MODEL), F32),
        ],
        scratch_shapes=[
            pltpu.VMEM((MIX_TM, D_MODEL), BF16),
            pltpu.VMEM((MIX_TM, D_CHUNK), BF16),
            pltpu.VMEM((MIX_TM, D_SSM), BF16),
            pltpu.VMEM((CONV_PAD, CONV_DIM), F32),
            pltpu.VMEM((D_STATE, D_SSM), F32),
            pltpu.VMEM((CHUNK_LEN, CHUNK_HEADS * CHUNK_LEN), BF16),
            pltpu.VMEM((n_s, D_MODEL), BF16),
            pltpu.VMEM((SSD_CHUNK, N_SPLIT * SSD_CHUNK), BF16),
        ],
        compiler_params=_params("arbitrary", "arbitrary"),
        name="mix_ffn",
    )(x, w_out_b, w_gate, w_up, w_down, pack, proj, proj, ws, bst, proj, proj, dt_raw, xs, ya_s, yb_s)


def kernel(x_prompt, x_sample, state_conv, state_ssm, norm_mix_g, w_in, chunk_v_norm_g, chunk_w_s,
           chunk_b_s, chunk_out_norm_g, ssd_conv_w, ssd_conv_b, ssd_dt_bias, ssd_a_log, ssd_d,
           ssd_norm_g, w_out, norm_ffn_g, w_gate, w_up, w_down, norm_final_g):
    depth = w_in.shape[0]
    batch, seq, _ = x_prompt.shape
    n_s = x_sample.shape[0]
    assert x_sample.shape[1] == 1 and seq % SSD_CHUNK == 0 and depth == 1

    hp = x_prompt.reshape(batch * seq, D_MODEL)
    hs = x_sample.reshape(n_s, D_MODEL)
    l = 0
    w_in_t = jnp.swapaxes(w_in, 1, 2)
    w_dt_t = jnp.pad(w_in_t[l, D_MAIN:, :], ((0, HEAD_PAD - SSM_HEADS), (0, 0))).astype(BF16)
    pack = _pack_params(
        g_mix=norm_mix_g[l], g_ffn=norm_ffn_g[l], g_final=norm_final_g,
        gv=chunk_v_norm_g[l], go=chunk_out_norm_g[l],
        ng=ssd_norm_g[l], drep=jnp.repeat(ssd_d[l], SSM_HEAD_DIM),
        cb=ssd_conv_b[l], cw=ssd_conv_w[l], dtb=ssd_dt_bias[l], alog=ssd_a_log[l],
        w0=jnp.repeat(chunk_w_s[l, :, 0, 0], CHUNK_HEAD_DIM),
        b0=jnp.repeat(chunk_b_s[l, :, 0], CHUNK_HEAD_DIM))

    proj_p, dt_p, proj_s, dt_s, w_gate_b, w_up_b, w_down_b, w_out_b = _in_proj(
        hp, hs, pack, w_in_t, l, w_dt_t, tm=1024, cast=(w_gate, w_up, w_down, w_out))

    ya_s, v_s, conv_s, split, b_s, c_s, dx_s = _sample_rows(
        proj_s, dt_s, state_conv[l].reshape(n_s, (CONV_W - 1) * CONV_DIM), pack)
    ssm_s, yb_s = _sample_state(split, b_s, c_s, state_ssm[l].reshape(n_s, D_SSM, D_STATE),
                                dx_s, proj_s, pack)

    y_p, conv_p, ssm_p, y_s = _mix_ffn(
        hp, proj_p, dt_p, w_out_b, w_gate_b, w_up_b, w_down_b, pack,
        chunk_w_s[l].astype(F32), chunk_b_s[l].astype(F32).T, hs, ya_s, yb_s, batch, seq)

    return (
        y_p.reshape(batch, seq, D_MODEL),
        y_s.reshape(n_s, 1, D_MODEL),
        conv_p[None],
        ssm_p.reshape(1, batch, SSM_HEADS, SSM_HEAD_DIM, D_STATE),
        conv_s.reshape(1, n_s, CONV_W - 1, CONV_DIM),
        ssm_s.reshape(1, n_s, SSM_HEADS, SSM_HEAD_DIM, D_STATE),
        v_s.reshape(1, n_s, 1, D_CHUNK),
    )
```

```python
import functools

import jax
import jax.numpy as jnp
from jax import lax
from jax.experimental import pallas as pl
from jax.experimental.pallas import tpu as pltpu

F32 = jnp.float32
BF16 = jnp.bfloat16
HIGHEST = lax.Precision.HIGHEST

D_MODEL = 2048
D_CHUNK = 1024
CHUNK_HEADS = 8
CHUNK_HEAD_DIM = 128
CHUNK_LEN = 128
D_SSM = 1024
SSM_HEAD_DIM = 64
SSM_HEADS = 16
SSM_GROUPS = 2
HEADS_PER_GROUP = SSM_HEADS // SSM_GROUPS
GROUP_WIDTH = D_SSM // SSM_GROUPS
D_STATE = 128
CONV_W = 4
CONV_DIM = D_SSM + 2 * SSM_GROUPS * D_STATE
SSD_CHUNK = 128
D_MAIN = 2 * D_CHUNK + D_SSM + CONV_DIM
D_FF = 5632
EPS = 1e-6

LANES = 128
SUBLANES = 8
BF16_SUBLANES = 16
HEAD_PAD = LANES
N_SPLIT = 3
VMEM_LIMIT = 58 * 1024 * 1024

U_BLK, V_BLK, Z_BLK = 0, 1, 2
XBC_BLK = 2


def _rms(x, g):
    return x * lax.rsqrt(jnp.mean(x * x, axis=-1, keepdims=True) + EPS) * g


def _silu(x):
    return x * jax.nn.sigmoid(x)


def _softplus(x):
    return jnp.maximum(x, 0.0) + jnp.log1p(jnp.exp(-jnp.abs(x)))


def _dot(a, b):
    return jnp.dot(a, b, preferred_element_type=F32)


def _dot_nt(a, b):
    return lax.dot_general(a, b, (((1,), (1,)), ((), ())), preferred_element_type=F32)


def _dot_exact(a, b):
    return jnp.dot(a, b, precision=HIGHEST, preferred_element_type=F32)


def _params(*semantics):
    return pltpu.CompilerParams(dimension_semantics=semantics, vmem_limit_bytes=VMEM_LIMIT)


PACK_ROWS = 16
PACK_LAYOUT = {
    "g_mix": (0, 1, 0, D_MODEL),
    "g_ffn": (1, 1, 0, D_MODEL),
    "g_final": (2, 1, 0, D_MODEL),
    "gv": (3, 1, 0, D_CHUNK),
    "go": (3, 1, D_CHUNK, D_CHUNK),
    "ng": (4, 1, 0, D_SSM),
    "drep": (4, 1, D_SSM, D_SSM),
    "cb": (5, 1, 0, CONV_DIM),
    "cw": (6, CONV_W, 0, CONV_DIM),
    "dtb": (10, 1, 0, HEAD_PAD),
    "alog": (10, 1, HEAD_PAD, HEAD_PAD),
    "w0": (11, 1, 0, D_CHUNK),
    "b0": (11, 1, D_CHUNK, D_CHUNK),
}


class _Packed:
    def __init__(self, ref, name):
        self.ref = ref
        self.row, self.rows, self.lane, self.lanes = PACK_LAYOUT[name]

    def __getitem__(self, idx):
        lanes = slice(self.lane, self.lane + self.lanes)
        if idx is Ellipsis:
            return self.ref[self.row:self.row + self.rows, lanes]
        rows, cols = idx
        assert cols == slice(None)
        return self.ref[self.row + rows.start:self.row + rows.stop, lanes]


def _pack_params(**vectors):
    segments = [[] for _ in range(PACK_ROWS)]
    for name, value in vectors.items():
        row, n_rows, lane, lanes = PACK_LAYOUT[name]
        value = value.astype(F32).reshape(n_rows, -1)
        assert value.shape[1] <= lanes
        for r in range(n_rows):
            segments[row + r].append((lane, lanes, value[r]))
    rows = []
    for segs in segments:
        parts, at = [], 0
        for lane, lanes, value in sorted(segs, key=lambda seg: seg[0]):
            parts += [jnp.zeros((lane - at,), F32), value, jnp.zeros((lanes - value.shape[0],), F32)]
            at = lane + lanes
        parts.append(jnp.zeros((D_MODEL - at,), F32))
        rows.append(jnp.concatenate(parts))
    return jnp.stack(rows)


def _head_expander(width, log2_per_head):
    row = lax.broadcasted_iota(jnp.int32, (HEAD_PAD, width), 0)
    col = lax.broadcasted_iota(jnp.int32, (HEAD_PAD, width), 1)
    return (row == jnp.right_shift(col, log2_per_head)).astype(F32)


IN_TN = 512


def _in_proj_kernel(x_hbm, xs_hbm, pack_ref, w_ref, wdt_ref, *refs, tm, n_cast):
    g_ref = _Packed(pack_ref, "g_mix")
    cast_in = refs[:n_cast]
    o_ref, dt_ref, os_ref, dts_ref = refs[n_cast:n_cast + 4]
    cast_out = refs[n_cast + 4:2 * n_cast + 4]
    xn_ref, wres_ref, xbuf_ref, sem = refs[2 * n_cast + 4:]
    i = pl.program_id(0)
    j = pl.program_id(1)
    last = pl.num_programs(0) - 1
    n_s = xs_hbm.shape[0]

    for src, dst in zip(cast_in, cast_out, strict=True):
        dst[...] = src[...].astype(BF16)

    def x_copy(tile):
        return pltpu.make_async_copy(x_hbm.at[pl.ds(tile * tm, tm), :], xbuf_ref, sem)

    def xs_copy():
        return pltpu.make_async_copy(xs_hbm, xbuf_ref.at[pl.ds(0, n_s), :], sem)

    @pl.when((i == 0) & (j == 0))
    def _():
        xs_copy().start()
        xs_copy().wait()
        nb = _rms(xbuf_ref[0:n_s, :], g_ref[...]).astype(BF16)
        xn_ref[tm:tm + n_s, :] = nb
        dts_ref[...] = _dot_nt(nb, wdt_ref[...])
        x_copy(0).start()

    @pl.when(i == 0)
    def _():
        wres_ref[j] = w_ref[...].T.astype(BF16)

    @pl.when(j == 0)
    def _():
        x_copy(i).wait()
        nb = _rms(xbuf_ref[...], g_ref[...]).astype(BF16)
        xn_ref[0:tm, :] = nb
        dt_ref[...] = _dot_nt(nb, wdt_ref[...])

    @pl.when((j == 1) & (i < last))
    def _():
        x_copy(i + 1).start()

    @pl.when(i < last)
    def _():
        o_ref[...] = _dot(xn_ref[0:tm, :], wres_ref[j])

    @pl.when(i == last)
    def _():
        res = _dot(xn_ref[...], wres_ref[j])
        o_ref[...] = res[0:tm, :]
        os_ref[...] = res[tm:, :]


def _in_proj(x, xs, pack, w_in_t, layer, w_dt_t, tm, cast=()):
    m = x.shape[0]
    n_s = xs.shape[0]
    n_i = m // tm
    n_j = D_MAIN // IN_TN
    assert n_j >= 2 and n_s <= tm and n_s % BF16_SUBLANES == 0
    cast_in, cast_out, cast_shapes = [], [], []
    n_steps = n_i * n_j

    def col_sample(i, j):
        return jnp.where(i == n_i - 1, j, 0)

    for w in cast:
        _, w_rows, w_cols = w.shape
        slab_rows = next(r for r in range(BF16_SUBLANES, w_rows + 1, BF16_SUBLANES)
                         if w_rows % r == 0 and w_rows // r <= n_steps)
        n_slabs = w_rows // slab_rows
        hold = n_steps // n_slabs
        slab = lambda i, j, n_slabs=n_slabs, hold=hold: jnp.minimum((i * n_j + j) // hold, n_slabs - 1)
        cast_in.append(pl.BlockSpec((None, slab_rows, w_cols), lambda i, j, slab=slab: (layer, slab(i, j), 0)))
        cast_out.append(pl.BlockSpec((slab_rows, w_cols), lambda i, j, slab=slab: (slab(i, j), 0)))
        cast_shapes.append(jax.ShapeDtypeStruct((w_rows, w_cols), BF16))
    return pl.pallas_call(
        functools.partial(_in_proj_kernel, tm=tm, n_cast=len(cast)),
        grid=(n_i, n_j),
        in_specs=[
            pl.BlockSpec(memory_space=pl.ANY),
            pl.BlockSpec(memory_space=pl.ANY),
            pl.BlockSpec((PACK_ROWS, D_MODEL), lambda i, j: (0, 0)),
            pl.BlockSpec((None, IN_TN, D_MODEL), lambda i, j: (layer, jnp.where(i == 0, j, n_j - 1), 0)),
            pl.BlockSpec((HEAD_PAD, D_MODEL), lambda i, j: (0, 0)),
            *cast_in,
        ],
        out_specs=[
            pl.BlockSpec((tm, IN_TN), lambda i, j: (i, j)),
            pl.BlockSpec((tm, HEAD_PAD), lambda i, j: (i, 0)),
            pl.BlockSpec((n_s, IN_TN), lambda i, j: (0, col_sample(i, j))),
            pl.BlockSpec((n_s, HEAD_PAD), lambda i, j: (0, 0)),
            *cast_out,
        ],
        out_shape=[
            jax.ShapeDtypeStruct((m, D_MAIN), F32),
            jax.ShapeDtypeStruct((m, HEAD_PAD), F32),
            jax.ShapeDtypeStruct((n_s, D_MAIN), F32),
            jax.ShapeDtypeStruct((n_s, HEAD_PAD), F32),
            *cast_shapes,
        ],
        scratch_shapes=[
            pltpu.VMEM((tm + n_s, D_MODEL), BF16),
            pltpu.VMEM((n_j, D_MODEL, IN_TN), BF16),
            pltpu.VMEM((tm, D_MODEL), F32),
            pltpu.SemaphoreType.DMA(()),
        ],
        compiler_params=_params("arbitrary", "arbitrary"),
        name="in_proj",
    )(x, xs, pack, w_in_t, w_dt_t, *cast)


def _mask_chunk_weights(ws_ref, wm_ref):
    row = lax.broadcasted_iota(jnp.int32, (CHUNK_LEN, CHUNK_LEN), 0)
    col = lax.broadcasted_iota(jnp.int32, (CHUNK_LEN, CHUNK_LEN), 1)
    for h in range(CHUNK_HEADS):
        wm_ref[:, h * CHUNK_LEN:(h + 1) * CHUNK_LEN] = jnp.where(row >= col, ws_ref[h], 0.0).astype(BF16)


def _chunk_gate(u_ref, v_ref, gv_ref, wm_ref, bst_ref, go_ref, store_y):
    vb = _rms(jax.nn.gelu(v_ref[...]), gv_ref[...]).astype(BF16)
    yield
    u = jax.nn.gelu(u_ref[...])
    yield
    zeros = jnp.zeros((CHUNK_LEN, CHUNK_HEAD_DIM), BF16)
    parts = []
    for h0 in range(0, CHUNK_HEADS, 2):
        w = wm_ref[:, h0 * CHUNK_LEN:(h0 + 2) * CHUNK_LEN]
        va = vb[:, h0 * CHUNK_HEAD_DIM:(h0 + 1) * CHUNK_HEAD_DIM]
        vb2 = vb[:, (h0 + 1) * CHUNK_HEAD_DIM:(h0 + 2) * CHUNK_HEAD_DIM]
        rhs = jnp.concatenate([jnp.concatenate([va, zeros], axis=1),
                               jnp.concatenate([zeros, vb2], axis=1)], axis=0)
        mixed = _dot(w, rhs)
        for k, h in enumerate((h0, h0 + 1)):
            sl = slice(h * CHUNK_HEAD_DIM, (h + 1) * CHUNK_HEAD_DIM)
            mh = mixed[:, k * CHUNK_HEAD_DIM:(k + 1) * CHUNK_HEAD_DIM] + bst_ref[:, h:h + 1]
            parts.append(u[:, sl] * mh)
        yield
    y = jnp.concatenate(parts, axis=1)
    store_y(_rms(y, go_ref[...]).astype(BF16))


CONV_PAD = 8


def _group_norm_out(y, z, g):
    y = y * _silu(z)
    parts = []
    for grp in range(SSM_GROUPS):
        yg = y[:, grp * GROUP_WIDTH:(grp + 1) * GROUP_WIDTH]
        parts.append(yg * lax.rsqrt(jnp.mean(yg * yg, axis=-1, keepdims=True) + EPS))
    return (jnp.concatenate(parts, axis=1) * g).astype(BF16)


def _split3(x):
    p1 = x.astype(BF16)
    r1 = x - p1.astype(F32)
    p2 = r1.astype(BF16)
    p3 = (r1 - p2.astype(F32)).astype(BF16)
    return p1, p2, p3


def _select_rows(sel3, x):
    return _dot(sel3, jnp.concatenate(_split3(x), axis=0))


def _ssd_chunk(first, xbc_ref, z_ref, dt_ref, cw_ref, cb_ref, dtb_ref, alog_ref, drep_ref, ng_ref,
               tril3_ref, nconv_ref, nssm_ref, tail_ref, ht_ref, store_y):
    q = SSD_CHUNK

    xbc = xbc_ref[...]
    tail = jnp.where(first, 0.0, tail_ref[...])
    ht_prev = jnp.where(first, 0.0, ht_ref[...])
    sub = lax.broadcasted_iota(jnp.int32, (CONV_PAD, CONV_DIM), 0)
    conv = cb_ref[...] + xbc * cw_ref[CONV_W - 1:CONV_W, :]
    for k in range(1, CONV_W):
        shifted = pltpu.roll(xbc, k, 0)
        head = jnp.where(sub < k, pltpu.roll(tail, k, 0), shifted[0:CONV_PAD, :])
        shifted = jnp.concatenate([head, shifted[CONV_PAD:, :]], axis=0)
        conv = conv + shifted * cw_ref[CONV_W - 1 - k:CONV_W - k, :]
    tail_ref[...] = xbc[q - CONV_PAD:, :]
    nconv_ref[...] = xbc[q - (CONV_W - 1):, :]
    yield

    act = _silu(conv)
    xs = act[:, :D_SSM]
    bmat = act[:, D_SSM:D_SSM + SSM_GROUPS * D_STATE]
    cmat = act[:, D_SSM + SSM_GROUPS * D_STATE:]
    yield

    dt = _softplus(dt_ref[...] + dtb_ref[...])
    a = -jnp.exp(alog_ref[...])
    acum = _select_rows(tril3_ref[...], dt * a)
    acum_t = acum.T
    low_lanes = lax.broadcasted_iota(jnp.int32, (q, 2 * SSM_HEAD_DIM), 1) < SSM_HEAD_DIM

    def expand(cols):
        pairs = []
        for h0 in range(0, SSM_HEADS, 2):
            a = jnp.broadcast_to(cols[:, h0:h0 + 1], (q, 2 * SSM_HEAD_DIM))
            b = jnp.broadcast_to(cols[:, h0 + 1:h0 + 2], (q, 2 * SSM_HEAD_DIM))
            pairs.append(jnp.where(low_lanes, a, b))
        return jnp.concatenate(pairs, axis=1)

    dt_rep = expand(dt)
    yield
    acum_rep = expand(acum)
    yield
    xdt = xs * dt_rep
    exp_a = jnp.exp(acum_rep)
    xw = xdt * jnp.exp(acum_rep[q - 1:q, :] - acum_rep)
    yield

    row = lax.broadcasted_iota(jnp.int32, (q, q), 0)
    col = lax.broadcasted_iota(jnp.int32, (q, q), 1)
    causal = row >= col
    low_half = col < SSM_HEAD_DIM
    ydiag, yoff, states = [], [], []
    for grp in range(SSM_GROUPS):
        bg_f = bmat[:, grp * D_STATE:(grp + 1) * D_STATE]
        bg = bg_f.astype(BF16)
        cg = cmat[:, grp * D_STATE:(grp + 1) * D_STATE].astype(BF16)
        cb = _dot_nt(cg, bg)
        ch = slice(grp * GROUP_WIDTH, (grp + 1) * GROUP_WIDTH)
        yoff.append(_dot(cg, ht_prev[:, ch].astype(BF16)))
        yield
        for quad in range(HEADS_PER_GROUP // 4):
            h0 = grp * HEADS_PER_GROUP + 4 * quad
            masks, blocks = [], []
            for k, h in enumerate(range(h0, h0 + 4)):
                diff = acum[:, h:h + 1] - acum_t[h:h + 1, :]
                masks.append((cb * jnp.exp(jnp.where(causal, diff, -jnp.inf))).astype(BF16))
                xp = xdt[:, (h - k % 2) * SSM_HEAD_DIM:(h - k % 2 + 2) * SSM_HEAD_DIM]
                own = jnp.where(low_half == (k % 2 == 0), xp, 0.0).astype(BF16)
                zero = jnp.zeros_like(own)
                blocks.append(jnp.concatenate([own, zero] if k < 2 else [zero, own], axis=1))
                if k % 2 == 1:
                    yield
            ydiag.append(_dot(jnp.concatenate(masks, axis=1), jnp.concatenate(blocks, axis=0)))
            yield
        states.append(_dot(bg_f.T.astype(BF16), xw[:, ch].astype(BF16)))
        yield
    y = drep_ref[...] * xs + jnp.concatenate(ydiag, axis=1) + jnp.concatenate(yoff, axis=1) * exp_a

    ht_new = ht_prev * exp_a[q - 1:q, :] + jnp.concatenate(states, axis=1)
    ht_ref[...] = ht_new
    nssm_ref[...] = ht_new.T
    yield
    store_y(_group_norm_out(y, z_ref[...], ng_ref[...]))


def _sample_rows_kernel(u_ref, v_ref, xbc_ref, dt_ref, sc_ref, pack_ref,
                        ya_ref, vrows_ref, nconv_ref, split_ref, b_ref, c_ref, dx_ref):
    gv_ref, w0_ref, b0_ref, go_ref, cw_ref, cb_ref, dtb_ref, alog_ref, drep_ref = (
        _Packed(pack_ref, name) for name in ("gv", "w0", "b0", "go", "cw", "cb", "dtb", "alog", "drep"))
    vr =_rms(jax.nn.gelu(v_ref[...]), gv_ref[...])
    vrows_ref[...] = vr
    mixed = vr * w0_ref[...] + b0_ref[...]
    ya_ref[...] = _rms(jax.nn.gelu(u_ref[...]) * mixed, go_ref[...]).astype(BF16)

    xbc = xbc_ref[...]
    conv = cb_ref[...]
    for k in range(CONV_W - 1):
        conv = conv + sc_ref[k] * cw_ref[k:k + 1, :]
    conv = conv + xbc * cw_ref[CONV_W - 1:CONV_W, :]
    for k in range(CONV_W - 2):
        nconv_ref[k] = sc_ref[k + 1]
    nconv_ref[CONV_W - 2] = xbc

    act = _silu(conv)
    xs = act[:, :D_SSM]
    b_ref[...] = act[:, D_SSM:D_SSM + SSM_GROUPS * D_STATE]
    c_ref[...] = act[:, D_SSM + SSM_GROUPS * D_STATE:]
    dx_ref[...] = drep_ref[...] * xs

    dt = _softplus(dt_ref[...] + dtb_ref[...])
    decay = jnp.exp(dt * (-jnp.exp(alog_ref[...])))
    xdt = xs * _dot_exact(dt, _head_expander(D_SSM, SSM_HEAD_DIM.bit_length() - 1))
    decay8 = _dot_exact(decay, _head_expander(HEAD_PAD, SUBLANES.bit_length() - 1))
    cols = jnp.concatenate([xdt, decay8], axis=1).T
    for p in range(N_SPLIT):
        piece = cols.astype(BF16)
        split_ref[:, p * LANES:(p + 1) * LANES] = piece
        cols = cols - piece.astype(F32)


SPLIT_ROWS = D_SSM + HEAD_PAD


def _sample_rows(proj, dt_raw, sconv, pack):
    n = proj.shape[0]
    full = lambda shape: pl.BlockSpec(shape, lambda i: (0,) * len(shape))
    return pl.pallas_call(
        _sample_rows_kernel,
        grid=(1,),
        in_specs=[
            pl.BlockSpec((n, D_CHUNK), lambda i: (0, U_BLK)),
            pl.BlockSpec((n, D_CHUNK), lambda i: (0, V_BLK)),
            pl.BlockSpec((n, CONV_DIM), lambda i: (0, XBC_BLK)),
            full((n, HEAD_PAD)),
            full((CONV_W - 1, n, CONV_DIM)),
            full((PACK_ROWS, D_MODEL)),
        ],
        out_specs=[
            full((n, D_CHUNK)), full((n, D_CHUNK)), full((CONV_W - 1, n, CONV_DIM)),
            full((SPLIT_ROWS, N_SPLIT * LANES)),
            full((n, SSM_GROUPS * D_STATE)), full((n, SSM_GROUPS * D_STATE)), full((n, D_SSM)),
        ],
        out_shape=[
            jax.ShapeDtypeStruct((n, D_CHUNK), BF16),
            jax.ShapeDtypeStruct((n, D_CHUNK), F32),
            jax.ShapeDtypeStruct((CONV_W - 1, n, CONV_DIM), F32),
            jax.ShapeDtypeStruct((SPLIT_ROWS, N_SPLIT * LANES), BF16),
            jax.ShapeDtypeStruct((n, SSM_GROUPS * D_STATE), F32),
            jax.ShapeDtypeStruct((n, SSM_GROUPS * D_STATE), F32),
            jax.ShapeDtypeStruct((n, D_SSM), F32),
        ],
        compiler_params=_params("arbitrary"),
        name="sample_rows",
    )(proj, proj, proj, dt_raw, sconv, pack)


SAMPLE_TB = 16


def _sample_state_kernel(split_ref, b_ref, c_ref, h0_ref, dx_ref, z_ref, pack_ref,
                         hn_ref, y_ref, yt_ref):
    ng_ref = _Packed(pack_ref, "ng")
    i = pl.program_id(0)
    n_tok = b_ref.shape[0]

    @pl.when(i == 0)
    def _():
        yt_ref[...] = jnp.zeros_like(yt_ref)

    tok = lax.broadcasted_iota(jnp.int32, (N_SPLIT * LANES, D_STATE), 0) % LANES
    lane = lax.broadcasted_iota(jnp.int32, (D_SSM, n_tok), 1)

    def per_group(row, grp):
        return jnp.broadcast_to(row[:, grp * D_STATE:(grp + 1) * D_STATE], (GROUP_WIDTH, D_STATE))

    for k in range(SAMPLE_TB):
        t = i * SAMPLE_TB + k
        onehot = (tok == t).astype(BF16)
        bcast = _dot(split_ref[...], onehot)
        decay = jnp.broadcast_to(
            bcast[D_SSM:, :].reshape(SSM_HEADS, 1, SUBLANES, D_STATE),
            (SSM_HEADS, SSM_HEAD_DIM // SUBLANES, SUBLANES, D_STATE)).reshape(D_SSM, D_STATE)
        brow = b_ref[pl.ds(t, 1), :]
        crow = c_ref[pl.ds(t, 1), :]
        bfull = jnp.concatenate([per_group(brow, g) for g in range(SSM_GROUPS)], axis=0)
        cfull = jnp.concatenate([per_group(crow, g) for g in range(SSM_GROUPS)], axis=0)
        h_new = h0_ref[k] * decay + bcast[:D_SSM, :] * bfull
        hn_ref[k] = h_new
        ysum = jnp.sum(h_new * cfull, axis=-1, keepdims=True)
        yt_ref[...] = jnp.where(lane == t, ysum, yt_ref[...])

    @pl.when(i == pl.num_programs(0) - 1)
    def _():
        y = yt_ref[...].T + dx_ref[...]
        y_ref[...] = _group_norm_out(y, z_ref[...], ng_ref[...])


def _sample_state(split, bmat, cmat, h0, dx, proj, pack):
    n = h0.shape[0]
    const2 = lambda i: (0, 0)
    return pl.pallas_call(
        _sample_state_kernel,
        grid=(n // SAMPLE_TB,),
        in_specs=[
            pl.BlockSpec((SPLIT_ROWS, N_SPLIT * LANES), const2),
            pl.BlockSpec((n, SSM_GROUPS * D_STATE), const2),
            pl.BlockSpec((n, SSM_GROUPS * D_STATE), const2),
            pl.BlockSpec((SAMPLE_TB, D_SSM, D_STATE), lambda i: (i, 0, 0)),
            pl.BlockSpec((n, D_SSM), const2),
            pl.BlockSpec((n, D_SSM), lambda i: (0, Z_BLK)),
            pl.BlockSpec((PACK_ROWS, D_MODEL), const2),
        ],
        out_specs=[
            pl.BlockSpec((SAMPLE_TB, D_SSM, D_STATE), lambda i: (i, 0, 0)),
            pl.BlockSpec((n, D_SSM), const2),
        ],
        out_shape=[
            jax.ShapeDtypeStruct((n, D_SSM, D_STATE), F32),
            jax.ShapeDtypeStruct((n, D_SSM), BF16),
        ],
        scratch_shapes=[pltpu.VMEM((D_SSM, n), F32)],
        compiler_params=_params("arbitrary"),
        name="sample_state",
    )(split, bmat, cmat, h0, dx, proj, pack)


FFN_TF = 512
MIX_TM = 512
CHUNKS_PER_TILE = MIX_TM // SSD_CHUNK
FFN_UP_PIECE = 256
FFN_DOWN_PIECE = 512
MIX_PHASES_PER_FFN_PIECE = 5
MIX_LEAD_PIECES = 0


def _interleave(main, side, side_per_main, lead=0):
    for k, _ in enumerate(main):
        if k >= lead:
            for _ in range(side_per_main):
                next(side, None)
    for _ in side:
        pass


def _mix_ffn_kernel(x_ref, wo_ref, wg_ref, wu_ref, wd_ref, pack_ref, u_ref, v_ref, ws_ref, bst_ref,
                    xbc_ref, z_ref, dt_ref, xs_ref, yas_ref, ybs_ref,
                    o_ref, nconv_ref, nssm_ref, os_ref,
                    m_ref, ya_ref, yb_ref, tail_ref, ht_ref, wm_ref, ms_ref, tril3_ref, *, chunks_per_seq):
    g_ref, gf_ref, gv_ref, go_ref, cw_ref, cb_ref, dtb_ref, alog_ref, drep_ref, ng_ref = (
        _Packed(pack_ref, name)
        for name in ("g_ffn", "g_final", "gv", "go", "cw", "cb", "dtb", "alog", "drep", "ng"))
    s = pl.program_id(0)
    f = pl.program_id(1)
    has_ffn = s >= 1
    is_last = s == pl.num_programs(0) - 1
    has_mix = jnp.logical_not(is_last) & (f < CHUNKS_PER_TILE)

    @pl.when((s == 0) & (f == 0))
    def _():
        _mask_chunk_weights(ws_ref, wm_ref)
        row = lax.broadcasted_iota(jnp.int32, tril3_ref.shape, 0)
        col = lax.broadcasted_iota(jnp.int32, tril3_ref.shape, 1)
        tril3_ref[...] = (row >= jnp.bitwise_and(col, SSD_CHUNK - 1)).astype(BF16)

    def project(x, ya, yb, m_out, o_out):
        h = x + _dot(jnp.concatenate([ya, yb], axis=1), wo_ref[...])
        m_out[...] = _rms(h, g_ref[...]).astype(BF16)
        o_out[...] = h

    @pl.when(has_ffn & (f == 0))
    def _():
        project(x_ref[...], ya_ref[...], yb_ref[...], m_ref, o_ref)

    @pl.when(is_last & (f == 0))
    def _():
        project(xs_ref[...], yas_ref[...], ybs_ref[...], ms_ref, os_ref)

    def ffn_step(with_sample=False):
        m = jnp.concatenate([m_ref[...], ms_ref[...]], axis=0) if with_sample else m_ref[...]
        acts = []
        for c in range(FFN_TF // FFN_UP_PIECE):
            cols = slice(c * FFN_UP_PIECE, (c + 1) * FFN_UP_PIECE)
            halves = []
            for w_ref in (wg_ref, wu_ref):
                halves.append(_dot(m, w_ref[:, cols]))
                yield
            acts.append((_silu(halves[0]) * halves[1]).astype(BF16))
        act = jnp.concatenate(acts, axis=1)
        for c in range(D_MODEL // FFN_DOWN_PIECE):
            cols = slice(c * FFN_DOWN_PIECE, (c + 1) * FFN_DOWN_PIECE)
            res = _dot(act, wd_ref[:, cols])
            o_ref[:, cols] += res[:MIX_TM]
            if with_sample:
                os_ref[:, cols] += res[MIX_TM:]
            yield

    chunk_rows = pl.ds(pl.multiple_of(f * SSD_CHUNK, SSD_CHUNK), SSD_CHUNK)

    def store_ya(ya):
        ya_ref[chunk_rows, :] = ya

    def store_yb(yb):
        yb_ref[chunk_rows, :] = yb

    def mix_step():
        first = (s * CHUNKS_PER_TILE + f) % chunks_per_seq == 0
        yield from _chunk_gate(u_ref, v_ref, gv_ref, wm_ref, bst_ref, go_ref, store_ya)
        yield
        yield from _ssd_chunk(first, xbc_ref, z_ref, dt_ref, cw_ref, cb_ref, dtb_ref, alog_ref, drep_ref,
                              ng_ref, tril3_ref, nconv_ref, nssm_ref, tail_ref, ht_ref, store_yb)

    @pl.when(has_ffn & has_mix)
    def _():
        _interleave(ffn_step(), mix_step(), MIX_PHASES_PER_FFN_PIECE, MIX_LEAD_PIECES)

    @pl.when(has_ffn & jnp.logical_not(has_mix) & jnp.logical_not(is_last))
    def _():
        _interleave(ffn_step(), iter(()), 0)

    @pl.when(is_last)
    def _():
        _interleave(ffn_step(with_sample=True), iter(()), 0)

    @pl.when(jnp.logical_not(has_ffn) & has_mix)
    def _():
        _interleave(iter(()), mix_step(), 0)

    @pl.when(has_ffn & (f == pl.num_programs(1) - 1))
    def _():
        o_ref[...] = _rms(o_ref[...], gf_ref[...])

    @pl.when(is_last & (f == pl.num_programs(1) - 1))
    def _():
        os_ref[...] = _rms(os_ref[...], gf_ref[...])


def _mix_ffn(x, proj, dt_raw, w_out_b, w_gate, w_up, w_down, pack, ws, bst, xs, ya_s, yb_s, batch, seq):
    m = x.shape[0]
    n_s = xs.shape[0]
    n_tiles = m // MIX_TM
    n_chunks = m // SSD_CHUNK
    chunks_per_seq = seq // SSD_CHUNK
    assert MIX_TM % SSD_CHUNK == 0 and seq % MIX_TM == 0 and CHUNKS_PER_TILE <= D_FF // FFN_TF

    def tile(s, f):
        return jnp.maximum(s - 1, 0)

    def chunk(s, f):
        return jnp.where(s >= n_tiles, n_chunks - 1,
                         s * CHUNKS_PER_TILE + jnp.minimum(f, CHUNKS_PER_TILE - 1))

    def seq_of(s, f):
        return chunk(s, f) // chunks_per_seq

    def ffn_block(s, f):
        return jnp.where(s == 0, 0, f)

    const2 = lambda s, f: (0, 0)
    return pl.pallas_call(
        functools.partial(_mix_ffn_kernel, chunks_per_seq=chunks_per_seq),
        grid=(n_tiles + 1, D_FF // FFN_TF),
        in_specs=[
            pl.BlockSpec((MIX_TM, D_MODEL), lambda s, f: (tile(s, f), 0)),
            pl.BlockSpec((D_MODEL, D_MODEL), const2, pipeline_mode=pl.Buffered(1)),
            pl.BlockSpec((D_MODEL, FFN_TF), lambda s, f: (0, ffn_block(s, f))),
            pl.BlockSpec((D_MODEL, FFN_TF), lambda s, f: (0, ffn_block(s, f))),
            pl.BlockSpec((FFN_TF, D_MODEL), lambda s, f: (ffn_block(s, f), 0)),
            pl.BlockSpec((PACK_ROWS, D_MODEL), const2),
            pl.BlockSpec((CHUNK_LEN, D_CHUNK), lambda s, f: (chunk(s, f), U_BLK)),
            pl.BlockSpec((CHUNK_LEN, D_CHUNK), lambda s, f: (chunk(s, f), V_BLK)),
            pl.BlockSpec((CHUNK_HEADS, CHUNK_LEN, CHUNK_LEN), lambda s, f: (0, 0, 0),
                         pipeline_mode=pl.Buffered(1)),
            pl.BlockSpec((CHUNK_LEN, CHUNK_HEADS), const2),
            pl.BlockSpec((SSD_CHUNK, CONV_DIM), lambda s, f: (chunk(s, f), XBC_BLK)),
            pl.BlockSpec((SSD_CHUNK, D_SSM), lambda s, f: (chunk(s, f), Z_BLK)),
            pl.BlockSpec((SSD_CHUNK, HEAD_PAD), lambda s, f: (chunk(s, f), 0)),
            pl.BlockSpec((n_s, D_MODEL), const2, pipeline_mode=pl.Buffered(1)),
            pl.BlockSpec((n_s, D_CHUNK), const2, pipeline_mode=pl.Buffered(1)),
            pl.BlockSpec((n_s, D_SSM), const2, pipeline_mode=pl.Buffered(1)),
        ],
        out_specs=[
            pl.BlockSpec((MIX_TM, D_MODEL), lambda s, f: (tile(s, f), 0)),
            pl.BlockSpec((None, CONV_W - 1, CONV_DIM), lambda s, f: (seq_of(s, f), 0, 0)),
            pl.BlockSpec((None, D_SSM, D_STATE), lambda s, f: (seq_of(s, f), 0, 0)),
            pl.BlockSpec((n_s, D_MODEL), const2),
        ],
        out_shape=[
            jax.ShapeDtypeStruct((m, D_MODEL), F32),
            jax.ShapeDtypeStruct((batch, CONV_W - 1, CONV_DIM), F32),
            jax.ShapeDtypeStruct((batch, D_SSM, D_STATE), F32),
            jax.ShapeDtypeStruct((n_s, D_MODEL), F32),
        ],
        scratch_shapes=[
            pltpu.VMEM((MIX_TM, D_MODEL), BF16),
            pltpu.VMEM((MIX_TM, D_CHUNK), BF16),
            pltpu.VMEM((MIX_TM, D_SSM), BF16),
            pltpu.VMEM((CONV_PAD, CONV_DIM), F32),
            pltpu.VMEM((D_STATE, D_SSM), F32),
            pltpu.VMEM((CHUNK_LEN, CHUNK_HEADS * CHUNK_LEN), BF16),
            pltpu.VMEM((n_s, D_MODEL), BF16),
            pltpu.VMEM((SSD_CHUNK, N_SPLIT * SSD_CHUNK), BF16),
        ],
        compiler_params=_params("arbitrary", "arbitrary"),
        name="mix_ffn",
    )(x, w_out_b, w_gate, w_up, w_down, pack, proj, proj, ws, bst, proj, proj, dt_raw, xs, ya_s, yb_s)


def kernel(x_prompt, x_sample, state_conv, state_ssm, norm_mix_g, w_in, chunk_v_norm_g, chunk_w_s,
           chunk_b_s, chunk_out_norm_g, ssd_conv_w, ssd_conv_b, ssd_dt_bias, ssd_a_log, ssd_d,
           ssd_norm_g, w_out, norm_ffn_g, w_gate, w_up, w_down, norm_final_g):
    depth = w_in.shape[0]
    batch, seq, _ = x_prompt.shape
    n_s = x_sample.shape[0]
    assert x_sample.shape[1] == 1 and seq % SSD_CHUNK == 0 and depth == 1

    hp = x_prompt.reshape(batch * seq, D_MODEL)
    hs = x_sample.reshape(n_s, D_MODEL)
    l = 0
    w_in_t = jnp.swapaxes(w_in, 1, 2)
    w_dt_t = jnp.pad(w_in_t[l, D_MAIN:, :], ((0, HEAD_PAD - SSM_HEADS), (0, 0))).astype(BF16)
    pack = _pack_params(
        g_mix=norm_mix_g[l], g_ffn=norm_ffn_g[l], g_final=norm_final_g,
        gv=chunk_v_norm_g[l], go=chunk_out_norm_g[l],
        ng=ssd_norm_g[l], drep=jnp.repeat(ssd_d[l], SSM_HEAD_DIM),
        cb=ssd_conv_b[l], cw=ssd_conv_w[l], dtb=ssd_dt_bias[l], alog=ssd_a_log[l],
        w0=jnp.repeat(chunk_w_s[l, :, 0, 0], CHUNK_HEAD_DIM),
        b0=jnp.repeat(chunk_b_s[l, :, 0], CHUNK_HEAD_DIM))

    proj_p, dt_p, proj_s, dt_s, w_gate_b, w_up_b, w_down_b, w_out_b = _in_proj(
        hp, hs, pack, w_in_t, l, w_dt_t, tm=1024, cast=(w_gate, w_up, w_down, w_out))

    ya_s, v_s, conv_s, split, b_s, c_s, dx_s = _sample_rows(
        proj_s, dt_s, jnp.swapaxes(state_conv[l], 0, 1), pack)
    ssm_s, yb_s = _sample_state(split, b_s, c_s, state_ssm[l].reshape(n_s, D_SSM, D_STATE),
                                dx_s, proj_s, pack)

    y_p, conv_p, ssm_p, y_s = _mix_ffn(
        hp, proj_p, dt_p, w_out_b, w_gate_b, w_up_b, w_down_b, pack,
        chunk_w_s[l].astype(F32), chunk_b_s[l].astype(F32).T, hs, ya_s, yb_s, batch, seq)

    return (
        y_p.reshape(batch, seq, D_MODEL),
        y_s.reshape(n_s, 1, D_MODEL),
        conv_p[None],
        ssm_p.reshape(1, batch, SSM_HEADS, SSM_HEAD_DIM, D_STATE),
        jnp.swapaxes(conv_s, 0, 1)[None],
        ssm_s.reshape(1, n_s, SSM_HEADS, SSM_HEAD_DIM, D_STATE),
        v_s.reshape(1, n_s, 1, D_CHUNK),
    )
```

```python
import functools

import jax
import jax.numpy as jnp
from jax import lax
from jax.experimental import pallas as pl
from jax.experimental.pallas import tpu as pltpu

F32 = jnp.float32
BF16 = jnp.bfloat16
HIGHEST = lax.Precision.HIGHEST

D_MODEL = 2048
D_CHUNK = 1024
CHUNK_HEADS = 8
CHUNK_HEAD_DIM = 128
CHUNK_LEN = 128
D_SSM = 1024
SSM_HEAD_DIM = 64
SSM_HEADS = 16
SSM_GROUPS = 2
HEADS_PER_GROUP = SSM_HEADS // SSM_GROUPS
GROUP_WIDTH = D_SSM // SSM_GROUPS
D_STATE = 128
CONV_W = 4
CONV_DIM = D_SSM + 2 * SSM_GROUPS * D_STATE
SSD_CHUNK = 128
D_MAIN = 2 * D_CHUNK + D_SSM + CONV_DIM
D_FF = 5632
EPS = 1e-6

LANES = 128
SUBLANES = 8
BF16_SUBLANES = 16
HEAD_PAD = LANES
N_SPLIT = 3
VMEM_LIMIT = 58 * 1024 * 1024

U_BLK, V_BLK, Z_BLK = 0, 1, 2
XBC_BLK = 2


def _rms(x, g):
    return x * lax.rsqrt(jnp.mean(x * x, axis=-1, keepdims=True) + EPS) * g


def _silu(x):
    return x * jax.nn.sigmoid(x)


def _softplus(x):
    return jnp.maximum(x, 0.0) + jnp.log1p(jnp.exp(-jnp.abs(x)))


def _dot(a, b):
    return jnp.dot(a, b, preferred_element_type=F32)


def _dot_nt(a, b):
    return lax.dot_general(a, b, (((1,), (1,)), ((), ())), preferred_element_type=F32)


def _dot_exact(a, b):
    return jnp.dot(a, b, precision=HIGHEST, preferred_element_type=F32)


def _params(*semantics):
    return pltpu.CompilerParams(dimension_semantics=semantics, vmem_limit_bytes=VMEM_LIMIT)


PACK_ROWS = 16
PACK_LAYOUT = {
    "g_mix": (0, 1, 0, D_MODEL),
    "g_ffn": (1, 1, 0, D_MODEL),
    "g_final": (2, 1, 0, D_MODEL),
    "gv": (3, 1, 0, D_CHUNK),
    "go": (3, 1, D_CHUNK, D_CHUNK),
    "ng": (4, 1, 0, D_SSM),
    "drep": (4, 1, D_SSM, D_SSM),
    "cb": (5, 1, 0, CONV_DIM),
    "cw": (6, CONV_W, 0, CONV_DIM),
    "dtb": (10, 1, 0, HEAD_PAD),
    "alog": (10, 1, HEAD_PAD, HEAD_PAD),
    "w0": (11, 1, 0, D_CHUNK),
    "b0": (11, 1, D_CHUNK, D_CHUNK),
}


class _Packed:
    def __init__(self, ref, name):
        self.ref = ref
        self.row, self.rows, self.lane, self.lanes = PACK_LAYOUT[name]

    def __getitem__(self, idx):
        lanes = slice(self.lane, self.lane + self.lanes)
        if idx is Ellipsis:
            return self.ref[self.row:self.row + self.rows, lanes]
        rows, cols = idx
        assert cols == slice(None)
        return self.ref[self.row + rows.start:self.row + rows.stop, lanes]


def _pack_params(**vectors):
    segments = [[] for _ in range(PACK_ROWS)]
    for name, value in vectors.items():
        row, n_rows, lane, lanes = PACK_LAYOUT[name]
        value = value.astype(F32).reshape(n_rows, -1)
        assert value.shape[1] <= lanes
        for r in range(n_rows):
            segments[row + r].append((lane, lanes, value[r]))
    rows = []
    for segs in segments:
        parts, at = [], 0
        for lane, lanes, value in sorted(segs, key=lambda seg: seg[0]):
            parts += [jnp.zeros((lane - at,), F32), value, jnp.zeros((lanes - value.shape[0],), F32)]
            at = lane + lanes
        parts.append(jnp.zeros((D_MODEL - at,), F32))
        rows.append(jnp.concatenate(parts))
    return jnp.stack(rows)


def _head_expander(width, log2_per_head):
    row = lax.broadcasted_iota(jnp.int32, (HEAD_PAD, width), 0)
    col = lax.broadcasted_iota(jnp.int32, (HEAD_PAD, width), 1)
    return (row == jnp.right_shift(col, log2_per_head)).astype(F32)


IN_TN = 512


def _in_proj_kernel(x_hbm, xs_hbm, pack_ref, w_ref, wdt_ref, *refs, tm, n_cast):
    g_ref = _Packed(pack_ref, "g_mix")
    cast_in = refs[:n_cast]
    o_ref, dt_ref, os_ref, dts_ref = refs[n_cast:n_cast + 4]
    cast_out = refs[n_cast + 4:2 * n_cast + 4]
    xn_ref, wres_ref, xbuf_ref, sem = refs[2 * n_cast + 4:]
    i = pl.program_id(0)
    j = pl.program_id(1)
    last = pl.num_programs(0) - 1
    n_s = xs_hbm.shape[0]

    for src, dst in zip(cast_in, cast_out, strict=True):
        dst[...] = src[...].astype(BF16)

    def x_copy(tile):
        return pltpu.make_async_copy(x_hbm.at[pl.ds(tile * tm, tm), :], xbuf_ref, sem)

    def xs_copy():
        return pltpu.make_async_copy(xs_hbm, xbuf_ref.at[pl.ds(0, n_s), :], sem)

    @pl.when((i == 0) & (j == 0))
    def _():
        xs_copy().start()
        xs_copy().wait()
        nb = _rms(xbuf_ref[0:n_s, :], g_ref[...]).astype(BF16)
        xn_ref[tm:tm + n_s, :] = nb
        dts_ref[...] = _dot_nt(nb, wdt_ref[...])
        x_copy(0).start()

    @pl.when(i == 0)
    def _():
        wres_ref[j] = w_ref[...].T.astype(BF16)

    @pl.when(j == 0)
    def _():
        x_copy(i).wait()
        nb = _rms(xbuf_ref[...], g_ref[...]).astype(BF16)
        xn_ref[0:tm, :] = nb
        dt_ref[...] = _dot_nt(nb, wdt_ref[...])

    @pl.when((j == 1) & (i < last))
    def _():
        x_copy(i + 1).start()

    @pl.when(i < last)
    def _():
        o_ref[...] = _dot(xn_ref[0:tm, :], wres_ref[j])

    @pl.when(i == last)
    def _():
        res = _dot(xn_ref[...], wres_ref[j])
        o_ref[...] = res[0:tm, :]
        os_ref[...] = res[tm:, :]


def _in_proj(x, xs, pack, w_in_t, layer, w_dt_t, tm, cast=()):
    m = x.shape[0]
    n_s = xs.shape[0]
    n_i = m // tm
    n_j = D_MAIN // IN_TN
    assert n_j >= 2 and n_s <= tm and n_s % BF16_SUBLANES == 0
    cast_in, cast_out, cast_shapes = [], [], []
    n_steps = n_i * n_j

    def col_sample(i, j):
        return jnp.where(i == n_i - 1, j, 0)

    for w in cast:
        _, w_rows, w_cols = w.shape
        slab_rows = next(r for r in range(BF16_SUBLANES, w_rows + 1, BF16_SUBLANES)
                         if w_rows % r == 0 and w_rows // r <= n_steps)
        n_slabs = w_rows // slab_rows
        hold = n_steps // n_slabs
        slab = lambda i, j, n_slabs=n_slabs, hold=hold: jnp.minimum((i * n_j + j) // hold, n_slabs - 1)
        cast_in.append(pl.BlockSpec((None, slab_rows, w_cols), lambda i, j, slab=slab: (layer, slab(i, j), 0)))
        cast_out.append(pl.BlockSpec((slab_rows, w_cols), lambda i, j, slab=slab: (slab(i, j), 0)))
        cast_shapes.append(jax.ShapeDtypeStruct((w_rows, w_cols), BF16))
    return pl.pallas_call(
        functools.partial(_in_proj_kernel, tm=tm, n_cast=len(cast)),
        grid=(n_i, n_j),
        in_specs=[
            pl.BlockSpec(memory_space=pl.ANY),
            pl.BlockSpec(memory_space=pl.ANY),
            pl.BlockSpec((PACK_ROWS, D_MODEL), lambda i, j: (0, 0)),
            pl.BlockSpec((None, IN_TN, D_MODEL), lambda i, j: (layer, jnp.where(i == 0, j, n_j - 1), 0)),
            pl.BlockSpec((HEAD_PAD, D_MODEL), lambda i, j: (0, 0)),
            *cast_in,
        ],
        out_specs=[
            pl.BlockSpec((tm, IN_TN), lambda i, j: (i, j)),
            pl.BlockSpec((tm, HEAD_PAD), lambda i, j: (i, 0)),
            pl.BlockSpec((n_s, IN_TN), lambda i, j: (0, col_sample(i, j))),
            pl.BlockSpec((n_s, HEAD_PAD), lambda i, j: (0, 0)),
            *cast_out,
        ],
        out_shape=[
            jax.ShapeDtypeStruct((m, D_MAIN), F32),
            jax.ShapeDtypeStruct((m, HEAD_PAD), F32),
            jax.ShapeDtypeStruct((n_s, D_MAIN), F32),
            jax.ShapeDtypeStruct((n_s, HEAD_PAD), F32),
            *cast_shapes,
        ],
        scratch_shapes=[
            pltpu.VMEM((tm + n_s, D_MODEL), BF16),
            pltpu.VMEM((n_j, D_MODEL, IN_TN), BF16),
            pltpu.VMEM((tm, D_MODEL), F32),
            pltpu.SemaphoreType.DMA(()),
        ],
        compiler_params=_params("arbitrary", "arbitrary"),
        name="in_proj",
    )(x, xs, pack, w_in_t, w_dt_t, *cast)


def _mask_chunk_weights(ws_ref, wm_ref):
    row = lax.broadcasted_iota(jnp.int32, (CHUNK_LEN, CHUNK_LEN), 0)
    col = lax.broadcasted_iota(jnp.int32, (CHUNK_LEN, CHUNK_LEN), 1)
    for h in range(CHUNK_HEADS):
        wm_ref[:, h * CHUNK_LEN:(h + 1) * CHUNK_LEN] = jnp.where(row >= col, ws_ref[h], 0.0).astype(BF16)


def _chunk_gate(u_ref, v_ref, gv_ref, wm_ref, bst_ref, go_ref, store_y):
    vb = _rms(jax.nn.gelu(v_ref[...]), gv_ref[...]).astype(BF16)
    yield
    u = jax.nn.gelu(u_ref[...])
    yield
    zeros = jnp.zeros((CHUNK_LEN, CHUNK_HEAD_DIM), BF16)
    parts = []
    for h0 in range(0, CHUNK_HEADS, 2):
        w = wm_ref[:, h0 * CHUNK_LEN:(h0 + 2) * CHUNK_LEN]
        va = vb[:, h0 * CHUNK_HEAD_DIM:(h0 + 1) * CHUNK_HEAD_DIM]
        vb2 = vb[:, (h0 + 1) * CHUNK_HEAD_DIM:(h0 + 2) * CHUNK_HEAD_DIM]
        rhs = jnp.concatenate([jnp.concatenate([va, zeros], axis=1),
                               jnp.concatenate([zeros, vb2], axis=1)], axis=0)
        mixed = _dot(w, rhs)
        for k, h in enumerate((h0, h0 + 1)):
            sl = slice(h * CHUNK_HEAD_DIM, (h + 1) * CHUNK_HEAD_DIM)
            mh = mixed[:, k * CHUNK_HEAD_DIM:(k + 1) * CHUNK_HEAD_DIM] + bst_ref[:, h:h + 1]
            parts.append(u[:, sl] * mh)
        yield
    y = jnp.concatenate(parts, axis=1)
    store_y(_rms(y, go_ref[...]).astype(BF16))


CONV_PAD = 8


def _group_norm_out(y, z, g):
    y = y * _silu(z)
    parts = []
    for grp in range(SSM_GROUPS):
        yg = y[:, grp * GROUP_WIDTH:(grp + 1) * GROUP_WIDTH]
        parts.append(yg * lax.rsqrt(jnp.mean(yg * yg, axis=-1, keepdims=True) + EPS))
    return (jnp.concatenate(parts, axis=1) * g).astype(BF16)


def _split3(x):
    p1 = x.astype(BF16)
    r1 = x - p1.astype(F32)
    p2 = r1.astype(BF16)
    p3 = (r1 - p2.astype(F32)).astype(BF16)
    return p1, p2, p3


def _select_rows(sel3, x):
    return _dot(sel3, jnp.concatenate(_split3(x), axis=0))


def _ssd_chunk(first, xbc_ref, z_ref, dt_ref, cw_ref, cb_ref, dtb_ref, alog_ref, drep_ref, ng_ref,
               tril3_ref, nconv_ref, nssm_ref, tail_ref, ht_ref, store_y):
    q = SSD_CHUNK

    xbc = xbc_ref[...]
    tail = jnp.where(first, 0.0, tail_ref[...])
    ht_prev = jnp.where(first, 0.0, ht_ref[...])
    sub = lax.broadcasted_iota(jnp.int32, (CONV_PAD, CONV_DIM), 0)
    conv = cb_ref[...] + xbc * cw_ref[CONV_W - 1:CONV_W, :]
    for k in range(1, CONV_W):
        shifted = pltpu.roll(xbc, k, 0)
        head = jnp.where(sub < k, pltpu.roll(tail, k, 0), shifted[0:CONV_PAD, :])
        shifted = jnp.concatenate([head, shifted[CONV_PAD:, :]], axis=0)
        conv = conv + shifted * cw_ref[CONV_W - 1 - k:CONV_W - k, :]
    tail_ref[...] = xbc[q - CONV_PAD:, :]
    nconv_ref[...] = xbc[q - (CONV_W - 1):, :]
    yield

    act = _silu(conv)
    xs = act[:, :D_SSM]
    bmat = act[:, D_SSM:D_SSM + SSM_GROUPS * D_STATE]
    cmat = act[:, D_SSM + SSM_GROUPS * D_STATE:]
    yield

    dt = _softplus(dt_ref[...] + dtb_ref[...])
    a = -jnp.exp(alog_ref[...])
    acum = _select_rows(tril3_ref[...], dt * a)
    acum_t = acum.T
    low_lanes = lax.broadcasted_iota(jnp.int32, (q, 2 * SSM_HEAD_DIM), 1) < SSM_HEAD_DIM

    def expand(cols):
        pairs = []
        for h0 in range(0, SSM_HEADS, 2):
            a = jnp.broadcast_to(cols[:, h0:h0 + 1], (q, 2 * SSM_HEAD_DIM))
            b = jnp.broadcast_to(cols[:, h0 + 1:h0 + 2], (q, 2 * SSM_HEAD_DIM))
            pairs.append(jnp.where(low_lanes, a, b))
        return jnp.concatenate(pairs, axis=1)

    dt_rep = expand(dt)
    yield
    acum_rep = expand(acum)
    yield
    xdt = xs * dt_rep
    exp_a = jnp.exp(acum_rep)
    xw = xdt * jnp.exp(acum_rep[q - 1:q, :] - acum_rep)
    yield

    row = lax.broadcasted_iota(jnp.int32, (q, q), 0)
    col = lax.broadcasted_iota(jnp.int32, (q, q), 1)
    causal = row >= col
    low_half = col < SSM_HEAD_DIM
    ydiag, yoff, states = [], [], []
    for grp in range(SSM_GROUPS):
        bg_f = bmat[:, grp * D_STATE:(grp + 1) * D_STATE]
        bg = bg_f.astype(BF16)
        cg = cmat[:, grp * D_STATE:(grp + 1) * D_STATE].astype(BF16)
        cb = _dot_nt(cg, bg)
        ch = slice(grp * GROUP_WIDTH, (grp + 1) * GROUP_WIDTH)
        yoff.append(_dot(cg, ht_prev[:, ch].astype(BF16)))
        yield
        for quad in range(HEADS_PER_GROUP // 4):
            h0 = grp * HEADS_PER_GROUP + 4 * quad
            masks, blocks = [], []
            for k, h in enumerate(range(h0, h0 + 4)):
                diff = acum[:, h:h + 1] - acum_t[h:h + 1, :]
                masks.append((cb * jnp.exp(jnp.where(causal, diff, -jnp.inf))).astype(BF16))
                xp = xdt[:, (h - k % 2) * SSM_HEAD_DIM:(h - k % 2 + 2) * SSM_HEAD_DIM]
                own = jnp.where(low_half == (k % 2 == 0), xp, 0.0).astype(BF16)
                zero = jnp.zeros_like(own)
                blocks.append(jnp.concatenate([own, zero] if k < 2 else [zero, own], axis=1))
                if k % 2 == 1:
                    yield
            ydiag.append(_dot(jnp.concatenate(masks, axis=1), jnp.concatenate(blocks, axis=0)))
            yield
        states.append(_dot(bg_f.T.astype(BF16), xw[:, ch].astype(BF16)))
        yield
    y = drep_ref[...] * xs + jnp.concatenate(ydiag, axis=1) + jnp.concatenate(yoff, axis=1) * exp_a

    ht_new = ht_prev * exp_a[q - 1:q, :] + jnp.concatenate(states, axis=1)
    ht_ref[...] = ht_new
    nssm_ref[...] = ht_new.T
    yield
    store_y(_group_norm_out(y, z_ref[...], ng_ref[...]))


def _sample_rows_kernel(u_ref, v_ref, xbc_ref, dt_ref, sc_ref, pack_ref,
                        ya_ref, vrows_ref, nconv_ref, split_ref, b_ref, c_ref, dx_ref):
    gv_ref, w0_ref, b0_ref, go_ref, cw_ref, cb_ref, dtb_ref, alog_ref, drep_ref = (
        _Packed(pack_ref, name) for name in ("gv", "w0", "b0", "go", "cw", "cb", "dtb", "alog", "drep"))
    vr =_rms(jax.nn.gelu(v_ref[...]), gv_ref[...])
    vrows_ref[...] = vr
    mixed = vr * w0_ref[...] + b0_ref[...]
    ya_ref[...] = _rms(jax.nn.gelu(u_ref[...]) * mixed, go_ref[...]).astype(BF16)

    xbc = xbc_ref[...]
    conv = cb_ref[...]
    for k in range(CONV_W - 1):
        conv = conv + sc_ref[k] * cw_ref[k:k + 1, :]
    conv = conv + xbc * cw_ref[CONV_W - 1:CONV_W, :]
    for k in range(CONV_W - 2):
        nconv_ref[k] = sc_ref[k + 1]
    nconv_ref[CONV_W - 2] = xbc

    act = _silu(conv)
    xs = act[:, :D_SSM]
    b_ref[...] = act[:, D_SSM:D_SSM + SSM_GROUPS * D_STATE]
    c_ref[...] = act[:, D_SSM + SSM_GROUPS * D_STATE:]
    dx_ref[...] = drep_ref[...] * xs

    dt = _softplus(dt_ref[...] + dtb_ref[...])
    decay = jnp.exp(dt * (-jnp.exp(alog_ref[...])))
    xdt = xs * _dot_exact(dt, _head_expander(D_SSM, SSM_HEAD_DIM.bit_length() - 1))
    decay8 = _dot_exact(decay, _head_expander(HEAD_PAD, SUBLANES.bit_length() - 1))
    cols = jnp.concatenate([xdt, decay8], axis=1).T
    for p in range(N_SPLIT):
        piece = cols.astype(BF16)
        split_ref[:, p * LANES:(p + 1) * LANES] = piece
        cols = cols - piece.astype(F32)


SPLIT_ROWS = D_SSM + HEAD_PAD


def _sample_rows(proj, dt_raw, sconv, pack):
    n = proj.shape[0]
    full = lambda shape: pl.BlockSpec(shape, lambda i: (0,) * len(shape))
    return pl.pallas_call(
        _sample_rows_kernel,
        grid=(1,),
        in_specs=[
            pl.BlockSpec((n, D_CHUNK), lambda i: (0, U_BLK)),
            pl.BlockSpec((n, D_CHUNK), lambda i: (0, V_BLK)),
            pl.BlockSpec((n, CONV_DIM), lambda i: (0, XBC_BLK)),
            full((n, HEAD_PAD)),
            full((CONV_W - 1, n, CONV_DIM)),
            full((PACK_ROWS, D_MODEL)),
        ],
        out_specs=[
            full((n, D_CHUNK)), full((n, D_CHUNK)), full((CONV_W - 1, n, CONV_DIM)),
            full((SPLIT_ROWS, N_SPLIT * LANES)),
            full((n, SSM_GROUPS * D_STATE)), full((n, SSM_GROUPS * D_STATE)), full((n, D_SSM)),
        ],
        out_shape=[
            jax.ShapeDtypeStruct((n, D_CHUNK), BF16),
            jax.ShapeDtypeStruct((n, D_CHUNK), F32),
            jax.ShapeDtypeStruct((CONV_W - 1, n, CONV_DIM), F32),
            jax.ShapeDtypeStruct((SPLIT_ROWS, N_SPLIT * LANES), BF16),
            jax.ShapeDtypeStruct((n, SSM_GROUPS * D_STATE), F32),
            jax.ShapeDtypeStruct((n, SSM_GROUPS * D_STATE), F32),
            jax.ShapeDtypeStruct((n, D_SSM), F32),
        ],
        compiler_params=_params("arbitrary"),
        name="sample_rows",
    )(proj, proj, proj, dt_raw, sconv, pack)


SAMPLE_TB = 16


def _sample_state_kernel(split_ref, b_ref, c_ref, h0_ref, dx_ref, z_ref, pack_ref,
                         hn_ref, y_ref, yt_ref):
    ng_ref = _Packed(pack_ref, "ng")
    i = pl.program_id(0)
    n_tok = b_ref.shape[0]

    @pl.when(i == 0)
    def _():
        yt_ref[...] = jnp.zeros_like(yt_ref)

    tok = lax.broadcasted_iota(jnp.int32, (N_SPLIT * LANES, D_STATE), 0) % LANES
    lane = lax.broadcasted_iota(jnp.int32, (D_SSM, n_tok), 1)

    def per_group(row, grp):
        return jnp.broadcast_to(row[:, grp * D_STATE:(grp + 1) * D_STATE], (GROUP_WIDTH, D_STATE))

    for k in range(SAMPLE_TB):
        t = i * SAMPLE_TB + k
        onehot = (tok == t).astype(BF16)
        bcast = _dot(split_ref[...], onehot)
        decay = jnp.broadcast_to(
            bcast[D_SSM:, :].reshape(SSM_HEADS, 1, SUBLANES, D_STATE),
            (SSM_HEADS, SSM_HEAD_DIM // SUBLANES, SUBLANES, D_STATE)).reshape(D_SSM, D_STATE)
        brow = b_ref[pl.ds(t, 1), :]
        crow = c_ref[pl.ds(t, 1), :]
        bfull = jnp.concatenate([per_group(brow, g) for g in range(SSM_GROUPS)], axis=0)
        cfull = jnp.concatenate([per_group(crow, g) for g in range(SSM_GROUPS)], axis=0)
        h_new = h0_ref[k] * decay + bcast[:D_SSM, :] * bfull
        hn_ref[k] = h_new
        ysum = jnp.sum(h_new * cfull, axis=-1, keepdims=True)
        yt_ref[...] = jnp.where(lane == t, ysum, yt_ref[...])

    @pl.when(i == pl.num_programs(0) - 1)
    def _():
        y = yt_ref[...].T + dx_ref[...]
        y_ref[...] = _group_norm_out(y, z_ref[...], ng_ref[...])


def _sample_state(split, bmat, cmat, h0, dx, proj, pack):
    n = h0.shape[0]
    const2 = lambda i: (0, 0)
    return pl.pallas_call(
        _sample_state_kernel,
        grid=(n // SAMPLE_TB,),
        in_specs=[
            pl.BlockSpec((SPLIT_ROWS, N_SPLIT * LANES), const2),
            pl.BlockSpec((n, SSM_GROUPS * D_STATE), const2),
            pl.BlockSpec((n, SSM_GROUPS * D_STATE), const2),
            pl.BlockSpec((SAMPLE_TB, D_SSM, D_STATE), lambda i: (i, 0, 0)),
            pl.BlockSpec((n, D_SSM), const2),
            pl.BlockSpec((n, D_SSM), lambda i: (0, Z_BLK)),
            pl.BlockSpec((PACK_ROWS, D_MODEL), const2),
        ],
        out_specs=[
            pl.BlockSpec((SAMPLE_TB, D_SSM, D_STATE), lambda i: (i, 0, 0)),
            pl.BlockSpec((n, D_SSM), const2),
        ],
        out_shape=[
            jax.ShapeDtypeStruct((n, D_SSM, D_STATE), F32),
            jax.ShapeDtypeStruct((n, D_SSM), BF16),
        ],
        scratch_shapes=[pltpu.VMEM((D_SSM, n), F32)],
        compiler_params=_params("arbitrary"),
        name="sample_state",
    )(split, bmat, cmat, h0, dx, proj, pack)


FFN_TF = 512
MIX_TM = 512
CHUNKS_PER_TILE = MIX_TM // SSD_CHUNK
FFN_UP_PIECE = 256
FFN_DOWN_PIECE = 1024
MIX_PHASES_PER_FFN_PIECE = 6
MIX_LEAD_PIECES = 0


def _interleave(main, side, side_per_main, lead=0):
    for k, _ in enumerate(main):
        if k >= lead:
            for _ in range(side_per_main):
                next(side, None)
    for _ in side:
        pass


def _mix_ffn_kernel(x_ref, wo_ref, wg_ref, wu_ref, wd_ref, pack_ref, u_ref, v_ref, ws_ref, bst_ref,
                    xbc_ref, z_ref, dt_ref, xs_ref, yas_ref, ybs_ref,
                    o_ref, nconv_ref, nssm_ref, os_ref,
                    m_ref, ya_ref, yb_ref, tail_ref, ht_ref, wm_ref, ms_ref, tril3_ref, *, chunks_per_seq):
    g_ref, gf_ref, gv_ref, go_ref, cw_ref, cb_ref, dtb_ref, alog_ref, drep_ref, ng_ref = (
        _Packed(pack_ref, name)
        for name in ("g_ffn", "g_final", "gv", "go", "cw", "cb", "dtb", "alog", "drep", "ng"))
    s = pl.program_id(0)
    f = pl.program_id(1)
    has_ffn = s >= 1
    is_last = s == pl.num_programs(0) - 1
    has_mix = jnp.logical_not(is_last) & (f < CHUNKS_PER_TILE)

    @pl.when((s == 0) & (f == 0))
    def _():
        _mask_chunk_weights(ws_ref, wm_ref)
        row = lax.broadcasted_iota(jnp.int32, tril3_ref.shape, 0)
        col = lax.broadcasted_iota(jnp.int32, tril3_ref.shape, 1)
        tril3_ref[...] = (row >= jnp.bitwise_and(col, SSD_CHUNK - 1)).astype(BF16)

    def project(x, ya, yb, m_out, o_out):
        h = x + _dot(jnp.concatenate([ya, yb], axis=1), wo_ref[...])
        m_out[...] = _rms(h, g_ref[...]).astype(BF16)
        o_out[...] = h

    @pl.when(has_ffn & (f == 0))
    def _():
        project(x_ref[...], ya_ref[...], yb_ref[...], m_ref, o_ref)

    @pl.when(is_last & (f == 0))
    def _():
        project(xs_ref[...], yas_ref[...], ybs_ref[...], ms_ref, os_ref)

    def ffn_step(with_sample=False):
        m = jnp.concatenate([m_ref[...], ms_ref[...]], axis=0) if with_sample else m_ref[...]
        acts = []
        for c in range(FFN_TF // FFN_UP_PIECE):
            cols = slice(c * FFN_UP_PIECE, (c + 1) * FFN_UP_PIECE)
            halves = []
            for w_ref in (wg_ref, wu_ref):
                halves.append(_dot(m, w_ref[:, cols]))
                yield
            acts.append((_silu(halves[0]) * halves[1]).astype(BF16))
        act = jnp.concatenate(acts, axis=1)
        for c in range(D_MODEL // FFN_DOWN_PIECE):
            cols = slice(c * FFN_DOWN_PIECE, (c + 1) * FFN_DOWN_PIECE)
            res = _dot(act, wd_ref[:, cols])
            o_ref[:, cols] += res[:MIX_TM]
            if with_sample:
                os_ref[:, cols] += res[MIX_TM:]
            yield

    chunk_rows = pl.ds(pl.multiple_of(f * SSD_CHUNK, SSD_CHUNK), SSD_CHUNK)

    def store_ya(ya):
        ya_ref[chunk_rows, :] = ya

    def store_yb(yb):
        yb_ref[chunk_rows, :] = yb

    def mix_step():
        first = (s * CHUNKS_PER_TILE + f) % chunks_per_seq == 0
        yield from _chunk_gate(u_ref, v_ref, gv_ref, wm_ref, bst_ref, go_ref, store_ya)
        yield
        yield from _ssd_chunk(first, xbc_ref, z_ref, dt_ref, cw_ref, cb_ref, dtb_ref, alog_ref, drep_ref,
                              ng_ref, tril3_ref, nconv_ref, nssm_ref, tail_ref, ht_ref, store_yb)

    @pl.when(has_ffn & has_mix)
    def _():
        _interleave(ffn_step(), mix_step(), MIX_PHASES_PER_FFN_PIECE, MIX_LEAD_PIECES)

    @pl.when(has_ffn & jnp.logical_not(has_mix) & jnp.logical_not(is_last))
    def _():
        _interleave(ffn_step(), iter(()), 0)

    @pl.when(is_last)
    def _():
        _interleave(ffn_step(with_sample=True), iter(()), 0)

    @pl.when(jnp.logical_not(has_ffn) & has_mix)
    def _():
        _interleave(iter(()), mix_step(), 0)

    @pl.when(has_ffn & (f == pl.num_programs(1) - 1))
    def _():
        o_ref[...] = _rms(o_ref[...], gf_ref[...])

    @pl.when(is_last & (f == pl.num_programs(1) - 1))
    def _():
        os_ref[...] = _rms(os_ref[...], gf_ref[...])


def _mix_ffn(x, proj, dt_raw, w_out_b, w_gate, w_up, w_down, pack, ws, bst, xs, ya_s, yb_s, batch, seq):
    m = x.shape[0]
    n_s = xs.shape[0]
    n_tiles = m // MIX_TM
    n_chunks = m // SSD_CHUNK
    chunks_per_seq = seq // SSD_CHUNK
    assert MIX_TM % SSD_CHUNK == 0 and seq % MIX_TM == 0 and CHUNKS_PER_TILE <= D_FF // FFN_TF

    def tile(s, f):
        return jnp.maximum(s - 1, 0)

    def chunk(s, f):
        return jnp.where(s >= n_tiles, n_chunks - 1,
                         s * CHUNKS_PER_TILE + jnp.minimum(f, CHUNKS_PER_TILE - 1))

    def seq_of(s, f):
        return chunk(s, f) // chunks_per_seq

    def ffn_block(s, f):
        return jnp.where(s == 0, 0, f)

    const2 = lambda s, f: (0, 0)
    return pl.pallas_call(
        functools.partial(_mix_ffn_kernel, chunks_per_seq=chunks_per_seq),
        grid=(n_tiles + 1, D_FF // FFN_TF),
        in_specs=[
            pl.BlockSpec((MIX_TM, D_MODEL), lambda s, f: (tile(s, f), 0)),
            pl.BlockSpec((D_MODEL, D_MODEL), const2, pipeline_mode=pl.Buffered(1)),
            pl.BlockSpec((D_MODEL, FFN_TF), lambda s, f: (0, ffn_block(s, f))),
            pl.BlockSpec((D_MODEL, FFN_TF), lambda s, f: (0, ffn_block(s, f))),
            pl.BlockSpec((FFN_TF, D_MODEL), lambda s, f: (ffn_block(s, f), 0)),
            pl.BlockSpec((PACK_ROWS, D_MODEL), const2),
            pl.BlockSpec((CHUNK_LEN, D_CHUNK), lambda s, f: (chunk(s, f), U_BLK)),
            pl.BlockSpec((CHUNK_LEN, D_CHUNK), lambda s, f: (chunk(s, f), V_BLK)),
            pl.BlockSpec((CHUNK_HEADS, CHUNK_LEN, CHUNK_LEN), lambda s, f: (0, 0, 0),
                         pipeline_mode=pl.Buffered(1)),
            pl.BlockSpec((CHUNK_LEN, CHUNK_HEADS), const2),
            pl.BlockSpec((SSD_CHUNK, CONV_DIM), lambda s, f: (chunk(s, f), XBC_BLK)),
            pl.BlockSpec((SSD_CHUNK, D_SSM), lambda s, f: (chunk(s, f), Z_BLK)),
            pl.BlockSpec((SSD_CHUNK, HEAD_PAD), lambda s, f: (chunk(s, f), 0)),
            pl.BlockSpec((n_s, D_MODEL), const2, pipeline_mode=pl.Buffered(1)),
            pl.BlockSpec((n_s, D_CHUNK), const2, pipeline_mode=pl.Buffered(1)),
            pl.BlockSpec((n_s, D_SSM), const2, pipeline_mode=pl.Buffered(1)),
        ],
        out_specs=[
            pl.BlockSpec((MIX_TM, D_MODEL), lambda s, f: (tile(s, f), 0)),
            pl.BlockSpec((None, CONV_W - 1, CONV_DIM), lambda s, f: (seq_of(s, f), 0, 0)),
            pl.BlockSpec((None, D_SSM, D_STATE), lambda s, f: (seq_of(s, f), 0, 0)),
            pl.BlockSpec((n_s, D_MODEL), const2),
        ],
        out_shape=[
            jax.ShapeDtypeStruct((m, D_MODEL), F32),
            jax.ShapeDtypeStruct((batch, CONV_W - 1, CONV_DIM), F32),
            jax.ShapeDtypeStruct((batch, D_SSM, D_STATE), F32),
            jax.ShapeDtypeStruct((n_s, D_MODEL), F32),
        ],
        scratch_shapes=[
            pltpu.VMEM((MIX_TM, D_MODEL), BF16),
            pltpu.VMEM((MIX_TM, D_CHUNK), BF16),
            pltpu.VMEM((MIX_TM, D_SSM), BF16),
            pltpu.VMEM((CONV_PAD, CONV_DIM), F32),
            pltpu.VMEM((D_STATE, D_SSM), F32),
            pltpu.VMEM((CHUNK_LEN, CHUNK_HEADS * CHUNK_LEN), BF16),
            pltpu.VMEM((n_s, D_MODEL), BF16),
            pltpu.VMEM((SSD_CHUNK, N_SPLIT * SSD_CHUNK), BF16),
        ],
        compiler_params=_params("arbitrary", "arbitrary"),
        name="mix_ffn",
    )(x, w_out_b, w_gate, w_up, w_down, pack, proj, proj, ws, bst, proj, proj, dt_raw, xs, ya_s, yb_s)


def kernel(x_prompt, x_sample, state_conv, state_ssm, norm_mix_g, w_in, chunk_v_norm_g, chunk_w_s,
           chunk_b_s, chunk_out_norm_g, ssd_conv_w, ssd_conv_b, ssd_dt_bias, ssd_a_log, ssd_d,
           ssd_norm_g, w_out, norm_ffn_g, w_gate, w_up, w_down, norm_final_g):
    depth = w_in.shape[0]
    batch, seq, _ = x_prompt.shape
    n_s = x_sample.shape[0]
    assert x_sample.shape[1] == 1 and seq % SSD_CHUNK == 0 and depth == 1

    hp = x_prompt.reshape(batch * seq, D_MODEL)
    hs = x_sample.reshape(n_s, D_MODEL)
    l = 0
    w_in_t = jnp.swapaxes(w_in, 1, 2)
    w_dt_t = jnp.pad(w_in_t[l, D_MAIN:, :], ((0, HEAD_PAD - SSM_HEADS), (0, 0))).astype(BF16)
    pack = _pack_params(
        g_mix=norm_mix_g[l], g_ffn=norm_ffn_g[l], g_final=norm_final_g,
        gv=chunk_v_norm_g[l], go=chunk_out_norm_g[l],
        ng=ssd_norm_g[l], drep=jnp.repeat(ssd_d[l], SSM_HEAD_DIM),
        cb=ssd_conv_b[l], cw=ssd_conv_w[l], dtb=ssd_dt_bias[l], alog=ssd_a_log[l],
        w0=jnp.repeat(chunk_w_s[l, :, 0, 0], CHUNK_HEAD_DIM),
        b0=jnp.repeat(chunk_b_s[l, :, 0], CHUNK_HEAD_DIM))

    proj_p, dt_p, proj_s, dt_s, w_gate_b, w_up_b, w_down_b, w_out_b = _in_proj(
        hp, hs, pack, w_in_t, l, w_dt_t, tm=1024, cast=(w_gate, w_up, w_down, w_out))

    ya_s, v_s, conv_s, split, b_s, c_s, dx_s = _sample_rows(
        proj_s, dt_s, jnp.swapaxes(state_conv[l], 0, 1), pack)
    ssm_s, yb_s = _sample_state(split, b_s, c_s, state_ssm[l].reshape(n_s, D_SSM, D_STATE),
                                dx_s, proj_s, pack)

    y_p, conv_p, ssm_p, y_s = _mix_ffn(
        hp, proj_p, dt_p, w_out_b, w_gate_b, w_up_b, w_down_b, pack,
        chunk_w_s[l].astype(F32), chunk_b_s[l].astype(F32).T, hs, ya_s, yb_s, batch, seq)

    return (
        y_p.reshape(batch, seq, D_MODEL),
        y_s.reshape(n_s, 1, D_MODEL),
        conv_p[None],
        ssm_p.reshape(1, batch, SSM_HEADS, SSM_HEAD_DIM, D_STATE),
        jnp.swapaxes(conv_s, 0, 1)[None],
        ssm_s.reshape(1, n_s, SSM_HEADS, SSM_HEAD_DIM, D_STATE),
        v_s.reshape(1, n_s, 1, D_CHUNK),
    )
```

```python
import functools

import jax
import jax.numpy as jnp
from jax import lax
from jax.experimental import pallas as pl
from jax.experimental.pallas import tpu as pltpu

F32 = jnp.float32
BF16 = jnp.bfloat16
HIGHEST = lax.Precision.HIGHEST

D_MODEL = 2048
D_CHUNK = 1024
CHUNK_HEADS = 8
CHUNK_HEAD_DIM = 128
CHUNK_LEN = 128
D_SSM = 1024
SSM_HEAD_DIM = 64
SSM_HEADS = 16
SSM_GROUPS = 2
HEADS_PER_GROUP = SSM_HEADS // SSM_GROUPS
GROUP_WIDTH = D_SSM // SSM_GROUPS
D_STATE = 128
CONV_W = 4
CONV_DIM = D_SSM + 2 * SSM_GROUPS * D_STATE
SSD_CHUNK = 128
D_MAIN = 2 * D_CHUNK + D_SSM + CONV_DIM
D_FF = 5632
EPS = 1e-6

LANES = 128
SUBLANES = 8
BF16_SUBLANES = 16
HEAD_PAD = LANES
N_SPLIT = 3
VMEM_LIMIT = 58 * 1024 * 1024

U_BLK, V_BLK, Z_BLK = 0, 1, 2
XBC_BLK = 2


def _rms(x, g):
    return x * lax.rsqrt(jnp.mean(x * x, axis=-1, keepdims=True) + EPS) * g


def _silu(x):
    return x * jax.nn.sigmoid(x)


def _softplus(x):
    return jnp.maximum(x, 0.0) + jnp.log1p(jnp.exp(-jnp.abs(x)))


def _dot(a, b):
    return jnp.dot(a, b, preferred_element_type=F32)


def _dot_nt(a, b):
    return lax.dot_general(a, b, (((1,), (1,)), ((), ())), preferred_element_type=F32)


def _dot_exact(a, b):
    return jnp.dot(a, b, precision=HIGHEST, preferred_element_type=F32)


def _params(*semantics):
    return pltpu.CompilerParams(dimension_semantics=semantics, vmem_limit_bytes=VMEM_LIMIT)


PACK_ROWS = 16
PACK_LAYOUT = {
    "g_mix": (0, 1, 0, D_MODEL),
    "g_ffn": (1, 1, 0, D_MODEL),
    "g_final": (2, 1, 0, D_MODEL),
    "gv": (3, 1, 0, D_CHUNK),
    "go": (3, 1, D_CHUNK, D_CHUNK),
    "ng": (4, 1, 0, D_SSM),
    "drep": (4, 1, D_SSM, D_SSM),
    "cb": (5, 1, 0, CONV_DIM),
    "cw": (6, CONV_W, 0, CONV_DIM),
    "dtb": (10, 1, 0, HEAD_PAD),
    "alog": (10, 1, HEAD_PAD, HEAD_PAD),
    "w0": (11, 1, 0, D_CHUNK),
    "b0": (11, 1, D_CHUNK, D_CHUNK),
}


class _Packed:
    def __init__(self, ref, name):
        self.ref = ref
        self.row, self.rows, self.lane, self.lanes = PACK_LAYOUT[name]

    def __getitem__(self, idx):
        lanes = slice(self.lane, self.lane + self.lanes)
        if idx is Ellipsis:
            return self.ref[self.row:self.row + self.rows, lanes]
        rows, cols = idx
        assert cols == slice(None)
        return self.ref[self.row + rows.start:self.row + rows.stop, lanes]


def _pack_params(**vectors):
    segments = [[] for _ in range(PACK_ROWS)]
    for name, value in vectors.items():
        row, n_rows, lane, lanes = PACK_LAYOUT[name]
        value = value.astype(F32).reshape(n_rows, -1)
        assert value.shape[1] <= lanes
        for r in range(n_rows):
            segments[row + r].append((lane, lanes, value[r]))
    parts = []
    for segs in segments:
        at = 0
        for lane, lanes, value in sorted(segs, key=lambda seg: seg[0]):
            parts += [jnp.zeros((lane - at,), F32), value, jnp.zeros((lanes - value.shape[0],), F32)]
            at = lane + lanes
        parts.append(jnp.zeros((D_MODEL - at,), F32))
    return jnp.concatenate([p for p in parts if p.shape[0]]).reshape(PACK_ROWS, D_MODEL)


def _head_expander(width, log2_per_head):
    row = lax.broadcasted_iota(jnp.int32, (HEAD_PAD, width), 0)
    col = lax.broadcasted_iota(jnp.int32, (HEAD_PAD, width), 1)
    return (row == jnp.right_shift(col, log2_per_head)).astype(F32)


IN_TN = 512


def _in_proj_kernel(x_hbm, xs_hbm, pack_ref, w_ref, wdt_ref, *refs, tm, n_cast):
    g_ref = _Packed(pack_ref, "g_mix")
    cast_in = refs[:n_cast]
    o_ref, dt_ref, os_ref, dts_ref = refs[n_cast:n_cast + 4]
    cast_out = refs[n_cast + 4:2 * n_cast + 4]
    xn_ref, wres_ref, xbuf_ref, sem = refs[2 * n_cast + 4:]
    i = pl.program_id(0)
    j = pl.program_id(1)
    last = pl.num_programs(0) - 1
    n_s = xs_hbm.shape[0]

    for src, dst in zip(cast_in, cast_out, strict=True):
        dst[...] = src[...].astype(BF16)

    def x_copy(tile):
        return pltpu.make_async_copy(x_hbm.at[pl.ds(tile * tm, tm), :], xbuf_ref, sem)

    def xs_copy():
        return pltpu.make_async_copy(xs_hbm, xbuf_ref.at[pl.ds(0, n_s), :], sem)

    @pl.when((i == 0) & (j == 0))
    def _():
        xs_copy().start()
        xs_copy().wait()
        nb = _rms(xbuf_ref[0:n_s, :], g_ref[...]).astype(BF16)
        xn_ref[tm:tm + n_s, :] = nb
        dts_ref[...] = _dot_nt(nb, wdt_ref[...])
        x_copy(0).start()

    @pl.when(i == 0)
    def _():
        wres_ref[j] = w_ref[...].T.astype(BF16)

    @pl.when(j == 0)
    def _():
        x_copy(i).wait()
        nb = _rms(xbuf_ref[...], g_ref[...]).astype(BF16)
        xn_ref[0:tm, :] = nb
        dt_ref[...] = _dot_nt(nb, wdt_ref[...])

    @pl.when((j == 1) & (i < last))
    def _():
        x_copy(i + 1).start()

    @pl.when(i < last)
    def _():
        o_ref[...] = _dot(xn_ref[0:tm, :], wres_ref[j])

    @pl.when(i == last)
    def _():
        res = _dot(xn_ref[...], wres_ref[j])
        o_ref[...] = res[0:tm, :]
        os_ref[...] = res[tm:, :]


def _in_proj(x, xs, pack, w_in_t, layer, w_dt_t, tm, cast=()):
    m = x.shape[0]
    n_s = xs.shape[0]
    n_i = m // tm
    n_j = D_MAIN // IN_TN
    assert n_j >= 2 and n_s <= tm and n_s % BF16_SUBLANES == 0
    cast_in, cast_out, cast_shapes = [], [], []
    n_steps = n_i * n_j

    def col_sample(i, j):
        return jnp.where(i == n_i - 1, j, 0)

    for w in cast:
        _, w_rows, w_cols = w.shape
        slab_rows = next(r for r in range(BF16_SUBLANES, w_rows + 1, BF16_SUBLANES)
                         if w_rows % r == 0 and w_rows // r <= n_steps)
        n_slabs = w_rows // slab_rows
        hold = n_steps // n_slabs
        slab = lambda i, j, n_slabs=n_slabs, hold=hold: jnp.minimum((i * n_j + j) // hold, n_slabs - 1)
        cast_in.append(pl.BlockSpec((None, slab_rows, w_cols), lambda i, j, slab=slab: (layer, slab(i, j), 0)))
        cast_out.append(pl.BlockSpec((slab_rows, w_cols), lambda i, j, slab=slab: (slab(i, j), 0)))
        cast_shapes.append(jax.ShapeDtypeStruct((w_rows, w_cols), BF16))
    return pl.pallas_call(
        functools.partial(_in_proj_kernel, tm=tm, n_cast=len(cast)),
        grid=(n_i, n_j),
        in_specs=[
            pl.BlockSpec(memory_space=pl.ANY),
            pl.BlockSpec(memory_space=pl.ANY),
            pl.BlockSpec((PACK_ROWS, D_MODEL), lambda i, j: (0, 0)),
            pl.BlockSpec((None, IN_TN, D_MODEL), lambda i, j: (layer, jnp.where(i == 0, j, n_j - 1), 0)),
            pl.BlockSpec((HEAD_PAD, D_MODEL), lambda i, j: (0, 0)),
            *cast_in,
        ],
        out_specs=[
            pl.BlockSpec((tm, IN_TN), lambda i, j: (i, j)),
            pl.BlockSpec((tm, HEAD_PAD), lambda i, j: (i, 0)),
            pl.BlockSpec((n_s, IN_TN), lambda i, j: (0, col_sample(i, j))),
            pl.BlockSpec((n_s, HEAD_PAD), lambda i, j: (0, 0)),
            *cast_out,
        ],
        out_shape=[
            jax.ShapeDtypeStruct((m, D_MAIN), F32),
            jax.ShapeDtypeStruct((m, HEAD_PAD), F32),
            jax.ShapeDtypeStruct((n_s, D_MAIN), F32),
            jax.ShapeDtypeStruct((n_s, HEAD_PAD), F32),
            *cast_shapes,
        ],
        scratch_shapes=[
            pltpu.VMEM((tm + n_s, D_MODEL), BF16),
            pltpu.VMEM((n_j, D_MODEL, IN_TN), BF16),
            pltpu.VMEM((tm, D_MODEL), F32),
            pltpu.SemaphoreType.DMA(()),
        ],
        compiler_params=_params("arbitrary", "arbitrary"),
        name="in_proj",
    )(x, xs, pack, w_in_t, w_dt_t, *cast)


def _mask_chunk_weights(ws_ref, wm_ref):
    row = lax.broadcasted_iota(jnp.int32, (CHUNK_LEN, CHUNK_LEN), 0)
    col = lax.broadcasted_iota(jnp.int32, (CHUNK_LEN, CHUNK_LEN), 1)
    for h in range(CHUNK_HEADS):
        wm_ref[:, h * CHUNK_LEN:(h + 1) * CHUNK_LEN] = jnp.where(row >= col, ws_ref[h], 0.0).astype(BF16)


def _chunk_gate(u_ref, v_ref, gv_ref, wm_ref, bst_ref, go_ref, store_y):
    vb = _rms(jax.nn.gelu(v_ref[...]), gv_ref[...]).astype(BF16)
    yield
    u = jax.nn.gelu(u_ref[...])
    yield
    zeros = jnp.zeros((CHUNK_LEN, CHUNK_HEAD_DIM), BF16)
    parts = []
    for h0 in range(0, CHUNK_HEADS, 2):
        w = wm_ref[:, h0 * CHUNK_LEN:(h0 + 2) * CHUNK_LEN]
        va = vb[:, h0 * CHUNK_HEAD_DIM:(h0 + 1) * CHUNK_HEAD_DIM]
        vb2 = vb[:, (h0 + 1) * CHUNK_HEAD_DIM:(h0 + 2) * CHUNK_HEAD_DIM]
        rhs = jnp.concatenate([jnp.concatenate([va, zeros], axis=1),
                               jnp.concatenate([zeros, vb2], axis=1)], axis=0)
        mixed = _dot(w, rhs)
        for k, h in enumerate((h0, h0 + 1)):
            sl = slice(h * CHUNK_HEAD_DIM, (h + 1) * CHUNK_HEAD_DIM)
            mh = mixed[:, k * CHUNK_HEAD_DIM:(k + 1) * CHUNK_HEAD_DIM] + bst_ref[:, h:h + 1]
            parts.append(u[:, sl] * mh)
        yield
    y = jnp.concatenate(parts, axis=1)
    store_y(_rms(y, go_ref[...]).astype(BF16))


CONV_PAD = 8


def _group_norm_out(y, z, g):
    y = y * _silu(z)
    parts = []
    for grp in range(SSM_GROUPS):
        yg = y[:, grp * GROUP_WIDTH:(grp + 1) * GROUP_WIDTH]
        parts.append(yg * lax.rsqrt(jnp.mean(yg * yg, axis=-1, keepdims=True) + EPS))
    return (jnp.concatenate(parts, axis=1) * g).astype(BF16)


def _split3(x):
    p1 = x.astype(BF16)
    r1 = x - p1.astype(F32)
    p2 = r1.astype(BF16)
    p3 = (r1 - p2.astype(F32)).astype(BF16)
    return p1, p2, p3


def _select_rows(sel3, x):
    return _dot(sel3, jnp.concatenate(_split3(x), axis=0))


def _ssd_chunk(first, xbc_ref, z_ref, dt_ref, cw_ref, cb_ref, dtb_ref, alog_ref, drep_ref, ng_ref,
               tril3_ref, nconv_ref, nssm_ref, tail_ref, ht_ref, store_y):
    q = SSD_CHUNK

    xbc = xbc_ref[...]
    tail = jnp.where(first, 0.0, tail_ref[...])
    ht_prev = jnp.where(first, 0.0, ht_ref[...])
    sub = lax.broadcasted_iota(jnp.int32, (CONV_PAD, CONV_DIM), 0)
    conv = cb_ref[...] + xbc * cw_ref[CONV_W - 1:CONV_W, :]
    for k in range(1, CONV_W):
        shifted = pltpu.roll(xbc, k, 0)
        head = jnp.where(sub < k, pltpu.roll(tail, k, 0), shifted[0:CONV_PAD, :])
        shifted = jnp.concatenate([head, shifted[CONV_PAD:, :]], axis=0)
        conv = conv + shifted * cw_ref[CONV_W - 1 - k:CONV_W - k, :]
    tail_ref[...] = xbc[q - CONV_PAD:, :]
    nconv_ref[...] = xbc[q - (CONV_W - 1):, :]
    yield

    act = _silu(conv)
    xs = act[:, :D_SSM]
    bmat = act[:, D_SSM:D_SSM + SSM_GROUPS * D_STATE]
    cmat = act[:, D_SSM + SSM_GROUPS * D_STATE:]
    yield

    dt = _softplus(dt_ref[...] + dtb_ref[...])
    a = -jnp.exp(alog_ref[...])
    acum = _select_rows(tril3_ref[...], dt * a)
    acum_t = acum.T
    low_lanes = lax.broadcasted_iota(jnp.int32, (q, 2 * SSM_HEAD_DIM), 1) < SSM_HEAD_DIM

    def expand(cols):
        pairs = []
        for h0 in range(0, SSM_HEADS, 2):
            a = jnp.broadcast_to(cols[:, h0:h0 + 1], (q, 2 * SSM_HEAD_DIM))
            b = jnp.broadcast_to(cols[:, h0 + 1:h0 + 2], (q, 2 * SSM_HEAD_DIM))
            pairs.append(jnp.where(low_lanes, a, b))
        return jnp.concatenate(pairs, axis=1)

    dt_rep = expand(dt)
    yield
    acum_rep = expand(acum)
    yield
    xdt = xs * dt_rep
    exp_a = jnp.exp(acum_rep)
    xw = xdt * jnp.exp(acum_rep[q - 1:q, :] - acum_rep)
    yield

    row = lax.broadcasted_iota(jnp.int32, (q, q), 0)
    col = lax.broadcasted_iota(jnp.int32, (q, q), 1)
    causal = row >= col
    low_half = col < SSM_HEAD_DIM
    ydiag, yoff, states = [], [], []
    for grp in range(SSM_GROUPS):
        bg_f = bmat[:, grp * D_STATE:(grp + 1) * D_STATE]
        bg = bg_f.astype(BF16)
        cg = cmat[:, grp * D_STATE:(grp + 1) * D_STATE].astype(BF16)
        cb = _dot_nt(cg, bg)
        ch = slice(grp * GROUP_WIDTH, (grp + 1) * GROUP_WIDTH)
        yoff.append(_dot(cg, ht_prev[:, ch].astype(BF16)))
        yield
        for quad in range(HEADS_PER_GROUP // 4):
            h0 = grp * HEADS_PER_GROUP + 4 * quad
            masks, blocks = [], []
            for k, h in enumerate(range(h0, h0 + 4)):
                diff = acum[:, h:h + 1] - acum_t[h:h + 1, :]
                masks.append((cb * jnp.exp(jnp.where(causal, diff, -jnp.inf))).astype(BF16))
                xp = xdt[:, (h - k % 2) * SSM_HEAD_DIM:(h - k % 2 + 2) * SSM_HEAD_DIM]
                own = jnp.where(low_half == (k % 2 == 0), xp, 0.0).astype(BF16)
                zero = jnp.zeros_like(own)
                blocks.append(jnp.concatenate([own, zero] if k < 2 else [zero, own], axis=1))
                if k % 2 == 1:
                    yield
            ydiag.append(_dot(jnp.concatenate(masks, axis=1), jnp.concatenate(blocks, axis=0)))
            yield
        states.append(_dot(bg_f.T.astype(BF16), xw[:, ch].astype(BF16)))
        yield
    y = drep_ref[...] * xs + jnp.concatenate(ydiag, axis=1) + jnp.concatenate(yoff, axis=1) * exp_a

    ht_new = ht_prev * exp_a[q - 1:q, :] + jnp.concatenate(states, axis=1)
    ht_ref[...] = ht_new
    nssm_ref[...] = ht_new.T
    yield
    store_y(_group_norm_out(y, z_ref[...], ng_ref[...]))


def _sample_rows_kernel(u_ref, v_ref, xbc_ref, dt_ref, sc_ref, pack_ref,
                        ya_ref, vrows_ref, nconv_ref, split_ref, b_ref, c_ref, dx_ref):
    gv_ref, w0_ref, b0_ref, go_ref, cw_ref, cb_ref, dtb_ref, alog_ref, drep_ref = (
        _Packed(pack_ref, name) for name in ("gv", "w0", "b0", "go", "cw", "cb", "dtb", "alog", "drep"))
    vr =_rms(jax.nn.gelu(v_ref[...]), gv_ref[...])
    vrows_ref[...] = vr
    mixed = vr * w0_ref[...] + b0_ref[...]
    ya_ref[...] = _rms(jax.nn.gelu(u_ref[...]) * mixed, go_ref[...]).astype(BF16)

    xbc = xbc_ref[...]
    conv = cb_ref[...]
    for k in range(CONV_W - 1):
        conv = conv + sc_ref[k] * cw_ref[k:k + 1, :]
    conv = conv + xbc * cw_ref[CONV_W - 1:CONV_W, :]
    for k in range(CONV_W - 2):
        nconv_ref[k] = sc_ref[k + 1]
    nconv_ref[CONV_W - 2] = xbc

    act = _silu(conv)
    xs = act[:, :D_SSM]
    b_ref[...] = act[:, D_SSM:D_SSM + SSM_GROUPS * D_STATE]
    c_ref[...] = act[:, D_SSM + SSM_GROUPS * D_STATE:]
    dx_ref[...] = drep_ref[...] * xs

    dt = _softplus(dt_ref[...] + dtb_ref[...])
    decay = jnp.exp(dt * (-jnp.exp(alog_ref[...])))
    xdt = xs * _dot_exact(dt, _head_expander(D_SSM, SSM_HEAD_DIM.bit_length() - 1))
    decay8 = _dot_exact(decay, _head_expander(HEAD_PAD, SUBLANES.bit_length() - 1))
    cols = jnp.concatenate([xdt, decay8], axis=1).T
    for p in range(N_SPLIT):
        piece = cols.astype(BF16)
        split_ref[:, p * LANES:(p + 1) * LANES] = piece
        cols = cols - piece.astype(F32)


SPLIT_ROWS = D_SSM + HEAD_PAD


def _sample_rows(proj, dt_raw, sconv, pack):
    n = proj.shape[0]
    full = lambda shape: pl.BlockSpec(shape, lambda i: (0,) * len(shape))
    return pl.pallas_call(
        _sample_rows_kernel,
        grid=(1,),
        in_specs=[
            pl.BlockSpec((n, D_CHUNK), lambda i: (0, U_BLK)),
            pl.BlockSpec((n, D_CHUNK), lambda i: (0, V_BLK)),
            pl.BlockSpec((n, CONV_DIM), lambda i: (0, XBC_BLK)),
            full((n, HEAD_PAD)),
            full((CONV_W - 1, n, CONV_DIM)),
            full((PACK_ROWS, D_MODEL)),
        ],
        out_specs=[
            full((n, D_CHUNK)), full((n, D_CHUNK)), full((CONV_W - 1, n, CONV_DIM)),
            full((SPLIT_ROWS, N_SPLIT * LANES)),
            full((n, SSM_GROUPS * D_STATE)), full((n, SSM_GROUPS * D_STATE)), full((n, D_SSM)),
        ],
        out_shape=[
            jax.ShapeDtypeStruct((n, D_CHUNK), BF16),
            jax.ShapeDtypeStruct((n, D_CHUNK), F32),
            jax.ShapeDtypeStruct((CONV_W - 1, n, CONV_DIM), F32),
            jax.ShapeDtypeStruct((SPLIT_ROWS, N_SPLIT * LANES), BF16),
            jax.ShapeDtypeStruct((n, SSM_GROUPS * D_STATE), F32),
            jax.ShapeDtypeStruct((n, SSM_GROUPS * D_STATE), F32),
            jax.ShapeDtypeStruct((n, D_SSM), F32),
        ],
        compiler_params=_params("arbitrary"),
        name="sample_rows",
    )(proj, proj, proj, dt_raw, sconv, pack)


SAMPLE_TB = 16


def _sample_state_kernel(split_ref, b_ref, c_ref, h0_ref, dx_ref, z_ref, pack_ref,
                         hn_ref, y_ref, yt_ref):
    ng_ref = _Packed(pack_ref, "ng")
    i = pl.program_id(0)
    n_tok = b_ref.shape[0]

    @pl.when(i == 0)
    def _():
        yt_ref[...] = jnp.zeros_like(yt_ref)

    tok = lax.broadcasted_iota(jnp.int32, (N_SPLIT * LANES, D_STATE), 0) % LANES
    lane = lax.broadcasted_iota(jnp.int32, (D_SSM, n_tok), 1)

    def per_group(row, grp):
        return jnp.broadcast_to(row[:, grp * D_STATE:(grp + 1) * D_STATE], (GROUP_WIDTH, D_STATE))

    for k in range(SAMPLE_TB):
        t = i * SAMPLE_TB + k
        onehot = (tok == t).astype(BF16)
        bcast = _dot(split_ref[...], onehot)
        decay = jnp.broadcast_to(
            bcast[D_SSM:, :].reshape(SSM_HEADS, 1, SUBLANES, D_STATE),
            (SSM_HEADS, SSM_HEAD_DIM // SUBLANES, SUBLANES, D_STATE)).reshape(D_SSM, D_STATE)
        brow = b_ref[pl.ds(t, 1), :]
        crow = c_ref[pl.ds(t, 1), :]
        bfull = jnp.concatenate([per_group(brow, g) for g in range(SSM_GROUPS)], axis=0)
        cfull = jnp.concatenate([per_group(crow, g) for g in range(SSM_GROUPS)], axis=0)
        h_new = h0_ref[k] * decay + bcast[:D_SSM, :] * bfull
        hn_ref[k] = h_new
        ysum = jnp.sum(h_new * cfull, axis=-1, keepdims=True)
        yt_ref[...] = jnp.where(lane == t, ysum, yt_ref[...])

    @pl.when(i == pl.num_programs(0) - 1)
    def _():
        y = yt_ref[...].T + dx_ref[...]
        y_ref[...] = _group_norm_out(y, z_ref[...], ng_ref[...])


def _sample_state(split, bmat, cmat, h0, dx, proj, pack):
    n = h0.shape[0]
    const2 = lambda i: (0, 0)
    return pl.pallas_call(
        _sample_state_kernel,
        grid=(n // SAMPLE_TB,),
        in_specs=[
            pl.BlockSpec((SPLIT_ROWS, N_SPLIT * LANES), const2),
            pl.BlockSpec((n, SSM_GROUPS * D_STATE), const2),
            pl.BlockSpec((n, SSM_GROUPS * D_STATE), const2),
            pl.BlockSpec((SAMPLE_TB, D_SSM, D_STATE), lambda i: (i, 0, 0)),
            pl.BlockSpec((n, D_SSM), const2),
            pl.BlockSpec((n, D_SSM), lambda i: (0, Z_BLK)),
            pl.BlockSpec((PACK_ROWS, D_MODEL), const2),
        ],
        out_specs=[
            pl.BlockSpec((SAMPLE_TB, D_SSM, D_STATE), lambda i: (i, 0, 0)),
            pl.BlockSpec((n, D_SSM), const2),
        ],
        out_shape=[
            jax.ShapeDtypeStruct((n, D_SSM, D_STATE), F32),
            jax.ShapeDtypeStruct((n, D_SSM), BF16),
        ],
        scratch_shapes=[pltpu.VMEM((D_SSM, n), F32)],
        compiler_params=_params("arbitrary"),
        name="sample_state",
    )(split, bmat, cmat, h0, dx, proj, pack)


FFN_TF = 512
MIX_TM = 512
CHUNKS_PER_TILE = MIX_TM // SSD_CHUNK
FFN_UP_PIECE = 256
FFN_DOWN_PIECE = 512
MIX_PHASES_PER_FFN_PIECE = 5
MIX_LEAD_PIECES = 0


def _interleave(main, side, side_per_main, lead=0):
    for k, _ in enumerate(main):
        if k >= lead:
            for _ in range(side_per_main):
                next(side, None)
    for _ in side:
        pass


def _mix_ffn_kernel(x_ref, wo_ref, wg_ref, wu_ref, wd_ref, pack_ref, u_ref, v_ref, ws_ref, bst_ref,
                    xbc_ref, z_ref, dt_ref, xs_ref, yas_ref, ybs_ref,
                    o_ref, nconv_ref, nssm_ref, os_ref,
                    m_ref, ya_ref, yb_ref, tail_ref, ht_ref, wm_ref, ms_ref, tril3_ref, *, chunks_per_seq):
    g_ref, gf_ref, gv_ref, go_ref, cw_ref, cb_ref, dtb_ref, alog_ref, drep_ref, ng_ref = (
        _Packed(pack_ref, name)
        for name in ("g_ffn", "g_final", "gv", "go", "cw", "cb", "dtb", "alog", "drep", "ng"))
    s = pl.program_id(0)
    f = pl.program_id(1)
    has_ffn = s >= 1
    is_last = s == pl.num_programs(0) - 1
    has_mix = jnp.logical_not(is_last) & (f < CHUNKS_PER_TILE)

    @pl.when((s == 0) & (f == 0))
    def _():
        _mask_chunk_weights(ws_ref, wm_ref)
        row = lax.broadcasted_iota(jnp.int32, tril3_ref.shape, 0)
        col = lax.broadcasted_iota(jnp.int32, tril3_ref.shape, 1)
        tril3_ref[...] = (row >= jnp.bitwise_and(col, SSD_CHUNK - 1)).astype(BF16)

    def project(x, ya, yb, m_out, o_out):
        h = x + _dot(jnp.concatenate([ya, yb], axis=1), wo_ref[...])
        m_out[...] = _rms(h, g_ref[...]).astype(BF16)
        o_out[...] = h

    @pl.when(has_ffn & (f == 0))
    def _():
        project(x_ref[...], ya_ref[...], yb_ref[...], m_ref, o_ref)

    @pl.when(is_last & (f == 0))
    def _():
        project(xs_ref[...], yas_ref[...], ybs_ref[...], ms_ref, os_ref)

    def ffn_step(with_sample=False):
        m = jnp.concatenate([m_ref[...], ms_ref[...]], axis=0) if with_sample else m_ref[...]
        acts = []
        for c in range(FFN_TF // FFN_UP_PIECE):
            cols = slice(c * FFN_UP_PIECE, (c + 1) * FFN_UP_PIECE)
            halves = []
            for w_ref in (wg_ref, wu_ref):
                halves.append(_dot(m, w_ref[:, cols]))
                yield
            acts.append((_silu(halves[0]) * halves[1]).astype(BF16))
        act = jnp.concatenate(acts, axis=1)
        for c in range(D_MODEL // FFN_DOWN_PIECE):
            cols = slice(c * FFN_DOWN_PIECE, (c + 1) * FFN_DOWN_PIECE)
            res = _dot(act, wd_ref[:, cols])
            o_ref[:, cols] += res[:MIX_TM]
            if with_sample:
                os_ref[:, cols] += res[MIX_TM:]
            yield

    chunk_rows = pl.ds(pl.multiple_of(f * SSD_CHUNK, SSD_CHUNK), SSD_CHUNK)

    def store_ya(ya):
        ya_ref[chunk_rows, :] = ya

    def store_yb(yb):
        yb_ref[chunk_rows, :] = yb

    def mix_step():
        first = (s * CHUNKS_PER_TILE + f) % chunks_per_seq == 0
        yield from _chunk_gate(u_ref, v_ref, gv_ref, wm_ref, bst_ref, go_ref, store_ya)
        yield
        yield from _ssd_chunk(first, xbc_ref, z_ref, dt_ref, cw_ref, cb_ref, dtb_ref, alog_ref, drep_ref,
                              ng_ref, tril3_ref, nconv_ref, nssm_ref, tail_ref, ht_ref, store_yb)

    @pl.when(has_ffn & has_mix)
    def _():
        _interleave(ffn_step(), mix_step(), MIX_PHASES_PER_FFN_PIECE, MIX_LEAD_PIECES)

    @pl.when(has_ffn & jnp.logical_not(has_mix) & jnp.logical_not(is_last))
    def _():
        _interleave(ffn_step(), iter(()), 0)

    @pl.when(is_last)
    def _():
        _interleave(ffn_step(with_sample=True), iter(()), 0)

    @pl.when(jnp.logical_not(has_ffn) & has_mix)
    def _():
        _interleave(iter(()), mix_step(), 0)

    @pl.when(has_ffn & (f == pl.num_programs(1) - 1))
    def _():
        o_ref[...] = _rms(o_ref[...], gf_ref[...])

    @pl.when(is_last & (f == pl.num_programs(1) - 1))
    def _():
        os_ref[...] = _rms(os_ref[...], gf_ref[...])


def _mix_ffn(x, proj, dt_raw, w_out_b, w_gate, w_up, w_down, pack, ws, bst, xs, ya_s, yb_s, batch, seq):
    m = x.shape[0]
    n_s = xs.shape[0]
    n_tiles = m // MIX_TM
    n_chunks = m // SSD_CHUNK
    chunks_per_seq = seq // SSD_CHUNK
    assert MIX_TM % SSD_CHUNK == 0 and seq % MIX_TM == 0 and CHUNKS_PER_TILE <= D_FF // FFN_TF

    def tile(s, f):
        return jnp.maximum(s - 1, 0)

    def chunk(s, f):
        return jnp.where(s >= n_tiles, n_chunks - 1,
                         s * CHUNKS_PER_TILE + jnp.minimum(f, CHUNKS_PER_TILE - 1))

    def seq_of(s, f):
        return chunk(s, f) // chunks_per_seq

    def ffn_block(s, f):
        return jnp.where(s == 0, 0, f)

    const2 = lambda s, f: (0, 0)
    return pl.pallas_call(
        functools.partial(_mix_ffn_kernel, chunks_per_seq=chunks_per_seq),
        grid=(n_tiles + 1, D_FF // FFN_TF),
        in_specs=[
            pl.BlockSpec((MIX_TM, D_MODEL), lambda s, f: (tile(s, f), 0)),
            pl.BlockSpec((D_MODEL, D_MODEL), const2, pipeline_mode=pl.Buffered(1)),
            pl.BlockSpec((D_MODEL, FFN_TF), lambda s, f: (0, ffn_block(s, f))),
            pl.BlockSpec((D_MODEL, FFN_TF), lambda s, f: (0, ffn_block(s, f))),
            pl.BlockSpec((FFN_TF, D_MODEL), lambda s, f: (ffn_block(s, f), 0)),
            pl.BlockSpec((PACK_ROWS, D_MODEL), const2),
            pl.BlockSpec((CHUNK_LEN, D_CHUNK), lambda s, f: (chunk(s, f), U_BLK)),
            pl.BlockSpec((CHUNK_LEN, D_CHUNK), lambda s, f: (chunk(s, f), V_BLK)),
            pl.BlockSpec((CHUNK_HEADS, CHUNK_LEN, CHUNK_LEN), lambda s, f: (0, 0, 0),
                         pipeline_mode=pl.Buffered(1)),
            pl.BlockSpec((CHUNK_LEN, CHUNK_HEADS), const2),
            pl.BlockSpec((SSD_CHUNK, CONV_DIM), lambda s, f: (chunk(s, f), XBC_BLK)),
            pl.BlockSpec((SSD_CHUNK, D_SSM), lambda s, f: (chunk(s, f), Z_BLK)),
            pl.BlockSpec((SSD_CHUNK, HEAD_PAD), lambda s, f: (chunk(s, f), 0)),
            pl.BlockSpec((n_s, D_MODEL), const2, pipeline_mode=pl.Buffered(1)),
            pl.BlockSpec((n_s, D_CHUNK), const2, pipeline_mode=pl.Buffered(1)),
            pl.BlockSpec((n_s, D_SSM), const2, pipeline_mode=pl.Buffered(1)),
        ],
        out_specs=[
            pl.BlockSpec((MIX_TM, D_MODEL), lambda s, f: (tile(s, f), 0)),
            pl.BlockSpec((None, CONV_W - 1, CONV_DIM), lambda s, f: (seq_of(s, f), 0, 0)),
            pl.BlockSpec((None, D_SSM, D_STATE), lambda s, f: (seq_of(s, f), 0, 0)),
            pl.BlockSpec((n_s, D_MODEL), const2),
        ],
        out_shape=[
            jax.ShapeDtypeStruct((m, D_MODEL), F32),
            jax.ShapeDtypeStruct((batch, CONV_W - 1, CONV_DIM), F32),
            jax.ShapeDtypeStruct((batch, D_SSM, D_STATE), F32),
            jax.ShapeDtypeStruct((n_s, D_MODEL), F32),
        ],
        scratch_shapes=[
            pltpu.VMEM((MIX_TM, D_MODEL), BF16),
            pltpu.VMEM((MIX_TM, D_CHUNK), BF16),
            pltpu.VMEM((MIX_TM, D_SSM), BF16),
            pltpu.VMEM((CONV_PAD, CONV_DIM), F32),
            pltpu.VMEM((D_STATE, D_SSM), F32),
            pltpu.VMEM((CHUNK_LEN, CHUNK_HEADS * CHUNK_LEN), BF16),
            pltpu.VMEM((n_s, D_MODEL), BF16),
            pltpu.VMEM((SSD_CHUNK, N_SPLIT * SSD_CHUNK), BF16),
        ],
        compiler_params=_params("arbitrary", "arbitrary"),
        name="mix_ffn",
    )(x, w_out_b, w_gate, w_up, w_down, pack, proj, proj, ws, bst, proj, proj, dt_raw, xs, ya_s, yb_s)


def kernel(x_prompt, x_sample, state_conv, state_ssm, norm_mix_g, w_in, chunk_v_norm_g, chunk_w_s,
           chunk_b_s, chunk_out_norm_g, ssd_conv_w, ssd_conv_b, ssd_dt_bias, ssd_a_log, ssd_d,
           ssd_norm_g, w_out, norm_ffn_g, w_gate, w_up, w_down, norm_final_g):
    depth = w_in.shape[0]
    batch, seq, _ = x_prompt.shape
    n_s = x_sample.shape[0]
    assert x_sample.shape[1] == 1 and seq % SSD_CHUNK == 0 and depth == 1

    hp = x_prompt.reshape(batch * seq, D_MODEL)
    hs = x_sample.reshape(n_s, D_MODEL)
    l = 0
    w_in_t = jnp.swapaxes(w_in, 1, 2)
    w_dt_t = jnp.pad(w_in_t[l, D_MAIN:, :], ((0, HEAD_PAD - SSM_HEADS), (0, 0))).astype(BF16)
    pack = _pack_params(
        g_mix=norm_mix_g[l], g_ffn=norm_ffn_g[l], g_final=norm_final_g,
        gv=chunk_v_norm_g[l], go=chunk_out_norm_g[l],
        ng=ssd_norm_g[l], drep=jnp.repeat(ssd_d[l], SSM_HEAD_DIM),
        cb=ssd_conv_b[l], cw=ssd_conv_w[l], dtb=ssd_dt_bias[l], alog=ssd_a_log[l],
        w0=jnp.repeat(chunk_w_s[l, :, 0, 0], CHUNK_HEAD_DIM),
        b0=jnp.repeat(chunk_b_s[l, :, 0], CHUNK_HEAD_DIM))

    proj_p, dt_p, proj_s, dt_s, w_gate_b, w_up_b, w_down_b, w_out_b = _in_proj(
        hp, hs, pack, w_in_t, l, w_dt_t, tm=1024, cast=(w_gate, w_up, w_down, w_out))

    ya_s, v_s, conv_s, split, b_s, c_s, dx_s = _sample_rows(
        proj_s, dt_s, jnp.swapaxes(state_conv[l], 0, 1), pack)
    ssm_s, yb_s = _sample_state(split, b_s, c_s, state_ssm[l].reshape(n_s, D_SSM, D_STATE),
                                dx_s, proj_s, pack)

    y_p, conv_p, ssm_p, y_s = _mix_ffn(
        hp, proj_p, dt_p, w_out_b, w_gate_b, w_up_b, w_down_b, pack,
        chunk_w_s[l].astype(F32), chunk_b_s[l].astype(F32).T, hs, ya_s, yb_s, batch, seq)

    return (
        y_p.reshape(batch, seq, D_MODEL),
        y_s.reshape(n_s, 1, D_MODEL),
        conv_p[None],
        ssm_p.reshape(1, batch, SSM_HEADS, SSM_HEAD_DIM, D_STATE),
        jnp.swapaxes(conv_s, 0, 1)[None],
        ssm_s.reshape(1, n_s, SSM_HEADS, SSM_HEAD_DIM, D_STATE),
        v_s.reshape(1, n_s, 1, D_CHUNK),
    )
```

```python
import functools

import jax
import jax.numpy as jnp
from jax import lax
from jax.experimental import pallas as pl
from jax.experimental.pallas import tpu as pltpu

F32 = jnp.float32
BF16 = jnp.bfloat16
HIGHEST = lax.Precision.HIGHEST

D_MODEL = 2048
D_CHUNK = 1024
CHUNK_HEADS = 8
CHUNK_HEAD_DIM = 128
CHUNK_LEN = 128
D_SSM = 1024
SSM_HEAD_DIM = 64
SSM_HEADS = 16
SSM_GROUPS = 2
HEADS_PER_GROUP = SSM_HEADS // SSM_GROUPS
GROUP_WIDTH = D_SSM // SSM_GROUPS
D_STATE = 128
CONV_W = 4
CONV_DIM = D_SSM + 2 * SSM_GROUPS * D_STATE
SSD_CHUNK = 128
D_MAIN = 2 * D_CHUNK + D_SSM + CONV_DIM
D_FF = 5632
EPS = 1e-6

LANES = 128
SUBLANES = 8
BF16_SUBLANES = 16
HEAD_PAD = LANES
N_SPLIT = 3
VMEM_LIMIT = 58 * 1024 * 1024

U_BLK, V_BLK, Z_BLK = 0, 1, 2
XBC_BLK = 2


def _rms(x, g):
    return x * lax.rsqrt(jnp.mean(x * x, axis=-1, keepdims=True) + EPS) * g


def _silu(x):
    return x * jax.nn.sigmoid(x)


def _softplus(x):
    return jnp.maximum(x, 0.0) + jnp.log1p(jnp.exp(-jnp.abs(x)))


def _dot(a, b):
    return jnp.dot(a, b, preferred_element_type=F32)


def _dot_nt(a, b):
    return lax.dot_general(a, b, (((1,), (1,)), ((), ())), preferred_element_type=F32)


def _dot_exact(a, b):
    return jnp.dot(a, b, precision=HIGHEST, preferred_element_type=F32)


def _params(*semantics):
    return pltpu.CompilerParams(dimension_semantics=semantics, vmem_limit_bytes=VMEM_LIMIT)


PACK_ROWS = 16
PACK_LAYOUT = {
    "g_mix": (0, 1, 0, D_MODEL),
    "g_ffn": (1, 1, 0, D_MODEL),
    "g_final": (2, 1, 0, D_MODEL),
    "gv": (3, 1, 0, D_CHUNK),
    "go": (3, 1, D_CHUNK, D_CHUNK),
    "ng": (4, 1, 0, D_SSM),
    "drep": (4, 1, D_SSM, D_SSM),
    "cb": (5, 1, 0, CONV_DIM),
    "cw": (6, CONV_W, 0, CONV_DIM),
    "dtb": (10, 1, 0, HEAD_PAD),
    "alog": (10, 1, HEAD_PAD, HEAD_PAD),
    "w0": (11, 1, 0, D_CHUNK),
    "b0": (11, 1, D_CHUNK, D_CHUNK),
}


class _Packed:
    def __init__(self, ref, name):
        self.ref = ref
        self.row, self.rows, self.lane, self.lanes = PACK_LAYOUT[name]

    def __getitem__(self, idx):
        lanes = slice(self.lane, self.lane + self.lanes)
        if idx is Ellipsis:
            return self.ref[self.row:self.row + self.rows, lanes]
        rows, cols = idx
        assert cols == slice(None)
        return self.ref[self.row + rows.start:self.row + rows.stop, lanes]


def _pack_params(**vectors):
    segments = [[] for _ in range(PACK_ROWS)]
    for name, value in vectors.items():
        row, n_rows, lane, lanes = PACK_LAYOUT[name]
        value = value.astype(F32).reshape(n_rows, -1)
        assert value.shape[1] <= lanes
        for r in range(n_rows):
            segments[row + r].append((lane, lanes, value[r]))
    parts = []
    for segs in segments:
        at = 0
        for lane, lanes, value in sorted(segs, key=lambda seg: seg[0]):
            parts += [jnp.zeros((lane - at,), F32), value, jnp.zeros((lanes - value.shape[0],), F32)]
            at = lane + lanes
        parts.append(jnp.zeros((D_MODEL - at,), F32))
    return jnp.concatenate([p for p in parts if p.shape[0]]).reshape(PACK_ROWS, D_MODEL)


def _head_expander(width, log2_per_head):
    row = lax.broadcasted_iota(jnp.int32, (HEAD_PAD, width), 0)
    col = lax.broadcasted_iota(jnp.int32, (HEAD_PAD, width), 1)
    return (row == jnp.right_shift(col, log2_per_head)).astype(F32)


IN_TN = 512


def _in_proj_kernel(x_hbm, xs_hbm, pack_ref, w_ref, wdt_ref, *refs, tm, n_cast):
    g_ref = _Packed(pack_ref, "g_mix")
    cast_in = refs[:n_cast]
    o_ref, dt_ref, os_ref, dts_ref = refs[n_cast:n_cast + 4]
    cast_out = refs[n_cast + 4:2 * n_cast + 4]
    xn_ref, wres_ref, xbuf_ref, sem = refs[2 * n_cast + 4:]
    i = pl.program_id(0)
    j = pl.program_id(1)
    last = pl.num_programs(0) - 1
    n_s = xs_hbm.shape[0]

    for src, dst in zip(cast_in, cast_out, strict=True):
        dst[...] = src[...].astype(BF16)

    def x_copy(tile):
        return pltpu.make_async_copy(x_hbm.at[pl.ds(tile * tm, tm), :], xbuf_ref, sem)

    def xs_copy():
        return pltpu.make_async_copy(xs_hbm, xbuf_ref.at[pl.ds(0, n_s), :], sem)

    @pl.when((i == 0) & (j == 0))
    def _():
        xs_copy().start()
        xs_copy().wait()
        nb = _rms(xbuf_ref[0:n_s, :], g_ref[...]).astype(BF16)
        xn_ref[tm:tm + n_s, :] = nb
        dts_ref[...] = _dot_nt(nb, wdt_ref[...])
        x_copy(0).start()

    @pl.when(i == 0)
    def _():
        wres_ref[j] = w_ref[...].T.astype(BF16)

    @pl.when(j == 0)
    def _():
        x_copy(i).wait()
        nb = _rms(xbuf_ref[...], g_ref[...]).astype(BF16)
        xn_ref[0:tm, :] = nb
        dt_ref[...] = _dot_nt(nb, wdt_ref[...])

    @pl.when((j == 1) & (i < last))
    def _():
        x_copy(i + 1).start()

    @pl.when(i < last)
    def _():
        o_ref[...] = _dot(xn_ref[0:tm, :], wres_ref[j])

    @pl.when(i == last)
    def _():
        res = _dot(xn_ref[...], wres_ref[j])
        o_ref[...] = res[0:tm, :]
        os_ref[...] = res[tm:, :]


def _in_proj(x, xs, pack, w_in_t, layer, w_dt_t, tm, cast=()):
    m = x.shape[0]
    n_s = xs.shape[0]
    n_i = m // tm
    n_j = D_MAIN // IN_TN
    assert n_j >= 2 and n_s <= tm and n_s % BF16_SUBLANES == 0
    cast_in, cast_out, cast_shapes = [], [], []
    n_steps = n_i * n_j

    def col_sample(i, j):
        return jnp.where(i == n_i - 1, j, 0)

    for w in cast:
        _, w_rows, w_cols = w.shape
        slab_rows = next(r for r in range(BF16_SUBLANES, w_rows + 1, BF16_SUBLANES)
                         if w_rows % r == 0 and w_rows // r <= n_steps)
        n_slabs = w_rows // slab_rows
        hold = n_steps // n_slabs
        slab = lambda i, j, n_slabs=n_slabs, hold=hold: jnp.minimum((i * n_j + j) // hold, n_slabs - 1)
        cast_in.append(pl.BlockSpec((None, slab_rows, w_cols), lambda i, j, slab=slab: (layer, slab(i, j), 0)))
        cast_out.append(pl.BlockSpec((slab_rows, w_cols), lambda i, j, slab=slab: (slab(i, j), 0)))
        cast_shapes.append(jax.ShapeDtypeStruct((w_rows, w_cols), BF16))
    return pl.pallas_call(
        functools.partial(_in_proj_kernel, tm=tm, n_cast=len(cast)),
        grid=(n_i, n_j),
        in_specs=[
            pl.BlockSpec(memory_space=pl.ANY),
            pl.BlockSpec(memory_space=pl.ANY),
            pl.BlockSpec((PACK_ROWS, D_MODEL), lambda i, j: (0, 0)),
            pl.BlockSpec((None, IN_TN, D_MODEL), lambda i, j: (layer, jnp.where(i == 0, j, n_j - 1), 0)),
            pl.BlockSpec((HEAD_PAD, D_MODEL), lambda i, j: (0, 0)),
            *cast_in,
        ],
        out_specs=[
            pl.BlockSpec((tm, IN_TN), lambda i, j: (i, j)),
            pl.BlockSpec((tm, HEAD_PAD), lambda i, j: (i, 0)),
            pl.BlockSpec((n_s, IN_TN), lambda i, j: (0, col_sample(i, j))),
            pl.BlockSpec((n_s, HEAD_PAD), lambda i, j: (0, 0)),
            *cast_out,
        ],
        out_shape=[
            jax.ShapeDtypeStruct((m, D_MAIN), F32),
            jax.ShapeDtypeStruct((m, HEAD_PAD), F32),
            jax.ShapeDtypeStruct((n_s, D_MAIN), F32),
            jax.ShapeDtypeStruct((n_s, HEAD_PAD), F32),
            *cast_shapes,
        ],
        scratch_shapes=[
            pltpu.VMEM((tm + n_s, D_MODEL), BF16),
            pltpu.VMEM((n_j, D_MODEL, IN_TN), BF16),
            pltpu.VMEM((tm, D_MODEL), F32),
            pltpu.SemaphoreType.DMA(()),
        ],
        compiler_params=_params("arbitrary", "arbitrary"),
        name="in_proj",
    )(x, xs, pack, w_in_t, w_dt_t, *cast)


def _mask_chunk_weights(ws_ref, wm_ref):
    row = lax.broadcasted_iota(jnp.int32, (CHUNK_LEN, CHUNK_LEN), 0)
    col = lax.broadcasted_iota(jnp.int32, (CHUNK_LEN, CHUNK_LEN), 1)
    for h in range(CHUNK_HEADS):
        wm_ref[:, h * CHUNK_LEN:(h + 1) * CHUNK_LEN] = jnp.where(row >= col, ws_ref[h], 0.0).astype(BF16)


def _chunk_gate(u_ref, v_ref, gv_ref, wm_ref, bst_ref, go_ref, store_y):
    vb = _rms(jax.nn.gelu(v_ref[...]), gv_ref[...]).astype(BF16)
    yield
    u = jax.nn.gelu(u_ref[...])
    yield
    zeros = jnp.zeros((CHUNK_LEN, CHUNK_HEAD_DIM), BF16)
    parts = []
    for h0 in range(0, CHUNK_HEADS, 2):
        w = wm_ref[:, h0 * CHUNK_LEN:(h0 + 2) * CHUNK_LEN]
        va = vb[:, h0 * CHUNK_HEAD_DIM:(h0 + 1) * CHUNK_HEAD_DIM]
        vb2 = vb[:, (h0 + 1) * CHUNK_HEAD_DIM:(h0 + 2) * CHUNK_HEAD_DIM]
        rhs = jnp.concatenate([jnp.concatenate([va, zeros], axis=1),
                               jnp.concatenate([zeros, vb2], axis=1)], axis=0)
        mixed = _dot(w, rhs)
        for k, h in enumerate((h0, h0 + 1)):
            sl = slice(h * CHUNK_HEAD_DIM, (h + 1) * CHUNK_HEAD_DIM)
            mh = mixed[:, k * CHUNK_HEAD_DIM:(k + 1) * CHUNK_HEAD_DIM] + bst_ref[:, h:h + 1]
            parts.append(u[:, sl] * mh)
        yield
    y = jnp.concatenate(parts, axis=1)
    store_y(_rms(y, go_ref[...]).astype(BF16))


CONV_PAD = 8


def _group_norm_out(y, z, g):
    y = y * _silu(z)
    parts = []
    for grp in range(SSM_GROUPS):
        yg = y[:, grp * GROUP_WIDTH:(grp + 1) * GROUP_WIDTH]
        parts.append(yg * lax.rsqrt(jnp.mean(yg * yg, axis=-1, keepdims=True) + EPS))
    return (jnp.concatenate(parts, axis=1) * g).astype(BF16)


def _split3(x):
    p1 = x.astype(BF16)
    r1 = x - p1.astype(F32)
    p2 = r1.astype(BF16)
    p3 = (r1 - p2.astype(F32)).astype(BF16)
    return p1, p2, p3


def _select_rows(sel3, x):
    return _dot(sel3, jnp.concatenate(_split3(x), axis=0))


def _ssd_chunk(first, xbc_ref, z_ref, dt_ref, cw_ref, cb_ref, dtb_ref, alog_ref, drep_ref, ng_ref,
               tril3_ref, nconv_ref, nssm_ref, tail_ref, ht_ref, store_y):
    q = SSD_CHUNK

    xbc = xbc_ref[...]
    tail = jnp.where(first, 0.0, tail_ref[...])
    ht_prev = jnp.where(first, 0.0, ht_ref[...])
    sub = lax.broadcasted_iota(jnp.int32, (CONV_PAD, CONV_DIM), 0)
    conv = cb_ref[...] + xbc * cw_ref[CONV_W - 1:CONV_W, :]
    for k in range(1, CONV_W):
        shifted = pltpu.roll(xbc, k, 0)
        head = jnp.where(sub < k, pltpu.roll(tail, k, 0), shifted[0:CONV_PAD, :])
        shifted = jnp.concatenate([head, shifted[CONV_PAD:, :]], axis=0)
        conv = conv + shifted * cw_ref[CONV_W - 1 - k:CONV_W - k, :]
    tail_ref[...] = xbc[q - CONV_PAD:, :]
    nconv_ref[...] = xbc[q - (CONV_W - 1):, :]
    yield

    act = _silu(conv)
    xs = act[:, :D_SSM]
    bmat = act[:, D_SSM:D_SSM + SSM_GROUPS * D_STATE]
    cmat = act[:, D_SSM + SSM_GROUPS * D_STATE:]
    yield

    dt = _softplus(dt_ref[...] + dtb_ref[...])
    a = -jnp.exp(alog_ref[...])
    acum = _select_rows(tril3_ref[...], dt * a)
    acum_t = acum.T
    low_lanes = lax.broadcasted_iota(jnp.int32, (q, 2 * SSM_HEAD_DIM), 1) < SSM_HEAD_DIM

    def expand(cols):
        pairs = []
        for h0 in range(0, SSM_HEADS, 2):
            a = jnp.broadcast_to(cols[:, h0:h0 + 1], (q, 2 * SSM_HEAD_DIM))
            b = jnp.broadcast_to(cols[:, h0 + 1:h0 + 2], (q, 2 * SSM_HEAD_DIM))
            pairs.append(jnp.where(low_lanes, a, b))
        return jnp.concatenate(pairs, axis=1)

    dt_rep = expand(dt)
    yield
    acum_rep = expand(acum)
    yield
    xdt = xs * dt_rep
    exp_a = jnp.exp(acum_rep)
    xw = xdt * jnp.exp(acum_rep[q - 1:q, :] - acum_rep)
    yield

    row = lax.broadcasted_iota(jnp.int32, (q, q), 0)
    col = lax.broadcasted_iota(jnp.int32, (q, q), 1)
    causal = row >= col
    low_half = col < SSM_HEAD_DIM
    ydiag, yoff, states = [], [], []
    for grp in range(SSM_GROUPS):
        bg_f = bmat[:, grp * D_STATE:(grp + 1) * D_STATE]
        bg = bg_f.astype(BF16)
        cg = cmat[:, grp * D_STATE:(grp + 1) * D_STATE].astype(BF16)
        cb = _dot_nt(cg, bg)
        ch = slice(grp * GROUP_WIDTH, (grp + 1) * GROUP_WIDTH)
        yoff.append(_dot(cg, ht_prev[:, ch].astype(BF16)))
        yield
        for quad in range(HEADS_PER_GROUP // 4):
            h0 = grp * HEADS_PER_GROUP + 4 * quad
            masks, blocks = [], []
            for k, h in enumerate(range(h0, h0 + 4)):
                diff = acum[:, h:h + 1] - acum_t[h:h + 1, :]
                masks.append((cb * jnp.exp(jnp.where(causal, diff, -jnp.inf))).astype(BF16))
                xp = xdt[:, (h - k % 2) * SSM_HEAD_DIM:(h - k % 2 + 2) * SSM_HEAD_DIM]
                own = jnp.where(low_half == (k % 2 == 0), xp, 0.0).astype(BF16)
                zero = jnp.zeros_like(own)
                blocks.append(jnp.concatenate([own, zero] if k < 2 else [zero, own], axis=1))
                if k % 2 == 1:
                    yield
            ydiag.append(_dot(jnp.concatenate(masks, axis=1), jnp.concatenate(blocks, axis=0)))
            yield
        states.append(_dot(bg_f.T.astype(BF16), xw[:, ch].astype(BF16)))
        yield
    y = drep_ref[...] * xs + jnp.concatenate(ydiag, axis=1) + jnp.concatenate(yoff, axis=1) * exp_a

    ht_new = ht_prev * exp_a[q - 1:q, :] + jnp.concatenate(states, axis=1)
    ht_ref[...] = ht_new
    nssm_ref[...] = ht_new.T
    yield
    store_y(_group_norm_out(y, z_ref[...], ng_ref[...]))


def _sample_rows_kernel(u_ref, v_ref, xbc_ref, dt_ref, sc_ref, pack_ref,
                        ya_ref, vrows_ref, nconv_ref, split_ref, b_ref, c_ref, dx_ref):
    gv_ref, w0_ref, b0_ref, go_ref, cw_ref, cb_ref, dtb_ref, alog_ref, drep_ref = (
        _Packed(pack_ref, name) for name in ("gv", "w0", "b0", "go", "cw", "cb", "dtb", "alog", "drep"))
    vr =_rms(jax.nn.gelu(v_ref[...]), gv_ref[...])
    vrows_ref[...] = vr
    mixed = vr * w0_ref[...] + b0_ref[...]
    ya_ref[...] = _rms(jax.nn.gelu(u_ref[...]) * mixed, go_ref[...]).astype(BF16)

    xbc = xbc_ref[...]
    conv = cb_ref[...]
    for k in range(CONV_W - 1):
        conv = conv + sc_ref[k] * cw_ref[k:k + 1, :]
    conv = conv + xbc * cw_ref[CONV_W - 1:CONV_W, :]
    for k in range(CONV_W - 2):
        nconv_ref[k] = sc_ref[k + 1]
    nconv_ref[CONV_W - 2] = xbc

    act = _silu(conv)
    xs = act[:, :D_SSM]
    b_ref[...] = act[:, D_SSM:D_SSM + SSM_GROUPS * D_STATE]
    c_ref[...] = act[:, D_SSM + SSM_GROUPS * D_STATE:]
    dx_ref[...] = drep_ref[...] * xs

    dt = _softplus(dt_ref[...] + dtb_ref[...])
    decay = jnp.exp(dt * (-jnp.exp(alog_ref[...])))
    xdt = xs * _dot_exact(dt, _head_expander(D_SSM, SSM_HEAD_DIM.bit_length() - 1))
    decay8 = _dot_exact(decay, _head_expander(HEAD_PAD, SUBLANES.bit_length() - 1))
    cols = jnp.concatenate([xdt, decay8], axis=1).T
    for p in range(N_SPLIT):
        piece = cols.astype(BF16)
        split_ref[:, p * LANES:(p + 1) * LANES] = piece
        cols = cols - piece.astype(F32)


SPLIT_ROWS = D_SSM + HEAD_PAD


def _sample_rows(proj, dt_raw, sconv, pack):
    n = proj.shape[0]
    full = lambda shape: pl.BlockSpec(shape, lambda i: (0,) * len(shape))
    return pl.pallas_call(
        _sample_rows_kernel,
        grid=(1,),
        in_specs=[
            pl.BlockSpec((n, D_CHUNK), lambda i: (0, U_BLK)),
            pl.BlockSpec((n, D_CHUNK), lambda i: (0, V_BLK)),
            pl.BlockSpec((n, CONV_DIM), lambda i: (0, XBC_BLK)),
            full((n, HEAD_PAD)),
            full((CONV_W - 1, n, CONV_DIM)),
            full((PACK_ROWS, D_MODEL)),
        ],
        out_specs=[
            full((n, D_CHUNK)), full((n, D_CHUNK)), full((CONV_W - 1, n, CONV_DIM)),
            full((SPLIT_ROWS, N_SPLIT * LANES)),
            full((n, SSM_GROUPS * D_STATE)), full((n, SSM_GROUPS * D_STATE)), full((n, D_SSM)),
        ],
        out_shape=[
            jax.ShapeDtypeStruct((n, D_CHUNK), BF16),
            jax.ShapeDtypeStruct((n, D_CHUNK), F32),
            jax.ShapeDtypeStruct((CONV_W - 1, n, CONV_DIM), F32),
            jax.ShapeDtypeStruct((SPLIT_ROWS, N_SPLIT * LANES), BF16),
            jax.ShapeDtypeStruct((n, SSM_GROUPS * D_STATE), F32),
            jax.ShapeDtypeStruct((n, SSM_GROUPS * D_STATE), F32),
            jax.ShapeDtypeStruct((n, D_SSM), F32),
        ],
        compiler_params=_params("arbitrary"),
        name="sample_rows",
    )(proj, proj, proj, dt_raw, sconv, pack)


SAMPLE_TB = 16


def _sample_state_kernel(split_ref, b_ref, c_ref, h0_ref, dx_ref, z_ref, pack_ref,
                         hn_ref, y_ref, yt_ref):
    ng_ref = _Packed(pack_ref, "ng")
    i = pl.program_id(0)
    n_tok = b_ref.shape[0]

    @pl.when(i == 0)
    def _():
        yt_ref[...] = jnp.zeros_like(yt_ref)

    tok = lax.broadcasted_iota(jnp.int32, (N_SPLIT * LANES, D_STATE), 0) % LANES
    lane = lax.broadcasted_iota(jnp.int32, (D_SSM, n_tok), 1)

    def per_group(row, grp):
        return jnp.broadcast_to(row[:, grp * D_STATE:(grp + 1) * D_STATE], (GROUP_WIDTH, D_STATE))

    pair = None
    for k in range(SAMPLE_TB):
        t = i * SAMPLE_TB + k
        if k % 2 == 0:
            onehot = jnp.concatenate([tok == t, tok == t + 1], axis=1).astype(BF16)
            pair = _dot(split_ref[...], onehot)
        bcast = pair[:, (k % 2) * D_STATE:(k % 2 + 1) * D_STATE]
        decay = jnp.broadcast_to(
            bcast[D_SSM:, :].reshape(SSM_HEADS, 1, SUBLANES, D_STATE),
            (SSM_HEADS, SSM_HEAD_DIM // SUBLANES, SUBLANES, D_STATE)).reshape(D_SSM, D_STATE)
        brow = b_ref[pl.ds(t, 1), :]
        crow = c_ref[pl.ds(t, 1), :]
        bfull = jnp.concatenate([per_group(brow, g) for g in range(SSM_GROUPS)], axis=0)
        cfull = jnp.concatenate([per_group(crow, g) for g in range(SSM_GROUPS)], axis=0)
        h_new = h0_ref[k] * decay + bcast[:D_SSM, :] * bfull
        hn_ref[k] = h_new
        ysum = jnp.sum(h_new * cfull, axis=-1, keepdims=True)
        yt_ref[...] = jnp.where(lane == t, ysum, yt_ref[...])

    @pl.when(i == pl.num_programs(0) - 1)
    def _():
        y = yt_ref[...].T + dx_ref[...]
        y_ref[...] = _group_norm_out(y, z_ref[...], ng_ref[...])


def _sample_state(split, bmat, cmat, h0, dx, proj, pack):
    n = h0.shape[0]
    const2 = lambda i: (0, 0)
    return pl.pallas_call(
        _sample_state_kernel,
        grid=(n // SAMPLE_TB,),
        in_specs=[
            pl.BlockSpec((SPLIT_ROWS, N_SPLIT * LANES), const2),
            pl.BlockSpec((n, SSM_GROUPS * D_STATE), const2),
            pl.BlockSpec((n, SSM_GROUPS * D_STATE), const2),
            pl.BlockSpec((SAMPLE_TB, D_SSM, D_STATE), lambda i: (i, 0, 0)),
            pl.BlockSpec((n, D_SSM), const2),
            pl.BlockSpec((n, D_SSM), lambda i: (0, Z_BLK)),
            pl.BlockSpec((PACK_ROWS, D_MODEL), const2),
        ],
        out_specs=[
            pl.BlockSpec((SAMPLE_TB, D_SSM, D_STATE), lambda i: (i, 0, 0)),
            pl.BlockSpec((n, D_SSM), const2),
        ],
        out_shape=[
            jax.ShapeDtypeStruct((n, D_SSM, D_STATE), F32),
            jax.ShapeDtypeStruct((n, D_SSM), BF16),
        ],
        scratch_shapes=[pltpu.VMEM((D_SSM, n), F32)],
        compiler_params=_params("arbitrary"),
        name="sample_state",
    )(split, bmat, cmat, h0, dx, proj, pack)


FFN_TF = 512
MIX_TM = 512
CHUNKS_PER_TILE = MIX_TM // SSD_CHUNK
FFN_UP_PIECE = 256
FFN_DOWN_PIECE = 512
MIX_PHASES_PER_FFN_PIECE = 5
MIX_LEAD_PIECES = 0


def _interleave(main, side, side_per_main, lead=0):
    for k, _ in enumerate(main):
        if k >= lead:
            for _ in range(side_per_main):
                next(side, None)
    for _ in side:
        pass


def _mix_ffn_kernel(x_ref, wo_ref, wg_ref, wu_ref, wd_ref, pack_ref, u_ref, v_ref, ws_ref, bst_ref,
                    xbc_ref, z_ref, dt_ref, xs_ref, yas_ref, ybs_ref,
                    o_ref, nconv_ref, nssm_ref, os_ref,
                    m_ref, ya_ref, yb_ref, tail_ref, ht_ref, wm_ref, ms_ref, tril3_ref, *, chunks_per_seq):
    g_ref, gf_ref, gv_ref, go_ref, cw_ref, cb_ref, dtb_ref, alog_ref, drep_ref, ng_ref = (
        _Packed(pack_ref, name)
        for name in ("g_ffn", "g_final", "gv", "go", "cw", "cb", "dtb", "alog", "drep", "ng"))
    s = pl.program_id(0)
    f = pl.program_id(1)
    has_ffn = s >= 1
    is_last = s == pl.num_programs(0) - 1
    has_mix = jnp.logical_not(is_last) & (f < CHUNKS_PER_TILE)

    @pl.when((s == 0) & (f == 0))
    def _():
        _mask_chunk_weights(ws_ref, wm_ref)
        row = lax.broadcasted_iota(jnp.int32, tril3_ref.shape, 0)
        col = lax.broadcasted_iota(jnp.int32, tril3_ref.shape, 1)
        tril3_ref[...] = (row >= jnp.bitwise_and(col, SSD_CHUNK - 1)).astype(BF16)

    def project(x, ya, yb, m_out, o_out):
        h = x + _dot(jnp.concatenate([ya, yb], axis=1), wo_ref[...])
        m_out[...] = _rms(h, g_ref[...]).astype(BF16)
        o_out[...] = h

    @pl.when(has_ffn & (f == 0))
    def _():
        project(x_ref[...], ya_ref[...], yb_ref[...], m_ref, o_ref)

    @pl.when(is_last & (f == 0))
    def _():
        project(xs_ref[...], yas_ref[...], ybs_ref[...], ms_ref, os_ref)

    def ffn_step(with_sample=False):
        m = jnp.concatenate([m_ref[...], ms_ref[...]], axis=0) if with_sample else m_ref[...]
        acts = []
        for c in range(FFN_TF // FFN_UP_PIECE):
            cols = slice(c * FFN_UP_PIECE, (c + 1) * FFN_UP_PIECE)
            halves = []
            for w_ref in (wg_ref, wu_ref):
                halves.append(_dot(m, w_ref[:, cols]))
                yield
            acts.append((_silu(halves[0]) * halves[1]).astype(BF16))
        act = jnp.concatenate(acts, axis=1)
        for c in range(D_MODEL // FFN_DOWN_PIECE):
            cols = slice(c * FFN_DOWN_PIECE, (c + 1) * FFN_DOWN_PIECE)
            res = _dot(act, wd_ref[:, cols])
            o_ref[:, cols] += res[:MIX_TM]
            if with_sample:
                os_ref[:, cols] += res[MIX_TM:]
            yield

    chunk_rows = pl.ds(pl.multiple_of(f * SSD_CHUNK, SSD_CHUNK), SSD_CHUNK)

    def store_ya(ya):
        ya_ref[chunk_rows, :] = ya

    def store_yb(yb):
        yb_ref[chunk_rows, :] = yb

    def mix_step():
        first = (s * CHUNKS_PER_TILE + f) % chunks_per_seq == 0
        yield from _chunk_gate(u_ref, v_ref, gv_ref, wm_ref, bst_ref, go_ref, store_ya)
        yield
        yield from _ssd_chunk(first, xbc_ref, z_ref, dt_ref, cw_ref, cb_ref, dtb_ref, alog_ref, drep_ref,
                              ng_ref, tril3_ref, nconv_ref, nssm_ref, tail_ref, ht_ref, store_yb)

    @pl.when(has_ffn & has_mix)
    def _():
        _interleave(ffn_step(), mix_step(), MIX_PHASES_PER_FFN_PIECE, MIX_LEAD_PIECES)

    @pl.when(has_ffn & jnp.logical_not(has_mix) & jnp.logical_not(is_last))
    def _():
        _interleave(ffn_step(), iter(()), 0)

    @pl.when(is_last)
    def _():
        _interleave(ffn_step(with_sample=True), iter(()), 0)

    @pl.when(jnp.logical_not(has_ffn) & has_mix)
    def _():
        _interleave(iter(()), mix_step(), 0)

    @pl.when(has_ffn & (f == pl.num_programs(1) - 1))
    def _():
        o_ref[...] = _rms(o_ref[...], gf_ref[...])

    @pl.when(is_last & (f == pl.num_programs(1) - 1))
    def _():
        os_ref[...] = _rms(os_ref[...], gf_ref[...])


def _mix_ffn(x, proj, dt_raw, w_out_b, w_gate, w_up, w_down, pack, ws, bst, xs, ya_s, yb_s, batch, seq):
    m = x.shape[0]
    n_s = xs.shape[0]
    n_tiles = m // MIX_TM
    n_chunks = m // SSD_CHUNK
    chunks_per_seq = seq // SSD_CHUNK
    assert MIX_TM % SSD_CHUNK == 0 and seq % MIX_TM == 0 and CHUNKS_PER_TILE <= D_FF // FFN_TF

    def tile(s, f):
        return jnp.maximum(s - 1, 0)

    def chunk(s, f):
        return jnp.where(s >= n_tiles, n_chunks - 1,
                         s * CHUNKS_PER_TILE + jnp.minimum(f, CHUNKS_PER_TILE - 1))

    def seq_of(s, f):
        return chunk(s, f) // chunks_per_seq

    def ffn_block(s, f):
        return jnp.where(s == 0, 0, f)

    const2 = lambda s, f: (0, 0)
    return pl.pallas_call(
        functools.partial(_mix_ffn_kernel, chunks_per_seq=chunks_per_seq),
        grid=(n_tiles + 1, D_FF // FFN_TF),
        in_specs=[
            pl.BlockSpec((MIX_TM, D_MODEL), lambda s, f: (tile(s, f), 0)),
            pl.BlockSpec((D_MODEL, D_MODEL), const2, pipeline_mode=pl.Buffered(1)),
            pl.BlockSpec((D_MODEL, FFN_TF), lambda s, f: (0, ffn_block(s, f))),
            pl.BlockSpec((D_MODEL, FFN_TF), lambda s, f: (0, ffn_block(s, f))),
            pl.BlockSpec((FFN_TF, D_MODEL), lambda s, f: (ffn_block(s, f), 0)),
            pl.BlockSpec((PACK_ROWS, D_MODEL), const2),
            pl.BlockSpec((CHUNK_LEN, D_CHUNK), lambda s, f: (chunk(s, f), U_BLK)),
            pl.BlockSpec((CHUNK_LEN, D_CHUNK), lambda s, f: (chunk(s, f), V_BLK)),
            pl.BlockSpec((CHUNK_HEADS, CHUNK_LEN, CHUNK_LEN), lambda s, f: (0, 0, 0),
                         pipeline_mode=pl.Buffered(1)),
            pl.BlockSpec((CHUNK_LEN, CHUNK_HEADS), const2),
            pl.BlockSpec((SSD_CHUNK, CONV_DIM), lambda s, f: (chunk(s, f), XBC_BLK)),
            pl.BlockSpec((SSD_CHUNK, D_SSM), lambda s, f: (chunk(s, f), Z_BLK)),
            pl.BlockSpec((SSD_CHUNK, HEAD_PAD), lambda s, f: (chunk(s, f), 0)),
            pl.BlockSpec((n_s, D_MODEL), const2, pipeline_mode=pl.Buffered(1)),
            pl.BlockSpec((n_s, D_CHUNK), const2, pipeline_mode=pl.Buffered(1)),
            pl.BlockSpec((n_s, D_SSM), const2, pipeline_mode=pl.Buffered(1)),
        ],
        out_specs=[
            pl.BlockSpec((MIX_TM, D_MODEL), lambda s, f: (tile(s, f), 0)),
            pl.BlockSpec((None, CONV_W - 1, CONV_DIM), lambda s, f: (seq_of(s, f), 0, 0)),
            pl.BlockSpec((None, D_SSM, D_STATE), lambda s, f: (seq_of(s, f), 0, 0)),
            pl.BlockSpec((n_s, D_MODEL), const2),
        ],
        out_shape=[
            jax.ShapeDtypeStruct((m, D_MODEL), F32),
            jax.ShapeDtypeStruct((batch, CONV_W - 1, CONV_DIM), F32),
            jax.ShapeDtypeStruct((batch, D_SSM, D_STATE), F32),
            jax.ShapeDtypeStruct((n_s, D_MODEL), F32),
        ],
        scratch_shapes=[
            pltpu.VMEM((MIX_TM, D_MODEL), BF16),
            pltpu.VMEM((MIX_TM, D_CHUNK), BF16),
            pltpu.VMEM((MIX_TM, D_SSM), BF16),
            pltpu.VMEM((CONV_PAD, CONV_DIM), F32),
            pltpu.VMEM((D_STATE, D_SSM), F32),
            pltpu.VMEM((CHUNK_LEN, CHUNK_HEADS * CHUNK_LEN), BF16),
            pltpu.VMEM((n_s, D_MODEL), BF16),
            pltpu.VMEM((SSD_CHUNK, N_SPLIT * SSD_CHUNK), BF16),
        ],
        compiler_params=_params("arbitrary", "arbitrary"),
        name="mix_ffn",
    )(x, w_out_b, w_gate, w_up, w_down, pack, proj, proj, ws, bst, proj, proj, dt_raw, xs, ya_s, yb_s)


def kernel(x_prompt, x_sample, state_conv, state_ssm, norm_mix_g, w_in, chunk_v_norm_g, chunk_w_s,
           chunk_b_s, chunk_out_norm_g, ssd_conv_w, ssd_conv_b, ssd_dt_bias, ssd_a_log, ssd_d,
           ssd_norm_g, w_out, norm_ffn_g, w_gate, w_up, w_down, norm_final_g):
    depth = w_in.shape[0]
    batch, seq, _ = x_prompt.shape
    n_s = x_sample.shape[0]
    assert x_sample.shape[1] == 1 and seq % SSD_CHUNK == 0 and depth == 1

    hp = x_prompt.reshape(batch * seq, D_MODEL)
    hs = x_sample.reshape(n_s, D_MODEL)
    l = 0
    w_in_t = jnp.swapaxes(w_in, 1, 2)
    w_dt_t = jnp.pad(w_in_t[l, D_MAIN:, :], ((0, HEAD_PAD - SSM_HEADS), (0, 0))).astype(BF16)
    pack = _pack_params(
        g_mix=norm_mix_g[l], g_ffn=norm_ffn_g[l], g_final=norm_final_g,
        gv=chunk_v_norm_g[l], go=chunk_out_norm_g[l],
        ng=ssd_norm_g[l], drep=jnp.repeat(ssd_d[l], SSM_HEAD_DIM),
        cb=ssd_conv_b[l], cw=ssd_conv_w[l], dtb=ssd_dt_bias[l], alog=ssd_a_log[l],
        w0=jnp.repeat(chunk_w_s[l, :, 0, 0], CHUNK_HEAD_DIM),
        b0=jnp.repeat(chunk_b_s[l, :, 0], CHUNK_HEAD_DIM))

    proj_p, dt_p, proj_s, dt_s, w_gate_b, w_up_b, w_down_b, w_out_b = _in_proj(
        hp, hs, pack, w_in_t, l, w_dt_t, tm=1024, cast=(w_gate, w_up, w_down, w_out))

    ya_s, v_s, conv_s, split, b_s, c_s, dx_s = _sample_rows(
        proj_s, dt_s, jnp.swapaxes(state_conv[l], 0, 1), pack)
    ssm_s, yb_s = _sample_state(split, b_s, c_s, state_ssm[l].reshape(n_s, D_SSM, D_STATE),
                                dx_s, proj_s, pack)

    y_p, conv_p, ssm_p, y_s = _mix_ffn(
        hp, proj_p, dt_p, w_out_b, w_gate_b, w_up_b, w_down_b, pack,
        chunk_w_s[l].astype(F32), chunk_b_s[l].astype(F32).T, hs, ya_s, yb_s, batch, seq)

    return (
        y_p.reshape(batch, seq, D_MODEL),
        y_s.reshape(n_s, 1, D_MODEL),
        conv_p[None],
        ssm_p.reshape(1, batch, SSM_HEADS, SSM_HEAD_DIM, D_STATE),
        jnp.swapaxes(conv_s, 0, 1)[None],
        ssm_s.reshape(1, n_s, SSM_HEADS, SSM_HEAD_DIM, D_STATE),
        v_s.reshape(1, n_s, 1, D_CHUNK),
    )
```
